```python
import jax, jax.numpy as jnp
from jax import lax
import numpy as np

D_MODEL = 1024
BATCH = 8
SEQ = 2048
DEPTH = 1

N_HEADS_MLA = 8
QK_NOPE_DIM = 64
QK_ROPE_DIM = 32
V_HEAD_DIM = 64
Q_LORA_RANK = 256
KV_LORA_RANK = 128
QK_HEAD_DIM = QK_NOPE_DIM + QK_ROPE_DIM
ROPE_THETA = 10000.0
Q_BLOCK = 128
MLA_WIDTH = N_HEADS_MLA * V_HEAD_DIM

N_FOURIER_GROUPS = 8
FOURIER_GROUP_DIM = 64
FOURIER_WIDTH = N_FOURIER_GROUPS * FOURIER_GROUP_DIM

N_EXPERTS = 256
TOP_K = 8
EXPERT_DIM = 256
SHARED_DIM = 256
ROUTED_SCALE = 2.5
EXPERT_BLOCK = 128

EPS = 1e-6
N_ADA = 6

IN_SPLITS = (Q_LORA_RANK, KV_LORA_RANK, QK_ROPE_DIM, FOURIER_WIDTH, D_MODEL, D_MODEL)
IN_WIDTH = sum(IN_SPLITS)
IN_OFFSETS = tuple(int(o) for o in np.cumsum(IN_SPLITS)[:-1])

kernel_name = "hybrid_mla_fnet_moe_adaln_block"


def rms_norm(x, g):
    xf = x.astype(jnp.float32)
    y = xf * lax.rsqrt(jnp.mean(xf * xf, axis=-1, keepdims=True) + EPS)
    return (y * g.astype(jnp.float32)).astype(x.dtype)


def rope_tables(seq_len):
    pos = jnp.arange(seq_len, dtype=jnp.float32)
    inv_freq = ROPE_THETA ** (-jnp.arange(0, QK_ROPE_DIM, 2, dtype=jnp.float32) / QK_ROPE_DIM)
    ang = pos[:, None] * inv_freq[None, :]
    return jnp.cos(ang)[None, :, None, :], jnp.sin(ang)[None, :, None, :]


def apply_rope_tail(t, cos, sin):
    nope, rot = t[..., :QK_NOPE_DIM], t[..., QK_NOPE_DIM:]
    r1, r2 = jnp.split(rot, 2, axis=-1)
    cos = cos.astype(t.dtype)
    sin = sin.astype(t.dtype)
    rot = jnp.concatenate([r1 * cos - r2 * sin, r2 * cos + r1 * sin], axis=-1)
    return jnp.concatenate([nope, rot], axis=-1)


def mla_mixer(z_q, z_kv, z_kr, q_a_norm_g, w_uq, kv_a_norm_g, w_ukv, q_norm_g, k_norm_g):
    B, S, _ = z_q.shape
    H = N_HEADS_MLA
    cq = rms_norm(z_q, q_a_norm_g)
    q = (cq @ w_uq).reshape(B, S, H, QK_HEAD_DIM)
    ckv = rms_norm(z_kv, kv_a_norm_g)
    kv = (ckv @ w_ukv).reshape(B, S, H, QK_NOPE_DIM + V_HEAD_DIM)
    k_nope, v = kv[..., :QK_NOPE_DIM], kv[..., QK_NOPE_DIM:]
    k_rope = jnp.broadcast_to(z_kr[:, :, None, :], (B, S, H, QK_ROPE_DIM))
    k = jnp.concatenate([k_nope, k_rope], axis=-1)
    q = rms_norm(q, q_norm_g)
    k = rms_norm(k, k_norm_g)
    cos, sin = rope_tables(S)
    q = apply_rope_tail(q, cos, sin)
    k = apply_rope_tail(k, cos, sin)
    q = q * jnp.asarray(QK_HEAD_DIM ** -0.5, dtype=q.dtype)
    n_blocks = S // Q_BLOCK
    qb = q.reshape(B, n_blocks, Q_BLOCK, H, QK_HEAD_DIM).transpose(1, 0, 2, 3, 4)

    def attend(q_blk):
        s = jnp.einsum('bqhd,bkhd->bhqk', q_blk, k, preferred_element_type=jnp.float32)
        p = jax.nn.softmax(s, axis=-1).astype(v.dtype)
        return jnp.einsum('bhqk,bkhd->bqhd', p, v)

    o = lax.map(attend, qb)
    return o.transpose(1, 0, 2, 3, 4).reshape(B, S, MLA_WIDTH)


def fourier_mixer(z_f):
    B, S, _ = z_f.shape
    zg = z_f.astype(jnp.float32).reshape(B, S, N_FOURIER_GROUPS, FOURIER_GROUP_DIM)
    zf = jnp.fft.fft2(zg, axes=(1, 3), norm='ortho')
    return jnp.real(zf).astype(z_f.dtype).reshape(B, S, FOURIER_WIDTH)


def swiglu(x, w_gate, w_up, w_down):
    return (jax.nn.silu(x @ w_gate) * (x @ w_up)) @ w_down


def moe_ffn(h, w_router, router_bias, w_exp_gate, w_exp_up, w_exp_down,
            w_sh_gate, w_sh_up, w_sh_down):
    B, S, D = h.shape
    T = B * S
    P = T * TOP_K
    n_blk = -(-P // EXPERT_BLOCK) + N_EXPERTS
    hf = h.reshape(T, D)
    scores = jax.nn.sigmoid(hf.astype(jnp.float32) @ w_router.astype(jnp.float32))
    _, idx = lax.top_k(scores + router_bias.astype(jnp.float32)[None, :], TOP_K)
    w_sel = jnp.take_along_axis(scores, idx, axis=-1)
    w_sel = w_sel / jnp.sum(w_sel, axis=-1, keepdims=True) * ROUTED_SCALE

    e_flat = idx.reshape(P).astype(jnp.int32)
    tok_flat = jnp.repeat(jnp.arange(T, dtype=jnp.int32), TOP_K)
    w_flat = w_sel.reshape(P).astype(h.dtype)
    order = jnp.argsort(e_flat, stable=True)
    e_sorted = e_flat[order]
    counts = jnp.bincount(e_flat, length=N_EXPERTS)
    starts = jnp.cumsum(counts) - counts
    padded = ((counts + EXPERT_BLOCK - 1) // EXPERT_BLOCK) * EXPERT_BLOCK
    p_ends = jnp.cumsum(padded)
    p_starts = p_ends - padded
    rank = jnp.arange(P, dtype=jnp.int32) - starts[e_sorted]
    slot = p_starts[e_sorted] + rank
    n_slots = n_blk * EXPERT_BLOCK
    slot_tok = jnp.zeros((n_slots,), jnp.int32).at[slot].set(tok_flat[order])
    slot_w = jnp.zeros((n_slots,), h.dtype).at[slot].set(w_flat[order])
    blk_start = jnp.arange(n_blk, dtype=p_ends.dtype) * EXPERT_BLOCK
    blk_expert = jnp.clip(jnp.searchsorted(p_ends, blk_start, side='right'), 0, N_EXPERTS - 1)

    def run_block(args):
        e, tok, wt = args
        xb = hf[tok]
        yb = swiglu(xb, w_exp_gate[e], w_exp_up[e], w_exp_down[e])
        return yb * wt[:, None]

    ys = lax.map(run_block, (blk_expert, slot_tok.reshape(n_blk, EXPERT_BLOCK),
                             slot_w.reshape(n_blk, EXPERT_BLOCK)))
    routed = jnp.zeros((T, D), h.dtype).at[slot_tok].add(ys.reshape(n_slots, D))
    shared = swiglu(hf, w_sh_gate, w_sh_up, w_sh_down)
    return (routed + shared).reshape(B, S, D)


def setup_inputs(seed: int = 0) -> dict:
    key = jax.random.key(seed)
    ks = iter(jax.random.split(key, 32))
    L, D = DEPTH, D_MODEL

    def dense(shape, fan_in, gain=1.0):
        return jax.random.normal(next(ks), shape, jnp.float32) * (gain * fan_in ** -0.5)

    def norm_gain(shape):
        return 1.0 + 0.1 * jax.random.normal(next(ks), shape, jnp.float32)

    return {
        'x': jax.random.normal(next(ks), (BATCH, SEQ, D), jnp.float32),
        'c': jax.random.normal(next(ks), (BATCH, D), jnp.float32),
        'w_ada': dense((L, D, N_ADA * D), D, 0.5),
        'b_ada': 0.02 * jax.random.normal(next(ks), (L, N_ADA * D), jnp.float32),
        'norm1_g': norm_gain((L, D)),
        'w_in': dense((L, D, IN_WIDTH), D),
        'q_a_norm_g': norm_gain((L, Q_LORA_RANK)),
        'w_uq': dense((L, Q_LORA_RANK, N_HEADS_MLA * QK_HEAD_DIM), Q_LORA_RANK),
        'kv_a_norm_g': norm_gain((L, KV_LORA_RANK)),
        'w_ukv': dense((L, KV_LORA_RANK, N_HEADS_MLA * (QK_NOPE_DIM + V_HEAD_DIM)), KV_LORA_RANK),
        'q_norm_g': norm_gain((L, QK_HEAD_DIM)),
        'k_norm_g': norm_gain((L, QK_HEAD_DIM)),
        'w_proj_attn': dense((L, MLA_WIDTH, D), MLA_WIDTH),
        'w_proj_fourier': dense((L, FOURIER_WIDTH, D), FOURIER_WIDTH),
        'w_out': dense((L, D, D), D),
        'norm2_g': norm_gain((L, D)),
        'w_router': dense((L, D, N_EXPERTS), D),
        'router_bias': 0.01 * jax.random.normal(next(ks), (L, N_EXPERTS), jnp.float32),
        'w_exp_gate': dense((L, N_EXPERTS, D, EXPERT_DIM), D),
        'w_exp_up': dense((L, N_EXPERTS, D, EXPERT_DIM), D),
        'w_exp_down': dense((L, N_EXPERTS, EXPERT_DIM, D), EXPERT_DIM),
        'w_sh_gate': dense((L, D, SHARED_DIM), D),
        'w_sh_up': dense((L, D, SHARED_DIM), D),
        'w_sh_down': dense((L, SHARED_DIM, D), SHARED_DIM),
    }


def reference(x, c, w_ada, b_ada, norm1_g, w_in, q_a_norm_g, w_uq, kv_a_norm_g, w_ukv,
              q_norm_g, k_norm_g, w_proj_attn, w_proj_fourier, w_out, norm2_g,
              w_router, router_bias, w_exp_gate, w_exp_up, w_exp_down,
              w_sh_gate, w_sh_up, w_sh_down):
    for l in range(DEPTH):
        ada = jax.nn.silu(c) @ w_ada[l] + b_ada[l]
        sh1, sc1, g1, sh2, sc2, g2 = jnp.split(ada[:, None, :], N_ADA, axis=-1)

        h = rms_norm(x, norm1_g[l]) * (1 + sc1) + sh1
        z = h @ w_in[l]
        z_q, z_kv, z_kr, z_f, gate_a, gate_f = jnp.split(z, IN_OFFSETS, axis=-1)
        y_a = mla_mixer(z_q, z_kv, z_kr, q_a_norm_g[l], w_uq[l], kv_a_norm_g[l], w_ukv[l],
                        q_norm_g[l], k_norm_g[l]) @ w_proj_attn[l]
        y_f = fourier_mixer(z_f) @ w_proj_fourier[l]
        merged = jax.nn.sigmoid(gate_a) * y_a + jax.nn.sigmoid(gate_f) * y_f
        x = x + g1 * (merged @ w_out[l])

        h2 = rms_norm(x, norm2_g[l]) * (1 + sc2) + sh2
        x = x + g2 * moe_ffn(h2, w_router[l], router_bias[l], w_exp_gate[l], w_exp_up[l],
                             w_exp_down[l], w_sh_gate[l], w_sh_up[l], w_sh_down[l])
    return x
```

```python
import functools
import math

import numpy as np
import jax
import jax.numpy as jnp
from jax import lax
from jax.experimental import pallas as pl
from jax.experimental.pallas import tpu as pltpu

N_HEADS = 8
QK_NOPE = 64
QK_ROPE = 32
V_DIM = 64
FOURIER_GROUP = 64
TOP_K = 8
ROUTED_SCALE = 2.5
EPS = 1e-6
ROPE_THETA = 10000.0
N_ADA = 6

LANES = 128
EXPERT_ROWS = 128
VMEM_LIMIT = 48 * 1024 * 1024

F32 = jnp.float32
BF16 = jnp.bfloat16


def _cparams(*sem):
    return pltpu.CompilerParams(dimension_semantics=sem, vmem_limit_bytes=VMEM_LIMIT)


def _tile(n, pref):
    t = min(n, pref)
    assert n % t == 0, (n, pref)
    return t


def _silu(v):
    return v * jax.nn.sigmoid(v)


def _ada_kernel(c_ref, w_ref, b_ref, o_ref):
    a = _silu(c_ref[...])
    o_ref[...] = jnp.dot(a, w_ref[...], preferred_element_type=F32,
                         precision=lax.Precision.HIGHEST) + b_ref[...]


def _ada(c, w_ada, b_ada):
    bsz, d = c.shape
    n = w_ada.shape[1]
    tn = _tile(n, d)
    return pl.pallas_call(
        _ada_kernel,
        grid=(n // tn,),
        in_specs=[pl.BlockSpec((bsz, d), lambda j: (0, 0)),
                  pl.BlockSpec((d, tn), lambda j: (0, j)),
                  pl.BlockSpec((1, tn), lambda j: (0, j))],
        out_specs=pl.BlockSpec((bsz, tn), lambda j: (0, j)),
        out_shape=jax.ShapeDtypeStruct((bsz, n), F32),
        compiler_params=_cparams("arbitrary"),
        name="ada",
    )(c, w_ada, b_ada.reshape(1, n))


def _head_norm_rope(t, g, cos, s1, s2):
    ms = jnp.sum(t * t, axis=-1, keepdims=True) * (1.0 / (QK_NOPE + QK_ROPE))
    t = t * lax.rsqrt(ms + EPS) * g
    half = QK_ROPE // 2
    return t * cos + pltpu.roll(t, LANES - half, 1) * s1 + pltpu.roll(t, half, 1) * s2


def _inproj_kernel(x_ref, mod_ref, g1_ref, wa_ref, wf_ref, wga_ref, wgf_ref,
                   gq_ref, gkv_ref, wuq_ref, wkv_ref, qg_ref, kg_ref,
                   cos_ref, s1_ref, s2_ref,
                   q_ref, k_ref, v_ref, zf_ref, sa_ref, sf_ref, *, ql, kvl):
    x = x_ref[...]
    mod = mod_ref[0]
    sh1, sc1 = mod[0:1], mod[1:2]
    r = lax.rsqrt(jnp.mean(x * x, axis=-1, keepdims=True) + EPS)
    h = (x * r * g1_ref[...]) * (1.0 + sc1) + sh1
    hb = h.astype(BF16)

    zf_ref[...] = jnp.dot(hb, wf_ref[...], preferred_element_type=F32).astype(BF16)
    sa_ref[...] = jax.nn.sigmoid(
        jnp.dot(hb, wga_ref[...], preferred_element_type=F32)).astype(BF16)
    sf_ref[...] = jax.nn.sigmoid(
        jnp.dot(hb, wgf_ref[...], preferred_element_type=F32)).astype(BF16)

    za = jnp.dot(hb, wa_ref[...], preferred_element_type=F32)
    zq = za[:, :ql]
    cq = zq * lax.rsqrt(jnp.mean(zq * zq, axis=-1, keepdims=True) + EPS) * gq_ref[...]
    qall = jnp.dot(cq.astype(BF16), wuq_ref[...], preferred_element_type=F32)

    zk = za[:, ql:]
    kvn = zk[:, :kvl]
    rk = lax.rsqrt(jnp.mean(kvn * kvn, axis=-1, keepdims=True) + EPS)
    lane = lax.broadcasted_iota(jnp.int32, zk.shape, 1)
    u = zk * jnp.where(lane < kvl, rk, 1.0) * gkv_ref[...]
    kvall = jnp.dot(u.astype(BF16), wkv_ref[...], preferred_element_type=F32)

    cos, s1, s2 = cos_ref[...], s1_ref[...], s2_ref[...]
    qg, kg = qg_ref[...], kg_ref[...]
    voff = N_HEADS * LANES
    for hd in range(N_HEADS):
        tq = qall[:, hd * LANES:(hd + 1) * LANES]
        q_ref[0, hd] = _head_norm_rope(tq, qg, cos, s1, s2).astype(BF16)
        tk = kvall[:, hd * LANES:(hd + 1) * LANES]
        k_ref[0, hd] = _head_norm_rope(tk, kg, cos, s1, s2).astype(BF16)
        v_ref[0, hd] = kvall[:, voff + hd * V_DIM: voff + (hd + 1) * V_DIM].astype(BF16)


def _rope_tables(seq):
    half = QK_ROPE // 2
    pos = np.arange(seq, dtype=np.float64)
    inv = ROPE_THETA ** (-np.arange(0, QK_ROPE, 2, dtype=np.float64) / QK_ROPE)
    ang = pos[:, None] * inv[None, :]
    c, s = np.cos(ang), np.sin(ang)
    cos = np.ones((seq, LANES)); s1 = np.zeros((seq, LANES)); s2 = np.zeros((seq, LANES))
    cos[:, QK_NOPE:QK_NOPE + half] = c
    cos[:, QK_NOPE + half:QK_NOPE + QK_ROPE] = c
    s1[:, QK_NOPE:QK_NOPE + half] = -s
    s2[:, QK_NOPE + half:QK_NOPE + QK_ROPE] = s
    return (jnp.asarray(cos, F32), jnp.asarray(s1, F32), jnp.asarray(s2, F32))


def _inproj(x2, mod, norm1_g, w_in, q_a_g, w_uq, kv_a_g, w_ukv, q_g, k_g, bsz, seq):
    t, d = x2.shape
    ql, kvl = q_a_g.shape[0], kv_a_g.shape[0]
    hq = QK_NOPE + QK_ROPE
    fw = w_in.shape[1] - ql - kvl - QK_ROPE - 2 * d
    o1, o2, o3, o4, o5 = ql, ql + kvl, ql + kvl + QK_ROPE, ql + kvl + QK_ROPE + fw, \
        ql + kvl + QK_ROPE + fw + d
    assert ql % LANES == 0 and kvl % LANES == 0

    wa = jnp.concatenate([w_in[:, :o3], jnp.zeros((d, LANES - QK_ROPE), F32)], axis=1).astype(BF16)
    wf = w_in[:, o3:o4].astype(BF16)
    wga = w_in[:, o4:o5].astype(BF16)
    wgf = w_in[:, o5:].astype(BF16)

    wuq = w_uq.reshape(ql, N_HEADS, hq)
    wuq = jnp.pad(wuq, ((0, 0), (0, 0), (0, LANES - hq))).reshape(ql, N_HEADS * LANES).astype(BF16)
    wukv = w_ukv.reshape(kvl, N_HEADS, QK_NOPE + V_DIM)
    wk = jnp.pad(wukv[:, :, :QK_NOPE], ((0, 0), (0, 0), (0, LANES - QK_NOPE)))
    place = jnp.zeros((QK_ROPE, N_HEADS, LANES), F32)
    place = place.at[jnp.arange(QK_ROPE), :, QK_NOPE + jnp.arange(QK_ROPE)].set(1.0)
    wk = jnp.concatenate([wk, place, jnp.zeros((LANES - QK_ROPE, N_HEADS, LANES), F32)], axis=0)
    wv = jnp.concatenate([wukv[:, :, QK_NOPE:], jnp.zeros((LANES, N_HEADS, V_DIM), F32)], axis=0)
    wkv = jnp.concatenate([wk.reshape(kvl + LANES, N_HEADS * LANES),
                           wv.reshape(kvl + LANES, N_HEADS * V_DIM)], axis=1).astype(BF16)

    gkv = jnp.concatenate([kv_a_g, jnp.ones((LANES,), F32)]).reshape(1, kvl + LANES)
    pad = jnp.zeros((LANES - hq,), F32)
    qg = jnp.concatenate([q_g * (hq ** -0.5), pad]).reshape(1, LANES)
    kg = jnp.concatenate([k_g, pad]).reshape(1, LANES)
    cos, s1, s2 = _rope_tables(seq)

    tm = _tile(seq, 512)
    tpb = seq // tm
    full = lambda shp: pl.BlockSpec(shp, lambda i: (0,) * len(shp))
    tok = lambda w: pl.BlockSpec((tm, w), lambda i: (i, 0))
    head = lambda w: pl.BlockSpec((1, N_HEADS, tm, w), lambda i: (i // tpb, 0, i % tpb, 0))
    rope = pl.BlockSpec((tm, LANES), lambda i: (i % tpb, 0))
    return pl.pallas_call(
        functools.partial(_inproj_kernel, ql=ql, kvl=kvl),
        grid=(t // tm,),
        in_specs=[tok(d),
                  pl.BlockSpec((1, N_ADA, d), lambda i: (i // tpb, 0, 0)),
                  full((1, d)), full(wa.shape), full(wf.shape), full(wga.shape), full(wgf.shape),
                  full((1, ql)), full(gkv.shape), full(wuq.shape), full(wkv.shape),
                  full((1, LANES)), full((1, LANES)), rope, rope, rope],
        out_specs=[head(LANES), head(LANES), head(V_DIM), tok(fw), tok(d), tok(d)],
        out_shape=[jax.ShapeDtypeStruct((bsz, N_HEADS, seq, LANES), BF16),
                   jax.ShapeDtypeStruct((bsz, N_HEADS, seq, LANES), BF16),
                   jax.ShapeDtypeStruct((bsz, N_HEADS, seq, V_DIM), BF16),
                   jax.ShapeDtypeStruct((t, fw), BF16),
                   jax.ShapeDtypeStruct((t, d), BF16),
                   jax.ShapeDtypeStruct((t, d), BF16)],
        compiler_params=_cparams("arbitrary"),
        name="inproj",
    )(x2, mod, norm1_g.reshape(1, d), wa, wf, wga, wgf, q_a_g.reshape(1, ql), gkv, wuq, wkv,
      qg, kg, cos, s1, s2)


def _attn_kernel(q_ref, k_ref, v_ref, o_ref):
    for hd in range(N_HEADS):
        s = lax.dot_general(q_ref[0, hd], k_ref[0, hd], (((1,), (1,)), ((), ())),
                            preferred_element_type=F32)
        m = jnp.max(s, axis=-1, keepdims=True)
        p = jnp.exp(s - m)
        l = jnp.sum(p, axis=-1, keepdims=True)
        o = jnp.dot(p.astype(BF16), v_ref[0, hd], preferred_element_type=F32)
        o_ref[0, :, hd * V_DIM:(hd + 1) * V_DIM] = (o / l).astype(BF16)


def _attention(q, k, v):
    bsz, _, seq, _ = q.shape
    tq = _tile(seq, 256)
    return pl.pallas_call(
        _attn_kernel,
        grid=(bsz, seq // tq),
        in_specs=[pl.BlockSpec((1, N_HEADS, tq, LANES), lambda b, j: (b, 0, j, 0)),
                  pl.BlockSpec((1, N_HEADS, seq, LANES), lambda b, j: (b, 0, 0, 0)),
                  pl.BlockSpec((1, N_HEADS, seq, V_DIM), lambda b, j: (b, 0, 0, 0))],
        out_specs=pl.BlockSpec((1, tq, N_HEADS * V_DIM), lambda b, j: (b, j, 0)),
        out_shape=jax.ShapeDtypeStruct((bsz, seq, N_HEADS * V_DIM), BF16),
        compiler_params=_cparams("arbitrary", "arbitrary"),
        name="attn",
    )(q, k, v)


def _fourier_kernel(z_ref, wc_ref, ws_ref, tab_ref, o_ref, u_ref, *, seq):
    @pl.when(pl.program_id(1) == 0)
    def _():
        z = z_ref[0]
        u_ref[:seq, :] = jnp.dot(z, wc_ref[...], preferred_element_type=F32).astype(BF16)
        u_ref[seq:, :] = jnp.dot(z, ws_ref[...], preferred_element_type=F32).astype(BF16)

    o_ref[0] = jnp.dot(tab_ref[...], u_ref[...], preferred_element_type=F32).astype(BF16)


def _fourier_tables(seq, fw):
    g = FOURIER_GROUP
    n = np.arange(seq, dtype=np.int64)
    ang = 2.0 * np.pi * ((n[:, None] * n[None, :]) % seq).astype(np.float64) / seq
    tab = np.concatenate([np.cos(ang), -np.sin(ang)], axis=1)
    c = np.arange(g, dtype=np.int64)
    angc = 2.0 * np.pi * ((c[:, None] * c[None, :]) % g).astype(np.float64) / g
    scale = 1.0 / math.sqrt(seq * g)
    eye = np.eye(fw // g)
    wc = np.kron(eye, np.cos(angc) * scale)
    ws = np.kron(eye, np.sin(angc) * scale)
    return (jnp.asarray(tab, F32).astype(BF16), jnp.asarray(wc, F32).astype(BF16),
            jnp.asarray(ws, F32).astype(BF16))


def _fourier(zf):
    bsz, seq, fw = zf.shape
    tab, wc, ws = _fourier_tables(seq, fw)
    tr = _tile(seq, 256)
    return pl.pallas_call(
        functools.partial(_fourier_kernel, seq=seq),
        grid=(bsz, seq // tr),
        in_specs=[pl.BlockSpec((1, seq, fw), lambda b, j: (b, 0, 0)),
                  pl.BlockSpec((fw, fw), lambda b, j: (0, 0)),
                  pl.BlockSpec((fw, fw), lambda b, j: (0, 0)),
                  pl.BlockSpec((tr, 2 * seq), lambda b, j: (j, 0))],
        out_specs=pl.BlockSpec((1, tr, fw), lambda b, j: (b, j, 0)),
        out_shape=jax.ShapeDtypeStruct((bsz, seq, fw), BF16),
        scratch_shapes=[pltpu.VMEM((2 * seq, fw), BF16)],
        compiler_params=_cparams("arbitrary", "arbitrary"),
        name="fourier",
    )(zf, wc, ws, tab)


def _merge_kernel(a_ref, f_ref, sa_ref, sf_ref, x_ref, mod_ref, wpa_ref, wpf_ref, wo_ref,
                  g2_ref, wrh_ref, wrl_ref, x1_ref, h2_ref, sc_ref):
    ya = jnp.dot(a_ref[...], wpa_ref[...], preferred_element_type=F32)
    yf = jnp.dot(f_ref[...], wpf_ref[...], preferred_element_type=F32)
    merged = sa_ref[...].astype(F32) * ya + sf_ref[...].astype(F32) * yf
    mod = mod_ref[0]
    g1, sh2, sc2 = mod[2:3], mod[3:4], mod[4:5]
    x1 = x_ref[...] + g1 * jnp.dot(merged.astype(BF16), wo_ref[...], preferred_element_type=F32)
    x1_ref[...] = x1
    r = lax.rsqrt(jnp.mean(x1 * x1, axis=-1, keepdims=True) + EPS)
    h2 = (x1 * r * g2_ref[...]) * (1.0 + sc2) + sh2
    h2_ref[...] = h2
    hh = h2.astype(BF16)
    hl = (h2 - hh.astype(F32)).astype(BF16)
    nt = (((1,), (1,)), ((), ()))
    lt = (lax.dot_general(wrh_ref[...], hh, nt, preferred_element_type=F32)
          + lax.dot_general(wrh_ref[...], hl, nt, preferred_element_type=F32)
          + lax.dot_general(wrl_ref[...], hh, nt, preferred_element_type=F32))
    sc_ref[...] = jax.nn.sigmoid(lt)


def _merge(attn, four, sa, sf, x2, mod, w_pa, w_pf, w_out, norm2_g, w_router, seq):
    t, d = x2.shape
    e = w_router.shape[1]
    wrt = w_router.T
    wrh = wrt.astype(BF16)
    wrl = (wrt - wrh.astype(F32)).astype(BF16)
    tm = _tile(seq, 512)
    tpb = seq // tm
    full = lambda shp: pl.BlockSpec(shp, lambda i: (0,) * len(shp))
    tok = lambda w: pl.BlockSpec((tm, w), lambda i: (i, 0))
    return pl.pallas_call(
        _merge_kernel,
        grid=(t // tm,),
        in_specs=[tok(attn.shape[1]), tok(four.shape[1]), tok(d), tok(d), tok(d),
                  pl.BlockSpec((1, N_ADA, d), lambda i: (i // tpb, 0, 0)),
                  full(w_pa.shape), full(w_pf.shape), full(w_out.shape), full((1, d)),
                  full((e, d)), full((e, d))],
        out_specs=[tok(d), tok(d), pl.BlockSpec((e, tm), lambda i: (0, i))],
        out_shape=[jax.ShapeDtypeStruct((t, d), F32), jax.ShapeDtypeStruct((t, d), F32),
                   jax.ShapeDtypeStruct((e, t), F32)],
        compiler_params=_cparams("arbitrary"),
        name="merge",
    )(attn, four, sa, sf, x2, mod, w_pa.astype(BF16), w_pf.astype(BF16), w_out.astype(BF16),
      norm2_g.reshape(1, d), wrh, wrl)


def _route_kernel(s_ref, b_ref, tri_ref, idx_ref, w_ref, rank_ref, cnt_ref, carry_ref):
    @pl.when(pl.program_id(0) == 0)
    def _():
        carry_ref[...] = jnp.zeros_like(carry_ref)

    sc = s_ref[...]
    e, tr = sc.shape
    row = lax.broadcasted_iota(jnp.int32, (e, tr), 0)
    v = sc + b_ref[...]
    sel = jnp.zeros((e, tr), F32)
    idxs, ws = [], []
    for _ in range(TOP_K):
        m = jnp.max(v, axis=0, keepdims=True)
        idx = jnp.min(jnp.where(v == m, row, e), axis=0, keepdims=True)
        oh = row == idx
        ws.append(jnp.sum(jnp.where(oh, sc, 0.0), axis=0, keepdims=True))
        idxs.append(idx)
        v = jnp.where(oh, -jnp.inf, v)
        sel = sel + oh.astype(F32)
    wsum = ws[0]
    for w in ws[1:]:
        wsum = wsum + w
    selb = sel.astype(BF16)
    cum = jnp.dot(selb, tri_ref[...], preferred_element_type=F32) + carry_ref[...]
    for kk in range(TOP_K):
        oh = row == idxs[kk]
        rk = jnp.sum(jnp.where(oh, cum, 0.0), axis=0, keepdims=True)
        idx_ref[kk:kk + 1, :] = idxs[kk]
        rank_ref[kk:kk + 1, :] = rk.astype(jnp.int32)
        w_ref[kk:kk + 1, :] = ws[kk] / wsum * ROUTED_SCALE
    tot = carry_ref[...] + jnp.dot(selb, jnp.ones((tr, tr), BF16), preferred_element_type=F32)
    carry_ref[...] = tot
    cnt_ref[...] = tot


def _route(scores_t, router_bias):
    e, t = scores_t.shape
    tr = _tile(t, 256)
    tri = jnp.asarray(np.triu(np.ones((tr, tr), np.float32), 1), BF16)
    bias = jnp.broadcast_to(router_bias.reshape(e, 1), (e, tr)).astype(F32)
    blk = pl.BlockSpec((TOP_K, tr), lambda i: (0, i))
    return pl.pallas_call(
        _route_kernel,
        grid=(t // tr,),
        in_specs=[pl.BlockSpec((e, tr), lambda i: (0, i)),
                  pl.BlockSpec((e, tr), lambda i: (0, 0)),
                  pl.BlockSpec((tr, tr), lambda i: (0, 0))],
        out_specs=[blk, blk, blk, pl.BlockSpec((e, tr), lambda i: (0, 0))],
        out_shape=[jax.ShapeDtypeStruct((TOP_K, t), jnp.int32),
                   jax.ShapeDtypeStruct((TOP_K, t), F32),
                   jax.ShapeDtypeStruct((TOP_K, t), jnp.int32),
                   jax.ShapeDtypeStruct((e, tr), F32)],
        scratch_shapes=[pltpu.VMEM((e, tr), F32)],
        compiler_params=_cparams("arbitrary"),
        name="route",
    )(scores_t, bias, tri)


def _row_copy(src, dst, sem):
    return pltpu.make_async_copy(src, dst, sem)


def _dispatch_kernel(slot_ref, h_hbm, xs_hbm, sem, *, td, nsteps):
    i = pl.program_id(0)
    cur = i % 2
    base = i * td

    def issue(r, c):
        for kk in range(TOP_K):
            s = slot_ref[kk, r]
            _row_copy(h_hbm.at[pl.ds(base + r, 1)], xs_hbm.at[pl.ds(s, 1)], sem.at[cur]).start()
        return c

    lax.fori_loop(0, td, issue, 0)

    def drain(which):
        def body(r, c):
            for _ in range(TOP_K):
                _row_copy(h_hbm.at[pl.ds(0, 1)], xs_hbm.at[pl.ds(0, 1)], sem.at[which]).wait()
            return c
        lax.fori_loop(0, td, body, 0)

    @pl.when(i > 0)
    def _():
        drain(1 - cur)

    @pl.when(i == nsteps - 1)
    def _():
        drain(cur)


def _dispatch(slot_t, h2, n_slots):
    t, d = h2.shape
    td = _tile(t, 256)
    nsteps = t // td
    return pl.pallas_call(
        functools.partial(_dispatch_kernel, td=td, nsteps=nsteps),
        grid=(nsteps,),
        in_specs=[pl.BlockSpec((TOP_K, td), lambda i: (0, i), memory_space=pltpu.SMEM),
                  pl.BlockSpec(memory_space=pl.ANY)],
        out_specs=pl.BlockSpec(memory_space=pl.ANY),
        out_shape=jax.ShapeDtypeStruct((n_slots, d), F32),
        scratch_shapes=[pltpu.SemaphoreType.DMA((2,))],
        compiler_params=_cparams("arbitrary"),
        name="dispatch",
    )(slot_t, h2)


def _expert_kernel(be_ref, nu_ref, xs_ref, wg_ref, wu_ref, wd_ref, ys_ref, wgb, wub, wdb):
    i = pl.program_id(0)

    @pl.when(i < nu_ref[0])
    def _():
        prev = be_ref[jnp.maximum(i - 1, 0)]

        @pl.when(jnp.logical_or(i == 0, be_ref[i] != prev))
        def _():
            wgb[...] = wg_ref[0].astype(BF16)
            wub[...] = wu_ref[0].astype(BF16)
            wdb[...] = wd_ref[0].astype(BF16)

        x = xs_ref[...].astype(BF16)
        g = jnp.dot(x, wgb[...], preferred_element_type=F32)
        u = jnp.dot(x, wub[...], preferred_element_type=F32)
        a = (_silu(g) * u).astype(BF16)
        ys_ref[...] = jnp.dot(a, wdb[...], preferred_element_type=F32)


def _experts(blk_expert, nblk_used, xs, w_g, w_u, w_d):
    n_slots, d = xs.shape
    f = w_g.shape[2]
    nblk = n_slots // EXPERT_ROWS
    rows = lambda i, be, nu: (jnp.minimum(i, nu[0] - 1), 0)
    wsel = lambda i, be, nu: (be[i], 0, 0)
    return pl.pallas_call(
        _expert_kernel,
        grid_spec=pltpu.PrefetchScalarGridSpec(
            num_scalar_prefetch=2,
            grid=(nblk,),
            in_specs=[pl.BlockSpec((EXPERT_ROWS, d), rows),
                      pl.BlockSpec((1, d, f), wsel),
                      pl.BlockSpec((1, d, f), wsel),
                      pl.BlockSpec((1, f, d), wsel)],
            out_specs=pl.BlockSpec((EXPERT_ROWS, d), rows),
            scratch_shapes=[pltpu.VMEM((d, f), BF16), pltpu.VMEM((d, f), BF16),
                            pltpu.VMEM((f, d), BF16)]),
        out_shape=jax.ShapeDtypeStruct((n_slots, d), F32),
        compiler_params=_cparams("arbitrary"),
        name="experts",
    )(blk_expert, nblk_used, xs, w_g, w_u, w_d)


def _combine_kernel(slot_ref, nslot_ref, w_ref, x1_ref, h2_ref, mod_ref, wsg_ref, wsu_ref,
                    wsd_ref, ys_hbm, o_ref, buf, sem, *, tc, nsteps):
    i = pl.program_id(0)
    cur = i % 2

    def issue(s_ref, which):
        def body(r, c):
            for kk in range(TOP_K):
                s = s_ref[kk, r]
                _row_copy(ys_hbm.at[pl.ds(s, 1)], buf.at[which, kk, pl.ds(r, 1)],
                          sem.at[which]).start()
            return c
        lax.fori_loop(0, tc, body, 0)

    @pl.when(i == 0)
    def _():
        issue(slot_ref, 0)

    @pl.when(i + 1 < nsteps)
    def _():
        issue(nslot_ref, 1 - cur)

    hb = h2_ref[...].astype(BF16)
    g = jnp.dot(hb, wsg_ref[...], preferred_element_type=F32)
    u = jnp.dot(hb, wsu_ref[...], preferred_element_type=F32)
    acc = jnp.dot((_silu(g) * u).astype(BF16), wsd_ref[...], preferred_element_type=F32)

    def drain(r, c):
        for kk in range(TOP_K):
            _row_copy(ys_hbm.at[pl.ds(0, 1)], buf.at[cur, kk, pl.ds(0, 1)], sem.at[cur]).wait()
        return c
    lax.fori_loop(0, tc, drain, 0)

    w = w_ref[...]
    for kk in range(TOP_K):
        acc = acc + w[:, kk:kk + 1] * buf[cur, kk]
    g2 = mod_ref[0][5:6]
    o_ref[...] = x1_ref[...] + g2 * acc


def _combine(slot_t, w_tk, x1, h2, mod, w_sg, w_su, w_sd, ys, seq):
    t, d = x1.shape
    tc = _tile(seq, 128)
    tpb = seq // tc
    nsteps = t // tc
    full = lambda shp: pl.BlockSpec(shp, lambda i: (0,) * len(shp))
    tok = lambda w: pl.BlockSpec((tc, w), lambda i: (i, 0))
    return pl.pallas_call(
        functools.partial(_combine_kernel, tc=tc, nsteps=nsteps),
        grid=(nsteps,),
        in_specs=[pl.BlockSpec((TOP_K, tc), lambda i: (0, i), memory_space=pltpu.SMEM),
                  pl.BlockSpec((TOP_K, tc), lambda i: (0, jnp.minimum(i + 1, nsteps - 1)),
                               memory_space=pltpu.SMEM),
                  tok(TOP_K), tok(d), tok(d),
                  pl.BlockSpec((1, N_ADA, d), lambda i: (i // tpb, 0, 0)),
                  full(w_sg.shape), full(w_su.shape), full(w_sd.shape),
                  pl.BlockSpec(memory_space=pl.ANY)],
        out_specs=tok(d),
        out_shape=jax.ShapeDtypeStruct((t, d), F32),
        scratch_shapes=[pltpu.VMEM((2, TOP_K, tc, d), F32), pltpu.SemaphoreType.DMA((2,))],
        compiler_params=_cparams("arbitrary"),
        name="combine",
    )(slot_t, slot_t, w_tk, x1, h2, mod, w_sg.astype(BF16), w_su.astype(BF16),
      w_sd.astype(BF16), ys)


def _layer(x, c, w_ada, b_ada, norm1_g, w_in, q_a_norm_g, w_uq, kv_a_norm_g, w_ukv,
           q_norm_g, k_norm_g, w_proj_attn, w_proj_fourier, w_out, norm2_g,
           w_router, router_bias, w_exp_gate, w_exp_up, w_exp_down,
           w_sh_gate, w_sh_up, w_sh_down):
    bsz, seq, d = x.shape
    t = bsz * seq
    e = w_router.shape[1]
    x2 = x.reshape(t, d)

    mod = _ada(c, w_ada, b_ada).reshape(bsz, N_ADA, d)
    q, k, v, zf, sa, sf = _inproj(x2, mod, norm1_g, w_in, q_a_norm_g, w_uq, kv_a_norm_g,
                                  w_ukv, q_norm_g, k_norm_g, bsz, seq)
    attn = _attention(q, k, v).reshape(t, N_HEADS * V_DIM)
    four = _fourier(zf.reshape(bsz, seq, zf.shape[1])).reshape(t, zf.shape[1])
    x1, h2, scores_t = _merge(attn, four, sa, sf, x2, mod, w_proj_attn, w_proj_fourier,
                              w_out, norm2_g, w_router, seq)

    idx_t, w_t, rank_t, cnt = _route(scores_t, router_bias)
    counts = cnt[:, 0].astype(jnp.int32)
    rows = EXPERT_ROWS
    nblk = -(-(t * TOP_K) // rows) + e
    padded = ((counts + rows - 1) // rows) * rows
    p_end = jnp.cumsum(padded)
    p_start = p_end - padded
    slot_t = rank_t + jnp.take(p_start, idx_t)
    nblk_used = (p_end[-1] // rows).astype(jnp.int32)
    blk_start = jnp.arange(nblk, dtype=jnp.int32) * rows
    blk_first = jnp.minimum(blk_start, p_end[-1] - 1)
    blk_expert = jnp.sum((p_end[None, :] <= blk_first[:, None]).astype(jnp.int32), axis=1)
    blk_expert = jnp.clip(blk_expert, 0, e - 1)

    xs = _dispatch(slot_t, h2, nblk * rows)
    ys = _experts(blk_expert, nblk_used.reshape(1), xs, w_exp_gate, w_exp_up, w_exp_down)
    out = _combine(slot_t, w_t.T, x1, h2, mod, w_sh_gate, w_sh_up, w_sh_down, ys, seq)
    return out.reshape(bsz, seq, d)


def kernel(x, c, w_ada, b_ada, norm1_g, w_in, q_a_norm_g, w_uq, kv_a_norm_g, w_ukv, q_norm_g,
           k_norm_g, w_proj_attn, w_proj_fourier, w_out, norm2_g, w_router, router_bias,
           w_exp_gate, w_exp_up, w_exp_down, w_sh_gate, w_sh_up, w_sh_down):
    for l in range(w_ada.shape[0]):
        x = _layer(x, c, w_ada[l], b_ada[l], norm1_g[l], w_in[l], q_a_norm_g[l], w_uq[l],
                   kv_a_norm_g[l], w_ukv[l], q_norm_g[l], k_norm_g[l], w_proj_attn[l],
                   w_proj_fourier[l], w_out[l], norm2_g[l], w_router[l], router_bias[l],
                   w_exp_gate[l], w_exp_up[l], w_exp_down[l], w_sh_gate[l], w_sh_up[l],
                   w_sh_down[l])
    return x
```

```python
import functools
import math

import numpy as np
import jax
import jax.numpy as jnp
from jax import lax
from jax.experimental import pallas as pl
from jax.experimental.pallas import tpu as pltpu

N_HEADS = 8
QK_NOPE = 64
QK_ROPE = 32
V_DIM = 64
FOURIER_GROUP = 64
TOP_K = 8
ROUTED_SCALE = 2.5
EPS = 1e-6
ROPE_THETA = 10000.0
N_ADA = 6

LANES = 128
EXPERT_ROWS = 128
VMEM_LIMIT = 48 * 1024 * 1024

F32 = jnp.float32
BF16 = jnp.bfloat16


def _cparams(*sem):
    return pltpu.CompilerParams(dimension_semantics=sem, vmem_limit_bytes=VMEM_LIMIT)


def _tile(n, pref):
    t = min(n, pref)
    assert n % t == 0, (n, pref)
    return t


def _silu(v):
    return v * jax.nn.sigmoid(v)


def _store_row_tiles(ref, val):
    rows, d = val.shape
    nch = d // LANES
    for j in range(nch):
        ref[pl.ds(j, rows, stride=nch), :] = val[:, j * LANES:(j + 1) * LANES]


def _load_row_tiles(ref, rows, nch):
    return jnp.concatenate([ref[pl.ds(j, rows, stride=nch), :] for j in range(nch)], axis=1)


def _ada_kernel(c_ref, w_ref, b_ref, o_ref):
    a = _silu(c_ref[...])
    o_ref[...] = jnp.dot(a, w_ref[...], preferred_element_type=F32,
                         precision=lax.Precision.HIGHEST) + b_ref[...]


def _ada(c, w_ada, b_ada):
    bsz, d = c.shape
    n = w_ada.shape[1]
    tn = _tile(n, d)
    return pl.pallas_call(
        _ada_kernel,
        grid=(n // tn,),
        in_specs=[pl.BlockSpec((bsz, d), lambda j: (0, 0)),
                  pl.BlockSpec((d, tn), lambda j: (0, j)),
                  pl.BlockSpec((1, tn), lambda j: (0, j))],
        out_specs=pl.BlockSpec((bsz, tn), lambda j: (0, j)),
        out_shape=jax.ShapeDtypeStruct((bsz, n), F32),
        compiler_params=_cparams("arbitrary"),
        name="ada",
    )(c, w_ada, b_ada.reshape(1, n))


def _head_norm_rope(t, g, cos, s1, s2):
    ms = jnp.sum(t * t, axis=-1, keepdims=True) * (1.0 / (QK_NOPE + QK_ROPE))
    t = t * lax.rsqrt(ms + EPS) * g
    half = QK_ROPE // 2
    return t * cos + pltpu.roll(t, LANES - half, 1) * s1 + pltpu.roll(t, half, 1) * s2


def _inproj_kernel(x_ref, mod_ref, g1_ref, wa_ref, wf_ref, wga_ref, wgf_ref,
                   gq_ref, gkv_ref, wuq_ref, wkv_ref, qg_ref, kg_ref,
                   cos_ref, s1_ref, s2_ref,
                   q_ref, k_ref, v_ref, zf_ref, sa_ref, sf_ref, *, ql, kvl):
    x = x_ref[...]
    mod = mod_ref[0]
    sh1, sc1 = mod[0:1], mod[1:2]
    r = lax.rsqrt(jnp.mean(x * x, axis=-1, keepdims=True) + EPS)
    h = (x * r * g1_ref[...]) * (1.0 + sc1) + sh1
    hb = h.astype(BF16)

    zf_ref[...] = jnp.dot(hb, wf_ref[...], preferred_element_type=F32).astype(BF16)
    sa_ref[...] = jax.nn.sigmoid(
        jnp.dot(hb, wga_ref[...], preferred_element_type=F32)).astype(BF16)
    sf_ref[...] = jax.nn.sigmoid(
        jnp.dot(hb, wgf_ref[...], preferred_element_type=F32)).astype(BF16)

    za = jnp.dot(hb, wa_ref[...], preferred_element_type=F32)
    zq = za[:, :ql]
    cq = zq * lax.rsqrt(jnp.mean(zq * zq, axis=-1, keepdims=True) + EPS) * gq_ref[...]
    qall = jnp.dot(cq.astype(BF16), wuq_ref[...], preferred_element_type=F32)

    zk = za[:, ql:]
    kvn = zk[:, :kvl]
    rk = lax.rsqrt(jnp.mean(kvn * kvn, axis=-1, keepdims=True) + EPS)
    lane = lax.broadcasted_iota(jnp.int32, zk.shape, 1)
    u = zk * jnp.where(lane < kvl, rk, 1.0) * gkv_ref[...]
    kvall = jnp.dot(u.astype(BF16), wkv_ref[...], preferred_element_type=F32)

    cos, s1, s2 = cos_ref[...], s1_ref[...], s2_ref[...]
    qg, kg = qg_ref[...], kg_ref[...]
    voff = N_HEADS * LANES
    for hd in range(N_HEADS):
        tq = qall[:, hd * LANES:(hd + 1) * LANES]
        q_ref[0, hd] = _head_norm_rope(tq, qg, cos, s1, s2).astype(BF16)
        tk = kvall[:, hd * LANES:(hd + 1) * LANES]
        k_ref[0, hd] = _head_norm_rope(tk, kg, cos, s1, s2).astype(BF16)
        v_ref[0, hd] = kvall[:, voff + hd * V_DIM: voff + (hd + 1) * V_DIM].astype(BF16)


def _rope_tables(seq):
    half = QK_ROPE // 2
    pos = np.arange(seq, dtype=np.float64)
    inv = ROPE_THETA ** (-np.arange(0, QK_ROPE, 2, dtype=np.float64) / QK_ROPE)
    ang = pos[:, None] * inv[None, :]
    c, s = np.cos(ang), np.sin(ang)
    cos = np.ones((seq, LANES)); s1 = np.zeros((seq, LANES)); s2 = np.zeros((seq, LANES))
    cos[:, QK_NOPE:QK_NOPE + half] = c
    cos[:, QK_NOPE + half:QK_NOPE + QK_ROPE] = c
    s1[:, QK_NOPE:QK_NOPE + half] = -s
    s2[:, QK_NOPE + half:QK_NOPE + QK_ROPE] = s
    return (jnp.asarray(cos, F32), jnp.asarray(s1, F32), jnp.asarray(s2, F32))


def _inproj(x2, mod, norm1_g, w_in, q_a_g, w_uq, kv_a_g, w_ukv, q_g, k_g, bsz, seq):
    t, d = x2.shape
    ql, kvl = q_a_g.shape[0], kv_a_g.shape[0]
    hq = QK_NOPE + QK_ROPE
    fw = w_in.shape[1] - ql - kvl - QK_ROPE - 2 * d
    o1, o2, o3, o4, o5 = ql, ql + kvl, ql + kvl + QK_ROPE, ql + kvl + QK_ROPE + fw, \
        ql + kvl + QK_ROPE + fw + d
    assert ql % LANES == 0 and kvl % LANES == 0

    wa = jnp.concatenate([w_in[:, :o3], jnp.zeros((d, LANES - QK_ROPE), F32)], axis=1).astype(BF16)
    wf = w_in[:, o3:o4].astype(BF16)
    wga = w_in[:, o4:o5].astype(BF16)
    wgf = w_in[:, o5:].astype(BF16)

    wuq = w_uq.reshape(ql, N_HEADS, hq)
    wuq = jnp.pad(wuq, ((0, 0), (0, 0), (0, LANES - hq))).reshape(ql, N_HEADS * LANES).astype(BF16)
    wukv = w_ukv.reshape(kvl, N_HEADS, QK_NOPE + V_DIM)
    wk = jnp.pad(wukv[:, :, :QK_NOPE], ((0, 0), (0, 0), (0, LANES - QK_NOPE)))
    place = jnp.zeros((QK_ROPE, N_HEADS, LANES), F32)
    place = place.at[jnp.arange(QK_ROPE), :, QK_NOPE + jnp.arange(QK_ROPE)].set(1.0)
    wk = jnp.concatenate([wk, place, jnp.zeros((LANES - QK_ROPE, N_HEADS, LANES), F32)], axis=0)
    wv = jnp.concatenate([wukv[:, :, QK_NOPE:], jnp.zeros((LANES, N_HEADS, V_DIM), F32)], axis=0)
    wkv = jnp.concatenate([wk.reshape(kvl + LANES, N_HEADS * LANES),
                           wv.reshape(kvl + LANES, N_HEADS * V_DIM)], axis=1).astype(BF16)

    gkv = jnp.concatenate([kv_a_g, jnp.ones((LANES,), F32)]).reshape(1, kvl + LANES)
    pad = jnp.zeros((LANES - hq,), F32)
    qg = jnp.concatenate([q_g * (hq ** -0.5), pad]).reshape(1, LANES)
    kg = jnp.concatenate([k_g, pad]).reshape(1, LANES)
    cos, s1, s2 = _rope_tables(seq)

    tm = _tile(seq, 512)
    tpb = seq // tm
    full = lambda shp: pl.BlockSpec(shp, lambda i: (0,) * len(shp))
    tok = lambda w: pl.BlockSpec((tm, w), lambda i: (i, 0))
    head = lambda w: pl.BlockSpec((1, N_HEADS, tm, w), lambda i: (i // tpb, 0, i % tpb, 0))
    rope = pl.BlockSpec((tm, LANES), lambda i: (i % tpb, 0))
    return pl.pallas_call(
        functools.partial(_inproj_kernel, ql=ql, kvl=kvl),
        grid=(t // tm,),
        in_specs=[tok(d),
                  pl.BlockSpec((1, N_ADA, d), lambda i: (i // tpb, 0, 0)),
                  full((1, d)), full(wa.shape), full(wf.shape), full(wga.shape), full(wgf.shape),
                  full((1, ql)), full(gkv.shape), full(wuq.shape), full(wkv.shape),
                  full((1, LANES)), full((1, LANES)), rope, rope, rope],
        out_specs=[head(LANES), head(LANES), head(V_DIM), tok(fw), tok(d), tok(d)],
        out_shape=[jax.ShapeDtypeStruct((bsz, N_HEADS, seq, LANES), BF16),
                   jax.ShapeDtypeStruct((bsz, N_HEADS, seq, LANES), BF16),
                   jax.ShapeDtypeStruct((bsz, N_HEADS, seq, V_DIM), BF16),
                   jax.ShapeDtypeStruct((t, fw), BF16),
                   jax.ShapeDtypeStruct((t, d), BF16),
                   jax.ShapeDtypeStruct((t, d), BF16)],
        compiler_params=_cparams("arbitrary"),
        name="inproj",
    )(x2, mod, norm1_g.reshape(1, d), wa, wf, wga, wgf, q_a_g.reshape(1, ql), gkv, wuq, wkv,
      qg, kg, cos, s1, s2)


def _attn_kernel(q_ref, k_ref, v_ref, o_ref):
    for hd in range(N_HEADS):
        s = lax.dot_general(q_ref[0, hd], k_ref[0, hd], (((1,), (1,)), ((), ())),
                            preferred_element_type=F32)
        m = jnp.max(s, axis=-1, keepdims=True)
        p = jnp.exp(s - m)
        l = jnp.sum(p, axis=-1, keepdims=True)
        o = jnp.dot(p.astype(BF16), v_ref[0, hd], preferred_element_type=F32)
        o_ref[0, :, hd * V_DIM:(hd + 1) * V_DIM] = (o / l).astype(BF16)


def _attention(q, k, v):
    bsz, _, seq, _ = q.shape
    tq = _tile(seq, 256)
    return pl.pallas_call(
        _attn_kernel,
        grid=(bsz, seq // tq),
        in_specs=[pl.BlockSpec((1, N_HEADS, tq, LANES), lambda b, j: (b, 0, j, 0)),
                  pl.BlockSpec((1, N_HEADS, seq, LANES), lambda b, j: (b, 0, 0, 0)),
                  pl.BlockSpec((1, N_HEADS, seq, V_DIM), lambda b, j: (b, 0, 0, 0))],
        out_specs=pl.BlockSpec((1, tq, N_HEADS * V_DIM), lambda b, j: (b, j, 0)),
        out_shape=jax.ShapeDtypeStruct((bsz, seq, N_HEADS * V_DIM), BF16),
        compiler_params=_cparams("arbitrary", "arbitrary"),
        name="attn",
    )(q, k, v)


def _fourier_kernel(z_ref, wc_ref, ws_ref, tab_ref, o_ref, u_ref, *, seq):
    @pl.when(pl.program_id(1) == 0)
    def _():
        z = z_ref[0]
        u_ref[:seq, :] = jnp.dot(z, wc_ref[...], preferred_element_type=F32).astype(BF16)
        u_ref[seq:, :] = jnp.dot(z, ws_ref[...], preferred_element_type=F32).astype(BF16)

    o_ref[0] = jnp.dot(tab_ref[...], u_ref[...], preferred_element_type=F32).astype(BF16)


def _fourier_tables(seq, fw):
    g = FOURIER_GROUP
    n = np.arange(seq, dtype=np.int64)
    ang = 2.0 * np.pi * ((n[:, None] * n[None, :]) % seq).astype(np.float64) / seq
    tab = np.concatenate([np.cos(ang), -np.sin(ang)], axis=1)
    c = np.arange(g, dtype=np.int64)
    angc = 2.0 * np.pi * ((c[:, None] * c[None, :]) % g).astype(np.float64) / g
    scale = 1.0 / math.sqrt(seq * g)
    eye = np.eye(fw // g)
    wc = np.kron(eye, np.cos(angc) * scale)
    ws = np.kron(eye, np.sin(angc) * scale)
    return (jnp.asarray(tab, F32).astype(BF16), jnp.asarray(wc, F32).astype(BF16),
            jnp.asarray(ws, F32).astype(BF16))


def _fourier(zf):
    bsz, seq, fw = zf.shape
    tab, wc, ws = _fourier_tables(seq, fw)
    tr = _tile(seq, 256)
    return pl.pallas_call(
        functools.partial(_fourier_kernel, seq=seq),
        grid=(bsz, seq // tr),
        in_specs=[pl.BlockSpec((1, seq, fw), lambda b, j: (b, 0, 0)),
                  pl.BlockSpec((fw, fw), lambda b, j: (0, 0)),
                  pl.BlockSpec((fw, fw), lambda b, j: (0, 0)),
                  pl.BlockSpec((tr, 2 * seq), lambda b, j: (j, 0))],
        out_specs=pl.BlockSpec((1, tr, fw), lambda b, j: (b, j, 0)),
        out_shape=jax.ShapeDtypeStruct((bsz, seq, fw), BF16),
        scratch_shapes=[pltpu.VMEM((2 * seq, fw), BF16)],
        compiler_params=_cparams("arbitrary", "arbitrary"),
        name="fourier",
    )(zf, wc, ws, tab)


def _merge_kernel(a_ref, f_ref, sa_ref, sf_ref, x_ref, mod_ref, wpa_ref, wpf_ref, wo_ref,
                  g2_ref, wrh_ref, wrl_ref, x1_ref, h2_ref, sc_ref):
    ya = jnp.dot(a_ref[...], wpa_ref[...], preferred_element_type=F32)
    yf = jnp.dot(f_ref[...], wpf_ref[...], preferred_element_type=F32)
    merged = sa_ref[...].astype(F32) * ya + sf_ref[...].astype(F32) * yf
    mod = mod_ref[0]
    g1, sh2, sc2 = mod[2:3], mod[3:4], mod[4:5]
    x1 = x_ref[...] + g1 * jnp.dot(merged.astype(BF16), wo_ref[...], preferred_element_type=F32)
    x1_ref[...] = x1
    r = lax.rsqrt(jnp.mean(x1 * x1, axis=-1, keepdims=True) + EPS)
    h2 = (x1 * r * g2_ref[...]) * (1.0 + sc2) + sh2
    _store_row_tiles(h2_ref, h2)
    hh = h2.astype(BF16)
    hl = (h2 - hh.astype(F32)).astype(BF16)
    nt = (((1,), (1,)), ((), ()))
    lt = (lax.dot_general(wrh_ref[...], hh, nt, preferred_element_type=F32)
          + lax.dot_general(wrh_ref[...], hl, nt, preferred_element_type=F32)
          + lax.dot_general(wrl_ref[...], hh, nt, preferred_element_type=F32))
    sc_ref[...] = jax.nn.sigmoid(lt)


def _merge(attn, four, sa, sf, x2, mod, w_pa, w_pf, w_out, norm2_g, w_router, seq):
    t, d = x2.shape
    e = w_router.shape[1]
    wrt = w_router.T
    wrh = wrt.astype(BF16)
    wrl = (wrt - wrh.astype(F32)).astype(BF16)
    tm = _tile(seq, 512)
    tpb = seq // tm
    full = lambda shp: pl.BlockSpec(shp, lambda i: (0,) * len(shp))
    tok = lambda w: pl.BlockSpec((tm, w), lambda i: (i, 0))
    return pl.pallas_call(
        _merge_kernel,
        grid=(t // tm,),
        in_specs=[tok(attn.shape[1]), tok(four.shape[1]), tok(d), tok(d), tok(d),
                  pl.BlockSpec((1, N_ADA, d), lambda i: (i // tpb, 0, 0)),
                  full(w_pa.shape), full(w_pf.shape), full(w_out.shape), full((1, d)),
                  full((e, d)), full((e, d))],
        out_specs=[tok(d), pl.BlockSpec((tm * (d // LANES), LANES), lambda i: (i, 0)),
                   pl.BlockSpec((e, tm), lambda i: (0, i))],
        out_shape=[jax.ShapeDtypeStruct((t, d), F32),
                   jax.ShapeDtypeStruct((t * (d // LANES), LANES), F32),
                   jax.ShapeDtypeStruct((e, t), F32)],
        compiler_params=_cparams("arbitrary"),
        name="merge",
    )(attn, four, sa, sf, x2, mod, w_pa.astype(BF16), w_pf.astype(BF16), w_out.astype(BF16),
      norm2_g.reshape(1, d), wrh, wrl)


def _route_kernel(s_ref, b_ref, tri_ref, idx_ref, w_ref, rank_ref, cnt_ref, carry_ref):
    @pl.when(pl.program_id(0) == 0)
    def _():
        carry_ref[...] = jnp.zeros_like(carry_ref)

    sc = s_ref[...]
    e, tr = sc.shape
    row = lax.broadcasted_iota(jnp.int32, (e, tr), 0)
    v = sc + b_ref[...]
    sel = jnp.zeros((e, tr), F32)
    idxs, ws = [], []
    for _ in range(TOP_K):
        m = jnp.max(v, axis=0, keepdims=True)
        idx = jnp.min(jnp.where(v == m, row, e), axis=0, keepdims=True)
        oh = row == idx
        ws.append(jnp.sum(jnp.where(oh, sc, 0.0), axis=0, keepdims=True))
        idxs.append(idx)
        v = jnp.where(oh, -jnp.inf, v)
        sel = sel + oh.astype(F32)
    wsum = ws[0]
    for w in ws[1:]:
        wsum = wsum + w
    selb = sel.astype(BF16)
    cum = jnp.dot(selb, tri_ref[...], preferred_element_type=F32) + carry_ref[...]
    for kk in range(TOP_K):
        oh = row == idxs[kk]
        rk = jnp.sum(jnp.where(oh, cum, 0.0), axis=0, keepdims=True)
        idx_ref[kk:kk + 1, :] = idxs[kk]
        rank_ref[kk:kk + 1, :] = rk.astype(jnp.int32)
        w_ref[kk:kk + 1, :] = ws[kk] / wsum * ROUTED_SCALE
    tot = carry_ref[...] + jnp.dot(selb, jnp.ones((tr, tr), BF16), preferred_element_type=F32)
    carry_ref[...] = tot
    cnt_ref[...] = tot


def _route(scores_t, router_bias):
    e, t = scores_t.shape
    tr = _tile(t, 256)
    tri = jnp.asarray(np.triu(np.ones((tr, tr), np.float32), 1), BF16)
    bias = jnp.broadcast_to(router_bias.reshape(e, 1), (e, tr)).astype(F32)
    blk = pl.BlockSpec((TOP_K, tr), lambda i: (0, i))
    return pl.pallas_call(
        _route_kernel,
        grid=(t // tr,),
        in_specs=[pl.BlockSpec((e, tr), lambda i: (0, i)),
                  pl.BlockSpec((e, tr), lambda i: (0, 0)),
                  pl.BlockSpec((tr, tr), lambda i: (0, 0))],
        out_specs=[blk, blk, blk, pl.BlockSpec((e, tr), lambda i: (0, 0))],
        out_shape=[jax.ShapeDtypeStruct((TOP_K, t), jnp.int32),
                   jax.ShapeDtypeStruct((TOP_K, t), F32),
                   jax.ShapeDtypeStruct((TOP_K, t), jnp.int32),
                   jax.ShapeDtypeStruct((e, tr), F32)],
        scratch_shapes=[pltpu.VMEM((e, tr), F32)],
        compiler_params=_cparams("arbitrary"),
        name="route",
    )(scores_t, bias, tri)


def _slots_kernel(idx_ref, rank_ref, ps_ref, slot_ref):
    ps = ps_ref[...]
    row = lax.broadcasted_iota(jnp.int32, ps.shape, 0)
    for kk in range(TOP_K):
        oh = row == idx_ref[kk:kk + 1, :]
        start = jnp.sum(jnp.where(oh, ps, 0), axis=0, keepdims=True)
        slot_ref[kk:kk + 1, :] = start + rank_ref[kk:kk + 1, :]


def _slots(idx_t, rank_t, p_start):
    _, t = idx_t.shape
    e = p_start.shape[0]
    ts = _tile(t, 512)
    ps = jnp.broadcast_to(p_start.reshape(e, 1), (e, ts))
    blk = pl.BlockSpec((TOP_K, ts), lambda i: (0, i))
    return pl.pallas_call(
        _slots_kernel,
        grid=(t // ts,),
        in_specs=[blk, blk, pl.BlockSpec((e, ts), lambda i: (0, 0))],
        out_specs=blk,
        out_shape=jax.ShapeDtypeStruct((TOP_K, t), jnp.int32),
        compiler_params=_cparams("arbitrary"),
        name="slots",
    )(idx_t, rank_t, ps)


def _invert_kernel(slot_ref, zero_hbm, inv_ref, sem, *, td):
    i = pl.program_id(0)

    @pl.when(i == 0)
    def _():
        fill = pltpu.make_async_copy(zero_hbm, inv_ref, sem)
        fill.start()
        fill.wait()

    base = i * td

    def body(r, c):
        for kk in range(TOP_K):
            inv_ref[slot_ref[kk, r]] = base + r
        return c
    lax.fori_loop(0, td, body, 0)


def _invert(slot_t, n_slots):
    _, t = slot_t.shape
    td = _tile(t, 512)
    return pl.pallas_call(
        functools.partial(_invert_kernel, td=td),
        grid=(t // td,),
        in_specs=[pl.BlockSpec((TOP_K, td), lambda i: (0, i), memory_space=pltpu.SMEM),
                  pl.BlockSpec(memory_space=pl.ANY)],
        out_specs=pl.BlockSpec((n_slots,), lambda i: (0,), memory_space=pltpu.SMEM),
        out_shape=jax.ShapeDtypeStruct((n_slots,), jnp.int32),
        scratch_shapes=[pltpu.SemaphoreType.DMA(())],
        compiler_params=_cparams("arbitrary"),
        name="invert",
    )(slot_t, jnp.zeros((n_slots,), jnp.int32))


def _row_copy(src, dst, sem):
    return pltpu.make_async_copy(src, dst, sem)


def _expert_kernel(be_ref, nxt_ref, par_ref, nu_ref, tok_ref, ntok_ref, h_hbm, wg_hbm, wu_hbm,
                   wd_hbm, ys_ref, xbuf, wgf, wuf, wdf, wgb, wub, wdb, sem, wsem, *, nch):
    i = pl.program_id(0)
    cur = i % 2
    nu = nu_ref[0]
    rows = EXPERT_ROWS

    def weight_copies(e, which):
        return (pltpu.make_async_copy(wg_hbm.at[e], wgf.at[which], wsem.at[which]),
                pltpu.make_async_copy(wu_hbm.at[e], wuf.at[which], wsem.at[which]),
                pltpu.make_async_copy(wd_hbm.at[e], wdf.at[which], wsem.at[which]))

    def fetch(t_ref, which):
        for r in range(rows):
            t = t_ref[0, 0, r]
            _row_copy(h_hbm.at[pl.ds(pl.multiple_of(t * nch, nch), nch)],
                      xbuf.at[which, pl.ds(r * nch, nch)], sem.at[which]).start()

    def drain(which):
        def body(r, c):
            _row_copy(h_hbm.at[pl.ds(0, nch)], xbuf.at[which, pl.ds(0, nch)],
                      sem.at[which]).wait()
            return c
        lax.fori_loop(0, rows, body, 0, unroll=8)

    @pl.when(i == 0)
    def _():
        for cp in weight_copies(be_ref[0], 0):
            cp.start(priority=1)
        fetch(tok_ref, 0)

    @pl.when(i < nu)
    def _():
        fetch(ntok_ref, 1 - cur)
        prev = be_ref[jnp.maximum(i - 1, 0)]

        @pl.when(jnp.logical_or(i == 0, be_ref[i] != prev))
        def _():
            par = par_ref[i]
            for cp in weight_copies(be_ref[i], par):
                cp.wait()
            wgb[...] = wgf[par].astype(BF16)
            wub[...] = wuf[par].astype(BF16)
            wdb[...] = wdf[par].astype(BF16)

            @pl.when(nxt_ref[i] >= 0)
            def _():
                for cp in weight_copies(nxt_ref[i], 1 - par):
                    cp.start(priority=1)

        drain(cur)
        x = _load_row_tiles(xbuf.at[cur], rows, nch).astype(BF16)
        g = jnp.dot(x, wgb[...], preferred_element_type=F32)
        u = jnp.dot(x, wub[...], preferred_element_type=F32)
        a = (_silu(g) * u).astype(BF16)
        _store_row_tiles(ys_ref, jnp.dot(a, wdb[...], preferred_element_type=F32))

    @pl.when(i == nu - 1)
    def _():
        drain(1 - cur)

    @pl.when(i >= nu)
    def _():
        ys_ref[...] = jnp.zeros_like(ys_ref)


def _experts(blk_expert, blk_next, blk_par, nblk_used, tok_of_slot, h2t, w_g, w_u, w_d):
    d, f = w_g.shape[1], w_g.shape[2]
    nch = d // LANES
    rows = EXPERT_ROWS
    nblk = blk_expert.shape[0]
    toks = tok_of_slot.reshape(nblk, 1, rows)
    last = lambda i, nu: jnp.minimum(i, nu[0] - 1)
    hbm = pl.BlockSpec(memory_space=pl.ANY)
    return pl.pallas_call(
        functools.partial(_expert_kernel, nch=nch),
        grid_spec=pltpu.PrefetchScalarGridSpec(
            num_scalar_prefetch=4,
            grid=(nblk,),
            in_specs=[pl.BlockSpec((1, 1, rows), lambda i, be, nx, pa, nu: (last(i, nu), 0, 0),
                                   memory_space=pltpu.SMEM),
                      pl.BlockSpec((1, 1, rows),
                                   lambda i, be, nx, pa, nu: (last(i + 1, nu), 0, 0),
                                   memory_space=pltpu.SMEM),
                      hbm, hbm, hbm, hbm],
            out_specs=pl.BlockSpec((rows * nch, LANES), lambda i, be, nx, pa, nu: (i, 0)),
            scratch_shapes=[pltpu.VMEM((2, rows * nch, LANES), F32),
                            pltpu.VMEM((2, d, f), F32), pltpu.VMEM((2, d, f), F32),
                            pltpu.VMEM((2, f, d), F32),
                            pltpu.VMEM((d, f), BF16), pltpu.VMEM((d, f), BF16),
                            pltpu.VMEM((f, d), BF16),
                            pltpu.SemaphoreType.DMA((2,)), pltpu.SemaphoreType.DMA((2,))]),
        out_shape=jax.ShapeDtypeStruct((nblk * rows * nch, LANES), F32),
        compiler_params=_cparams("arbitrary"),
        name="experts",
    )(blk_expert, blk_next, blk_par, nblk_used, toks, toks, h2t, w_g, w_u, w_d)


def _combine_kernel(slot_ref, nslot_ref, w_ref, x1_ref, h2_ref, mod_ref, wsg_ref, wsu_ref,
                    wsd_ref, ys_hbm, o_ref, buf, sem, *, tc, nsteps, nch):
    i = pl.program_id(0)
    cur = i % 2

    def issue(s_ref, which):
        def body(r, c):
            for kk in range(TOP_K):
                s = s_ref[kk, r]
                _row_copy(ys_hbm.at[pl.ds(pl.multiple_of(s * nch, nch), nch)],
                          buf.at[which, kk, pl.ds(pl.multiple_of(r * nch, nch), nch)],
                          sem.at[which]).start()
            return c
        lax.fori_loop(0, tc, body, 0)

    @pl.when(i == 0)
    def _():
        issue(slot_ref, 0)

    @pl.when(i + 1 < nsteps)
    def _():
        issue(nslot_ref, 1 - cur)

    hb = _load_row_tiles(h2_ref, tc, nch).astype(BF16)
    g = jnp.dot(hb, wsg_ref[...], preferred_element_type=F32)
    u = jnp.dot(hb, wsu_ref[...], preferred_element_type=F32)
    acc = jnp.dot((_silu(g) * u).astype(BF16), wsd_ref[...], preferred_element_type=F32)

    def drain(r, c):
        for kk in range(TOP_K):
            _row_copy(ys_hbm.at[pl.ds(0, nch)], buf.at[cur, kk, pl.ds(0, nch)], sem.at[cur]).wait()
        return c
    lax.fori_loop(0, tc, drain, 0)

    w = w_ref[...]
    for kk in range(TOP_K):
        acc = acc + w[:, kk:kk + 1] * _load_row_tiles(buf.at[cur, kk], tc, nch)
    g2 = mod_ref[0][5:6]
    o_ref[...] = x1_ref[...] + g2 * acc


def _combine(slot_t, w_tk, x1, h2t, mod, w_sg, w_su, w_sd, ys, seq):
    t, d = x1.shape
    nch = d // LANES
    tc = _tile(seq, 128)
    tpb = seq // tc
    nsteps = t // tc
    full = lambda shp: pl.BlockSpec(shp, lambda i: (0,) * len(shp))
    tok = lambda w: pl.BlockSpec((tc, w), lambda i: (i, 0))
    return pl.pallas_call(
        functools.partial(_combine_kernel, tc=tc, nsteps=nsteps, nch=nch),
        grid=(nsteps,),
        in_specs=[pl.BlockSpec((TOP_K, tc), lambda i: (0, i), memory_space=pltpu.SMEM),
                  pl.BlockSpec((TOP_K, tc), lambda i: (0, jnp.minimum(i + 1, nsteps - 1)),
                               memory_space=pltpu.SMEM),
                  tok(TOP_K), tok(d),
                  pl.BlockSpec((tc * nch, LANES), lambda i: (i, 0)),
                  pl.BlockSpec((1, N_ADA, d), lambda i: (i // tpb, 0, 0)),
                  full(w_sg.shape), full(w_su.shape), full(w_sd.shape),
                  pl.BlockSpec(memory_space=pl.ANY)],
        out_specs=tok(d),
        out_shape=jax.ShapeDtypeStruct((t, d), F32),
        scratch_shapes=[pltpu.VMEM((2, TOP_K, tc * nch, LANES), F32),
                        pltpu.SemaphoreType.DMA((2,))],
        compiler_params=_cparams("arbitrary"),
        name="combine",
    )(slot_t, slot_t, w_tk, x1, h2t, mod, w_sg.astype(BF16), w_su.astype(BF16),
      w_sd.astype(BF16), ys)


def _layer(x, c, w_ada, b_ada, norm1_g, w_in, q_a_norm_g, w_uq, kv_a_norm_g, w_ukv,
           q_norm_g, k_norm_g, w_proj_attn, w_proj_fourier, w_out, norm2_g,
           w_router, router_bias, w_exp_gate, w_exp_up, w_exp_down,
           w_sh_gate, w_sh_up, w_sh_down):
    bsz, seq, d = x.shape
    t = bsz * seq
    e = w_router.shape[1]
    x2 = x.reshape(t, d)

    mod = _ada(c, w_ada, b_ada).reshape(bsz, N_ADA, d)
    q, k, v, zf, sa, sf = _inproj(x2, mod, norm1_g, w_in, q_a_norm_g, w_uq, kv_a_norm_g,
                                  w_ukv, q_norm_g, k_norm_g, bsz, seq)
    attn = _attention(q, k, v).reshape(t, N_HEADS * V_DIM)
    four = _fourier(zf.reshape(bsz, seq, zf.shape[1])).reshape(t, zf.shape[1])
    x1, h2, scores_t = _merge(attn, four, sa, sf, x2, mod, w_proj_attn, w_proj_fourier,
                              w_out, norm2_g, w_router, seq)

    idx_t, w_t, rank_t, cnt = _route(scores_t, router_bias)
    counts = cnt[:, 0].astype(jnp.int32)
    rows = EXPERT_ROWS
    nblk = -(-(t * TOP_K) // rows) + e
    padded = ((counts + rows - 1) // rows) * rows
    p_end = jnp.cumsum(padded)
    p_start = p_end - padded
    nblk_used = (p_end[-1] // rows).astype(jnp.int32)
    blk_start = jnp.arange(nblk, dtype=jnp.int32) * rows
    blk_first = jnp.minimum(blk_start, p_end[-1] - 1)
    hit = p_end[None, :] <= blk_first[:, None]
    blk_expert = jnp.clip(jnp.sum(hit.astype(jnp.int32), axis=1), 0, e - 1)

    slot_t = _slots(idx_t, rank_t, p_start.astype(jnp.int32))
    tok_of_slot = _invert(slot_t, nblk * rows)
    eid = jnp.arange(e, dtype=jnp.int32)
    cand = jnp.where(counts > 0, eid, e)
    nxt = jnp.concatenate([lax.cummin(cand[::-1])[::-1][1:], jnp.full((1,), e, jnp.int32)])
    nxt = jnp.where(nxt >= e, -1, nxt)
    run = jnp.cumsum((counts > 0).astype(jnp.int32)) - 1
    onehot = blk_expert[:, None] == eid[None, :]
    blk_next = jnp.sum(jnp.where(onehot, nxt[None, :], 0), axis=1).astype(jnp.int32)
    blk_par = (jnp.sum(jnp.where(onehot, run[None, :], 0), axis=1) % 2).astype(jnp.int32)
    ys = _experts(blk_expert, blk_next, blk_par, nblk_used.reshape(1), tok_of_slot, h2,
                  w_exp_gate, w_exp_up, w_exp_down)
    out = _combine(slot_t, w_t.T, x1, h2, mod, w_sh_gate, w_sh_up, w_sh_down, ys, seq)
    return out.reshape(bsz, seq, d)


def kernel(x, c, w_ada, b_ada, norm1_g, w_in, q_a_norm_g, w_uq, kv_a_norm_g, w_ukv, q_norm_g,
           k_norm_g, w_proj_attn, w_proj_fourier, w_out, norm2_g, w_router, router_bias,
           w_exp_gate, w_exp_up, w_exp_down, w_sh_gate, w_sh_up, w_sh_down):
    for l in range(w_ada.shape[0]):
        x = _layer(x, c, w_ada[l], b_ada[l], norm1_g[l], w_in[l], q_a_norm_g[l], w_uq[l],
                   kv_a_norm_g[l], w_ukv[l], q_norm_g[l], k_norm_g[l], w_proj_attn[l],
                   w_proj_fourier[l], w_out[l], norm2_g[l], w_router[l], router_bias[l],
                   w_exp_gate[l], w_exp_up[l], w_exp_down[l], w_sh_gate[l], w_sh_up[l],
                   w_sh_down[l])
    return x
```

```python
import functools
import math

import numpy as np
import jax
import jax.numpy as jnp
from jax import lax
from jax.experimental import pallas as pl
from jax.experimental.pallas import tpu as pltpu

N_HEADS = 8
QK_NOPE = 64
QK_ROPE = 32
V_DIM = 64
FOURIER_GROUP = 64
TOP_K = 8
ROUTED_SCALE = 2.5
EPS = 1e-6
ROPE_THETA = 10000.0
N_ADA = 6

LANES = 128
EXPERT_ROWS = 256
ROW_GROUP = 16
VMEM_LIMIT = 48 * 1024 * 1024

F32 = jnp.float32
BF16 = jnp.bfloat16


def _cparams(*sem):
    return pltpu.CompilerParams(dimension_semantics=sem, vmem_limit_bytes=VMEM_LIMIT)


def _tile(n, pref):
    t = min(n, pref)
    assert n % t == 0, (n, pref)
    return t


def _silu(v):
    return v * jax.nn.sigmoid(v)


def _store_row_tiles(ref, val):
    rows, d = val.shape
    nch = d // LANES
    for j in range(nch):
        ref[pl.ds(j, rows, stride=nch), :] = val[:, j * LANES:(j + 1) * LANES]


def _load_row_tiles(ref, rows, nch):
    return jnp.concatenate([ref[pl.ds(j, rows, stride=nch), :] for j in range(nch)], axis=1)


def _ada_kernel(c_ref, w_ref, b_ref, o_ref):
    a = _silu(c_ref[...])
    o_ref[...] = jnp.dot(a, w_ref[...], preferred_element_type=F32,
                         precision=lax.Precision.HIGHEST) + b_ref[...]


def _ada(c, w_ada, b_ada):
    bsz, d = c.shape
    n = w_ada.shape[1]
    tn = _tile(n, d)
    return pl.pallas_call(
        _ada_kernel,
        grid=(n // tn,),
        in_specs=[pl.BlockSpec((bsz, d), lambda j: (0, 0)),
                  pl.BlockSpec((d, tn), lambda j: (0, j)),
                  pl.BlockSpec((1, tn), lambda j: (0, j))],
        out_specs=pl.BlockSpec((bsz, tn), lambda j: (0, j)),
        out_shape=jax.ShapeDtypeStruct((bsz, n), F32),
        compiler_params=_cparams("arbitrary"),
        name="ada",
    )(c, w_ada, b_ada.reshape(1, n))


def _head_norm_rope(t, g, cos, s1, s2):
    ms = jnp.sum(t * t, axis=-1, keepdims=True) * (1.0 / (QK_NOPE + QK_ROPE))
    t = t * lax.rsqrt(ms + EPS) * g
    half = QK_ROPE // 2
    return t * cos + pltpu.roll(t, LANES - half, 1) * s1 + pltpu.roll(t, half, 1) * s2


def _inproj_kernel(x_ref, mod_ref, g1_ref, wa_ref, wf_ref, wga_ref, wgf_ref,
                   gq_ref, gkv_ref, wuq_ref, wkv_ref, qg_ref, kg_ref,
                   cos_ref, s1_ref, s2_ref,
                   q_ref, k_ref, v_ref, zf_ref, sa_ref, sf_ref, *, ql, kvl):
    x = x_ref[...]
    mod = mod_ref[0]
    sh1, sc1 = mod[0:1], mod[1:2]
    r = lax.rsqrt(jnp.mean(x * x, axis=-1, keepdims=True) + EPS)
    h = (x * r * g1_ref[...]) * (1.0 + sc1) + sh1
    hb = h.astype(BF16)

    zf_ref[...] = jnp.dot(hb, wf_ref[...], preferred_element_type=F32).astype(BF16)
    sa_ref[...] = jax.nn.sigmoid(
        jnp.dot(hb, wga_ref[...], preferred_element_type=F32)).astype(BF16)
    sf_ref[...] = jax.nn.sigmoid(
        jnp.dot(hb, wgf_ref[...], preferred_element_type=F32)).astype(BF16)

    za = jnp.dot(hb, wa_ref[...], preferred_element_type=F32)
    zq = za[:, :ql]
    cq = zq * lax.rsqrt(jnp.mean(zq * zq, axis=-1, keepdims=True) + EPS) * gq_ref[...]
    qall = jnp.dot(cq.astype(BF16), wuq_ref[...], preferred_element_type=F32)

    zk = za[:, ql:]
    kvn = zk[:, :kvl]
    rk = lax.rsqrt(jnp.mean(kvn * kvn, axis=-1, keepdims=True) + EPS)
    lane = lax.broadcasted_iota(jnp.int32, zk.shape, 1)
    u = zk * jnp.where(lane < kvl, rk, 1.0) * gkv_ref[...]
    kvall = jnp.dot(u.astype(BF16), wkv_ref[...], preferred_element_type=F32)

    cos, s1, s2 = cos_ref[...], s1_ref[...], s2_ref[...]
    qg, kg = qg_ref[...], kg_ref[...]
    voff = N_HEADS * LANES
    for hd in range(N_HEADS):
        tq = qall[:, hd * LANES:(hd + 1) * LANES]
        q_ref[0, hd] = _head_norm_rope(tq, qg, cos, s1, s2).astype(BF16)
        tk = kvall[:, hd * LANES:(hd + 1) * LANES]
        k_ref[0, hd] = _head_norm_rope(tk, kg, cos, s1, s2).astype(BF16)
        v_ref[0, hd] = kvall[:, voff + hd * V_DIM: voff + (hd + 1) * V_DIM].astype(BF16)


def _rope_tables(seq):
    half = QK_ROPE // 2
    pos = np.arange(seq, dtype=np.float64)
    inv = ROPE_THETA ** (-np.arange(0, QK_ROPE, 2, dtype=np.float64) / QK_ROPE)
    ang = pos[:, None] * inv[None, :]
    c, s = np.cos(ang), np.sin(ang)
    cos = np.ones((seq, LANES)); s1 = np.zeros((seq, LANES)); s2 = np.zeros((seq, LANES))
    cos[:, QK_NOPE:QK_NOPE + half] = c
    cos[:, QK_NOPE + half:QK_NOPE + QK_ROPE] = c
    s1[:, QK_NOPE:QK_NOPE + half] = -s
    s2[:, QK_NOPE + half:QK_NOPE + QK_ROPE] = s
    return (jnp.asarray(cos, F32), jnp.asarray(s1, F32), jnp.asarray(s2, F32))


def _inproj(x2, mod, norm1_g, w_in, q_a_g, w_uq, kv_a_g, w_ukv, q_g, k_g, bsz, seq):
    t, d = x2.shape
    ql, kvl = q_a_g.shape[0], kv_a_g.shape[0]
    hq = QK_NOPE + QK_ROPE
    fw = w_in.shape[1] - ql - kvl - QK_ROPE - 2 * d
    o1, o2, o3, o4, o5 = ql, ql + kvl, ql + kvl + QK_ROPE, ql + kvl + QK_ROPE + fw, \
        ql + kvl + QK_ROPE + fw + d
    assert ql % LANES == 0 and kvl % LANES == 0

    wa = jnp.concatenate([w_in[:, :o3], jnp.zeros((d, LANES - QK_ROPE), F32)], axis=1).astype(BF16)
    wf = w_in[:, o3:o4].astype(BF16)
    wga = w_in[:, o4:o5].astype(BF16)
    wgf = w_in[:, o5:].astype(BF16)

    wuq = w_uq.reshape(ql, N_HEADS, hq)
    wuq = jnp.pad(wuq, ((0, 0), (0, 0), (0, LANES - hq))).reshape(ql, N_HEADS * LANES).astype(BF16)
    wukv = w_ukv.reshape(kvl, N_HEADS, QK_NOPE + V_DIM)
    wk = jnp.pad(wukv[:, :, :QK_NOPE], ((0, 0), (0, 0), (0, LANES - QK_NOPE)))
    place = jnp.zeros((QK_ROPE, N_HEADS, LANES), F32)
    place = place.at[jnp.arange(QK_ROPE), :, QK_NOPE + jnp.arange(QK_ROPE)].set(1.0)
    wk = jnp.concatenate([wk, place, jnp.zeros((LANES - QK_ROPE, N_HEADS, LANES), F32)], axis=0)
    wv = jnp.concatenate([wukv[:, :, QK_NOPE:], jnp.zeros((LANES, N_HEADS, V_DIM), F32)], axis=0)
    wkv = jnp.concatenate([wk.reshape(kvl + LANES, N_HEADS * LANES),
                           wv.reshape(kvl + LANES, N_HEADS * V_DIM)], axis=1).astype(BF16)

    gkv = jnp.concatenate([kv_a_g, jnp.ones((LANES,), F32)]).reshape(1, kvl + LANES)
    pad = jnp.zeros((LANES - hq,), F32)
    qg = jnp.concatenate([q_g * (hq ** -0.5), pad]).reshape(1, LANES)
    kg = jnp.concatenate([k_g, pad]).reshape(1, LANES)
    cos, s1, s2 = _rope_tables(seq)

    tm = _tile(seq, 512)
    tpb = seq // tm
    full = lambda shp: pl.BlockSpec(shp, lambda i: (0,) * len(shp))
    tok = lambda w: pl.BlockSpec((tm, w), lambda i: (i, 0))
    head = lambda w: pl.BlockSpec((1, N_HEADS, tm, w), lambda i: (i // tpb, 0, i % tpb, 0))
    rope = pl.BlockSpec((tm, LANES), lambda i: (i % tpb, 0))
    return pl.pallas_call(
        functools.partial(_inproj_kernel, ql=ql, kvl=kvl),
        grid=(t // tm,),
        in_specs=[tok(d),
                  pl.BlockSpec((1, N_ADA, d), lambda i: (i // tpb, 0, 0)),
                  full((1, d)), full(wa.shape), full(wf.shape), full(wga.shape), full(wgf.shape),
                  full((1, ql)), full(gkv.shape), full(wuq.shape), full(wkv.shape),
                  full((1, LANES)), full((1, LANES)), rope, rope, rope],
        out_specs=[head(LANES), head(LANES), head(V_DIM), tok(fw), tok(d), tok(d)],
        out_shape=[jax.ShapeDtypeStruct((bsz, N_HEADS, seq, LANES), BF16),
                   jax.ShapeDtypeStruct((bsz, N_HEADS, seq, LANES), BF16),
                   jax.ShapeDtypeStruct((bsz, N_HEADS, seq, V_DIM), BF16),
                   jax.ShapeDtypeStruct((t, fw), BF16),
                   jax.ShapeDtypeStruct((t, d), BF16),
                   jax.ShapeDtypeStruct((t, d), BF16)],
        compiler_params=_cparams("arbitrary"),
        name="inproj",
    )(x2, mod, norm1_g.reshape(1, d), wa, wf, wga, wgf, q_a_g.reshape(1, ql), gkv, wuq, wkv,
      qg, kg, cos, s1, s2)


def _attn_kernel(q_ref, k_ref, v_ref, o_ref):
    for hd in range(N_HEADS):
        s = lax.dot_general(q_ref[0, hd], k_ref[0, hd], (((1,), (1,)), ((), ())),
                            preferred_element_type=F32)
        m = jnp.max(s, axis=-1, keepdims=True)
        p = jnp.exp(s - m)
        l = jnp.sum(p, axis=-1, keepdims=True)
        o = jnp.dot(p.astype(BF16), v_ref[0, hd], preferred_element_type=F32)
        o_ref[0, :, hd * V_DIM:(hd + 1) * V_DIM] = (o / l).astype(BF16)


def _attention(q, k, v):
    bsz, _, seq, _ = q.shape
    tq = _tile(seq, 256)
    return pl.pallas_call(
        _attn_kernel,
        grid=(bsz, seq // tq),
        in_specs=[pl.BlockSpec((1, N_HEADS, tq, LANES), lambda b, j: (b, 0, j, 0)),
                  pl.BlockSpec((1, N_HEADS, seq, LANES), lambda b, j: (b, 0, 0, 0)),
                  pl.BlockSpec((1, N_HEADS, seq, V_DIM), lambda b, j: (b, 0, 0, 0))],
        out_specs=pl.BlockSpec((1, tq, N_HEADS * V_DIM), lambda b, j: (b, j, 0)),
        out_shape=jax.ShapeDtypeStruct((bsz, seq, N_HEADS * V_DIM), BF16),
        compiler_params=_cparams("arbitrary", "arbitrary"),
        name="attn",
    )(q, k, v)


def _fourier_kernel(z_ref, wc_ref, ws_ref, tab_ref, o_ref, u_ref, *, seq):
    @pl.when(pl.program_id(1) == 0)
    def _():
        z = z_ref[0]
        u_ref[:seq, :] = jnp.dot(z, wc_ref[...], preferred_element_type=F32).astype(BF16)
        u_ref[seq:, :] = jnp.dot(z, ws_ref[...], preferred_element_type=F32).astype(BF16)

    o_ref[0] = jnp.dot(tab_ref[...], u_ref[...], preferred_element_type=F32).astype(BF16)


def _fourier_tables(seq, fw):
    g = FOURIER_GROUP
    n = np.arange(seq, dtype=np.int64)
    ang = 2.0 * np.pi * ((n[:, None] * n[None, :]) % seq).astype(np.float64) / seq
    tab = np.concatenate([np.cos(ang), -np.sin(ang)], axis=1)
    c = np.arange(g, dtype=np.int64)
    angc = 2.0 * np.pi * ((c[:, None] * c[None, :]) % g).astype(np.float64) / g
    scale = 1.0 / math.sqrt(seq * g)
    eye = np.eye(fw // g)
    wc = np.kron(eye, np.cos(angc) * scale)
    ws = np.kron(eye, np.sin(angc) * scale)
    return (jnp.asarray(tab, F32).astype(BF16), jnp.asarray(wc, F32).astype(BF16),
            jnp.asarray(ws, F32).astype(BF16))


def _fourier(zf):
    bsz, seq, fw = zf.shape
    tab, wc, ws = _fourier_tables(seq, fw)
    tr = _tile(seq, 256)
    return pl.pallas_call(
        functools.partial(_fourier_kernel, seq=seq),
        grid=(bsz, seq // tr),
        in_specs=[pl.BlockSpec((1, seq, fw), lambda b, j: (b, 0, 0)),
                  pl.BlockSpec((fw, fw), lambda b, j: (0, 0)),
                  pl.BlockSpec((fw, fw), lambda b, j: (0, 0)),
                  pl.BlockSpec((tr, 2 * seq), lambda b, j: (j, 0))],
        out_specs=pl.BlockSpec((1, tr, fw), lambda b, j: (b, j, 0)),
        out_shape=jax.ShapeDtypeStruct((bsz, seq, fw), BF16),
        scratch_shapes=[pltpu.VMEM((2 * seq, fw), BF16)],
        compiler_params=_cparams("arbitrary", "arbitrary"),
        name="fourier",
    )(zf, wc, ws, tab)


def _merge_kernel(a_ref, f_ref, sa_ref, sf_ref, x_ref, mod_ref, wpa_ref, wpf_ref, wo_ref,
                  g2_ref, wrh_ref, wrl_ref, x1_ref, h2_ref, sc_ref):
    ya = jnp.dot(a_ref[...], wpa_ref[...], preferred_element_type=F32)
    yf = jnp.dot(f_ref[...], wpf_ref[...], preferred_element_type=F32)
    merged = sa_ref[...].astype(F32) * ya + sf_ref[...].astype(F32) * yf
    mod = mod_ref[0]
    g1, sh2, sc2 = mod[2:3], mod[3:4], mod[4:5]
    x1 = x_ref[...] + g1 * jnp.dot(merged.astype(BF16), wo_ref[...], preferred_element_type=F32)
    x1_ref[...] = x1
    r = lax.rsqrt(jnp.mean(x1 * x1, axis=-1, keepdims=True) + EPS)
    h2 = (x1 * r * g2_ref[...]) * (1.0 + sc2) + sh2
    _store_row_tiles(h2_ref, h2)
    hh = h2.astype(BF16)
    hl = (h2 - hh.astype(F32)).astype(BF16)
    nt = (((1,), (1,)), ((), ()))
    lt = (lax.dot_general(wrh_ref[...], hh, nt, preferred_element_type=F32)
          + lax.dot_general(wrh_ref[...], hl, nt, preferred_element_type=F32)
          + lax.dot_general(wrl_ref[...], hh, nt, preferred_element_type=F32))
    sc_ref[...] = jax.nn.sigmoid(lt)


def _merge(attn, four, sa, sf, x2, mod, w_pa, w_pf, w_out, norm2_g, w_router, seq):
    t, d = x2.shape
    e = w_router.shape[1]
    wrt = w_router.T
    wrh = wrt.astype(BF16)
    wrl = (wrt - wrh.astype(F32)).astype(BF16)
    tm = _tile(seq, 512)
    tpb = seq // tm
    full = lambda shp: pl.BlockSpec(shp, lambda i: (0,) * len(shp))
    tok = lambda w: pl.BlockSpec((tm, w), lambda i: (i, 0))
    return pl.pallas_call(
        _merge_kernel,
        grid=(t // tm,),
        in_specs=[tok(attn.shape[1]), tok(four.shape[1]), tok(d), tok(d), tok(d),
                  pl.BlockSpec((1, N_ADA, d), lambda i: (i // tpb, 0, 0)),
                  full(w_pa.shape), full(w_pf.shape), full(w_out.shape), full((1, d)),
                  full((e, d)), full((e, d))],
        out_specs=[tok(d), pl.BlockSpec((tm * (d // LANES), LANES), lambda i: (i, 0)),
                   pl.BlockSpec((e, tm), lambda i: (0, i))],
        out_shape=[jax.ShapeDtypeStruct((t, d), F32),
                   jax.ShapeDtypeStruct((t * (d // LANES), LANES), F32),
                   jax.ShapeDtypeStruct((e, t), F32)],
        compiler_params=_cparams("arbitrary"),
        name="merge",
    )(attn, four, sa, sf, x2, mod, w_pa.astype(BF16), w_pf.astype(BF16), w_out.astype(BF16),
      norm2_g.reshape(1, d), wrh, wrl)


def _route_kernel(s_ref, b_ref, tri_ref, idx_ref, w_ref, rank_ref, cnt_ref, carry_ref):
    @pl.when(pl.program_id(0) == 0)
    def _():
        carry_ref[...] = jnp.zeros_like(carry_ref)

    sc = s_ref[...]
    e, tr = sc.shape
    row = lax.broadcasted_iota(jnp.int32, (e, tr), 0)
    v = sc + b_ref[...]
    sel = jnp.zeros((e, tr), F32)
    idxs, ws = [], []
    for _ in range(TOP_K):
        m = jnp.max(v, axis=0, keepdims=True)
        idx = jnp.min(jnp.where(v == m, row, e), axis=0, keepdims=True)
        oh = row == idx
        ws.append(jnp.sum(jnp.where(oh, sc, 0.0), axis=0, keepdims=True))
        idxs.append(idx)
        v = jnp.where(oh, -jnp.inf, v)
        sel = sel + oh.astype(F32)
    wsum = ws[0]
    for w in ws[1:]:
        wsum = wsum + w
    selb = sel.astype(BF16)
    cum = jnp.dot(selb, tri_ref[...], preferred_element_type=F32) + carry_ref[...]
    for kk in range(TOP_K):
        oh = row == idxs[kk]
        rk = jnp.sum(jnp.where(oh, cum, 0.0), axis=0, keepdims=True)
        idx_ref[kk:kk + 1, :] = idxs[kk]
        rank_ref[kk:kk + 1, :] = rk.astype(jnp.int32)
        w_ref[kk:kk + 1, :] = ws[kk] / wsum * ROUTED_SCALE
    tot = carry_ref[...] + jnp.dot(selb, jnp.ones((tr, tr), BF16), preferred_element_type=F32)
    carry_ref[...] = tot
    cnt_ref[...] = tot


def _route(scores_t, router_bias):
    e, t = scores_t.shape
    tr = _tile(t, 256)
    tri = jnp.asarray(np.triu(np.ones((tr, tr), np.float32), 1), BF16)
    bias = jnp.broadcast_to(router_bias.reshape(e, 1), (e, tr)).astype(F32)
    blk = pl.BlockSpec((TOP_K, tr), lambda i: (0, i))
    return pl.pallas_call(
        _route_kernel,
        grid=(t // tr,),
        in_specs=[pl.BlockSpec((e, tr), lambda i: (0, i)),
                  pl.BlockSpec((e, tr), lambda i: (0, 0)),
                  pl.BlockSpec((tr, tr), lambda i: (0, 0))],
        out_specs=[blk, blk, blk, pl.BlockSpec((e, tr), lambda i: (0, 0))],
        out_shape=[jax.ShapeDtypeStruct((TOP_K, t), jnp.int32),
                   jax.ShapeDtypeStruct((TOP_K, t), F32),
                   jax.ShapeDtypeStruct((TOP_K, t), jnp.int32),
                   jax.ShapeDtypeStruct((e, tr), F32)],
        scratch_shapes=[pltpu.VMEM((e, tr), F32)],
        compiler_params=_cparams("arbitrary"),
        name="route",
    )(scores_t, bias, tri)


def _slots_kernel(idx_ref, rank_ref, ps_ref, slot_ref):
    ps = ps_ref[...]
    row = lax.broadcasted_iota(jnp.int32, ps.shape, 0)
    for kk in range(TOP_K):
        oh = row == idx_ref[kk:kk + 1, :]
        start = jnp.sum(jnp.where(oh, ps, 0), axis=0, keepdims=True)
        slot_ref[kk:kk + 1, :] = start + rank_ref[kk:kk + 1, :]


def _slots(idx_t, rank_t, p_start):
    _, t = idx_t.shape
    e = p_start.shape[0]
    ts = _tile(t, 512)
    ps = jnp.broadcast_to(p_start.reshape(e, 1), (e, ts))
    blk = pl.BlockSpec((TOP_K, ts), lambda i: (0, i))
    return pl.pallas_call(
        _slots_kernel,
        grid=(t // ts,),
        in_specs=[blk, blk, pl.BlockSpec((e, ts), lambda i: (0, 0))],
        out_specs=blk,
        out_shape=jax.ShapeDtypeStruct((TOP_K, t), jnp.int32),
        compiler_params=_cparams("arbitrary"),
        name="slots",
    )(idx_t, rank_t, ps)


def _invert_kernel(slot_ref, zero_hbm, inv_ref, sem, *, td):
    i = pl.program_id(0)

    @pl.when(i == 0)
    def _():
        fill = pltpu.make_async_copy(zero_hbm, inv_ref, sem)
        fill.start()
        fill.wait()

    base = i * td

    def body(r, c):
        for kk in range(TOP_K):
            inv_ref[slot_ref[kk, r]] = base + r
        return c
    lax.fori_loop(0, td, body, 0)


def _invert(slot_t, n_slots):
    _, t = slot_t.shape
    td = _tile(t, 512)
    return pl.pallas_call(
        functools.partial(_invert_kernel, td=td),
        grid=(t // td,),
        in_specs=[pl.BlockSpec((TOP_K, td), lambda i: (0, i), memory_space=pltpu.SMEM),
                  pl.BlockSpec(memory_space=pl.ANY)],
        out_specs=pl.BlockSpec((n_slots,), lambda i: (0,), memory_space=pltpu.SMEM),
        out_shape=jax.ShapeDtypeStruct((n_slots,), jnp.int32),
        scratch_shapes=[pltpu.SemaphoreType.DMA(())],
        compiler_params=_cparams("arbitrary"),
        name="invert",
    )(slot_t, jnp.zeros((n_slots,), jnp.int32))


def _row_copy(src, dst, sem):
    return pltpu.make_async_copy(src, dst, sem)


def _expert_kernel(be_ref, nxt_ref, par_ref, nv_ref, nu_ref, tok_ref, ntok_ref, h_hbm, wg_hbm, wu_hbm,
                   wd_hbm, ys_ref, xbuf, wgf, wuf, wdf, wgb, wub, wdb, sem, wsem, *, nch):
    i = pl.program_id(0)
    cur = i % 2
    nu = nu_ref[0]
    rows = EXPERT_ROWS

    def weight_copies(e, which):
        return (pltpu.make_async_copy(wg_hbm.at[e], wgf.at[which], wsem.at[which]),
                pltpu.make_async_copy(wu_hbm.at[e], wuf.at[which], wsem.at[which]),
                pltpu.make_async_copy(wd_hbm.at[e], wdf.at[which], wsem.at[which]))

    def fetch(t_ref, which, nvalid):
        for g in range(rows // ROW_GROUP):
            @pl.when(g * ROW_GROUP < nvalid)
            def _():
                for r in range(g * ROW_GROUP, (g + 1) * ROW_GROUP):
                    t = t_ref[0, 0, r]
                    _row_copy(h_hbm.at[pl.ds(pl.multiple_of(t * nch, nch), nch)],
                              xbuf.at[which, pl.ds(r * nch, nch)], sem.at[which]).start()

    def drain(which, nvalid):
        def body(g, c):
            for _ in range(ROW_GROUP):
                _row_copy(h_hbm.at[pl.ds(0, nch)], xbuf.at[which, pl.ds(0, nch)],
                          sem.at[which]).wait()
            return c
        lax.fori_loop(0, (nvalid + ROW_GROUP - 1) // ROW_GROUP, body, 0)

    @pl.when(i == 0)
    def _():
        xbuf[...] = jnp.zeros_like(xbuf)
        for cp in weight_copies(be_ref[0], 0):
            cp.start(priority=1)
        fetch(tok_ref, 0, nv_ref[0])

    @pl.when(i < nu)
    def _():
        fetch(ntok_ref, 1 - cur, jnp.where(i + 1 < nu, nv_ref[jnp.minimum(i + 1, nu - 1)], 0))
        prev = be_ref[jnp.maximum(i - 1, 0)]

        @pl.when(jnp.logical_or(i == 0, be_ref[i] != prev))
        def _():
            par = par_ref[i]
            for cp in weight_copies(be_ref[i], par):
                cp.wait()
            wgb[...] = wgf[par].astype(BF16)
            wub[...] = wuf[par].astype(BF16)
            wdb[...] = wdf[par].astype(BF16)

            @pl.when(nxt_ref[i] >= 0)
            def _():
                for cp in weight_copies(nxt_ref[i], 1 - par):
                    cp.start(priority=1)

        drain(cur, nv_ref[i])
        x = _load_row_tiles(xbuf.at[cur], rows, nch).astype(BF16)
        g = jnp.dot(x, wgb[...], preferred_element_type=F32)
        u = jnp.dot(x, wub[...], preferred_element_type=F32)
        a = (_silu(g) * u).astype(BF16)
        _store_row_tiles(ys_ref, jnp.dot(a, wdb[...], preferred_element_type=F32))

    @pl.when(i >= nu)
    def _():
        ys_ref[...] = jnp.zeros_like(ys_ref)


def _experts(blk_expert, blk_next, blk_par, blk_valid, nblk_used, tok_of_slot, h2t, w_g, w_u, w_d):
    d, f = w_g.shape[1], w_g.shape[2]
    nch = d // LANES
    rows = EXPERT_ROWS
    nblk = blk_expert.shape[0]
    toks = tok_of_slot.reshape(nblk, 1, rows)
    last = lambda i, nu: jnp.minimum(i, nu[0] - 1)
    hbm = pl.BlockSpec(memory_space=pl.ANY)
    return pl.pallas_call(
        functools.partial(_expert_kernel, nch=nch),
        grid_spec=pltpu.PrefetchScalarGridSpec(
            num_scalar_prefetch=5,
            grid=(nblk,),
            in_specs=[pl.BlockSpec((1, 1, rows), lambda i, be, nx, pa, nv, nu: (last(i, nu), 0, 0),
                                   memory_space=pltpu.SMEM),
                      pl.BlockSpec((1, 1, rows),
                                   lambda i, be, nx, pa, nv, nu: (last(i + 1, nu), 0, 0),
                                   memory_space=pltpu.SMEM),
                      hbm, hbm, hbm, hbm],
            out_specs=pl.BlockSpec((rows * nch, LANES), lambda i, be, nx, pa, nv, nu: (i, 0)),
            scratch_shapes=[pltpu.VMEM((2, rows * nch, LANES), F32),
                            pltpu.VMEM((2, d, f), F32), pltpu.VMEM((2, d, f), F32),
                            pltpu.VMEM((2, f, d), F32),
                            pltpu.VMEM((d, f), BF16), pltpu.VMEM((d, f), BF16),
                            pltpu.VMEM((f, d), BF16),
                            pltpu.SemaphoreType.DMA((2,)), pltpu.SemaphoreType.DMA((2,))]),
        out_shape=jax.ShapeDtypeStruct((nblk * rows * nch, LANES), F32),
        compiler_params=_cparams("arbitrary"),
        name="experts",
    )(blk_expert, blk_next, blk_par, blk_valid, nblk_used, toks, toks, h2t, w_g, w_u, w_d)


def _combine_kernel(slot_ref, nslot_ref, w_ref, x1_ref, h2_ref, mod_ref, wsg_ref, wsu_ref,
                    wsd_ref, ys_hbm, o_ref, buf, sem, *, tc, nsteps, nch):
    i = pl.program_id(0)
    cur = i % 2

    def issue(s_ref, which):
        def body(r, c):
            for kk in range(TOP_K):
                s = s_ref[kk, r]
                _row_copy(ys_hbm.at[pl.ds(pl.multiple_of(s * nch, nch), nch)],
                          buf.at[which, kk, pl.ds(pl.multiple_of(r * nch, nch), nch)],
                          sem.at[which]).start(priority=kk % 2)
            return c
        lax.fori_loop(0, tc, body, 0)

    @pl.when(i == 0)
    def _():
        issue(slot_ref, 0)

    @pl.when(i + 1 < nsteps)
    def _():
        issue(nslot_ref, 1 - cur)

    hb = _load_row_tiles(h2_ref, tc, nch).astype(BF16)
    g = jnp.dot(hb, wsg_ref[...], preferred_element_type=F32)
    u = jnp.dot(hb, wsu_ref[...], preferred_element_type=F32)
    acc = jnp.dot((_silu(g) * u).astype(BF16), wsd_ref[...], preferred_element_type=F32)

    def drain(r, c):
        for kk in range(TOP_K):
            _row_copy(ys_hbm.at[pl.ds(0, nch)], buf.at[cur, kk, pl.ds(0, nch)], sem.at[cur]).wait()
        return c
    lax.fori_loop(0, tc, drain, 0)

    w = w_ref[...]
    for kk in range(TOP_K):
        acc = acc + w[:, kk:kk + 1] * _load_row_tiles(buf.at[cur, kk], tc, nch)
    g2 = mod_ref[0][5:6]
    o_ref[...] = x1_ref[...] + g2 * acc


def _combine(slot_t, w_tk, x1, h2t, mod, w_sg, w_su, w_sd, ys, seq):
    t, d = x1.shape
    nch = d // LANES
    tc = _tile(seq, 128)
    tpb = seq // tc
    nsteps = t // tc
    full = lambda shp: pl.BlockSpec(shp, lambda i: (0,) * len(shp))
    tok = lambda w: pl.BlockSpec((tc, w), lambda i: (i, 0))
    return pl.pallas_call(
        functools.partial(_combine_kernel, tc=tc, nsteps=nsteps, nch=nch),
        grid=(nsteps,),
        in_specs=[pl.BlockSpec((TOP_K, tc), lambda i: (0, i), memory_space=pltpu.SMEM),
                  pl.BlockSpec((TOP_K, tc), lambda i: (0, jnp.minimum(i + 1, nsteps - 1)),
                               memory_space=pltpu.SMEM),
                  tok(TOP_K), tok(d),
                  pl.BlockSpec((tc * nch, LANES), lambda i: (i, 0)),
                  pl.BlockSpec((1, N_ADA, d), lambda i: (i // tpb, 0, 0)),
                  full(w_sg.shape), full(w_su.shape), full(w_sd.shape),
                  pl.BlockSpec(memory_space=pl.ANY)],
        out_specs=tok(d),
        out_shape=jax.ShapeDtypeStruct((t, d), F32),
        scratch_shapes=[pltpu.VMEM((2, TOP_K, tc * nch, LANES), F32),
                        pltpu.SemaphoreType.DMA((2,))],
        compiler_params=_cparams("arbitrary"),
        name="combine",
    )(slot_t, slot_t, w_tk, x1, h2t, mod, w_sg.astype(BF16), w_su.astype(BF16),
      w_sd.astype(BF16), ys)


def _layer(x, c, w_ada, b_ada, norm1_g, w_in, q_a_norm_g, w_uq, kv_a_norm_g, w_ukv,
           q_norm_g, k_norm_g, w_proj_attn, w_proj_fourier, w_out, norm2_g,
           w_router, router_bias, w_exp_gate, w_exp_up, w_exp_down,
           w_sh_gate, w_sh_up, w_sh_down):
    bsz, seq, d = x.shape
    t = bsz * seq
    e = w_router.shape[1]
    x2 = x.reshape(t, d)

    mod = _ada(c, w_ada, b_ada).reshape(bsz, N_ADA, d)
    q, k, v, zf, sa, sf = _inproj(x2, mod, norm1_g, w_in, q_a_norm_g, w_uq, kv_a_norm_g,
                                  w_ukv, q_norm_g, k_norm_g, bsz, seq)
    attn = _attention(q, k, v).reshape(t, N_HEADS * V_DIM)
    four = _fourier(zf.reshape(bsz, seq, zf.shape[1])).reshape(t, zf.shape[1])
    x1, h2, scores_t = _merge(attn, four, sa, sf, x2, mod, w_proj_attn, w_proj_fourier,
                              w_out, norm2_g, w_router, seq)

    idx_t, w_t, rank_t, cnt = _route(scores_t, router_bias)
    counts = cnt[:, 0].astype(jnp.int32)
    rows = EXPERT_ROWS
    nblk = -(-(t * TOP_K) // rows) + e
    padded = ((counts + rows - 1) // rows) * rows
    p_end = jnp.cumsum(padded)
    p_start = p_end - padded
    nblk_used = (p_end[-1] // rows).astype(jnp.int32)
    blk_start = jnp.arange(nblk, dtype=jnp.int32) * rows
    blk_first = jnp.minimum(blk_start, p_end[-1] - 1)
    hit = p_end[None, :] <= blk_first[:, None]
    blk_expert = jnp.clip(jnp.sum(hit.astype(jnp.int32), axis=1), 0, e - 1)

    slot_t = _slots(idx_t, rank_t, p_start.astype(jnp.int32))
    tok_of_slot = _invert(slot_t, nblk * rows)
    eid = jnp.arange(e, dtype=jnp.int32)
    cand = jnp.where(counts > 0, eid, e)
    nxt = jnp.concatenate([lax.cummin(cand[::-1])[::-1][1:], jnp.full((1,), e, jnp.int32)])
    nxt = jnp.where(nxt >= e, -1, nxt)
    run = jnp.cumsum((counts > 0).astype(jnp.int32)) - 1
    onehot = blk_expert[:, None] == eid[None, :]
    blk_next = jnp.sum(jnp.where(onehot, nxt[None, :], 0), axis=1).astype(jnp.int32)
    blk_par = (jnp.sum(jnp.where(onehot, run[None, :], 0), axis=1) % 2).astype(jnp.int32)
    seg_end = (p_start + counts).astype(jnp.int32)
    blk_end = jnp.sum(jnp.where(onehot, seg_end[None, :], 0), axis=1)
    blk_valid = jnp.clip(blk_end - blk_start, 0, rows).astype(jnp.int32)
    ys = _experts(blk_expert, blk_next, blk_par, blk_valid, nblk_used.reshape(1), tok_of_slot, h2,
                  w_exp_gate, w_exp_up, w_exp_down)
    out = _combine(slot_t, w_t.T, x1, h2, mod, w_sh_gate, w_sh_up, w_sh_down, ys, seq)
    return out.reshape(bsz, seq, d)


def kernel(x, c, w_ada, b_ada, norm1_g, w_in, q_a_norm_g, w_uq, kv_a_norm_g, w_ukv, q_norm_g,
           k_norm_g, w_proj_attn, w_proj_fourier, w_out, norm2_g, w_router, router_bias,
           w_exp_gate, w_exp_up, w_exp_down, w_sh_gate, w_sh_up, w_sh_down):
    for l in range(w_ada.shape[0]):
        x = _layer(x, c, w_ada[l], b_ada[l], norm1_g[l], w_in[l], q_a_norm_g[l], w_uq[l],
                   kv_a_norm_g[l], w_ukv[l], q_norm_g[l], k_norm_g[l], w_proj_attn[l],
                   w_proj_fourier[l], w_out[l], norm2_g[l], w_router[l], router_bias[l],
                   w_exp_gate[l], w_exp_up[l], w_exp_down[l], w_sh_gate[l], w_sh_up[l],
                   w_sh_down[l])
    return x
```

```python
import functools
import math

import numpy as np
import jax
import jax.numpy as jnp
from jax import lax
from jax.experimental import pallas as pl
from jax.experimental.pallas import tpu as pltpu

N_HEADS = 8
QK_NOPE = 64
QK_ROPE = 32
V_DIM = 64
FOURIER_GROUP = 64
TOP_K = 8
ROUTED_SCALE = 2.5
EPS = 1e-6
ROPE_THETA = 10000.0
N_ADA = 6

LANES = 128
EXPERT_ROWS = 256
ROW_GROUP = 16
VMEM_LIMIT = 48 * 1024 * 1024

F32 = jnp.float32
BF16 = jnp.bfloat16


def _cparams(*sem):
    return pltpu.CompilerParams(dimension_semantics=sem, vmem_limit_bytes=VMEM_LIMIT)


def _tile(n, pref):
    t = min(n, pref)
    assert n % t == 0, (n, pref)
    return t


def _silu(v):
    return v * jax.nn.sigmoid(v)


def _store_row_tiles(ref, val):
    rows, d = val.shape
    nch = d // LANES
    for j in range(nch):
        ref[pl.ds(j, rows, stride=nch), :] = val[:, j * LANES:(j + 1) * LANES]


def _load_row_tiles(ref, rows, nch):
    return jnp.concatenate([ref[pl.ds(j, rows, stride=nch), :] for j in range(nch)], axis=1)


def _ada_kernel(c_ref, w_ref, b_ref, o_ref):
    a = _silu(c_ref[...])
    o_ref[...] = jnp.dot(a, w_ref[...], preferred_element_type=F32,
                         precision=lax.Precision.HIGHEST) + b_ref[...]


def _ada(c, w_ada, b_ada):
    bsz, d = c.shape
    n = w_ada.shape[1]
    tn = _tile(n, d)
    return pl.pallas_call(
        _ada_kernel,
        grid=(n // tn,),
        in_specs=[pl.BlockSpec((bsz, d), lambda j: (0, 0)),
                  pl.BlockSpec((d, tn), lambda j: (0, j)),
                  pl.BlockSpec((1, tn), lambda j: (0, j))],
        out_specs=pl.BlockSpec((bsz, tn), lambda j: (0, j)),
        out_shape=jax.ShapeDtypeStruct((bsz, n), F32),
        compiler_params=_cparams("arbitrary"),
        name="ada",
    )(c, w_ada, b_ada.reshape(1, n))


def _head_norm_rope(t, trot, a, b):
    ms = jnp.sum(t * t, axis=-1, keepdims=True) * (1.0 / (QK_NOPE + QK_ROPE))
    return (t * a + trot * b) * lax.rsqrt(ms + EPS)


def _inproj_kernel(x_ref, mod_ref, g1_ref, wa_ref, wf_ref, wga_ref, wgf_ref,
                   gq_ref, gkv_ref, wuq_ref, wkv_ref,
                   aq_ref, bq_ref, ak_ref, bk_ref,
                   q_ref, k_ref, v_ref, zf_ref, sa_ref, sf_ref, *, ql, kvl):
    x = x_ref[...]
    mod = mod_ref[0]
    sh1, sc1 = mod[0:1], mod[1:2]
    r = lax.rsqrt(jnp.mean(x * x, axis=-1, keepdims=True) + EPS)
    h = (x * r * g1_ref[...]) * (1.0 + sc1) + sh1
    hb = h.astype(BF16)

    zf_ref[...] = jnp.dot(hb, wf_ref[...], preferred_element_type=F32).astype(BF16)
    sa_ref[...] = jax.nn.sigmoid(
        jnp.dot(hb, wga_ref[...], preferred_element_type=F32)).astype(BF16)
    sf_ref[...] = jax.nn.sigmoid(
        jnp.dot(hb, wgf_ref[...], preferred_element_type=F32)).astype(BF16)

    za = jnp.dot(hb, wa_ref[...], preferred_element_type=F32)
    zq = za[:, :ql]
    cq = zq * lax.rsqrt(jnp.mean(zq * zq, axis=-1, keepdims=True) + EPS) * gq_ref[...]
    qall = jnp.dot(cq.astype(BF16), wuq_ref[...], preferred_element_type=F32)

    zk = za[:, ql:]
    kvn = zk[:, :kvl]
    rk = lax.rsqrt(jnp.mean(kvn * kvn, axis=-1, keepdims=True) + EPS)
    lane = lax.broadcasted_iota(jnp.int32, zk.shape, 1)
    u = zk * jnp.where(lane < kvl, rk, 1.0) * gkv_ref[...]
    kvall = jnp.dot(u.astype(BF16), wkv_ref[...], preferred_element_type=F32)

    aq, bq, ak, bk = aq_ref[...], bq_ref[...], ak_ref[...], bk_ref[...]
    hw = N_HEADS * LANES
    for hd in range(N_HEADS):
        lo, hi = hd * LANES, (hd + 1) * LANES
        q_ref[0, hd] = _head_norm_rope(qall[:, lo:hi], qall[:, hw + lo:hw + hi],
                                       aq, bq).astype(BF16)
        k_ref[0, hd] = _head_norm_rope(kvall[:, lo:hi], kvall[:, hw + lo:hw + hi],
                                       ak, bk).astype(BF16)
        v_ref[0, hd] = kvall[:, 2 * hw + hd * V_DIM: 2 * hw + (hd + 1) * V_DIM].astype(BF16)


def _rope_tables(seq):
    half = QK_ROPE // 2
    pos = np.arange(seq, dtype=np.float64)
    inv = ROPE_THETA ** (-np.arange(0, QK_ROPE, 2, dtype=np.float64) / QK_ROPE)
    ang = pos[:, None] * inv[None, :]
    c, s = np.cos(ang), np.sin(ang)
    cos = np.ones((seq, LANES)); sin = np.zeros((seq, LANES))
    cos[:, QK_NOPE:QK_NOPE + half] = c
    cos[:, QK_NOPE + half:QK_NOPE + QK_ROPE] = c
    sin[:, QK_NOPE:QK_NOPE + half] = -s
    sin[:, QK_NOPE + half:QK_NOPE + QK_ROPE] = s
    return jnp.asarray(cos, F32), jnp.asarray(sin, F32)


def _partner_columns(w):
    half = QK_ROPE // 2
    lo, mid, hi = QK_NOPE, QK_NOPE + half, QK_NOPE + QK_ROPE
    z = jnp.zeros_like(w)
    return jnp.concatenate([z[..., :lo], w[..., mid:hi], w[..., lo:mid], z[..., hi:]], axis=-1)


def _inproj(x2, mod, norm1_g, w_in, q_a_g, w_uq, kv_a_g, w_ukv, q_g, k_g, bsz, seq):
    t, d = x2.shape
    ql, kvl = q_a_g.shape[0], kv_a_g.shape[0]
    hq = QK_NOPE + QK_ROPE
    fw = w_in.shape[1] - ql - kvl - QK_ROPE - 2 * d
    o1, o2, o3, o4, o5 = ql, ql + kvl, ql + kvl + QK_ROPE, ql + kvl + QK_ROPE + fw, \
        ql + kvl + QK_ROPE + fw + d
    assert ql % LANES == 0 and kvl % LANES == 0

    wa = jnp.concatenate([w_in[:, :o3], jnp.zeros((d, LANES - QK_ROPE), F32)], axis=1).astype(BF16)
    wf = w_in[:, o3:o4].astype(BF16)
    wga = w_in[:, o4:o5].astype(BF16)
    wgf = w_in[:, o5:].astype(BF16)

    wuq = w_uq.reshape(ql, N_HEADS, hq)
    wuq = jnp.pad(wuq, ((0, 0), (0, 0), (0, LANES - hq)))
    wuq = jnp.concatenate([wuq.reshape(ql, N_HEADS * LANES),
                           _partner_columns(wuq).reshape(ql, N_HEADS * LANES)], axis=1).astype(BF16)
    wukv = w_ukv.reshape(kvl, N_HEADS, QK_NOPE + V_DIM)
    wk = jnp.pad(wukv[:, :, :QK_NOPE], ((0, 0), (0, 0), (0, LANES - QK_NOPE)))
    place = jnp.zeros((QK_ROPE, N_HEADS, LANES), F32)
    place = place.at[jnp.arange(QK_ROPE), :, QK_NOPE + jnp.arange(QK_ROPE)].set(1.0)
    wk = jnp.concatenate([wk, place, jnp.zeros((LANES - QK_ROPE, N_HEADS, LANES), F32)], axis=0)
    wv = jnp.concatenate([wukv[:, :, QK_NOPE:], jnp.zeros((LANES, N_HEADS, V_DIM), F32)], axis=0)
    wkv = jnp.concatenate([wk.reshape(kvl + LANES, N_HEADS * LANES),
                           _partner_columns(wk).reshape(kvl + LANES, N_HEADS * LANES),
                           wv.reshape(kvl + LANES, N_HEADS * V_DIM)], axis=1).astype(BF16)

    gkv = jnp.concatenate([kv_a_g, jnp.ones((LANES,), F32)]).reshape(1, kvl + LANES)
    pad = jnp.zeros((LANES - hq,), F32)
    qg = jnp.concatenate([q_g * (hq ** -0.5), pad])
    kg = jnp.concatenate([k_g, pad])
    cos, sin = _rope_tables(seq)
    aq, bq = qg[None, :] * cos, _partner_columns(qg)[None, :] * sin
    ak, bk = kg[None, :] * cos, _partner_columns(kg)[None, :] * sin

    tm = _tile(seq, 512)
    tpb = seq // tm
    full = lambda shp: pl.BlockSpec(shp, lambda i: (0,) * len(shp))
    tok = lambda w: pl.BlockSpec((tm, w), lambda i: (i, 0))
    head = lambda w: pl.BlockSpec((1, N_HEADS, tm, w), lambda i: (i // tpb, 0, i % tpb, 0))
    rope = pl.BlockSpec((tm, LANES), lambda i: (i % tpb, 0))
    return pl.pallas_call(
        functools.partial(_inproj_kernel, ql=ql, kvl=kvl),
        grid=(t // tm,),
        in_specs=[tok(d),
                  pl.BlockSpec((1, N_ADA, d), lambda i: (i // tpb, 0, 0)),
                  full((1, d)), full(wa.shape), full(wf.shape), full(wga.shape), full(wgf.shape),
                  full((1, ql)), full(gkv.shape), full(wuq.shape), full(wkv.shape),
                  rope, rope, rope, rope],
        out_specs=[head(LANES), head(LANES), head(V_DIM), tok(fw), tok(d), tok(d)],
        out_shape=[jax.ShapeDtypeStruct((bsz, N_HEADS, seq, LANES), BF16),
                   jax.ShapeDtypeStruct((bsz, N_HEADS, seq, LANES), BF16),
                   jax.ShapeDtypeStruct((bsz, N_HEADS, seq, V_DIM), BF16),
                   jax.ShapeDtypeStruct((t, fw), BF16),
                   jax.ShapeDtypeStruct((t, d), BF16),
                   jax.ShapeDtypeStruct((t, d), BF16)],
        compiler_params=_cparams("arbitrary"),
        name="inproj",
    )(x2, mod, norm1_g.reshape(1, d), wa, wf, wga, wgf, q_a_g.reshape(1, ql), gkv, wuq, wkv,
      aq, bq, ak, bk)


def _attn_kernel(q_ref, k_ref, v_ref, o_ref):
    for hd in range(N_HEADS):
        s = lax.dot_general(q_ref[0, hd], k_ref[0, hd], (((1,), (1,)), ((), ())),
                            preferred_element_type=F32)
        m = jnp.max(s, axis=-1, keepdims=True)
        p = jnp.exp(s - m)
        l = jnp.sum(p, axis=-1, keepdims=True)
        o = jnp.dot(p.astype(BF16), v_ref[0, hd], preferred_element_type=F32)
        o_ref[0, :, hd * V_DIM:(hd + 1) * V_DIM] = (o / l).astype(BF16)


def _attention(q, k, v):
    bsz, _, seq, _ = q.shape
    tq = _tile(seq, 256)
    return pl.pallas_call(
        _attn_kernel,
        grid=(bsz, seq // tq),
        in_specs=[pl.BlockSpec((1, N_HEADS, tq, LANES), lambda b, j: (b, 0, j, 0)),
                  pl.BlockSpec((1, N_HEADS, seq, LANES), lambda b, j: (b, 0, 0, 0)),
                  pl.BlockSpec((1, N_HEADS, seq, V_DIM), lambda b, j: (b, 0, 0, 0))],
        out_specs=pl.BlockSpec((1, tq, N_HEADS * V_DIM), lambda b, j: (b, j, 0)),
        out_shape=jax.ShapeDtypeStruct((bsz, seq, N_HEADS * V_DIM), BF16),
        compiler_params=_cparams("arbitrary", "arbitrary"),
        name="attn",
    )(q, k, v)


def _fourier_kernel(z_ref, wc_ref, ws_ref, tab_ref, o_ref, u_ref, *, seq):
    @pl.when(pl.program_id(1) == 0)
    def _():
        z = z_ref[0]
        u_ref[:seq, :] = jnp.dot(z, wc_ref[...], preferred_element_type=F32).astype(BF16)
        u_ref[seq:, :] = jnp.dot(z, ws_ref[...], preferred_element_type=F32).astype(BF16)

    o_ref[0] = jnp.dot(tab_ref[...], u_ref[...], preferred_element_type=F32).astype(BF16)


def _fourier_tables(seq, fw):
    g = FOURIER_GROUP
    n = np.arange(seq, dtype=np.int64)
    ang = 2.0 * np.pi * ((n[:, None] * n[None, :]) % seq).astype(np.float64) / seq
    tab = np.concatenate([np.cos(ang), -np.sin(ang)], axis=1)
    c = np.arange(g, dtype=np.int64)
    angc = 2.0 * np.pi * ((c[:, None] * c[None, :]) % g).astype(np.float64) / g
    scale = 1.0 / math.sqrt(seq * g)
    eye = np.eye(fw // g)
    wc = np.kron(eye, np.cos(angc) * scale)
    ws = np.kron(eye, np.sin(angc) * scale)
    return (jnp.asarray(tab, F32).astype(BF16), jnp.asarray(wc, F32).astype(BF16),
            jnp.asarray(ws, F32).astype(BF16))


def _fourier(zf):
    bsz, seq, fw = zf.shape
    tab, wc, ws = _fourier_tables(seq, fw)
    tr = _tile(seq, 256)
    return pl.pallas_call(
        functools.partial(_fourier_kernel, seq=seq),
        grid=(bsz, seq // tr),
        in_specs=[pl.BlockSpec((1, seq, fw), lambda b, j: (b, 0, 0)),
                  pl.BlockSpec((fw, fw), lambda b, j: (0, 0)),
                  pl.BlockSpec((fw, fw), lambda b, j: (0, 0)),
                  pl.BlockSpec((tr, 2 * seq), lambda b, j: (j, 0))],
        out_specs=pl.BlockSpec((1, tr, fw), lambda b, j: (b, j, 0)),
        out_shape=jax.ShapeDtypeStruct((bsz, seq, fw), BF16),
        scratch_shapes=[pltpu.VMEM((2 * seq, fw), BF16)],
        compiler_params=_cparams("arbitrary", "arbitrary"),
        name="fourier",
    )(zf, wc, ws, tab)


def _merge_kernel(a_ref, f_ref, sa_ref, sf_ref, x_ref, mod_ref, wpa_ref, wpf_ref, wo_ref,
                  g2_ref, wrh_ref, wrl_ref, x1_ref, h2_ref, sc_ref):
    ya = jnp.dot(a_ref[...], wpa_ref[...], preferred_element_type=F32)
    yf = jnp.dot(f_ref[...], wpf_ref[...], preferred_element_type=F32)
    merged = sa_ref[...].astype(F32) * ya + sf_ref[...].astype(F32) * yf
    mod = mod_ref[0]
    g1, sh2, sc2 = mod[2:3], mod[3:4], mod[4:5]
    x1 = x_ref[...] + g1 * jnp.dot(merged.astype(BF16), wo_ref[...], preferred_element_type=F32)
    x1_ref[...] = x1
    r = lax.rsqrt(jnp.mean(x1 * x1, axis=-1, keepdims=True) + EPS)
    h2 = (x1 * r * g2_ref[...]) * (1.0 + sc2) + sh2
    _store_row_tiles(h2_ref, h2)
    hh = h2.astype(BF16)
    hl = (h2 - hh.astype(F32)).astype(BF16)
    nt = (((1,), (1,)), ((), ()))
    lt = (lax.dot_general(wrh_ref[...], hh, nt, preferred_element_type=F32)
          + lax.dot_general(wrh_ref[...], hl, nt, preferred_element_type=F32)
          + lax.dot_general(wrl_ref[...], hh, nt, preferred_element_type=F32))
    sc_ref[...] = jax.nn.sigmoid(lt)


def _merge(attn, four, sa, sf, x2, mod, w_pa, w_pf, w_out, norm2_g, w_router, seq):
    t, d = x2.shape
    e = w_router.shape[1]
    wrt = w_router.T
    wrh = wrt.astype(BF16)
    wrl = (wrt - wrh.astype(F32)).astype(BF16)
    tm = _tile(seq, 512)
    tpb = seq // tm
    full = lambda shp: pl.BlockSpec(shp, lambda i: (0,) * len(shp))
    tok = lambda w: pl.BlockSpec((tm, w), lambda i: (i, 0))
    return pl.pallas_call(
        _merge_kernel,
        grid=(t // tm,),
        in_specs=[tok(attn.shape[1]), tok(four.shape[1]), tok(d), tok(d), tok(d),
                  pl.BlockSpec((1, N_ADA, d), lambda i: (i // tpb, 0, 0)),
                  full(w_pa.shape), full(w_pf.shape), full(w_out.shape), full((1, d)),
                  full((e, d)), full((e, d))],
        out_specs=[tok(d), pl.BlockSpec((tm * (d // LANES), LANES), lambda i: (i, 0)),
                   pl.BlockSpec((e, tm), lambda i: (0, i))],
        out_shape=[jax.ShapeDtypeStruct((t, d), F32),
                   jax.ShapeDtypeStruct((t * (d // LANES), LANES), F32),
                   jax.ShapeDtypeStruct((e, t), F32)],
        compiler_params=_cparams("arbitrary"),
        name="merge",
    )(attn, four, sa, sf, x2, mod, w_pa.astype(BF16), w_pf.astype(BF16), w_out.astype(BF16),
      norm2_g.reshape(1, d), wrh, wrl)


def _route_kernel(s_ref, b_ref, tri_ref, idx_ref, w_ref, rank_ref, cnt_ref, carry_ref):
    @pl.when(pl.program_id(0) == 0)
    def _():
        carry_ref[...] = jnp.zeros_like(carry_ref)

    sc = s_ref[...]
    e, tr = sc.shape
    row = lax.broadcasted_iota(jnp.int32, (e, tr), 0)
    v = sc + b_ref[...]
    sel = jnp.zeros((e, tr), F32)
    idxs, ws = [], []
    for _ in range(TOP_K):
        m = jnp.max(v, axis=0, keepdims=True)
        idx = jnp.min(jnp.where(v == m, row, e), axis=0, keepdims=True)
        oh = row == idx
        ws.append(jnp.sum(jnp.where(oh, sc, 0.0), axis=0, keepdims=True))
        idxs.append(idx)
        v = jnp.where(oh, -jnp.inf, v)
        sel = sel + oh.astype(F32)
    wsum = ws[0]
    for w in ws[1:]:
        wsum = wsum + w
    selb = sel.astype(BF16)
    cum = jnp.dot(selb, tri_ref[...], preferred_element_type=F32) + carry_ref[...]
    for kk in range(TOP_K):
        oh = row == idxs[kk]
        rk = jnp.sum(jnp.where(oh, cum, 0.0), axis=0, keepdims=True)
        idx_ref[kk:kk + 1, :] = idxs[kk]
        rank_ref[kk:kk + 1, :] = rk.astype(jnp.int32)
        w_ref[kk:kk + 1, :] = ws[kk] / wsum * ROUTED_SCALE
    tot = carry_ref[...] + jnp.dot(selb, jnp.ones((tr, tr), BF16), preferred_element_type=F32)
    carry_ref[...] = tot
    cnt_ref[...] = tot


def _route(scores_t, router_bias):
    e, t = scores_t.shape
    tr = _tile(t, 256)
    tri = jnp.asarray(np.triu(np.ones((tr, tr), np.float32), 1), BF16)
    bias = jnp.broadcast_to(router_bias.reshape(e, 1), (e, tr)).astype(F32)
    blk = pl.BlockSpec((TOP_K, tr), lambda i: (0, i))
    return pl.pallas_call(
        _route_kernel,
        grid=(t // tr,),
        in_specs=[pl.BlockSpec((e, tr), lambda i: (0, i)),
                  pl.BlockSpec((e, tr), lambda i: (0, 0)),
                  pl.BlockSpec((tr, tr), lambda i: (0, 0))],
        out_specs=[blk, blk, blk, pl.BlockSpec((e, tr), lambda i: (0, 0))],
        out_shape=[jax.ShapeDtypeStruct((TOP_K, t), jnp.int32),
                   jax.ShapeDtypeStruct((TOP_K, t), F32),
                   jax.ShapeDtypeStruct((TOP_K, t), jnp.int32),
                   jax.ShapeDtypeStruct((e, tr), F32)],
        scratch_shapes=[pltpu.VMEM((e, tr), F32)],
        compiler_params=_cparams("arbitrary"),
        name="route",
    )(scores_t, bias, tri)


def _slots_kernel(idx_ref, rank_ref, ps_ref, slot_ref):
    ps = ps_ref[...]
    row = lax.broadcasted_iota(jnp.int32, ps.shape, 0)
    for kk in range(TOP_K):
        oh = row == idx_ref[kk:kk + 1, :]
        start = jnp.sum(jnp.where(oh, ps, 0), axis=0, keepdims=True)
        slot_ref[kk:kk + 1, :] = start + rank_ref[kk:kk + 1, :]


def _slots(idx_t, rank_t, p_start):
    _, t = idx_t.shape
    e = p_start.shape[0]
    ts = _tile(t, 512)
    ps = jnp.broadcast_to(p_start.reshape(e, 1), (e, ts))
    blk = pl.BlockSpec((TOP_K, ts), lambda i: (0, i))
    return pl.pallas_call(
        _slots_kernel,
        grid=(t // ts,),
        in_specs=[blk, blk, pl.BlockSpec((e, ts), lambda i: (0, 0))],
        out_specs=blk,
        out_shape=jax.ShapeDtypeStruct((TOP_K, t), jnp.int32),
        compiler_params=_cparams("arbitrary"),
        name="slots",
    )(idx_t, rank_t, ps)


def _invert_kernel(slot_ref, zero_hbm, inv_ref, sem, *, td):
    i = pl.program_id(0)

    @pl.when(i == 0)
    def _():
        fill = pltpu.make_async_copy(zero_hbm, inv_ref, sem)
        fill.start()
        fill.wait()

    base = i * td

    def body(r, c):
        for kk in range(TOP_K):
            inv_ref[slot_ref[kk, r]] = base + r
        return c
    lax.fori_loop(0, td, body, 0)


def _invert(slot_t, n_slots):
    _, t = slot_t.shape
    td = _tile(t, 512)
    return pl.pallas_call(
        functools.partial(_invert_kernel, td=td),
        grid=(t // td,),
        in_specs=[pl.BlockSpec((TOP_K, td), lambda i: (0, i), memory_space=pltpu.SMEM),
                  pl.BlockSpec(memory_space=pl.ANY)],
        out_specs=pl.BlockSpec((n_slots,), lambda i: (0,), memory_space=pltpu.SMEM),
        out_shape=jax.ShapeDtypeStruct((n_slots,), jnp.int32),
        scratch_shapes=[pltpu.SemaphoreType.DMA(())],
        compiler_params=_cparams("arbitrary"),
        name="invert",
    )(slot_t, jnp.zeros((n_slots,), jnp.int32))


def _row_copy(src, dst, sem):
    return pltpu.make_async_copy(src, dst, sem)


def _expert_kernel(be_ref, nxt_ref, par_ref, nv_ref, nu_ref, tok_ref, ntok_ref, h_hbm, wg_hbm, wu_hbm,
                   wd_hbm, ys_ref, xbuf, wgf, wuf, wdf, wgb, wub, wdb, sem, wsem, *, nch):
    i = pl.program_id(0)
    cur = i % 2
    nu = nu_ref[0]
    rows = EXPERT_ROWS

    def weight_copies(e, which):
        return (pltpu.make_async_copy(wg_hbm.at[e], wgf.at[which], wsem.at[which]),
                pltpu.make_async_copy(wu_hbm.at[e], wuf.at[which], wsem.at[which]),
                pltpu.make_async_copy(wd_hbm.at[e], wdf.at[which], wsem.at[which]))

    def fetch(t_ref, which, nvalid):
        for g in range(rows // ROW_GROUP):
            @pl.when(g * ROW_GROUP < nvalid)
            def _():
                for r in range(g * ROW_GROUP, (g + 1) * ROW_GROUP):
                    t = t_ref[0, 0, r]
                    _row_copy(h_hbm.at[pl.ds(pl.multiple_of(t * nch, nch), nch)],
                              xbuf.at[which, pl.ds(r * nch, nch)], sem.at[which]).start()

    def drain(which, nvalid):
        def body(g, c):
            for _ in range(ROW_GROUP):
                _row_copy(h_hbm.at[pl.ds(0, nch)], xbuf.at[which, pl.ds(0, nch)],
                          sem.at[which]).wait()
            return c
        lax.fori_loop(0, (nvalid + ROW_GROUP - 1) // ROW_GROUP, body, 0)

    @pl.when(i == 0)
    def _():
        xbuf[...] = jnp.zeros_like(xbuf)
        for cp in weight_copies(be_ref[0], 0):
            cp.start(priority=1)
        fetch(tok_ref, 0, nv_ref[0])

    @pl.when(i < nu)
    def _():
        fetch(ntok_ref, 1 - cur, jnp.where(i + 1 < nu, nv_ref[jnp.minimum(i + 1, nu - 1)], 0))
        prev = be_ref[jnp.maximum(i - 1, 0)]

        @pl.when(jnp.logical_or(i == 0, be_ref[i] != prev))
        def _():
            par = par_ref[i]
            for cp in weight_copies(be_ref[i], par):
                cp.wait()
            wgb[...] = wgf[par].astype(BF16)
            wub[...] = wuf[par].astype(BF16)
            wdb[...] = wdf[par].astype(BF16)

            @pl.when(nxt_ref[i] >= 0)
            def _():
                for cp in weight_copies(nxt_ref[i], 1 - par):
                    cp.start(priority=1)

        drain(cur, nv_ref[i])
        x = _load_row_tiles(xbuf.at[cur], rows, nch).astype(BF16)
        g = jnp.dot(x, wgb[...], preferred_element_type=F32)
        u = jnp.dot(x, wub[...], preferred_element_type=F32)
        a = (_silu(g) * u).astype(BF16)
        _store_row_tiles(ys_ref, jnp.dot(a, wdb[...], preferred_element_type=F32))

    @pl.when(i >= nu)
    def _():
        ys_ref[...] = jnp.zeros_like(ys_ref)


def _experts(blk_expert, blk_next, blk_par, blk_valid, nblk_used, tok_of_slot, h2t, w_g, w_u, w_d):
    d, f = w_g.shape[1], w_g.shape[2]
    nch = d // LANES
    rows = EXPERT_ROWS
    nblk = blk_expert.shape[0]
    toks = tok_of_slot.reshape(nblk, 1, rows)
    last = lambda i, nu: jnp.minimum(i, nu[0] - 1)
    hbm = pl.BlockSpec(memory_space=pl.ANY)
    return pl.pallas_call(
        functools.partial(_expert_kernel, nch=nch),
        grid_spec=pltpu.PrefetchScalarGridSpec(
            num_scalar_prefetch=5,
            grid=(nblk,),
            in_specs=[pl.BlockSpec((1, 1, rows), lambda i, be, nx, pa, nv, nu: (last(i, nu), 0, 0),
                                   memory_space=pltpu.SMEM),
                      pl.BlockSpec((1, 1, rows),
                                   lambda i, be, nx, pa, nv, nu: (last(i + 1, nu), 0, 0),
                                   memory_space=pltpu.SMEM),
                      hbm, hbm, hbm, hbm],
            out_specs=pl.BlockSpec((rows * nch, LANES), lambda i, be, nx, pa, nv, nu: (i, 0)),
            scratch_shapes=[pltpu.VMEM((2, rows * nch, LANES), F32),
                            pltpu.VMEM((2, d, f), F32), pltpu.VMEM((2, d, f), F32),
                            pltpu.VMEM((2, f, d), F32),
                            pltpu.VMEM((d, f), BF16), pltpu.VMEM((d, f), BF16),
                            pltpu.VMEM((f, d), BF16),
                            pltpu.SemaphoreType.DMA((2,)), pltpu.SemaphoreType.DMA((2,))]),
        out_shape=jax.ShapeDtypeStruct((nblk * rows * nch, LANES), F32),
        compiler_params=_cparams("arbitrary"),
        name="experts",
    )(blk_expert, blk_next, blk_par, blk_valid, nblk_used, toks, toks, h2t, w_g, w_u, w_d)


def _combine_kernel(slot_ref, nslot_ref, w_ref, x1_ref, h2_ref, mod_ref, wsg_ref, wsu_ref,
                    wsd_ref, ys_hbm, o_ref, buf, sem, *, tc, nsteps, nch):
    i = pl.program_id(0)
    cur = i % 2

    def issue(s_ref, which):
        def body(r, c):
            for kk in range(TOP_K):
                s = s_ref[kk, r]
                _row_copy(ys_hbm.at[pl.ds(pl.multiple_of(s * nch, nch), nch)],
                          buf.at[which, kk, pl.ds(pl.multiple_of(r * nch, nch), nch)],
                          sem.at[which]).start(priority=kk % 2)
            return c
        lax.fori_loop(0, tc, body, 0)

    @pl.when(i == 0)
    def _():
        issue(slot_ref, 0)

    @pl.when(i + 1 < nsteps)
    def _():
        issue(nslot_ref, 1 - cur)

    hb = _load_row_tiles(h2_ref, tc, nch).astype(BF16)
    g = jnp.dot(hb, wsg_ref[...], preferred_element_type=F32)
    u = jnp.dot(hb, wsu_ref[...], preferred_element_type=F32)
    acc = jnp.dot((_silu(g) * u).astype(BF16), wsd_ref[...], preferred_element_type=F32)

    def drain(r, c):
        for kk in range(TOP_K):
            _row_copy(ys_hbm.at[pl.ds(0, nch)], buf.at[cur, kk, pl.ds(0, nch)], sem.at[cur]).wait()
        return c
    lax.fori_loop(0, tc, drain, 0)

    w = w_ref[...]
    for kk in range(TOP_K):
        acc = acc + w[:, kk:kk + 1] * _load_row_tiles(buf.at[cur, kk], tc, nch)
    g2 = mod_ref[0][5:6]
    o_ref[...] = x1_ref[...] + g2 * acc


def _combine(slot_t, w_tk, x1, h2t, mod, w_sg, w_su, w_sd, ys, seq):
    t, d = x1.shape
    nch = d // LANES
    tc = _tile(seq, 128)
    tpb = seq // tc
    nsteps = t // tc
    full = lambda shp: pl.BlockSpec(shp, lambda i: (0,) * len(shp))
    tok = lambda w: pl.BlockSpec((tc, w), lambda i: (i, 0))
    return pl.pallas_call(
        functools.partial(_combine_kernel, tc=tc, nsteps=nsteps, nch=nch),
        grid=(nsteps,),
        in_specs=[pl.BlockSpec((TOP_K, tc), lambda i: (0, i), memory_space=pltpu.SMEM),
                  pl.BlockSpec((TOP_K, tc), lambda i: (0, jnp.minimum(i + 1, nsteps - 1)),
                               memory_space=pltpu.SMEM),
                  tok(TOP_K), tok(d),
                  pl.BlockSpec((tc * nch, LANES), lambda i: (i, 0)),
                  pl.BlockSpec((1, N_ADA, d), lambda i: (i // tpb, 0, 0)),
                  full(w_sg.shape), full(w_su.shape), full(w_sd.shape),
                  pl.BlockSpec(memory_space=pl.ANY)],
        out_specs=tok(d),
        out_shape=jax.ShapeDtypeStruct((t, d), F32),
        scratch_shapes=[pltpu.VMEM((2, TOP_K, tc * nch, LANES), F32),
                        pltpu.SemaphoreType.DMA((2,))],
        compiler_params=_cparams("arbitrary"),
        name="combine",
    )(slot_t, slot_t, w_tk, x1, h2t, mod, w_sg.astype(BF16), w_su.astype(BF16),
      w_sd.astype(BF16), ys)


def _layer(x, c, w_ada, b_ada, norm1_g, w_in, q_a_norm_g, w_uq, kv_a_norm_g, w_ukv,
           q_norm_g, k_norm_g, w_proj_attn, w_proj_fourier, w_out, norm2_g,
           w_router, router_bias, w_exp_gate, w_exp_up, w_exp_down,
           w_sh_gate, w_sh_up, w_sh_down):
    bsz, seq, d = x.shape
    t = bsz * seq
    e = w_router.shape[1]
    x2 = x.reshape(t, d)

    mod = _ada(c, w_ada, b_ada).reshape(bsz, N_ADA, d)
    q, k, v, zf, sa, sf = _inproj(x2, mod, norm1_g, w_in, q_a_norm_g, w_uq, kv_a_norm_g,
                                  w_ukv, q_norm_g, k_norm_g, bsz, seq)
    attn = _attention(q, k, v).reshape(t, N_HEADS * V_DIM)
    four = _fourier(zf.reshape(bsz, seq, zf.shape[1])).reshape(t, zf.shape[1])
    x1, h2, scores_t = _merge(attn, four, sa, sf, x2, mod, w_proj_attn, w_proj_fourier,
                              w_out, norm2_g, w_router, seq)

    idx_t, w_t, rank_t, cnt = _route(scores_t, router_bias)
    counts = cnt[:, 0].astype(jnp.int32)
    rows = EXPERT_ROWS
    nblk = -(-(t * TOP_K) // rows) + e
    padded = ((counts + rows - 1) // rows) * rows
    p_end = jnp.cumsum(padded)
    p_start = p_end - padded
    nblk_used = (p_end[-1] // rows).astype(jnp.int32)
    blk_start = jnp.arange(nblk, dtype=jnp.int32) * rows
    blk_first = jnp.minimum(blk_start, p_end[-1] - 1)
    hit = p_end[None, :] <= blk_first[:, None]
    blk_expert = jnp.clip(jnp.sum(hit.astype(jnp.int32), axis=1), 0, e - 1)

    slot_t = _slots(idx_t, rank_t, p_start.astype(jnp.int32))
    tok_of_slot = _invert(slot_t, nblk * rows)
    eid = jnp.arange(e, dtype=jnp.int32)
    cand = jnp.where(counts > 0, eid, e)
    nxt = jnp.concatenate([lax.cummin(cand[::-1])[::-1][1:], jnp.full((1,), e, jnp.int32)])
    nxt = jnp.where(nxt >= e, -1, nxt)
    run = jnp.cumsum((counts > 0).astype(jnp.int32)) - 1
    onehot = blk_expert[:, None] == eid[None, :]
    blk_next = jnp.sum(jnp.where(onehot, nxt[None, :], 0), axis=1).astype(jnp.int32)
    blk_par = (jnp.sum(jnp.where(onehot, run[None, :], 0), axis=1) % 2).astype(jnp.int32)
    seg_end = (p_start + counts).astype(jnp.int32)
    blk_end = jnp.sum(jnp.where(onehot, seg_end[None, :], 0), axis=1)
    blk_valid = jnp.clip(blk_end - blk_start, 0, rows).astype(jnp.int32)
    ys = _experts(blk_expert, blk_next, blk_par, blk_valid, nblk_used.reshape(1), tok_of_slot, h2,
                  w_exp_gate, w_exp_up, w_exp_down)
    out = _combine(slot_t, w_t.T, x1, h2, mod, w_sh_gate, w_sh_up, w_sh_down, ys, seq)
    return out.reshape(bsz, seq, d)


def kernel(x, c, w_ada, b_ada, norm1_g, w_in, q_a_norm_g, w_uq, kv_a_norm_g, w_ukv, q_norm_g,
           k_norm_g, w_proj_attn, w_proj_fourier, w_out, norm2_g, w_router, router_bias,
           w_exp_gate, w_exp_up, w_exp_down, w_sh_gate, w_sh_up, w_sh_down):
    for l in range(w_ada.shape[0]):
        x = _layer(x, c, w_ada[l], b_ada[l], norm1_g[l], w_in[l], q_a_norm_g[l], w_uq[l],
                   kv_a_norm_g[l], w_ukv[l], q_norm_g[l], k_norm_g[l], w_proj_attn[l],
                   w_proj_fourier[l], w_out[l], norm2_g[l], w_router[l], router_bias[l],
                   w_exp_gate[l], w_exp_up[l], w_exp_down[l], w_sh_gate[l], w_sh_up[l],
                   w_sh_down[l])
    return x
```

```python
import functools
import math

import numpy as np
import jax
import jax.numpy as jnp
from jax import lax
from jax.experimental import pallas as pl
from jax.experimental.pallas import tpu as pltpu
from jax.experimental.pallas import tpu_sc as plsc

N_HEADS = 8
QK_NOPE = 64
QK_ROPE = 32
V_DIM = 64
FOURIER_GROUP = 64
TOP_K = 8
ROUTED_SCALE = 2.5
EPS = 1e-6
ROPE_THETA = 10000.0
N_ADA = 6

LANES = 128
EXPERT_ROWS = 256
ROW_GROUP = 16
SC_WINDOW = 128
VMEM_LIMIT = 48 * 1024 * 1024

F32 = jnp.float32
BF16 = jnp.bfloat16


def _cparams(*sem):
    return pltpu.CompilerParams(dimension_semantics=sem, vmem_limit_bytes=VMEM_LIMIT)


def _tile(n, pref):
    t = min(n, pref)
    assert n % t == 0, (n, pref)
    return t


def _silu(v):
    return v * jax.nn.sigmoid(v)


def _store_row_tiles(ref, val):
    rows, d = val.shape
    nch = d // LANES
    for j in range(nch):
        ref[pl.ds(j, rows, stride=nch), :] = val[:, j * LANES:(j + 1) * LANES]


def _load_row_tiles(ref, rows, nch):
    return jnp.concatenate([ref[pl.ds(j, rows, stride=nch), :] for j in range(nch)], axis=1)


def _ada_kernel(c_ref, w_ref, b_ref, o_ref):
    a = _silu(c_ref[...])
    o_ref[...] = jnp.dot(a, w_ref[...], preferred_element_type=F32,
                         precision=lax.Precision.HIGHEST) + b_ref[...]


def _ada(c, w_ada, b_ada):
    bsz, d = c.shape
    n = w_ada.shape[1]
    tn = _tile(n, d)
    return pl.pallas_call(
        _ada_kernel,
        grid=(n // tn,),
        in_specs=[pl.BlockSpec((bsz, d), lambda j: (0, 0)),
                  pl.BlockSpec((d, tn), lambda j: (0, j)),
                  pl.BlockSpec((1, tn), lambda j: (0, j))],
        out_specs=pl.BlockSpec((bsz, tn), lambda j: (0, j)),
        out_shape=jax.ShapeDtypeStruct((bsz, n), F32),
        compiler_params=_cparams("arbitrary"),
        name="ada",
    )(c, w_ada, b_ada.reshape(1, n))


def _head_norm_rope(t, trot, a, b):
    ms = jnp.sum(t * t, axis=-1, keepdims=True) * (1.0 / (QK_NOPE + QK_ROPE))
    return (t * a + trot * b) * lax.rsqrt(ms + EPS)


def _inproj_kernel(x_ref, mod_ref, g1_ref, wa_ref, wf_ref, wga_ref, wgf_ref,
                   gq_ref, gkv_ref, wuq_ref, wkv_ref,
                   aq_ref, bq_ref, ak_ref, bk_ref,
                   q_ref, k_ref, v_ref, zf_ref, sa_ref, sf_ref, *, ql, kvl):
    x = x_ref[...]
    mod = mod_ref[0]
    sh1, sc1 = mod[0:1], mod[1:2]
    r = lax.rsqrt(jnp.mean(x * x, axis=-1, keepdims=True) + EPS)
    h = (x * r * g1_ref[...]) * (1.0 + sc1) + sh1
    hb = h.astype(BF16)

    zf_ref[...] = jnp.dot(hb, wf_ref[...], preferred_element_type=F32).astype(BF16)
    sa_ref[...] = jax.nn.sigmoid(
        jnp.dot(hb, wga_ref[...], preferred_element_type=F32)).astype(BF16)
    sf_ref[...] = jax.nn.sigmoid(
        jnp.dot(hb, wgf_ref[...], preferred_element_type=F32)).astype(BF16)

    za = jnp.dot(hb, wa_ref[...], preferred_element_type=F32)
    zq = za[:, :ql]
    cq = zq * lax.rsqrt(jnp.mean(zq * zq, axis=-1, keepdims=True) + EPS) * gq_ref[...]
    qall = jnp.dot(cq.astype(BF16), wuq_ref[...], preferred_element_type=F32)

    zk = za[:, ql:]
    kvn = zk[:, :kvl]
    rk = lax.rsqrt(jnp.mean(kvn * kvn, axis=-1, keepdims=True) + EPS)
    lane = lax.broadcasted_iota(jnp.int32, zk.shape, 1)
    u = zk * jnp.where(lane < kvl, rk, 1.0) * gkv_ref[...]
    kvall = jnp.dot(u.astype(BF16), wkv_ref[...], preferred_element_type=F32)

    aq, bq, ak, bk = aq_ref[...], bq_ref[...], ak_ref[...], bk_ref[...]
    hw = N_HEADS * LANES
    for hd in range(N_HEADS):
        lo, hi = hd * LANES, (hd + 1) * LANES
        q_ref[0, hd] = _head_norm_rope(qall[:, lo:hi], qall[:, hw + lo:hw + hi],
                                       aq, bq).astype(BF16)
        k_ref[0, hd] = _head_norm_rope(kvall[:, lo:hi], kvall[:, hw + lo:hw + hi],
                                       ak, bk).astype(BF16)
        v_ref[0, hd] = kvall[:, 2 * hw + hd * V_DIM: 2 * hw + (hd + 1) * V_DIM].astype(BF16)


def _rope_tables(seq):
    half = QK_ROPE // 2
    pos = np.arange(seq, dtype=np.float64)
    inv = ROPE_THETA ** (-np.arange(0, QK_ROPE, 2, dtype=np.float64) / QK_ROPE)
    ang = pos[:, None] * inv[None, :]
    c, s = np.cos(ang), np.sin(ang)
    cos = np.ones((seq, LANES)); sin = np.zeros((seq, LANES))
    cos[:, QK_NOPE:QK_NOPE + half] = c
    cos[:, QK_NOPE + half:QK_NOPE + QK_ROPE] = c
    sin[:, QK_NOPE:QK_NOPE + half] = -s
    sin[:, QK_NOPE + half:QK_NOPE + QK_ROPE] = s
    return jnp.asarray(cos, F32), jnp.asarray(sin, F32)


def _partner_columns(w):
    half = QK_ROPE // 2
    lo, mid, hi = QK_NOPE, QK_NOPE + half, QK_NOPE + QK_ROPE
    z = jnp.zeros_like(w)
    return jnp.concatenate([z[..., :lo], w[..., mid:hi], w[..., lo:mid], z[..., hi:]], axis=-1)


def _inproj(x2, mod, norm1_g, w_in, q_a_g, w_uq, kv_a_g, w_ukv, q_g, k_g, bsz, seq):
    t, d = x2.shape
    ql, kvl = q_a_g.shape[0], kv_a_g.shape[0]
    hq = QK_NOPE + QK_ROPE
    fw = w_in.shape[1] - ql - kvl - QK_ROPE - 2 * d
    o1, o2, o3, o4, o5 = ql, ql + kvl, ql + kvl + QK_ROPE, ql + kvl + QK_ROPE + fw, \
        ql + kvl + QK_ROPE + fw + d
    assert ql % LANES == 0 and kvl % LANES == 0

    wa = jnp.concatenate([w_in[:, :o3], jnp.zeros((d, LANES - QK_ROPE), F32)], axis=1).astype(BF16)
    wf = w_in[:, o3:o4].astype(BF16)
    wga = w_in[:, o4:o5].astype(BF16)
    wgf = w_in[:, o5:].astype(BF16)

    wuq = w_uq.reshape(ql, N_HEADS, hq)
    wuq = jnp.pad(wuq, ((0, 0), (0, 0), (0, LANES - hq)))
    wuq = jnp.concatenate([wuq.reshape(ql, N_HEADS * LANES),
                           _partner_columns(wuq).reshape(ql, N_HEADS * LANES)], axis=1).astype(BF16)
    wukv = w_ukv.reshape(kvl, N_HEADS, QK_NOPE + V_DIM)
    wk = jnp.pad(wukv[:, :, :QK_NOPE], ((0, 0), (0, 0), (0, LANES - QK_NOPE)))
    place = jnp.zeros((QK_ROPE, N_HEADS, LANES), F32)
    place = place.at[jnp.arange(QK_ROPE), :, QK_NOPE + jnp.arange(QK_ROPE)].set(1.0)
    wk = jnp.concatenate([wk, place, jnp.zeros((LANES - QK_ROPE, N_HEADS, LANES), F32)], axis=0)
    wv = jnp.concatenate([wukv[:, :, QK_NOPE:], jnp.zeros((LANES, N_HEADS, V_DIM), F32)], axis=0)
    wkv = jnp.concatenate([wk.reshape(kvl + LANES, N_HEADS * LANES),
                           _partner_columns(wk).reshape(kvl + LANES, N_HEADS * LANES),
                           wv.reshape(kvl + LANES, N_HEADS * V_DIM)], axis=1).astype(BF16)

    gkv = jnp.concatenate([kv_a_g, jnp.ones((LANES,), F32)]).reshape(1, kvl + LANES)
    pad = jnp.zeros((LANES - hq,), F32)
    qg = jnp.concatenate([q_g * (hq ** -0.5), pad])
    kg = jnp.concatenate([k_g, pad])
    cos, sin = _rope_tables(seq)
    aq, bq = qg[None, :] * cos, _partner_columns(qg)[None, :] * sin
    ak, bk = kg[None, :] * cos, _partner_columns(kg)[None, :] * sin

    tm = _tile(seq, 512)
    tpb = seq // tm
    full = lambda shp: pl.BlockSpec(shp, lambda i: (0,) * len(shp))
    tok = lambda w: pl.BlockSpec((tm, w), lambda i: (i, 0))
    head = lambda w: pl.BlockSpec((1, N_HEADS, tm, w), lambda i: (i // tpb, 0, i % tpb, 0))
    rope = pl.BlockSpec((tm, LANES), lambda i: (i % tpb, 0))
    return pl.pallas_call(
        functools.partial(_inproj_kernel, ql=ql, kvl=kvl),
        grid=(t // tm,),
        in_specs=[tok(d),
                  pl.BlockSpec((1, N_ADA, d), lambda i: (i // tpb, 0, 0)),
                  full((1, d)), full(wa.shape), full(wf.shape), full(wga.shape), full(wgf.shape),
                  full((1, ql)), full(gkv.shape), full(wuq.shape), full(wkv.shape),
                  rope, rope, rope, rope],
        out_specs=[head(LANES), head(LANES), head(V_DIM), tok(fw), tok(d), tok(d)],
        out_shape=[jax.ShapeDtypeStruct((bsz, N_HEADS, seq, LANES), BF16),
                   jax.ShapeDtypeStruct((bsz, N_HEADS, seq, LANES), BF16),
                   jax.ShapeDtypeStruct((bsz, N_HEADS, seq, V_DIM), BF16),
                   jax.ShapeDtypeStruct((t, fw), BF16),
                   jax.ShapeDtypeStruct((t, d), BF16),
                   jax.ShapeDtypeStruct((t, d), BF16)],
        compiler_params=_cparams("arbitrary"),
        name="inproj",
    )(x2, mod, norm1_g.reshape(1, d), wa, wf, wga, wgf, q_a_g.reshape(1, ql), gkv, wuq, wkv,
      aq, bq, ak, bk)


def _attn_kernel(q_ref, k_ref, v_ref, o_ref):
    for hd in range(N_HEADS):
        s = lax.dot_general(q_ref[0, hd], k_ref[0, hd], (((1,), (1,)), ((), ())),
                            preferred_element_type=F32)
        m = jnp.max(s, axis=-1, keepdims=True)
        p = jnp.exp(s - m)
        l = jnp.sum(p, axis=-1, keepdims=True)
        o = jnp.dot(p.astype(BF16), v_ref[0, hd], preferred_element_type=F32)
        o_ref[0, :, hd * V_DIM:(hd + 1) * V_DIM] = (o / l).astype(BF16)


def _attention(q, k, v):
    bsz, _, seq, _ = q.shape
    tq = _tile(seq, 256)
    return pl.pallas_call(
        _attn_kernel,
        grid=(bsz, seq // tq),
        in_specs=[pl.BlockSpec((1, N_HEADS, tq, LANES), lambda b, j: (b, 0, j, 0)),
                  pl.BlockSpec((1, N_HEADS, seq, LANES), lambda b, j: (b, 0, 0, 0)),
                  pl.BlockSpec((1, N_HEADS, seq, V_DIM), lambda b, j: (b, 0, 0, 0))],
        out_specs=pl.BlockSpec((1, tq, N_HEADS * V_DIM), lambda b, j: (b, j, 0)),
        out_shape=jax.ShapeDtypeStruct((bsz, seq, N_HEADS * V_DIM), BF16),
        compiler_params=_cparams("arbitrary", "arbitrary"),
        name="attn",
    )(q, k, v)


def _fourier_kernel(z_ref, wc_ref, ws_ref, tab_ref, o_ref, u_ref, *, seq):
    @pl.when(pl.program_id(1) == 0)
    def _():
        z = z_ref[0]
        u_ref[:seq, :] = jnp.dot(z, wc_ref[...], preferred_element_type=F32).astype(BF16)
        u_ref[seq:, :] = jnp.dot(z, ws_ref[...], preferred_element_type=F32).astype(BF16)

    o_ref[0] = jnp.dot(tab_ref[...], u_ref[...], preferred_element_type=F32).astype(BF16)


def _fourier_tables(seq, fw):
    g = FOURIER_GROUP
    n = np.arange(seq, dtype=np.int64)
    ang = 2.0 * np.pi * ((n[:, None] * n[None, :]) % seq).astype(np.float64) / seq
    tab = np.concatenate([np.cos(ang), -np.sin(ang)], axis=1)
    c = np.arange(g, dtype=np.int64)
    angc = 2.0 * np.pi * ((c[:, None] * c[None, :]) % g).astype(np.float64) / g
    scale = 1.0 / math.sqrt(seq * g)
    eye = np.eye(fw // g)
    wc = np.kron(eye, np.cos(angc) * scale)
    ws = np.kron(eye, np.sin(angc) * scale)
    return (jnp.asarray(tab, F32).astype(BF16), jnp.asarray(wc, F32).astype(BF16),
            jnp.asarray(ws, F32).astype(BF16))


def _fourier(zf):
    bsz, seq, fw = zf.shape
    tab, wc, ws = _fourier_tables(seq, fw)
    tr = _tile(seq, 256)
    return pl.pallas_call(
        functools.partial(_fourier_kernel, seq=seq),
        grid=(bsz, seq // tr),
        in_specs=[pl.BlockSpec((1, seq, fw), lambda b, j: (b, 0, 0)),
                  pl.BlockSpec((fw, fw), lambda b, j: (0, 0)),
                  pl.BlockSpec((fw, fw), lambda b, j: (0, 0)),
                  pl.BlockSpec((tr, 2 * seq), lambda b, j: (j, 0))],
        out_specs=pl.BlockSpec((1, tr, fw), lambda b, j: (b, j, 0)),
        out_shape=jax.ShapeDtypeStruct((bsz, seq, fw), BF16),
        scratch_shapes=[pltpu.VMEM((2 * seq, fw), BF16)],
        compiler_params=_cparams("arbitrary", "arbitrary"),
        name="fourier",
    )(zf, wc, ws, tab)


def _merge_kernel(a_ref, f_ref, sa_ref, sf_ref, x_ref, mod_ref, wpa_ref, wpf_ref, wo_ref,
                  g2_ref, wrh_ref, wrl_ref, x1_ref, h2_ref, sc_ref):
    ya = jnp.dot(a_ref[...], wpa_ref[...], preferred_element_type=F32)
    yf = jnp.dot(f_ref[...], wpf_ref[...], preferred_element_type=F32)
    merged = sa_ref[...].astype(F32) * ya + sf_ref[...].astype(F32) * yf
    mod = mod_ref[0]
    g1, sh2, sc2 = mod[2:3], mod[3:4], mod[4:5]
    x1 = x_ref[...] + g1 * jnp.dot(merged.astype(BF16), wo_ref[...], preferred_element_type=F32)
    x1_ref[...] = x1
    r = lax.rsqrt(jnp.mean(x1 * x1, axis=-1, keepdims=True) + EPS)
    h2 = (x1 * r * g2_ref[...]) * (1.0 + sc2) + sh2
    _store_row_tiles(h2_ref, h2)
    hh = h2.astype(BF16)
    hl = (h2 - hh.astype(F32)).astype(BF16)
    nt = (((1,), (1,)), ((), ()))
    lt = (lax.dot_general(wrh_ref[...], hh, nt, preferred_element_type=F32)
          + lax.dot_general(wrh_ref[...], hl, nt, preferred_element_type=F32)
          + lax.dot_general(wrl_ref[...], hh, nt, preferred_element_type=F32))
    sc_ref[...] = jax.nn.sigmoid(lt)


def _merge(attn, four, sa, sf, x2, mod, w_pa, w_pf, w_out, norm2_g, w_router, seq):
    t, d = x2.shape
    e = w_router.shape[1]
    wrt = w_router.T
    wrh = wrt.astype(BF16)
    wrl = (wrt - wrh.astype(F32)).astype(BF16)
    tm = _tile(seq, 512)
    tpb = seq // tm
    full = lambda shp: pl.BlockSpec(shp, lambda i: (0,) * len(shp))
    tok = lambda w: pl.BlockSpec((tm, w), lambda i: (i, 0))
    return pl.pallas_call(
        _merge_kernel,
        grid=(t // tm,),
        in_specs=[tok(attn.shape[1]), tok(four.shape[1]), tok(d), tok(d), tok(d),
                  pl.BlockSpec((1, N_ADA, d), lambda i: (i // tpb, 0, 0)),
                  full(w_pa.shape), full(w_pf.shape), full(w_out.shape), full((1, d)),
                  full((e, d)), full((e, d))],
        out_specs=[tok(d), pl.BlockSpec((tm * (d // LANES), LANES), lambda i: (i, 0)),
                   pl.BlockSpec((e, tm), lambda i: (0, i))],
        out_shape=[jax.ShapeDtypeStruct((t, d), F32),
                   jax.ShapeDtypeStruct((t * (d // LANES), LANES), F32),
                   jax.ShapeDtypeStruct((e, t), F32)],
        compiler_params=_cparams("arbitrary"),
        name="merge",
    )(attn, four, sa, sf, x2, mod, w_pa.astype(BF16), w_pf.astype(BF16), w_out.astype(BF16),
      norm2_g.reshape(1, d), wrh, wrl)


def _route_kernel(s_ref, b_ref, tri_ref, idx_ref, w_ref, rank_ref, cnt_ref, carry_ref):
    @pl.when(pl.program_id(0) == 0)
    def _():
        carry_ref[...] = jnp.zeros_like(carry_ref)

    sc = s_ref[...]
    e, tr = sc.shape
    row = lax.broadcasted_iota(jnp.int32, (e, tr), 0)
    v = sc + b_ref[...]
    sel = jnp.zeros((e, tr), F32)
    idxs, ws = [], []
    for _ in range(TOP_K):
        m = jnp.max(v, axis=0, keepdims=True)
        idx = jnp.min(jnp.where(v == m, row, e), axis=0, keepdims=True)
        oh = row == idx
        ws.append(jnp.sum(jnp.where(oh, sc, 0.0), axis=0, keepdims=True))
        idxs.append(idx)
        v = jnp.where(oh, -jnp.inf, v)
        sel = sel + oh.astype(F32)
    wsum = ws[0]
    for w in ws[1:]:
        wsum = wsum + w
    selb = sel.astype(BF16)
    cum = jnp.dot(selb, tri_ref[...], preferred_element_type=F32) + carry_ref[...]
    for kk in range(TOP_K):
        oh = row == idxs[kk]
        rk = jnp.sum(jnp.where(oh, cum, 0.0), axis=0, keepdims=True)
        idx_ref[kk:kk + 1, :] = idxs[kk]
        rank_ref[kk:kk + 1, :] = rk.astype(jnp.int32)
        w_ref[kk:kk + 1, :] = ws[kk] / wsum * ROUTED_SCALE
    tot = carry_ref[...] + jnp.dot(selb, jnp.ones((tr, tr), BF16), preferred_element_type=F32)
    carry_ref[...] = tot
    cnt_ref[...] = tot


def _route(scores_t, router_bias):
    e, t = scores_t.shape
    tr = _tile(t, 256)
    tri = jnp.asarray(np.triu(np.ones((tr, tr), np.float32), 1), BF16)
    bias = jnp.broadcast_to(router_bias.reshape(e, 1), (e, tr)).astype(F32)
    blk = pl.BlockSpec((TOP_K, tr), lambda i: (0, i))
    return pl.pallas_call(
        _route_kernel,
        grid=(t // tr,),
        in_specs=[pl.BlockSpec((e, tr), lambda i: (0, i)),
                  pl.BlockSpec((e, tr), lambda i: (0, 0)),
                  pl.BlockSpec((tr, tr), lambda i: (0, 0))],
        out_specs=[blk, blk, blk, pl.BlockSpec((e, tr), lambda i: (0, 0))],
        out_shape=[jax.ShapeDtypeStruct((TOP_K, t), jnp.int32),
                   jax.ShapeDtypeStruct((TOP_K, t), F32),
                   jax.ShapeDtypeStruct((TOP_K, t), jnp.int32),
                   jax.ShapeDtypeStruct((e, tr), F32)],
        scratch_shapes=[pltpu.VMEM((e, tr), F32)],
        compiler_params=_cparams("arbitrary"),
        name="route",
    )(scores_t, bias, tri)


def _slots_kernel(idx_ref, rank_ref, ps_ref, slot_ref):
    ps = ps_ref[...]
    row = lax.broadcasted_iota(jnp.int32, ps.shape, 0)
    for kk in range(TOP_K):
        oh = row == idx_ref[kk:kk + 1, :]
        start = jnp.sum(jnp.where(oh, ps, 0), axis=0, keepdims=True)
        slot_ref[kk:kk + 1, :] = start + rank_ref[kk:kk + 1, :]


def _slots(idx_t, rank_t, p_start):
    _, t = idx_t.shape
    e = p_start.shape[0]
    ts = _tile(t, 512)
    ps = jnp.broadcast_to(p_start.reshape(e, 1), (e, ts))
    blk = pl.BlockSpec((TOP_K, ts), lambda i: (0, i))
    return pl.pallas_call(
        _slots_kernel,
        grid=(t // ts,),
        in_specs=[blk, blk, pl.BlockSpec((e, ts), lambda i: (0, 0))],
        out_specs=blk,
        out_shape=jax.ShapeDtypeStruct((TOP_K, t), jnp.int32),
        compiler_params=_cparams("arbitrary"),
        name="slots",
    )(idx_t, rank_t, ps)


def _invert(slot_t, n_slots):
    k, t = slot_t.shape
    p = k * t
    win = _tile(p, SC_WINDOW)
    tok = jnp.broadcast_to((jnp.arange(p, dtype=jnp.int32) % t)[:, None], (p, LANES))
    mesh = plsc.VectorSubcoreMesh(core_axis_name="core", subcore_axis_name="subcore")

    @pl.kernel(out_type=jax.ShapeDtypeStruct((n_slots, LANES), jnp.int32), mesh=mesh,
               scratch_types=[])
    def scatter_rows(tok_hbm, slot_hbm, inv_hbm):
        def body(tok_vmem, slot_vmem):
            pltpu.sync_copy(tok_vmem, inv_hbm.at[slot_vmem.at[0]])

        pltpu.emit_pipeline(
            body, grid=(p // win,),
            in_specs=[pl.BlockSpec((win, LANES), lambda i: (i, 0)),
                      pl.BlockSpec((1, win), lambda i: (0, i))],
            out_specs=[], core_axis_name=("core", "subcore"),
            dimension_semantics=(pltpu.PARALLEL,))(tok_hbm, slot_hbm)

    return scatter_rows(tok, slot_t.reshape(1, p))[:, 0]


def _row_copy(src, dst, sem):
    return pltpu.make_async_copy(src, dst, sem)


def _expert_kernel(be_ref, nxt_ref, par_ref, nv_ref, nu_ref, tok_ref, ntok_ref, h_hbm, wg_hbm, wu_hbm,
                   wd_hbm, ys_ref, xbuf, wgf, wuf, wdf, wgb, wub, wdb, sem, wsem, *, nch, ntok):
    i = pl.program_id(0)
    cur = i % 2
    nu = nu_ref[0]
    rows = EXPERT_ROWS

    def weight_copies(e, which):
        return (pltpu.make_async_copy(wg_hbm.at[e], wgf.at[which], wsem.at[which]),
                pltpu.make_async_copy(wu_hbm.at[e], wuf.at[which], wsem.at[which]),
                pltpu.make_async_copy(wd_hbm.at[e], wdf.at[which], wsem.at[which]))

    def fetch(t_ref, which, nvalid):
        for g in range(rows // ROW_GROUP):
            @pl.when(g * ROW_GROUP < nvalid)
            def _():
                for r in range(g * ROW_GROUP, (g + 1) * ROW_GROUP):
                    t = jnp.clip(t_ref[0, 0, r], 0, ntok - 1)
                    _row_copy(h_hbm.at[pl.ds(pl.multiple_of(t * nch, nch), nch)],
                              xbuf.at[which, pl.ds(r * nch, nch)], sem.at[which]).start()

    def drain(which, nvalid):
        def body(g, c):
            for _ in range(ROW_GROUP):
                _row_copy(h_hbm.at[pl.ds(0, nch)], xbuf.at[which, pl.ds(0, nch)],
                          sem.at[which]).wait()
            return c
        lax.fori_loop(0, (nvalid + ROW_GROUP - 1) // ROW_GROUP, body, 0)

    @pl.when(i == 0)
    def _():
        xbuf[...] = jnp.zeros_like(xbuf)
        for cp in weight_copies(be_ref[0], 0):
            cp.start(priority=1)
        fetch(tok_ref, 0, nv_ref[0])

    @pl.when(i < nu)
    def _():
        fetch(ntok_ref, 1 - cur, jnp.where(i + 1 < nu, nv_ref[jnp.minimum(i + 1, nu - 1)], 0))
        prev = be_ref[jnp.maximum(i - 1, 0)]

        @pl.when(jnp.logical_or(i == 0, be_ref[i] != prev))
        def _():
            par = par_ref[i]
            for cp in weight_copies(be_ref[i], par):
                cp.wait()
            wgb[...] = wgf[par].astype(BF16)
            wub[...] = wuf[par].astype(BF16)
            wdb[...] = wdf[par].astype(BF16)

            @pl.when(nxt_ref[i] >= 0)
            def _():
                for cp in weight_copies(nxt_ref[i], 1 - par):
                    cp.start(priority=1)

        drain(cur, nv_ref[i])
        x = _load_row_tiles(xbuf.at[cur], rows, nch).astype(BF16)
        g = jnp.dot(x, wgb[...], preferred_element_type=F32)
        u = jnp.dot(x, wub[...], preferred_element_type=F32)
        a = (_silu(g) * u).astype(BF16)
        _store_row_tiles(ys_ref, jnp.dot(a, wdb[...], preferred_element_type=F32))

    @pl.when(i >= nu)
    def _():
        ys_ref[...] = jnp.zeros_like(ys_ref)


def _experts(blk_expert, blk_next, blk_par, blk_valid, nblk_used, tok_of_slot, h2t, w_g, w_u, w_d):
    d, f = w_g.shape[1], w_g.shape[2]
    nch = d // LANES
    rows = EXPERT_ROWS
    nblk = blk_expert.shape[0]
    toks = tok_of_slot.reshape(nblk, 1, rows)
    last = lambda i, nu: jnp.minimum(i, nu[0] - 1)
    hbm = pl.BlockSpec(memory_space=pl.ANY)
    return pl.pallas_call(
        functools.partial(_expert_kernel, nch=nch, ntok=h2t.shape[0] // nch),
        grid_spec=pltpu.PrefetchScalarGridSpec(
            num_scalar_prefetch=5,
            grid=(nblk,),
            in_specs=[pl.BlockSpec((1, 1, rows), lambda i, be, nx, pa, nv, nu: (last(i, nu), 0, 0),
                                   memory_space=pltpu.SMEM),
                      pl.BlockSpec((1, 1, rows),
                                   lambda i, be, nx, pa, nv, nu: (last(i + 1, nu), 0, 0),
                                   memory_space=pltpu.SMEM),
                      hbm, hbm, hbm, hbm],
            out_specs=pl.BlockSpec((rows * nch, LANES), lambda i, be, nx, pa, nv, nu: (i, 0)),
            scratch_shapes=[pltpu.VMEM((2, rows * nch, LANES), F32),
                            pltpu.VMEM((2, d, f), F32), pltpu.VMEM((2, d, f), F32),
                            pltpu.VMEM((2, f, d), F32),
                            pltpu.VMEM((d, f), BF16), pltpu.VMEM((d, f), BF16),
                            pltpu.VMEM((f, d), BF16),
                            pltpu.SemaphoreType.DMA((2,)), pltpu.SemaphoreType.DMA((2,))]),
        out_shape=jax.ShapeDtypeStruct((nblk * rows * nch, LANES), F32),
        compiler_params=_cparams("arbitrary"),
        name="experts",
    )(blk_expert, blk_next, blk_par, blk_valid, nblk_used, toks, toks, h2t, w_g, w_u, w_d)


def _combine_kernel(slot_ref, nslot_ref, w_ref, x1_ref, h2_ref, mod_ref, wsg_ref, wsu_ref,
                    wsd_ref, ys_hbm, o_ref, buf, sem, *, tc, nsteps, nch):
    i = pl.program_id(0)
    cur = i % 2

    def issue(s_ref, which):
        def body(r, c):
            for kk in range(TOP_K):
                s = s_ref[kk, r]
                _row_copy(ys_hbm.at[pl.ds(pl.multiple_of(s * nch, nch), nch)],
                          buf.at[which, kk, pl.ds(pl.multiple_of(r * nch, nch), nch)],
                          sem.at[which]).start(priority=kk % 2)
            return c
        lax.fori_loop(0, tc, body, 0)

    @pl.when(i == 0)
    def _():
        issue(slot_ref, 0)

    @pl.when(i + 1 < nsteps)
    def _():
        issue(nslot_ref, 1 - cur)

    hb = _load_row_tiles(h2_ref, tc, nch).astype(BF16)
    g = jnp.dot(hb, wsg_ref[...], preferred_element_type=F32)
    u = jnp.dot(hb, wsu_ref[...], preferred_element_type=F32)
    acc = jnp.dot((_silu(g) * u).astype(BF16), wsd_ref[...], preferred_element_type=F32)

    def drain(r, c):
        for kk in range(TOP_K):
            _row_copy(ys_hbm.at[pl.ds(0, nch)], buf.at[cur, kk, pl.ds(0, nch)], sem.at[cur]).wait()
        return c
    lax.fori_loop(0, tc, drain, 0)

    w = w_ref[...]
    for kk in range(TOP_K):
        acc = acc + w[:, kk:kk + 1] * _load_row_tiles(buf.at[cur, kk], tc, nch)
    g2 = mod_ref[0][5:6]
    o_ref[...] = x1_ref[...] + g2 * acc


def _combine(slot_t, w_tk, x1, h2t, mod, w_sg, w_su, w_sd, ys, seq):
    t, d = x1.shape
    nch = d // LANES
    tc = _tile(seq, 128)
    tpb = seq // tc
    nsteps = t // tc
    full = lambda shp: pl.BlockSpec(shp, lambda i: (0,) * len(shp))
    tok = lambda w: pl.BlockSpec((tc, w), lambda i: (i, 0))
    return pl.pallas_call(
        functools.partial(_combine_kernel, tc=tc, nsteps=nsteps, nch=nch),
        grid=(nsteps,),
        in_specs=[pl.BlockSpec((TOP_K, tc), lambda i: (0, i), memory_space=pltpu.SMEM),
                  pl.BlockSpec((TOP_K, tc), lambda i: (0, jnp.minimum(i + 1, nsteps - 1)),
                               memory_space=pltpu.SMEM),
                  tok(TOP_K), tok(d),
                  pl.BlockSpec((tc * nch, LANES), lambda i: (i, 0)),
                  pl.BlockSpec((1, N_ADA, d), lambda i: (i // tpb, 0, 0)),
                  full(w_sg.shape), full(w_su.shape), full(w_sd.shape),
                  pl.BlockSpec(memory_space=pl.ANY)],
        out_specs=tok(d),
        out_shape=jax.ShapeDtypeStruct((t, d), F32),
        scratch_shapes=[pltpu.VMEM((2, TOP_K, tc * nch, LANES), F32),
                        pltpu.SemaphoreType.DMA((2,))],
        compiler_params=_cparams("arbitrary"),
        name="combine",
    )(slot_t, slot_t, w_tk, x1, h2t, mod, w_sg.astype(BF16), w_su.astype(BF16),
      w_sd.astype(BF16), ys)


def _layer(x, c, w_ada, b_ada, norm1_g, w_in, q_a_norm_g, w_uq, kv_a_norm_g, w_ukv,
           q_norm_g, k_norm_g, w_proj_attn, w_proj_fourier, w_out, norm2_g,
           w_router, router_bias, w_exp_gate, w_exp_up, w_exp_down,
           w_sh_gate, w_sh_up, w_sh_down):
    bsz, seq, d = x.shape
    t = bsz * seq
    e = w_router.shape[1]
    x2 = x.reshape(t, d)

    mod = _ada(c, w_ada, b_ada).reshape(bsz, N_ADA, d)
    q, k, v, zf, sa, sf = _inproj(x2, mod, norm1_g, w_in, q_a_norm_g, w_uq, kv_a_norm_g,
                                  w_ukv, q_norm_g, k_norm_g, bsz, seq)
    attn = _attention(q, k, v).reshape(t, N_HEADS * V_DIM)
    four = _fourier(zf.reshape(bsz, seq, zf.shape[1])).reshape(t, zf.shape[1])
    x1, h2, scores_t = _merge(attn, four, sa, sf, x2, mod, w_proj_attn, w_proj_fourier,
                              w_out, norm2_g, w_router, seq)

    idx_t, w_t, rank_t, cnt = _route(scores_t, router_bias)
    counts = cnt[:, 0].astype(jnp.int32)
    rows = EXPERT_ROWS
    nblk = -(-(t * TOP_K) // rows) + e
    padded = ((counts + rows - 1) // rows) * rows
    p_end = jnp.cumsum(padded)
    p_start = p_end - padded
    nblk_used = (p_end[-1] // rows).astype(jnp.int32)
    blk_start = jnp.arange(nblk, dtype=jnp.int32) * rows
    blk_first = jnp.minimum(blk_start, p_end[-1] - 1)
    hit = p_end[None, :] <= blk_first[:, None]
    blk_expert = jnp.clip(jnp.sum(hit.astype(jnp.int32), axis=1), 0, e - 1)

    slot_t = _slots(idx_t, rank_t, p_start.astype(jnp.int32))
    tok_of_slot = _invert(slot_t, nblk * rows)
    eid = jnp.arange(e, dtype=jnp.int32)
    cand = jnp.where(counts > 0, eid, e)
    nxt = jnp.concatenate([lax.cummin(cand[::-1])[::-1][1:], jnp.full((1,), e, jnp.int32)])
    nxt = jnp.where(nxt >= e, -1, nxt)
    run = jnp.cumsum((counts > 0).astype(jnp.int32)) - 1
    onehot = blk_expert[:, None] == eid[None, :]
    blk_next = jnp.sum(jnp.where(onehot, nxt[None, :], 0), axis=1).astype(jnp.int32)
    blk_par = (jnp.sum(jnp.where(onehot, run[None, :], 0), axis=1) % 2).astype(jnp.int32)
    seg_end = (p_start + counts).astype(jnp.int32)
    blk_end = jnp.sum(jnp.where(onehot, seg_end[None, :], 0), axis=1)
    blk_valid = jnp.clip(blk_end - blk_start, 0, rows).astype(jnp.int32)
    ys = _experts(blk_expert, blk_next, blk_par, blk_valid, nblk_used.reshape(1), tok_of_slot, h2,
                  w_exp_gate, w_exp_up, w_exp_down)
    out = _combine(slot_t, w_t.T, x1, h2, mod, w_sh_gate, w_sh_up, w_sh_down, ys, seq)
    return out.reshape(bsz, seq, d)


def kernel(x, c, w_ada, b_ada, norm1_g, w_in, q_a_norm_g, w_uq, kv_a_norm_g, w_ukv, q_norm_g,
           k_norm_g, w_proj_attn, w_proj_fourier, w_out, norm2_g, w_router, router_bias,
           w_exp_gate, w_exp_up, w_exp_down, w_sh_gate, w_sh_up, w_sh_down):
    for l in range(w_ada.shape[0]):
        x = _layer(x, c, w_ada[l], b_ada[l], norm1_g[l], w_in[l], q_a_norm_g[l], w_uq[l],
                   kv_a_norm_g[l], w_ukv[l], q_norm_g[l], k_norm_g[l], w_proj_attn[l],
                   w_proj_fourier[l], w_out[l], norm2_g[l], w_router[l], router_bias[l],
                   w_exp_gate[l], w_exp_up[l], w_exp_down[l], w_sh_gate[l], w_sh_up[l],
                   w_sh_down[l])
    return x
```

```python
import functools
import math

import numpy as np
import jax
import jax.numpy as jnp
from jax import lax
from jax.experimental import pallas as pl
from jax.experimental.pallas import tpu as pltpu
from jax.experimental.pallas import tpu_sc as plsc

N_HEADS = 8
QK_NOPE = 64
QK_ROPE = 32
V_DIM = 64
FOURIER_GROUP = 64
TOP_K = 8
ROUTED_SCALE = 2.5
EPS = 1e-6
ROPE_THETA = 10000.0
N_ADA = 6

LANES = 128
EXPERT_ROWS = 256
ROW_GROUP = 16
SC_WINDOW = 128
VMEM_LIMIT = 48 * 1024 * 1024

F32 = jnp.float32
BF16 = jnp.bfloat16


def _cparams(*sem):
    return pltpu.CompilerParams(dimension_semantics=sem, vmem_limit_bytes=VMEM_LIMIT)


def _tile(n, pref):
    t = min(n, pref)
    assert n % t == 0, (n, pref)
    return t


def _silu(v):
    return v * jax.nn.sigmoid(v)


def _store_row_tiles(ref, val):
    rows, d = val.shape
    nch = d // LANES
    for j in range(nch):
        ref[pl.ds(j, rows, stride=nch), :] = val[:, j * LANES:(j + 1) * LANES]


def _load_row_tiles(ref, rows, nch):
    return jnp.concatenate([ref[pl.ds(j, rows, stride=nch), :] for j in range(nch)], axis=1)


def _pack_halves(m):
    d = m.shape[1]
    lo = lax.bitcast_convert_type(m[:, :d // 2].astype(BF16).astype(F32), jnp.uint32)
    hi = lax.bitcast_convert_type(m[:, d // 2:].astype(BF16).astype(F32), jnp.uint32)
    w = (lo >> 16) | (hi & jnp.uint32(0xFFFF0000))
    return w[:, :d // 4], w[:, d // 4:]


def _unpack_halves(w0, w1):
    def lo(w):
        return lax.bitcast_convert_type(w << 16, F32)

    def hi(w):
        return lax.bitcast_convert_type(w & jnp.uint32(0xFFFF0000), F32)
    return jnp.concatenate([lo(w0), lo(w1), hi(w0), hi(w1)], axis=1).astype(BF16)


def _ada_kernel(c_ref, w_ref, b_ref, o_ref):
    a = _silu(c_ref[...])
    o_ref[...] = jnp.dot(a, w_ref[...], preferred_element_type=F32,
                         precision=lax.Precision.HIGHEST) + b_ref[...]


def _ada(c, w_ada, b_ada):
    bsz, d = c.shape
    n = w_ada.shape[1]
    tn = _tile(n, d)
    return pl.pallas_call(
        _ada_kernel,
        grid=(n // tn,),
        in_specs=[pl.BlockSpec((bsz, d), lambda j: (0, 0)),
                  pl.BlockSpec((d, tn), lambda j: (0, j)),
                  pl.BlockSpec((1, tn), lambda j: (0, j))],
        out_specs=pl.BlockSpec((bsz, tn), lambda j: (0, j)),
        out_shape=jax.ShapeDtypeStruct((bsz, n), F32),
        compiler_params=_cparams("arbitrary"),
        name="ada",
    )(c, w_ada, b_ada.reshape(1, n))


def _head_norm_rope(t, trot, a, b):
    ms = jnp.sum(t * t, axis=-1, keepdims=True) * (1.0 / (QK_NOPE + QK_ROPE))
    return (t * a + trot * b) * lax.rsqrt(ms + EPS)


def _inproj_kernel(x_ref, mod_ref, g1_ref, wa_ref, wf_ref, wga_ref, wgf_ref,
                   gq_ref, gkv_ref, wuq_ref, wkv_ref,
                   aq_ref, bq_ref, ak_ref, bk_ref,
                   q_ref, k_ref, v_ref, zf_ref, sa_ref, sf_ref, *, ql, kvl):
    x = x_ref[...]
    mod = mod_ref[0]
    sh1, sc1 = mod[0:1], mod[1:2]
    r = lax.rsqrt(jnp.mean(x * x, axis=-1, keepdims=True) + EPS)
    h = (x * r * g1_ref[...]) * (1.0 + sc1) + sh1
    hb = h.astype(BF16)

    zf_ref[...] = jnp.dot(hb, wf_ref[...], preferred_element_type=F32).astype(BF16)
    sa_ref[...] = jax.nn.sigmoid(
        jnp.dot(hb, wga_ref[...], preferred_element_type=F32)).astype(BF16)
    sf_ref[...] = jax.nn.sigmoid(
        jnp.dot(hb, wgf_ref[...], preferred_element_type=F32)).astype(BF16)

    za = jnp.dot(hb, wa_ref[...], preferred_element_type=F32)
    zq = za[:, :ql]
    cq = zq * lax.rsqrt(jnp.mean(zq * zq, axis=-1, keepdims=True) + EPS) * gq_ref[...]
    qall = jnp.dot(cq.astype(BF16), wuq_ref[...], preferred_element_type=F32)

    zk = za[:, ql:]
    kvn = zk[:, :kvl]
    rk = lax.rsqrt(jnp.mean(kvn * kvn, axis=-1, keepdims=True) + EPS)
    lane = lax.broadcasted_iota(jnp.int32, zk.shape, 1)
    u = zk * jnp.where(lane < kvl, rk, 1.0) * gkv_ref[...]
    kvall = jnp.dot(u.astype(BF16), wkv_ref[...], preferred_element_type=F32)

    aq, bq, ak, bk = aq_ref[...], bq_ref[...], ak_ref[...], bk_ref[...]
    hw = N_HEADS * LANES
    for hd in range(N_HEADS):
        lo, hi = hd * LANES, (hd + 1) * LANES
        q_ref[0, hd] = _head_norm_rope(qall[:, lo:hi], qall[:, hw + lo:hw + hi],
                                       aq, bq).astype(BF16)
        k_ref[0, hd] = _head_norm_rope(kvall[:, lo:hi], kvall[:, hw + lo:hw + hi],
                                       ak, bk).astype(BF16)
        v_ref[0, hd] = kvall[:, 2 * hw + hd * V_DIM: 2 * hw + (hd + 1) * V_DIM].astype(BF16)


def _rope_tables(seq):
    half = QK_ROPE // 2
    pos = np.arange(seq, dtype=np.float64)
    inv = ROPE_THETA ** (-np.arange(0, QK_ROPE, 2, dtype=np.float64) / QK_ROPE)
    ang = pos[:, None] * inv[None, :]
    c, s = np.cos(ang), np.sin(ang)
    cos = np.ones((seq, LANES)); sin = np.zeros((seq, LANES))
    cos[:, QK_NOPE:QK_NOPE + half] = c
    cos[:, QK_NOPE + half:QK_NOPE + QK_ROPE] = c
    sin[:, QK_NOPE:QK_NOPE + half] = -s
    sin[:, QK_NOPE + half:QK_NOPE + QK_ROPE] = s
    return jnp.asarray(cos, F32), jnp.asarray(sin, F32)


def _partner_columns(w):
    half = QK_ROPE // 2
    lo, mid, hi = QK_NOPE, QK_NOPE + half, QK_NOPE + QK_ROPE
    z = jnp.zeros_like(w)
    return jnp.concatenate([z[..., :lo], w[..., mid:hi], w[..., lo:mid], z[..., hi:]], axis=-1)


def _inproj(x2, mod, norm1_g, w_in, q_a_g, w_uq, kv_a_g, w_ukv, q_g, k_g, bsz, seq):
    t, d = x2.shape
    ql, kvl = q_a_g.shape[0], kv_a_g.shape[0]
    hq = QK_NOPE + QK_ROPE
    fw = w_in.shape[1] - ql - kvl - QK_ROPE - 2 * d
    o1, o2, o3, o4, o5 = ql, ql + kvl, ql + kvl + QK_ROPE, ql + kvl + QK_ROPE + fw, \
        ql + kvl + QK_ROPE + fw + d
    assert ql % LANES == 0 and kvl % LANES == 0

    wa = jnp.concatenate([w_in[:, :o3], jnp.zeros((d, LANES - QK_ROPE), F32)], axis=1).astype(BF16)
    wf = w_in[:, o3:o4].astype(BF16)
    wga = w_in[:, o4:o5].astype(BF16)
    wgf = w_in[:, o5:].astype(BF16)

    wuq = w_uq.reshape(ql, N_HEADS, hq)
    wuq = jnp.pad(wuq, ((0, 0), (0, 0), (0, LANES - hq)))
    wuq = jnp.concatenate([wuq.reshape(ql, N_HEADS * LANES),
                           _partner_columns(wuq).reshape(ql, N_HEADS * LANES)], axis=1).astype(BF16)
    wukv = w_ukv.reshape(kvl, N_HEADS, QK_NOPE + V_DIM)
    wk = jnp.pad(wukv[:, :, :QK_NOPE], ((0, 0), (0, 0), (0, LANES - QK_NOPE)))
    place = jnp.zeros((QK_ROPE, N_HEADS, LANES), F32)
    place = place.at[jnp.arange(QK_ROPE), :, QK_NOPE + jnp.arange(QK_ROPE)].set(1.0)
    wk = jnp.concatenate([wk, place, jnp.zeros((LANES - QK_ROPE, N_HEADS, LANES), F32)], axis=0)
    wv = jnp.concatenate([wukv[:, :, QK_NOPE:], jnp.zeros((LANES, N_HEADS, V_DIM), F32)], axis=0)
    wkv = jnp.concatenate([wk.reshape(kvl + LANES, N_HEADS * LANES),
                           _partner_columns(wk).reshape(kvl + LANES, N_HEADS * LANES),
                           wv.reshape(kvl + LANES, N_HEADS * V_DIM)], axis=1).astype(BF16)

    gkv = jnp.concatenate([kv_a_g, jnp.ones((LANES,), F32)]).reshape(1, kvl + LANES)
    pad = jnp.zeros((LANES - hq,), F32)
    qg = jnp.concatenate([q_g * (hq ** -0.5), pad])
    kg = jnp.concatenate([k_g, pad])
    cos, sin = _rope_tables(seq)
    aq, bq = qg[None, :] * cos, _partner_columns(qg)[None, :] * sin
    ak, bk = kg[None, :] * cos, _partner_columns(kg)[None, :] * sin

    tm = _tile(seq, 512)
    tpb = seq // tm
    full = lambda shp: pl.BlockSpec(shp, lambda i: (0,) * len(shp))
    tok = lambda w: pl.BlockSpec((tm, w), lambda i: (i, 0))
    head = lambda w: pl.BlockSpec((1, N_HEADS, tm, w), lambda i: (i // tpb, 0, i % tpb, 0))
    rope = pl.BlockSpec((tm, LANES), lambda i: (i % tpb, 0))
    return pl.pallas_call(
        functools.partial(_inproj_kernel, ql=ql, kvl=kvl),
        grid=(t // tm,),
        in_specs=[tok(d),
                  pl.BlockSpec((1, N_ADA, d), lambda i: (i // tpb, 0, 0)),
                  full((1, d)), full(wa.shape), full(wf.shape), full(wga.shape), full(wgf.shape),
                  full((1, ql)), full(gkv.shape), full(wuq.shape), full(wkv.shape),
                  rope, rope, rope, rope],
        out_specs=[head(LANES), head(LANES), head(V_DIM), tok(fw), tok(d), tok(d)],
        out_shape=[jax.ShapeDtypeStruct((bsz, N_HEADS, seq, LANES), BF16),
                   jax.ShapeDtypeStruct((bsz, N_HEADS, seq, LANES), BF16),
                   jax.ShapeDtypeStruct((bsz, N_HEADS, seq, V_DIM), BF16),
                   jax.ShapeDtypeStruct((t, fw), BF16),
                   jax.ShapeDtypeStruct((t, d), BF16),
                   jax.ShapeDtypeStruct((t, d), BF16)],
        compiler_params=_cparams("arbitrary"),
        name="inproj",
    )(x2, mod, norm1_g.reshape(1, d), wa, wf, wga, wgf, q_a_g.reshape(1, ql), gkv, wuq, wkv,
      aq, bq, ak, bk)


def _attn_kernel(q_ref, k_ref, v_ref, o_ref):
    for hd in range(N_HEADS):
        s = lax.dot_general(q_ref[0, hd], k_ref[0, hd], (((1,), (1,)), ((), ())),
                            preferred_element_type=F32)
        m = jnp.max(s, axis=-1, keepdims=True)
        p = jnp.exp(s - m)
        l = jnp.sum(p, axis=-1, keepdims=True)
        o = jnp.dot(p.astype(BF16), v_ref[0, hd], preferred_element_type=F32)
        o_ref[0, :, hd * V_DIM:(hd + 1) * V_DIM] = (o / l).astype(BF16)


def _attention(q, k, v):
    bsz, _, seq, _ = q.shape
    tq = _tile(seq, 256)
    return pl.pallas_call(
        _attn_kernel,
        grid=(bsz, seq // tq),
        in_specs=[pl.BlockSpec((1, N_HEADS, tq, LANES), lambda b, j: (b, 0, j, 0)),
                  pl.BlockSpec((1, N_HEADS, seq, LANES), lambda b, j: (b, 0, 0, 0)),
                  pl.BlockSpec((1, N_HEADS, seq, V_DIM), lambda b, j: (b, 0, 0, 0))],
        out_specs=pl.BlockSpec((1, tq, N_HEADS * V_DIM), lambda b, j: (b, j, 0)),
        out_shape=jax.ShapeDtypeStruct((bsz, seq, N_HEADS * V_DIM), BF16),
        compiler_params=_cparams("arbitrary", "arbitrary"),
        name="attn",
    )(q, k, v)


def _fourier_kernel(z_ref, wc_ref, ws_ref, tab_ref, o_ref, u_ref, *, seq):
    @pl.when(pl.program_id(1) == 0)
    def _():
        z = z_ref[0]
        u_ref[:seq, :] = jnp.dot(z, wc_ref[...], preferred_element_type=F32).astype(BF16)
        u_ref[seq:, :] = jnp.dot(z, ws_ref[...], preferred_element_type=F32).astype(BF16)

    o_ref[0] = jnp.dot(tab_ref[...], u_ref[...], preferred_element_type=F32).astype(BF16)


def _fourier_tables(seq, fw):
    g = FOURIER_GROUP
    n = np.arange(seq, dtype=np.int64)
    ang = 2.0 * np.pi * ((n[:, None] * n[None, :]) % seq).astype(np.float64) / seq
    tab = np.concatenate([np.cos(ang), -np.sin(ang)], axis=1)
    c = np.arange(g, dtype=np.int64)
    angc = 2.0 * np.pi * ((c[:, None] * c[None, :]) % g).astype(np.float64) / g
    scale = 1.0 / math.sqrt(seq * g)
    eye = np.eye(fw // g)
    wc = np.kron(eye, np.cos(angc) * scale)
    ws = np.kron(eye, np.sin(angc) * scale)
    return (jnp.asarray(tab, F32).astype(BF16), jnp.asarray(wc, F32).astype(BF16),
            jnp.asarray(ws, F32).astype(BF16))


def _fourier(zf):
    bsz, seq, fw = zf.shape
    tab, wc, ws = _fourier_tables(seq, fw)
    tr = _tile(seq, 256)
    return pl.pallas_call(
        functools.partial(_fourier_kernel, seq=seq),
        grid=(bsz, seq // tr),
        in_specs=[pl.BlockSpec((1, seq, fw), lambda b, j: (b, 0, 0)),
                  pl.BlockSpec((fw, fw), lambda b, j: (0, 0)),
                  pl.BlockSpec((fw, fw), lambda b, j: (0, 0)),
                  pl.BlockSpec((tr, 2 * seq), lambda b, j: (j, 0))],
        out_specs=pl.BlockSpec((1, tr, fw), lambda b, j: (b, j, 0)),
        out_shape=jax.ShapeDtypeStruct((bsz, seq, fw), BF16),
        scratch_shapes=[pltpu.VMEM((2 * seq, fw), BF16)],
        compiler_params=_cparams("arbitrary", "arbitrary"),
        name="fourier",
    )(zf, wc, ws, tab)


def _merge_kernel(a_ref, f_ref, sa_ref, sf_ref, x_ref, mod_ref, wpa_ref, wpf_ref, wo_ref,
                  g2_ref, wrh_ref, wrl_ref, x1_ref, h2_ref, sc_ref):
    ya = jnp.dot(a_ref[...], wpa_ref[...], preferred_element_type=F32)
    yf = jnp.dot(f_ref[...], wpf_ref[...], preferred_element_type=F32)
    merged = sa_ref[...].astype(F32) * ya + sf_ref[...].astype(F32) * yf
    mod = mod_ref[0]
    g1, sh2, sc2 = mod[2:3], mod[3:4], mod[4:5]
    x1 = x_ref[...] + g1 * jnp.dot(merged.astype(BF16), wo_ref[...], preferred_element_type=F32)
    x1_ref[...] = x1
    r = lax.rsqrt(jnp.mean(x1 * x1, axis=-1, keepdims=True) + EPS)
    h2 = (x1 * r * g2_ref[...]) * (1.0 + sc2) + sh2
    h2_ref[0], h2_ref[1] = _pack_halves(h2)
    hh = h2.astype(BF16)
    hl = (h2 - hh.astype(F32)).astype(BF16)
    nt = (((1,), (1,)), ((), ()))
    lt = (lax.dot_general(wrh_ref[...], hh, nt, preferred_element_type=F32)
          + lax.dot_general(wrh_ref[...], hl, nt, preferred_element_type=F32)
          + lax.dot_general(wrl_ref[...], hh, nt, preferred_element_type=F32))
    sc_ref[...] = jax.nn.sigmoid(lt)


def _merge(attn, four, sa, sf, x2, mod, w_pa, w_pf, w_out, norm2_g, w_router, seq):
    t, d = x2.shape
    e = w_router.shape[1]
    wrt = w_router.T
    wrh = wrt.astype(BF16)
    wrl = (wrt - wrh.astype(F32)).astype(BF16)
    tm = _tile(seq, 512)
    tpb = seq // tm
    full = lambda shp: pl.BlockSpec(shp, lambda i: (0,) * len(shp))
    tok = lambda w: pl.BlockSpec((tm, w), lambda i: (i, 0))
    return pl.pallas_call(
        _merge_kernel,
        grid=(t // tm,),
        in_specs=[tok(attn.shape[1]), tok(four.shape[1]), tok(d), tok(d), tok(d),
                  pl.BlockSpec((1, N_ADA, d), lambda i: (i // tpb, 0, 0)),
                  full(w_pa.shape), full(w_pf.shape), full(w_out.shape), full((1, d)),
                  full((e, d)), full((e, d))],
        out_specs=[tok(d), pl.BlockSpec((2, tm, d // 4), lambda i: (0, i, 0)),
                   pl.BlockSpec((e, tm), lambda i: (0, i))],
        out_shape=[jax.ShapeDtypeStruct((t, d), F32),
                   jax.ShapeDtypeStruct((2, t, d // 4), jnp.uint32),
                   jax.ShapeDtypeStruct((e, t), F32)],
        compiler_params=_cparams("arbitrary"),
        name="merge",
    )(attn, four, sa, sf, x2, mod, w_pa.astype(BF16), w_pf.astype(BF16), w_out.astype(BF16),
      norm2_g.reshape(1, d), wrh, wrl)


def _route_kernel(s_ref, b_ref, tri_ref, idx_ref, w_ref, rank_ref, cnt_ref, carry_ref):
    @pl.when(pl.program_id(0) == 0)
    def _():
        carry_ref[...] = jnp.zeros_like(carry_ref)

    sc = s_ref[...]
    e, tr = sc.shape
    row = lax.broadcasted_iota(jnp.int32, (e, tr), 0)
    v = sc + b_ref[...]
    sel = jnp.zeros((e, tr), F32)
    idxs, ws = [], []
    for _ in range(TOP_K):
        m = jnp.max(v, axis=0, keepdims=True)
        idx = jnp.min(jnp.where(v == m, row, e), axis=0, keepdims=True)
        oh = row == idx
        ws.append(jnp.sum(jnp.where(oh, sc, 0.0), axis=0, keepdims=True))
        idxs.append(idx)
        v = jnp.where(oh, -jnp.inf, v)
        sel = sel + oh.astype(F32)
    wsum = ws[0]
    for w in ws[1:]:
        wsum = wsum + w
    selb = sel.astype(BF16)
    cum = jnp.dot(selb, tri_ref[...], preferred_element_type=F32) + carry_ref[...]
    for kk in range(TOP_K):
        oh = row == idxs[kk]
        rk = jnp.sum(jnp.where(oh, cum, 0.0), axis=0, keepdims=True)
        idx_ref[kk:kk + 1, :] = idxs[kk]
        rank_ref[kk:kk + 1, :] = rk.astype(jnp.int32)
        w_ref[kk:kk + 1, :] = ws[kk] / wsum * ROUTED_SCALE
    tot = carry_ref[...] + jnp.dot(selb, jnp.ones((tr, tr), BF16), preferred_element_type=F32)
    carry_ref[...] = tot
    cnt_ref[...] = tot


def _route(scores_t, router_bias):
    e, t = scores_t.shape
    tr = _tile(t, 256)
    tri = jnp.asarray(np.triu(np.ones((tr, tr), np.float32), 1), BF16)
    bias = jnp.broadcast_to(router_bias.reshape(e, 1), (e, tr)).astype(F32)
    blk = pl.BlockSpec((TOP_K, tr), lambda i: (0, i))
    return pl.pallas_call(
        _route_kernel,
        grid=(t // tr,),
        in_specs=[pl.BlockSpec((e, tr), lambda i: (0, i)),
                  pl.BlockSpec((e, tr), lambda i: (0, 0)),
                  pl.BlockSpec((tr, tr), lambda i: (0, 0))],
        out_specs=[blk, blk, blk, pl.BlockSpec((e, tr), lambda i: (0, 0))],
        out_shape=[jax.ShapeDtypeStruct((TOP_K, t), jnp.int32),
                   jax.ShapeDtypeStruct((TOP_K, t), F32),
                   jax.ShapeDtypeStruct((TOP_K, t), jnp.int32),
                   jax.ShapeDtypeStruct((e, tr), F32)],
        scratch_shapes=[pltpu.VMEM((e, tr), F32)],
        compiler_params=_cparams("arbitrary"),
        name="route",
    )(scores_t, bias, tri)


def _slots_kernel(idx_ref, rank_ref, ps_ref, slot_ref):
    ps = ps_ref[...]
    row = lax.broadcasted_iota(jnp.int32, ps.shape, 0)
    for kk in range(TOP_K):
        oh = row == idx_ref[kk:kk + 1, :]
        start = jnp.sum(jnp.where(oh, ps, 0), axis=0, keepdims=True)
        slot_ref[kk:kk + 1, :] = start + rank_ref[kk:kk + 1, :]


def _slots(idx_t, rank_t, p_start):
    _, t = idx_t.shape
    e = p_start.shape[0]
    ts = _tile(t, 512)
    ps = jnp.broadcast_to(p_start.reshape(e, 1), (e, ts))
    blk = pl.BlockSpec((TOP_K, ts), lambda i: (0, i))
    return pl.pallas_call(
        _slots_kernel,
        grid=(t // ts,),
        in_specs=[blk, blk, pl.BlockSpec((e, ts), lambda i: (0, 0))],
        out_specs=blk,
        out_shape=jax.ShapeDtypeStruct((TOP_K, t), jnp.int32),
        compiler_params=_cparams("arbitrary"),
        name="slots",
    )(idx_t, rank_t, ps)


def _dispatch(h2p, slot_t, n_slots):
    _, t, c = h2p.shape
    k = slot_t.shape[0]
    win = _tile(2 * t, SC_WINDOW)
    rows = h2p.reshape(2 * t, c)
    dest = jnp.concatenate([slot_t, slot_t + n_slots], axis=1)
    mesh = plsc.VectorSubcoreMesh(core_axis_name="core", subcore_axis_name="subcore")

    @pl.kernel(out_type=jax.ShapeDtypeStruct((2 * n_slots, c), h2p.dtype), mesh=mesh,
               scratch_types=[])
    def scatter_rows(x_hbm, s_hbm, o_hbm):
        def body(x_vmem, s_vmem):
            pltpu.sync_copy(x_vmem, o_hbm.at[s_vmem.at[0]])

        pltpu.emit_pipeline(
            body, grid=(2 * t // win, k),
            in_specs=[pl.BlockSpec((win, c), lambda i, j: (i, 0)),
                      pl.BlockSpec((1, win), lambda i, j: (j, i))],
            out_specs=[], core_axis_name=("core", "subcore"),
            dimension_semantics=(pltpu.PARALLEL, pltpu.ARBITRARY))(x_hbm, s_hbm)

    return scatter_rows(rows, dest).reshape(2, n_slots, c)


def _row_copy(src, dst, sem):
    return pltpu.make_async_copy(src, dst, sem)


def _expert_kernel(be_ref, nxt_ref, par_ref, nu_ref, xs_ref, wg_hbm, wu_hbm, wd_hbm, ys_ref,
                   wgf, wuf, wdf, wgb, wub, wdb, wsem):
    i = pl.program_id(0)
    nu = nu_ref[0]

    def weight_copies(e, which):
        return (pltpu.make_async_copy(wg_hbm.at[e], wgf.at[which], wsem.at[which]),
                pltpu.make_async_copy(wu_hbm.at[e], wuf.at[which], wsem.at[which]),
                pltpu.make_async_copy(wd_hbm.at[e], wdf.at[which], wsem.at[which]))

    @pl.when(i == 0)
    def _():
        for cp in weight_copies(be_ref[0], 0):
            cp.start(priority=1)

    @pl.when(i < nu)
    def _():
        prev = be_ref[jnp.maximum(i - 1, 0)]

        @pl.when(jnp.logical_or(i == 0, be_ref[i] != prev))
        def _():
            par = par_ref[i]
            for cp in weight_copies(be_ref[i], par):
                cp.wait()
            wgb[...] = wgf[par].astype(BF16)
            wub[...] = wuf[par].astype(BF16)
            wdb[...] = wdf[par].astype(BF16)

            @pl.when(nxt_ref[i] >= 0)
            def _():
                for cp in weight_copies(nxt_ref[i], 1 - par):
                    cp.start(priority=1)

        x = _unpack_halves(xs_ref[0], xs_ref[1])
        g = jnp.dot(x, wgb[...], preferred_element_type=F32)
        u = jnp.dot(x, wub[...], preferred_element_type=F32)
        a = (_silu(g) * u).astype(BF16)
        _store_row_tiles(ys_ref, jnp.dot(a, wdb[...], preferred_element_type=F32))

    @pl.when(i >= nu)
    def _():
        ys_ref[...] = jnp.zeros_like(ys_ref)


def _experts(blk_expert, blk_next, blk_par, nblk_used, xs, w_g, w_u, w_d):
    d, f = w_g.shape[1], w_g.shape[2]
    nch = d // LANES
    rows = EXPERT_ROWS
    nblk = blk_expert.shape[0]
    c = xs.shape[2]
    hbm = pl.BlockSpec(memory_space=pl.ANY)
    return pl.pallas_call(
        _expert_kernel,
        grid_spec=pltpu.PrefetchScalarGridSpec(
            num_scalar_prefetch=4,
            grid=(nblk,),
            in_specs=[pl.BlockSpec((2, rows, c),
                                   lambda i, be, nx, pa, nu: (0, jnp.minimum(i, nu[0] - 1), 0)),
                      hbm, hbm, hbm],
            out_specs=pl.BlockSpec((rows * nch, LANES), lambda i, be, nx, pa, nu: (i, 0)),
            scratch_shapes=[pltpu.VMEM((2, d, f), F32), pltpu.VMEM((2, d, f), F32),
                            pltpu.VMEM((2, f, d), F32),
                            pltpu.VMEM((d, f), BF16), pltpu.VMEM((d, f), BF16),
                            pltpu.VMEM((f, d), BF16),
                            pltpu.SemaphoreType.DMA((2,))]),
        out_shape=jax.ShapeDtypeStruct((nblk * rows * nch, LANES), F32),
        compiler_params=_cparams("arbitrary"),
        name="experts",
    )(blk_expert, blk_next, blk_par, nblk_used, xs, w_g, w_u, w_d)


def _combine_kernel(slot_ref, nslot_ref, w_ref, x1_ref, h2_ref, mod_ref, wsg_ref, wsu_ref,
                    wsd_ref, ys_hbm, o_ref, buf, sem, *, tc, nsteps, nch):
    i = pl.program_id(0)
    cur = i % 2

    def issue(s_ref, which):
        def body(r, c):
            for kk in range(TOP_K):
                s = s_ref[kk, r]
                _row_copy(ys_hbm.at[pl.ds(pl.multiple_of(s * nch, nch), nch)],
                          buf.at[which, kk, pl.ds(pl.multiple_of(r * nch, nch), nch)],
                          sem.at[which]).start(priority=kk % 2)
            return c
        lax.fori_loop(0, tc, body, 0)

    @pl.when(i == 0)
    def _():
        issue(slot_ref, 0)

    @pl.when(i + 1 < nsteps)
    def _():
        issue(nslot_ref, 1 - cur)

    hb = _unpack_halves(h2_ref[0], h2_ref[1])
    g = jnp.dot(hb, wsg_ref[...], preferred_element_type=F32)
    u = jnp.dot(hb, wsu_ref[...], preferred_element_type=F32)
    acc = jnp.dot((_silu(g) * u).astype(BF16), wsd_ref[...], preferred_element_type=F32)

    def drain(r, c):
        for kk in range(TOP_K):
            _row_copy(ys_hbm.at[pl.ds(0, nch)], buf.at[cur, kk, pl.ds(0, nch)], sem.at[cur]).wait()
        return c
    lax.fori_loop(0, tc, drain, 0)

    w = w_ref[...]
    for kk in range(TOP_K):
        acc = acc + w[:, kk:kk + 1] * _load_row_tiles(buf.at[cur, kk], tc, nch)
    g2 = mod_ref[0][5:6]
    o_ref[...] = x1_ref[...] + g2 * acc


def _combine(slot_t, w_tk, x1, h2t, mod, w_sg, w_su, w_sd, ys, seq):
    t, d = x1.shape
    nch = d // LANES
    tc = _tile(seq, 128)
    tpb = seq // tc
    nsteps = t // tc
    full = lambda shp: pl.BlockSpec(shp, lambda i: (0,) * len(shp))
    tok = lambda w: pl.BlockSpec((tc, w), lambda i: (i, 0))
    return pl.pallas_call(
        functools.partial(_combine_kernel, tc=tc, nsteps=nsteps, nch=nch),
        grid=(nsteps,),
        in_specs=[pl.BlockSpec((TOP_K, tc), lambda i: (0, i), memory_space=pltpu.SMEM),
                  pl.BlockSpec((TOP_K, tc), lambda i: (0, jnp.minimum(i + 1, nsteps - 1)),
                               memory_space=pltpu.SMEM),
                  tok(TOP_K), tok(d),
                  pl.BlockSpec((2, tc, d // 4), lambda i: (0, i, 0)),
                  pl.BlockSpec((1, N_ADA, d), lambda i: (i // tpb, 0, 0)),
                  full(w_sg.shape), full(w_su.shape), full(w_sd.shape),
                  pl.BlockSpec(memory_space=pl.ANY)],
        out_specs=tok(d),
        out_shape=jax.ShapeDtypeStruct((t, d), F32),
        scratch_shapes=[pltpu.VMEM((2, TOP_K, tc * nch, LANES), F32),
                        pltpu.SemaphoreType.DMA((2,))],
        compiler_params=_cparams("arbitrary"),
        name="combine",
    )(slot_t, slot_t, w_tk, x1, h2t, mod, w_sg.astype(BF16), w_su.astype(BF16),
      w_sd.astype(BF16), ys)


def _layer(x, c, w_ada, b_ada, norm1_g, w_in, q_a_norm_g, w_uq, kv_a_norm_g, w_ukv,
           q_norm_g, k_norm_g, w_proj_attn, w_proj_fourier, w_out, norm2_g,
           w_router, router_bias, w_exp_gate, w_exp_up, w_exp_down,
           w_sh_gate, w_sh_up, w_sh_down):
    bsz, seq, d = x.shape
    t = bsz * seq
    e = w_router.shape[1]
    x2 = x.reshape(t, d)

    mod = _ada(c, w_ada, b_ada).reshape(bsz, N_ADA, d)
    q, k, v, zf, sa, sf = _inproj(x2, mod, norm1_g, w_in, q_a_norm_g, w_uq, kv_a_norm_g,
                                  w_ukv, q_norm_g, k_norm_g, bsz, seq)
    attn = _attention(q, k, v).reshape(t, N_HEADS * V_DIM)
    four = _fourier(zf.reshape(bsz, seq, zf.shape[1])).reshape(t, zf.shape[1])
    x1, h2, scores_t = _merge(attn, four, sa, sf, x2, mod, w_proj_attn, w_proj_fourier,
                              w_out, norm2_g, w_router, seq)

    idx_t, w_t, rank_t, cnt = _route(scores_t, router_bias)
    counts = cnt[:, 0].astype(jnp.int32)
    rows = EXPERT_ROWS
    nblk = -(-(t * TOP_K) // rows) + e
    padded = ((counts + rows - 1) // rows) * rows
    p_end = jnp.cumsum(padded)
    p_start = p_end - padded
    nblk_used = (p_end[-1] // rows).astype(jnp.int32)
    blk_start = jnp.arange(nblk, dtype=jnp.int32) * rows
    blk_first = jnp.minimum(blk_start, p_end[-1] - 1)
    hit = p_end[None, :] <= blk_first[:, None]
    blk_expert = jnp.clip(jnp.sum(hit.astype(jnp.int32), axis=1), 0, e - 1)

    slot_t = _slots(idx_t, rank_t, p_start.astype(jnp.int32))
    xs = _dispatch(h2, slot_t, nblk * rows)
    eid = jnp.arange(e, dtype=jnp.int32)
    cand = jnp.where(counts > 0, eid, e)
    nxt = jnp.concatenate([lax.cummin(cand[::-1])[::-1][1:], jnp.full((1,), e, jnp.int32)])
    nxt = jnp.where(nxt >= e, -1, nxt)
    run = jnp.cumsum((counts > 0).astype(jnp.int32)) - 1
    onehot = blk_expert[:, None] == eid[None, :]
    blk_next = jnp.sum(jnp.where(onehot, nxt[None, :], 0), axis=1).astype(jnp.int32)
    blk_par = (jnp.sum(jnp.where(onehot, run[None, :], 0), axis=1) % 2).astype(jnp.int32)
    ys = _experts(blk_expert, blk_next, blk_par, nblk_used.reshape(1), xs,
                  w_exp_gate, w_exp_up, w_exp_down)
    out = _combine(slot_t, w_t.T, x1, h2, mod, w_sh_gate, w_sh_up, w_sh_down, ys, seq)
    return out.reshape(bsz, seq, d)


def kernel(x, c, w_ada, b_ada, norm1_g, w_in, q_a_norm_g, w_uq, kv_a_norm_g, w_ukv, q_norm_g,
           k_norm_g, w_proj_attn, w_proj_fourier, w_out, norm2_g, w_router, router_bias,
           w_exp_gate, w_exp_up, w_exp_down, w_sh_gate, w_sh_up, w_sh_down):
    for l in range(w_ada.shape[0]):
        x = _layer(x, c, w_ada[l], b_ada[l], norm1_g[l], w_in[l], q_a_norm_g[l], w_uq[l],
                   kv_a_norm_g[l], w_ukv[l], q_norm_g[l], k_norm_g[l], w_proj_attn[l],
                   w_proj_fourier[l], w_out[l], norm2_g[l], w_router[l], router_bias[l],
                   w_exp_gate[l], w_exp_up[l], w_exp_down[l], w_sh_gate[l], w_sh_up[l],
                   w_sh_down[l])
    return x
```

```python
import functools
import math

import numpy as np
import jax
import jax.numpy as jnp
from jax import lax
from jax.experimental import pallas as pl
from jax.experimental.pallas import tpu as pltpu
from jax.experimental.pallas import tpu_sc as plsc

N_HEADS = 8
QK_NOPE = 64
QK_ROPE = 32
V_DIM = 64
FOURIER_GROUP = 64
TOP_K = 8
ROUTED_SCALE = 2.5
EPS = 1e-6
ROPE_THETA = 10000.0
N_ADA = 6

LANES = 128
EXPERT_ROWS = 256
SC_WINDOW = 128
VMEM_LIMIT = 48 * 1024 * 1024

F32 = jnp.float32
BF16 = jnp.bfloat16


def _cparams(*sem):
    return pltpu.CompilerParams(dimension_semantics=sem, vmem_limit_bytes=VMEM_LIMIT)


def _tile(n, pref):
    t = min(n, pref)
    assert n % t == 0, (n, pref)
    return t


def _silu(v):
    return v * jax.nn.sigmoid(v)


def _pack_halves(m):
    d = m.shape[1]
    lo = lax.bitcast_convert_type(m[:, :d // 2].astype(BF16).astype(F32), jnp.uint32)
    hi = lax.bitcast_convert_type(m[:, d // 2:].astype(BF16).astype(F32), jnp.uint32)
    w = (lo >> 16) | (hi & jnp.uint32(0xFFFF0000))
    return w[:, :d // 4], w[:, d // 4:]


def _unpack_halves(w0, w1):
    def lo(w):
        return lax.bitcast_convert_type(w << 16, F32)

    def hi(w):
        return lax.bitcast_convert_type(w & jnp.uint32(0xFFFF0000), F32)
    return jnp.concatenate([lo(w0), lo(w1), hi(w0), hi(w1)], axis=1).astype(BF16)


def _ada_kernel(c_ref, w_ref, b_ref, o_ref):
    a = _silu(c_ref[...])
    o_ref[...] = jnp.dot(a, w_ref[...], preferred_element_type=F32,
                         precision=lax.Precision.HIGHEST) + b_ref[...]


def _ada(c, w_ada, b_ada):
    bsz, d = c.shape
    n = w_ada.shape[1]
    tn = _tile(n, d)
    return pl.pallas_call(
        _ada_kernel,
        grid=(n // tn,),
        in_specs=[pl.BlockSpec((bsz, d), lambda j: (0, 0)),
                  pl.BlockSpec((d, tn), lambda j: (0, j)),
                  pl.BlockSpec((1, tn), lambda j: (0, j))],
        out_specs=pl.BlockSpec((bsz, tn), lambda j: (0, j)),
        out_shape=jax.ShapeDtypeStruct((bsz, n), F32),
        compiler_params=_cparams("arbitrary"),
        name="ada",
    )(c, w_ada, b_ada.reshape(1, n))


def _head_norm_rope(t, trot, a, b):
    ms = jnp.sum(t * t, axis=-1, keepdims=True) * (1.0 / (QK_NOPE + QK_ROPE))
    return (t * a + trot * b) * lax.rsqrt(ms + EPS)


def _inproj_kernel(x_ref, mod_ref, g1_ref, wa_ref, wf_ref, wga_ref, wgf_ref,
                   gq_ref, gkv_ref, wuq_ref, wkv_ref,
                   aq_ref, bq_ref, ak_ref, bk_ref,
                   q_ref, k_ref, v_ref, zf_ref, sa_ref, sf_ref, *, ql, kvl):
    x = x_ref[...]
    mod = mod_ref[0]
    sh1, sc1 = mod[0:1], mod[1:2]
    r = lax.rsqrt(jnp.mean(x * x, axis=-1, keepdims=True) + EPS)
    h = (x * r * g1_ref[...]) * (1.0 + sc1) + sh1
    hb = h.astype(BF16)

    zf_ref[...] = jnp.dot(hb, wf_ref[...], preferred_element_type=F32).astype(BF16)
    sa_ref[...] = jax.nn.sigmoid(
        jnp.dot(hb, wga_ref[...], preferred_element_type=F32)).astype(BF16)
    sf_ref[...] = jax.nn.sigmoid(
        jnp.dot(hb, wgf_ref[...], preferred_element_type=F32)).astype(BF16)

    za = jnp.dot(hb, wa_ref[...], preferred_element_type=F32)
    zq = za[:, :ql]
    cq = zq * lax.rsqrt(jnp.mean(zq * zq, axis=-1, keepdims=True) + EPS) * gq_ref[...]
    qall = jnp.dot(cq.astype(BF16), wuq_ref[...], preferred_element_type=F32)

    zk = za[:, ql:]
    kvn = zk[:, :kvl]
    rk = lax.rsqrt(jnp.mean(kvn * kvn, axis=-1, keepdims=True) + EPS)
    lane = lax.broadcasted_iota(jnp.int32, zk.shape, 1)
    u = zk * jnp.where(lane < kvl, rk, 1.0) * gkv_ref[...]
    kvall = jnp.dot(u.astype(BF16), wkv_ref[...], preferred_element_type=F32)

    aq, bq, ak, bk = aq_ref[...], bq_ref[...], ak_ref[...], bk_ref[...]
    hw = N_HEADS * LANES
    for hd in range(N_HEADS):
        lo, hi = hd * LANES, (hd + 1) * LANES
        q_ref[0, hd] = _head_norm_rope(qall[:, lo:hi], qall[:, hw + lo:hw + hi],
                                       aq, bq).astype(BF16)
        k_ref[0, hd] = _head_norm_rope(kvall[:, lo:hi], kvall[:, hw + lo:hw + hi],
                                       ak, bk).astype(BF16)
        v_ref[0, hd] = kvall[:, 2 * hw + hd * V_DIM: 2 * hw + (hd + 1) * V_DIM].astype(BF16)


def _rope_tables(seq):
    half = QK_ROPE // 2
    pos = np.arange(seq, dtype=np.float64)
    inv = ROPE_THETA ** (-np.arange(0, QK_ROPE, 2, dtype=np.float64) / QK_ROPE)
    ang = pos[:, None] * inv[None, :]
    c, s = np.cos(ang), np.sin(ang)
    cos = np.ones((seq, LANES)); sin = np.zeros((seq, LANES))
    cos[:, QK_NOPE:QK_NOPE + half] = c
    cos[:, QK_NOPE + half:QK_NOPE + QK_ROPE] = c
    sin[:, QK_NOPE:QK_NOPE + half] = -s
    sin[:, QK_NOPE + half:QK_NOPE + QK_ROPE] = s
    return jnp.asarray(cos, F32), jnp.asarray(sin, F32)


def _partner_columns(w):
    half = QK_ROPE // 2
    lo, mid, hi = QK_NOPE, QK_NOPE + half, QK_NOPE + QK_ROPE
    z = jnp.zeros_like(w)
    return jnp.concatenate([z[..., :lo], w[..., mid:hi], w[..., lo:mid], z[..., hi:]], axis=-1)


def _inproj(x2, mod, norm1_g, w_in, q_a_g, w_uq, kv_a_g, w_ukv, q_g, k_g, bsz, seq):
    t, d = x2.shape
    ql, kvl = q_a_g.shape[0], kv_a_g.shape[0]
    hq = QK_NOPE + QK_ROPE
    fw = w_in.shape[1] - ql - kvl - QK_ROPE - 2 * d
    o1, o2, o3, o4, o5 = ql, ql + kvl, ql + kvl + QK_ROPE, ql + kvl + QK_ROPE + fw, \
        ql + kvl + QK_ROPE + fw + d
    assert ql % LANES == 0 and kvl % LANES == 0

    wa = jnp.concatenate([w_in[:, :o3], jnp.zeros((d, LANES - QK_ROPE), F32)], axis=1).astype(BF16)
    wf = w_in[:, o3:o4].astype(BF16)
    wga = w_in[:, o4:o5].astype(BF16)
    wgf = w_in[:, o5:].astype(BF16)

    wuq = w_uq.reshape(ql, N_HEADS, hq)
    wuq = jnp.pad(wuq, ((0, 0), (0, 0), (0, LANES - hq)))
    wuq = jnp.concatenate([wuq.reshape(ql, N_HEADS * LANES),
                           _partner_columns(wuq).reshape(ql, N_HEADS * LANES)], axis=1).astype(BF16)
    wukv = w_ukv.reshape(kvl, N_HEADS, QK_NOPE + V_DIM)
    wk = jnp.pad(wukv[:, :, :QK_NOPE], ((0, 0), (0, 0), (0, LANES - QK_NOPE)))
    place = jnp.zeros((QK_ROPE, N_HEADS, LANES), F32)
    place = place.at[jnp.arange(QK_ROPE), :, QK_NOPE + jnp.arange(QK_ROPE)].set(1.0)
    wk = jnp.concatenate([wk, place, jnp.zeros((LANES - QK_ROPE, N_HEADS, LANES), F32)], axis=0)
    wv = jnp.concatenate([wukv[:, :, QK_NOPE:], jnp.zeros((LANES, N_HEADS, V_DIM), F32)], axis=0)
    wkv = jnp.concatenate([wk.reshape(kvl + LANES, N_HEADS * LANES),
                           _partner_columns(wk).reshape(kvl + LANES, N_HEADS * LANES),
                           wv.reshape(kvl + LANES, N_HEADS * V_DIM)], axis=1).astype(BF16)

    gkv = jnp.concatenate([kv_a_g, jnp.ones((LANES,), F32)]).reshape(1, kvl + LANES)
    pad = jnp.zeros((LANES - hq,), F32)
    qg = jnp.concatenate([q_g * (hq ** -0.5), pad])
    kg = jnp.concatenate([k_g, pad])
    cos, sin = _rope_tables(seq)
    aq, bq = qg[None, :] * cos, _partner_columns(qg)[None, :] * sin
    ak, bk = kg[None, :] * cos, _partner_columns(kg)[None, :] * sin

    tm = _tile(seq, 512)
    tpb = seq // tm
    full = lambda shp: pl.BlockSpec(shp, lambda i: (0,) * len(shp))
    tok = lambda w: pl.BlockSpec((tm, w), lambda i: (i, 0))
    head = lambda w: pl.BlockSpec((1, N_HEADS, tm, w), lambda i: (i // tpb, 0, i % tpb, 0))
    rope = pl.BlockSpec((tm, LANES), lambda i: (i % tpb, 0))
    return pl.pallas_call(
        functools.partial(_inproj_kernel, ql=ql, kvl=kvl),
        grid=(t // tm,),
        in_specs=[tok(d),
                  pl.BlockSpec((1, N_ADA, d), lambda i: (i // tpb, 0, 0)),
                  full((1, d)), full(wa.shape), full(wf.shape), full(wga.shape), full(wgf.shape),
                  full((1, ql)), full(gkv.shape), full(wuq.shape), full(wkv.shape),
                  rope, rope, rope, rope],
        out_specs=[head(LANES), head(LANES), head(V_DIM), tok(fw), tok(d), tok(d)],
        out_shape=[jax.ShapeDtypeStruct((bsz, N_HEADS, seq, LANES), BF16),
                   jax.ShapeDtypeStruct((bsz, N_HEADS, seq, LANES), BF16),
                   jax.ShapeDtypeStruct((bsz, N_HEADS, seq, V_DIM), BF16),
                   jax.ShapeDtypeStruct((t, fw), BF16),
                   jax.ShapeDtypeStruct((t, d), BF16),
                   jax.ShapeDtypeStruct((t, d), BF16)],
        compiler_params=_cparams("arbitrary"),
        name="inproj",
    )(x2, mod, norm1_g.reshape(1, d), wa, wf, wga, wgf, q_a_g.reshape(1, ql), gkv, wuq, wkv,
      aq, bq, ak, bk)


def _attn_kernel(q_ref, k_ref, v_ref, o_ref):
    for hd in range(N_HEADS):
        s = lax.dot_general(q_ref[0, hd], k_ref[0, hd], (((1,), (1,)), ((), ())),
                            preferred_element_type=F32)
        m = jnp.max(s, axis=-1, keepdims=True)
        p = jnp.exp(s - m)
        l = jnp.sum(p, axis=-1, keepdims=True)
        o = jnp.dot(p.astype(BF16), v_ref[0, hd], preferred_element_type=F32)
        o_ref[0, :, hd * V_DIM:(hd + 1) * V_DIM] = (o / l).astype(BF16)


def _attention(q, k, v):
    bsz, _, seq, _ = q.shape
    tq = _tile(seq, 256)
    return pl.pallas_call(
        _attn_kernel,
        grid=(bsz, seq // tq),
        in_specs=[pl.BlockSpec((1, N_HEADS, tq, LANES), lambda b, j: (b, 0, j, 0)),
                  pl.BlockSpec((1, N_HEADS, seq, LANES), lambda b, j: (b, 0, 0, 0)),
                  pl.BlockSpec((1, N_HEADS, seq, V_DIM), lambda b, j: (b, 0, 0, 0))],
        out_specs=pl.BlockSpec((1, tq, N_HEADS * V_DIM), lambda b, j: (b, j, 0)),
        out_shape=jax.ShapeDtypeStruct((bsz, seq, N_HEADS * V_DIM), BF16),
        compiler_params=_cparams("arbitrary", "arbitrary"),
        name="attn",
    )(q, k, v)


def _fourier_kernel(z_ref, wc_ref, ws_ref, tab_ref, o_ref, u_ref, *, seq):
    @pl.when(pl.program_id(1) == 0)
    def _():
        z = z_ref[0]
        u_ref[:seq, :] = jnp.dot(z, wc_ref[...], preferred_element_type=F32).astype(BF16)
        u_ref[seq:, :] = jnp.dot(z, ws_ref[...], preferred_element_type=F32).astype(BF16)

    o_ref[0] = jnp.dot(tab_ref[...], u_ref[...], preferred_element_type=F32).astype(BF16)


def _fourier_tables(seq, fw):
    g = FOURIER_GROUP
    n = np.arange(seq, dtype=np.int64)
    ang = 2.0 * np.pi * ((n[:, None] * n[None, :]) % seq).astype(np.float64) / seq
    tab = np.concatenate([np.cos(ang), -np.sin(ang)], axis=1)
    c = np.arange(g, dtype=np.int64)
    angc = 2.0 * np.pi * ((c[:, None] * c[None, :]) % g).astype(np.float64) / g
    scale = 1.0 / math.sqrt(seq * g)
    eye = np.eye(fw // g)
    wc = np.kron(eye, np.cos(angc) * scale)
    ws = np.kron(eye, np.sin(angc) * scale)
    return (jnp.asarray(tab, F32).astype(BF16), jnp.asarray(wc, F32).astype(BF16),
            jnp.asarray(ws, F32).astype(BF16))


def _fourier(zf):
    bsz, seq, fw = zf.shape
    tab, wc, ws = _fourier_tables(seq, fw)
    tr = _tile(seq, 256)
    return pl.pallas_call(
        functools.partial(_fourier_kernel, seq=seq),
        grid=(bsz, seq // tr),
        in_specs=[pl.BlockSpec((1, seq, fw), lambda b, j: (b, 0, 0)),
                  pl.BlockSpec((fw, fw), lambda b, j: (0, 0)),
                  pl.BlockSpec((fw, fw), lambda b, j: (0, 0)),
                  pl.BlockSpec((tr, 2 * seq), lambda b, j: (j, 0))],
        out_specs=pl.BlockSpec((1, tr, fw), lambda b, j: (b, j, 0)),
        out_shape=jax.ShapeDtypeStruct((bsz, seq, fw), BF16),
        scratch_shapes=[pltpu.VMEM((2 * seq, fw), BF16)],
        compiler_params=_cparams("arbitrary", "arbitrary"),
        name="fourier",
    )(zf, wc, ws, tab)


def _merge_kernel(a_ref, f_ref, sa_ref, sf_ref, x_ref, mod_ref, wpa_ref, wpf_ref, wo_ref,
                  g2_ref, wrh_ref, wrl_ref, x1_ref, h2_ref, sc_ref):
    ya = jnp.dot(a_ref[...], wpa_ref[...], preferred_element_type=F32)
    yf = jnp.dot(f_ref[...], wpf_ref[...], preferred_element_type=F32)
    merged = sa_ref[...].astype(F32) * ya + sf_ref[...].astype(F32) * yf
    mod = mod_ref[0]
    g1, sh2, sc2 = mod[2:3], mod[3:4], mod[4:5]
    x1 = x_ref[...] + g1 * jnp.dot(merged.astype(BF16), wo_ref[...], preferred_element_type=F32)
    x1_ref[...] = x1
    r = lax.rsqrt(jnp.mean(x1 * x1, axis=-1, keepdims=True) + EPS)
    h2 = (x1 * r * g2_ref[...]) * (1.0 + sc2) + sh2
    h2_ref[0], h2_ref[1] = _pack_halves(h2)
    hh = h2.astype(BF16)
    hl = (h2 - hh.astype(F32)).astype(BF16)
    nt = (((1,), (1,)), ((), ()))
    lt = (lax.dot_general(wrh_ref[...], hh, nt, preferred_element_type=F32)
          + lax.dot_general(wrh_ref[...], hl, nt, preferred_element_type=F32)
          + lax.dot_general(wrl_ref[...], hh, nt, preferred_element_type=F32))
    sc_ref[...] = jax.nn.sigmoid(lt)


def _merge(attn, four, sa, sf, x2, mod, w_pa, w_pf, w_out, norm2_g, w_router, seq):
    t, d = x2.shape
    e = w_router.shape[1]
    wrt = w_router.T
    wrh = wrt.astype(BF16)
    wrl = (wrt - wrh.astype(F32)).astype(BF16)
    tm = _tile(seq, 512)
    tpb = seq // tm
    full = lambda shp: pl.BlockSpec(shp, lambda i: (0,) * len(shp))
    tok = lambda w: pl.BlockSpec((tm, w), lambda i: (i, 0))
    return pl.pallas_call(
        _merge_kernel,
        grid=(t // tm,),
        in_specs=[tok(attn.shape[1]), tok(four.shape[1]), tok(d), tok(d), tok(d),
                  pl.BlockSpec((1, N_ADA, d), lambda i: (i // tpb, 0, 0)),
                  full(w_pa.shape), full(w_pf.shape), full(w_out.shape), full((1, d)),
                  full((e, d)), full((e, d))],
        out_specs=[tok(d), pl.BlockSpec((2, tm, d // 4), lambda i: (0, i, 0)),
                   pl.BlockSpec((e, tm), lambda i: (0, i))],
        out_shape=[jax.ShapeDtypeStruct((t, d), F32),
                   jax.ShapeDtypeStruct((2, t, d // 4), jnp.uint32),
                   jax.ShapeDtypeStruct((e, t), F32)],
        compiler_params=_cparams("arbitrary"),
        name="merge",
    )(attn, four, sa, sf, x2, mod, w_pa.astype(BF16), w_pf.astype(BF16), w_out.astype(BF16),
      norm2_g.reshape(1, d), wrh, wrl)


def _route_kernel(s_ref, b_ref, tri_ref, idx_ref, w_ref, rank_ref, cnt_ref, carry_ref):
    @pl.when(pl.program_id(0) == 0)
    def _():
        carry_ref[...] = jnp.zeros_like(carry_ref)

    sc = s_ref[...]
    e, tr = sc.shape
    row = lax.broadcasted_iota(jnp.int32, (e, tr), 0)
    v = sc + b_ref[...]
    sel = jnp.zeros((e, tr), F32)
    idxs, ws = [], []
    for _ in range(TOP_K):
        m = jnp.max(v, axis=0, keepdims=True)
        idx = jnp.min(jnp.where(v == m, row, e), axis=0, keepdims=True)
        oh = row == idx
        ws.append(jnp.sum(jnp.where(oh, sc, 0.0), axis=0, keepdims=True))
        idxs.append(idx)
        v = jnp.where(oh, -jnp.inf, v)
        sel = sel + oh.astype(F32)
    wsum = ws[0]
    for w in ws[1:]:
        wsum = wsum + w
    selb = sel.astype(BF16)
    cum = jnp.dot(selb, tri_ref[...], preferred_element_type=F32) + carry_ref[...]
    for kk in range(TOP_K):
        oh = row == idxs[kk]
        rk = jnp.sum(jnp.where(oh, cum, 0.0), axis=0, keepdims=True)
        idx_ref[kk:kk + 1, :] = idxs[kk]
        rank_ref[kk:kk + 1, :] = rk.astype(jnp.int32)
        w_ref[kk:kk + 1, :] = ws[kk] / wsum * ROUTED_SCALE
    tot = carry_ref[...] + jnp.dot(selb, jnp.ones((tr, tr), BF16), preferred_element_type=F32)
    carry_ref[...] = tot
    cnt_ref[...] = tot


def _route(scores_t, router_bias):
    e, t = scores_t.shape
    tr = _tile(t, 256)
    tri = jnp.asarray(np.triu(np.ones((tr, tr), np.float32), 1), BF16)
    bias = jnp.broadcast_to(router_bias.reshape(e, 1), (e, tr)).astype(F32)
    blk = pl.BlockSpec((TOP_K, tr), lambda i: (0, i))
    return pl.pallas_call(
        _route_kernel,
        grid=(t // tr,),
        in_specs=[pl.BlockSpec((e, tr), lambda i: (0, i)),
                  pl.BlockSpec((e, tr), lambda i: (0, 0)),
                  pl.BlockSpec((tr, tr), lambda i: (0, 0))],
        out_specs=[blk, blk, blk, pl.BlockSpec((e, tr), lambda i: (0, 0))],
        out_shape=[jax.ShapeDtypeStruct((TOP_K, t), jnp.int32),
                   jax.ShapeDtypeStruct((TOP_K, t), F32),
                   jax.ShapeDtypeStruct((TOP_K, t), jnp.int32),
                   jax.ShapeDtypeStruct((e, tr), F32)],
        scratch_shapes=[pltpu.VMEM((e, tr), F32)],
        compiler_params=_cparams("arbitrary"),
        name="route",
    )(scores_t, bias, tri)


def _slots_kernel(idx_ref, rank_ref, ps_ref, slot_ref):
    ps = ps_ref[...]
    row = lax.broadcasted_iota(jnp.int32, ps.shape, 0)
    for kk in range(TOP_K):
        oh = row == idx_ref[kk:kk + 1, :]
        start = jnp.sum(jnp.where(oh, ps, 0), axis=0, keepdims=True)
        slot_ref[kk:kk + 1, :] = start + rank_ref[kk:kk + 1, :]


def _slots(idx_t, rank_t, p_start):
    _, t = idx_t.shape
    e = p_start.shape[0]
    ts = _tile(t, 512)
    ps = jnp.broadcast_to(p_start.reshape(e, 1), (e, ts))
    blk = pl.BlockSpec((TOP_K, ts), lambda i: (0, i))
    return pl.pallas_call(
        _slots_kernel,
        grid=(t // ts,),
        in_specs=[blk, blk, pl.BlockSpec((e, ts), lambda i: (0, 0))],
        out_specs=blk,
        out_shape=jax.ShapeDtypeStruct((TOP_K, t), jnp.int32),
        compiler_params=_cparams("arbitrary"),
        name="slots",
    )(idx_t, rank_t, ps)


def _dispatch(h2p, slot_t, n_slots):
    _, t, c = h2p.shape
    k = slot_t.shape[0]
    win = _tile(2 * t, SC_WINDOW)
    rows = h2p.reshape(2 * t, c)
    dest = jnp.concatenate([slot_t, slot_t + n_slots], axis=1)
    mesh = plsc.VectorSubcoreMesh(core_axis_name="core", subcore_axis_name="subcore")

    @pl.kernel(out_type=jax.ShapeDtypeStruct((2 * n_slots, c), h2p.dtype), mesh=mesh,
               scratch_types=[])
    def scatter_rows(x_hbm, s_hbm, o_hbm):
        def body(x_vmem, s_vmem):
            pltpu.sync_copy(x_vmem, o_hbm.at[s_vmem.at[0]])

        pltpu.emit_pipeline(
            body, grid=(2 * t // win, k),
            in_specs=[pl.BlockSpec((win, c), lambda i, j: (i, 0)),
                      pl.BlockSpec((1, win), lambda i, j: (j, i))],
            out_specs=[], core_axis_name=("core", "subcore"),
            dimension_semantics=(pltpu.PARALLEL, pltpu.ARBITRARY))(x_hbm, s_hbm)

    return scatter_rows(rows, dest).reshape(2, n_slots, c)


def _expert_kernel(be_ref, nxt_ref, par_ref, nu_ref, xs_ref, wg_hbm, wu_hbm, wd_hbm, ys_ref,
                   wgf, wuf, wdf, wgb, wub, wdb, wsem):
    i = pl.program_id(0)
    nu = nu_ref[0]

    def weight_copies(e, which):
        return (pltpu.make_async_copy(wg_hbm.at[e], wgf.at[which], wsem.at[which]),
                pltpu.make_async_copy(wu_hbm.at[e], wuf.at[which], wsem.at[which]),
                pltpu.make_async_copy(wd_hbm.at[e], wdf.at[which], wsem.at[which]))

    @pl.when(i == 0)
    def _():
        for cp in weight_copies(be_ref[0], 0):
            cp.start(priority=1)

    @pl.when(i < nu)
    def _():
        prev = be_ref[jnp.maximum(i - 1, 0)]

        @pl.when(jnp.logical_or(i == 0, be_ref[i] != prev))
        def _():
            par = par_ref[i]
            for cp in weight_copies(be_ref[i], par):
                cp.wait()
            wgb[...] = wgf[par].astype(BF16)
            wub[...] = wuf[par].astype(BF16)
            wdb[...] = wdf[par].astype(BF16)

            @pl.when(nxt_ref[i] >= 0)
            def _():
                for cp in weight_copies(nxt_ref[i], 1 - par):
                    cp.start(priority=1)

        x = _unpack_halves(xs_ref[0], xs_ref[1])
        g = jnp.dot(x, wgb[...], preferred_element_type=F32)
        u = jnp.dot(x, wub[...], preferred_element_type=F32)
        a = (_silu(g) * u).astype(BF16)
        ys_ref[0], ys_ref[1] = _pack_halves(jnp.dot(a, wdb[...], preferred_element_type=F32))

    @pl.when(i >= nu)
    def _():
        ys_ref[...] = jnp.zeros_like(ys_ref)


def _experts(blk_expert, blk_next, blk_par, nblk_used, xs, w_g, w_u, w_d):
    d, f = w_g.shape[1], w_g.shape[2]
    rows = EXPERT_ROWS
    nblk = blk_expert.shape[0]
    c = xs.shape[2]
    hbm = pl.BlockSpec(memory_space=pl.ANY)
    return pl.pallas_call(
        _expert_kernel,
        grid_spec=pltpu.PrefetchScalarGridSpec(
            num_scalar_prefetch=4,
            grid=(nblk,),
            in_specs=[pl.BlockSpec((2, rows, c),
                                   lambda i, be, nx, pa, nu: (0, jnp.minimum(i, nu[0] - 1), 0)),
                      hbm, hbm, hbm],
            out_specs=pl.BlockSpec((2, rows, c), lambda i, be, nx, pa, nu: (0, i, 0)),
            scratch_shapes=[pltpu.VMEM((2, d, f), F32), pltpu.VMEM((2, d, f), F32),
                            pltpu.VMEM((2, f, d), F32),
                            pltpu.VMEM((d, f), BF16), pltpu.VMEM((d, f), BF16),
                            pltpu.VMEM((f, d), BF16),
                            pltpu.SemaphoreType.DMA((2,))]),
        out_shape=jax.ShapeDtypeStruct((2, nblk * rows, c), jnp.uint32),
        compiler_params=_cparams("arbitrary"),
        name="experts",
    )(blk_expert, blk_next, blk_par, nblk_used, xs, w_g, w_u, w_d)


def _gather(ys, slot_t):
    _, n_slots, c = ys.shape
    k, t = slot_t.shape
    p = 2 * k * t
    win = _tile(p, SC_WINDOW)
    src = jnp.concatenate([slot_t, slot_t + n_slots], axis=0).reshape(1, p)
    mesh = plsc.VectorSubcoreMesh(core_axis_name="core", subcore_axis_name="subcore")

    @pl.kernel(out_type=jax.ShapeDtypeStruct((p, c), ys.dtype), mesh=mesh, scratch_types=[])
    def gather_rows(y_hbm, s_hbm, o_hbm):
        def body(s_vmem, o_vmem):
            pltpu.sync_copy(y_hbm.at[s_vmem.at[0]], o_vmem)

        pltpu.emit_pipeline(
            body, grid=(p // win,),
            in_specs=[pl.BlockSpec((1, win), lambda i: (0, i))],
            out_specs=[pl.BlockSpec((win, c), lambda i: (i, 0))],
            core_axis_name=("core", "subcore"),
            dimension_semantics=(pltpu.PARALLEL,))(s_hbm, o_hbm)

    return gather_rows(ys.reshape(2 * n_slots, c), src).reshape(2, k, t, c)


def _combine_kernel(y_ref, w_ref, x1_ref, h2_ref, mod_ref, wsg_ref, wsu_ref, wsd_ref, o_ref):
    hb = _unpack_halves(h2_ref[0], h2_ref[1])
    g = jnp.dot(hb, wsg_ref[...], preferred_element_type=F32)
    u = jnp.dot(hb, wsu_ref[...], preferred_element_type=F32)
    acc = jnp.dot((_silu(g) * u).astype(BF16), wsd_ref[...], preferred_element_type=F32)
    w = w_ref[...]
    for kk in range(TOP_K):
        acc = acc + w[:, kk:kk + 1] * _unpack_halves(y_ref[0, kk], y_ref[1, kk]).astype(F32)
    g2 = mod_ref[0][5:6]
    o_ref[...] = x1_ref[...] + g2 * acc


def _combine(y_tok, w_tk, x1, h2p, mod, w_sg, w_su, w_sd, seq):
    t, d = x1.shape
    c = d // 4
    tc = _tile(seq, 256)
    tpb = seq // tc
    full = lambda shp: pl.BlockSpec(shp, lambda i: (0,) * len(shp))
    tok = lambda w: pl.BlockSpec((tc, w), lambda i: (i, 0))
    return pl.pallas_call(
        _combine_kernel,
        grid=(t // tc,),
        in_specs=[pl.BlockSpec((2, TOP_K, tc, c), lambda i: (0, 0, i, 0)),
                  tok(TOP_K), tok(d),
                  pl.BlockSpec((2, tc, c), lambda i: (0, i, 0)),
                  pl.BlockSpec((1, N_ADA, d), lambda i: (i // tpb, 0, 0)),
                  full(w_sg.shape), full(w_su.shape), full(w_sd.shape)],
        out_specs=tok(d),
        out_shape=jax.ShapeDtypeStruct((t, d), F32),
        compiler_params=_cparams("arbitrary"),
        name="combine",
    )(y_tok, w_tk, x1, h2p, mod, w_sg.astype(BF16), w_su.astype(BF16), w_sd.astype(BF16))


def _layer(x, c, w_ada, b_ada, norm1_g, w_in, q_a_norm_g, w_uq, kv_a_norm_g, w_ukv,
           q_norm_g, k_norm_g, w_proj_attn, w_proj_fourier, w_out, norm2_g,
           w_router, router_bias, w_exp_gate, w_exp_up, w_exp_down,
           w_sh_gate, w_sh_up, w_sh_down):
    bsz, seq, d = x.shape
    t = bsz * seq
    e = w_router.shape[1]
    x2 = x.reshape(t, d)

    mod = _ada(c, w_ada, b_ada).reshape(bsz, N_ADA, d)
    q, k, v, zf, sa, sf = _inproj(x2, mod, norm1_g, w_in, q_a_norm_g, w_uq, kv_a_norm_g,
                                  w_ukv, q_norm_g, k_norm_g, bsz, seq)
    attn = _attention(q, k, v).reshape(t, N_HEADS * V_DIM)
    four = _fourier(zf.reshape(bsz, seq, zf.shape[1])).reshape(t, zf.shape[1])
    x1, h2, scores_t = _merge(attn, four, sa, sf, x2, mod, w_proj_attn, w_proj_fourier,
                              w_out, norm2_g, w_router, seq)

    idx_t, w_t, rank_t, cnt = _route(scores_t, router_bias)
    counts = cnt[:, 0].astype(jnp.int32)
    rows = EXPERT_ROWS
    nblk = -(-(t * TOP_K) // rows) + e
    padded = ((counts + rows - 1) // rows) * rows
    p_end = jnp.cumsum(padded)
    p_start = p_end - padded
    nblk_used = (p_end[-1] // rows).astype(jnp.int32)
    blk_start = jnp.arange(nblk, dtype=jnp.int32) * rows
    blk_first = jnp.minimum(blk_start, p_end[-1] - 1)
    hit = p_end[None, :] <= blk_first[:, None]
    blk_expert = jnp.clip(jnp.sum(hit.astype(jnp.int32), axis=1), 0, e - 1)

    slot_t = _slots(idx_t, rank_t, p_start.astype(jnp.int32))
    xs = _dispatch(h2, slot_t, nblk * rows)
    eid = jnp.arange(e, dtype=jnp.int32)
    cand = jnp.where(counts > 0, eid, e)
    nxt = jnp.concatenate([lax.cummin(cand[::-1])[::-1][1:], jnp.full((1,), e, jnp.int32)])
    nxt = jnp.where(nxt >= e, -1, nxt)
    run = jnp.cumsum((counts > 0).astype(jnp.int32)) - 1
    onehot = blk_expert[:, None] == eid[None, :]
    blk_next = jnp.sum(jnp.where(onehot, nxt[None, :], 0), axis=1).astype(jnp.int32)
    blk_par = (jnp.sum(jnp.where(onehot, run[None, :], 0), axis=1) % 2).astype(jnp.int32)
    ys = _experts(blk_expert, blk_next, blk_par, nblk_used.reshape(1), xs,
                  w_exp_gate, w_exp_up, w_exp_down)
    out = _combine(_gather(ys, slot_t), w_t.T, x1, h2, mod, w_sh_gate, w_sh_up, w_sh_down, seq)
    return out.reshape(bsz, seq, d)


def kernel(x, c, w_ada, b_ada, norm1_g, w_in, q_a_norm_g, w_uq, kv_a_norm_g, w_ukv, q_norm_g,
           k_norm_g, w_proj_attn, w_proj_fourier, w_out, norm2_g, w_router, router_bias,
           w_exp_gate, w_exp_up, w_exp_down, w_sh_gate, w_sh_up, w_sh_down):
    for l in range(w_ada.shape[0]):
        x = _layer(x, c, w_ada[l], b_ada[l], norm1_g[l], w_in[l], q_a_norm_g[l], w_uq[l],
                   kv_a_norm_g[l], w_ukv[l], q_norm_g[l], k_norm_g[l], w_proj_attn[l],
                   w_proj_fourier[l], w_out[l], norm2_g[l], w_router[l], router_bias[l],
                   w_exp_gate[l], w_exp_up[l], w_exp_down[l], w_sh_gate[l], w_sh_up[l],
                   w_sh_down[l])
    return x
```

```python
import functools
import math

import numpy as np
import jax
import jax.numpy as jnp
from jax import lax
from jax.experimental import pallas as pl
from jax.experimental.pallas import tpu as pltpu
from jax.experimental.pallas import tpu_sc as plsc

N_HEADS = 8
QK_NOPE = 64
QK_ROPE = 32
V_DIM = 64
FOURIER_GROUP = 64
TOP_K = 8
ROUTED_SCALE = 2.5
EPS = 1e-6
ROPE_THETA = 10000.0
N_ADA = 6

LANES = 128
EXPERT_ROWS = 256
X_RING = 4
SC_WINDOW = 128
VMEM_LIMIT = 48 * 1024 * 1024

F32 = jnp.float32
BF16 = jnp.bfloat16


def _cparams(*sem):
    return pltpu.CompilerParams(dimension_semantics=sem, vmem_limit_bytes=VMEM_LIMIT)


def _tile(n, pref):
    t = min(n, pref)
    assert n % t == 0, (n, pref)
    return t


def _silu(v):
    return v * jax.nn.sigmoid(v)


def _pack_halves(m):
    d = m.shape[1]
    lo = lax.bitcast_convert_type(m[:, :d // 2].astype(BF16).astype(F32), jnp.uint32)
    hi = lax.bitcast_convert_type(m[:, d // 2:].astype(BF16).astype(F32), jnp.uint32)
    w = (lo >> 16) | (hi & jnp.uint32(0xFFFF0000))
    return w[:, :d // 4], w[:, d // 4:]


def _unpack_halves(w0, w1):
    def lo(w):
        return lax.bitcast_convert_type(w << 16, F32)

    def hi(w):
        return lax.bitcast_convert_type(w & jnp.uint32(0xFFFF0000), F32)
    return jnp.concatenate([lo(w0), lo(w1), hi(w0), hi(w1)], axis=1).astype(BF16)


def _ada_kernel(c_ref, w_ref, b_ref, o_ref):
    a = _silu(c_ref[...])
    o_ref[...] = jnp.dot(a, w_ref[...], preferred_element_type=F32,
                         precision=lax.Precision.HIGHEST) + b_ref[...]


def _ada(c, w_ada, b_ada):
    bsz, d = c.shape
    n = w_ada.shape[1]
    tn = _tile(n, d)
    return pl.pallas_call(
        _ada_kernel,
        grid=(n // tn,),
        in_specs=[pl.BlockSpec((bsz, d), lambda j: (0, 0)),
                  pl.BlockSpec((d, tn), lambda j: (0, j)),
                  pl.BlockSpec((1, tn), lambda j: (0, j))],
        out_specs=pl.BlockSpec((bsz, tn), lambda j: (0, j)),
        out_shape=jax.ShapeDtypeStruct((bsz, n), F32),
        compiler_params=_cparams("arbitrary"),
        name="ada",
    )(c, w_ada, b_ada.reshape(1, n))


def _head_norm_rope(t, trot, a, b):
    ms = jnp.sum(t * t, axis=-1, keepdims=True) * (1.0 / (QK_NOPE + QK_ROPE))
    return (t * a + trot * b) * lax.rsqrt(ms + EPS)


def _inproj_kernel(x_ref, mod_ref, g1_ref, wa_ref, wf_ref, wga_ref, wgf_ref,
                   gq_ref, gkv_ref, wuq_ref, wkv_ref,
                   aq_ref, bq_ref, ak_ref, bk_ref,
                   q_ref, k_ref, v_ref, zf_ref, sa_ref, sf_ref, *, ql, kvl):
    x = x_ref[...]
    mod = mod_ref[0]
    sh1, sc1 = mod[0:1], mod[1:2]
    r = lax.rsqrt(jnp.mean(x * x, axis=-1, keepdims=True) + EPS)
    h = (x * r * g1_ref[...]) * (1.0 + sc1) + sh1
    hb = h.astype(BF16)

    zf_ref[...] = jnp.dot(hb, wf_ref[...], preferred_element_type=F32).astype(BF16)
    sa_ref[...] = jax.nn.sigmoid(
        jnp.dot(hb, wga_ref[...], preferred_element_type=F32)).astype(BF16)
    sf_ref[...] = jax.nn.sigmoid(
        jnp.dot(hb, wgf_ref[...], preferred_element_type=F32)).astype(BF16)

    za = jnp.dot(hb, wa_ref[...], preferred_element_type=F32)
    zq = za[:, :ql]
    cq = zq * lax.rsqrt(jnp.mean(zq * zq, axis=-1, keepdims=True) + EPS) * gq_ref[...]
    qall = jnp.dot(cq.astype(BF16), wuq_ref[...], preferred_element_type=F32)

    zk = za[:, ql:]
    kvn = zk[:, :kvl]
    rk = lax.rsqrt(jnp.mean(kvn * kvn, axis=-1, keepdims=True) + EPS)
    lane = lax.broadcasted_iota(jnp.int32, zk.shape, 1)
    u = zk * jnp.where(lane < kvl, rk, 1.0) * gkv_ref[...]
    kvall = jnp.dot(u.astype(BF16), wkv_ref[...], preferred_element_type=F32)

    aq, bq, ak, bk = aq_ref[...], bq_ref[...], ak_ref[...], bk_ref[...]
    hw = N_HEADS * LANES
    for hd in range(N_HEADS):
        lo, hi = hd * LANES, (hd + 1) * LANES
        q_ref[0, hd] = _head_norm_rope(qall[:, lo:hi], qall[:, hw + lo:hw + hi],
                                       aq, bq).astype(BF16)
        k_ref[0, hd] = _head_norm_rope(kvall[:, lo:hi], kvall[:, hw + lo:hw + hi],
                                       ak, bk).astype(BF16)
        v_ref[0, hd] = kvall[:, 2 * hw + hd * V_DIM: 2 * hw + (hd + 1) * V_DIM].astype(BF16)


def _rope_tables(seq):
    half = QK_ROPE // 2
    pos = np.arange(seq, dtype=np.float64)
    inv = ROPE_THETA ** (-np.arange(0, QK_ROPE, 2, dtype=np.float64) / QK_ROPE)
    ang = pos[:, None] * inv[None, :]
    c, s = np.cos(ang), np.sin(ang)
    cos = np.ones((seq, LANES)); sin = np.zeros((seq, LANES))
    cos[:, QK_NOPE:QK_NOPE + half] = c
    cos[:, QK_NOPE + half:QK_NOPE + QK_ROPE] = c
    sin[:, QK_NOPE:QK_NOPE + half] = -s
    sin[:, QK_NOPE + half:QK_NOPE + QK_ROPE] = s
    return jnp.asarray(cos, F32), jnp.asarray(sin, F32)


def _partner_columns(w):
    half = QK_ROPE // 2
    lo, mid, hi = QK_NOPE, QK_NOPE + half, QK_NOPE + QK_ROPE
    z = jnp.zeros_like(w)
    return jnp.concatenate([z[..., :lo], w[..., mid:hi], w[..., lo:mid], z[..., hi:]], axis=-1)


def _inproj(x2, mod, norm1_g, w_in, q_a_g, w_uq, kv_a_g, w_ukv, q_g, k_g, bsz, seq):
    t, d = x2.shape
    ql, kvl = q_a_g.shape[0], kv_a_g.shape[0]
    hq = QK_NOPE + QK_ROPE
    fw = w_in.shape[1] - ql - kvl - QK_ROPE - 2 * d
    o1, o2, o3, o4, o5 = ql, ql + kvl, ql + kvl + QK_ROPE, ql + kvl + QK_ROPE + fw, \
        ql + kvl + QK_ROPE + fw + d
    assert ql % LANES == 0 and kvl % LANES == 0

    wa = jnp.concatenate([w_in[:, :o3], jnp.zeros((d, LANES - QK_ROPE), F32)], axis=1).astype(BF16)
    wf = w_in[:, o3:o4].astype(BF16)
    wga = w_in[:, o4:o5].astype(BF16)
    wgf = w_in[:, o5:].astype(BF16)

    wuq = w_uq.reshape(ql, N_HEADS, hq)
    wuq = jnp.pad(wuq, ((0, 0), (0, 0), (0, LANES - hq)))
    wuq = jnp.concatenate([wuq.reshape(ql, N_HEADS * LANES),
                           _partner_columns(wuq).reshape(ql, N_HEADS * LANES)], axis=1).astype(BF16)
    wukv = w_ukv.reshape(kvl, N_HEADS, QK_NOPE + V_DIM)
    wk = jnp.pad(wukv[:, :, :QK_NOPE], ((0, 0), (0, 0), (0, LANES - QK_NOPE)))
    place = jnp.zeros((QK_ROPE, N_HEADS, LANES), F32)
    place = place.at[jnp.arange(QK_ROPE), :, QK_NOPE + jnp.arange(QK_ROPE)].set(1.0)
    wk = jnp.concatenate([wk, place, jnp.zeros((LANES - QK_ROPE, N_HEADS, LANES), F32)], axis=0)
    wv = jnp.concatenate([wukv[:, :, QK_NOPE:], jnp.zeros((LANES, N_HEADS, V_DIM), F32)], axis=0)
    wkv = jnp.concatenate([wk.reshape(kvl + LANES, N_HEADS * LANES),
                           _partner_columns(wk).reshape(kvl + LANES, N_HEADS * LANES),
                           wv.reshape(kvl + LANES, N_HEADS * V_DIM)], axis=1).astype(BF16)

    gkv = jnp.concatenate([kv_a_g, jnp.ones((LANES,), F32)]).reshape(1, kvl + LANES)
    pad = jnp.zeros((LANES - hq,), F32)
    qg = jnp.concatenate([q_g * (hq ** -0.5), pad])
    kg = jnp.concatenate([k_g, pad])
    cos, sin = _rope_tables(seq)
    aq, bq = qg[None, :] * cos, _partner_columns(qg)[None, :] * sin
    ak, bk = kg[None, :] * cos, _partner_columns(kg)[None, :] * sin

    tm = _tile(seq, 512)
    tpb = seq // tm
    full = lambda shp: pl.BlockSpec(shp, lambda i: (0,) * len(shp))
    tok = lambda w: pl.BlockSpec((tm, w), lambda i: (i, 0))
    head = lambda w: pl.BlockSpec((1, N_HEADS, tm, w), lambda i: (i // tpb, 0, i % tpb, 0))
    rope = pl.BlockSpec((tm, LANES), lambda i: (i % tpb, 0))
    return pl.pallas_call(
        functools.partial(_inproj_kernel, ql=ql, kvl=kvl),
        grid=(t // tm,),
        in_specs=[tok(d),
                  pl.BlockSpec((1, N_ADA, d), lambda i: (i // tpb, 0, 0)),
                  full((1, d)), full(wa.shape), full(wf.shape), full(wga.shape), full(wgf.shape),
                  full((1, ql)), full(gkv.shape), full(wuq.shape), full(wkv.shape),
                  rope, rope, rope, rope],
        out_specs=[head(LANES), head(LANES), head(V_DIM), tok(fw), tok(d), tok(d)],
        out_shape=[jax.ShapeDtypeStruct((bsz, N_HEADS, seq, LANES), BF16),
                   jax.ShapeDtypeStruct((bsz, N_HEADS, seq, LANES), BF16),
                   jax.ShapeDtypeStruct((bsz, N_HEADS, seq, V_DIM), BF16),
                   jax.ShapeDtypeStruct((t, fw), BF16),
                   jax.ShapeDtypeStruct((t, d), BF16),
                   jax.ShapeDtypeStruct((t, d), BF16)],
        compiler_params=_cparams("arbitrary"),
        name="inproj",
    )(x2, mod, norm1_g.reshape(1, d), wa, wf, wga, wgf, q_a_g.reshape(1, ql), gkv, wuq, wkv,
      aq, bq, ak, bk)


def _attn_kernel(q_ref, k_ref, v_ref, o_ref):
    for hd in range(N_HEADS):
        s = lax.dot_general(q_ref[0, hd], k_ref[0, hd], (((1,), (1,)), ((), ())),
                            preferred_element_type=F32)
        m = jnp.max(s, axis=-1, keepdims=True)
        p = jnp.exp(s - m)
        l = jnp.sum(p, axis=-1, keepdims=True)
        o = jnp.dot(p.astype(BF16), v_ref[0, hd], preferred_element_type=F32)
        o_ref[0, :, hd * V_DIM:(hd + 1) * V_DIM] = (o / l).astype(BF16)


def _attention(q, k, v):
    bsz, _, seq, _ = q.shape
    tq = _tile(seq, 256)
    return pl.pallas_call(
        _attn_kernel,
        grid=(bsz, seq // tq),
        in_specs=[pl.BlockSpec((1, N_HEADS, tq, LANES), lambda b, j: (b, 0, j, 0)),
                  pl.BlockSpec((1, N_HEADS, seq, LANES), lambda b, j: (b, 0, 0, 0)),
                  pl.BlockSpec((1, N_HEADS, seq, V_DIM), lambda b, j: (b, 0, 0, 0))],
        out_specs=pl.BlockSpec((1, tq, N_HEADS * V_DIM), lambda b, j: (b, j, 0)),
        out_shape=jax.ShapeDtypeStruct((bsz, seq, N_HEADS * V_DIM), BF16),
        compiler_params=_cparams("arbitrary", "arbitrary"),
        name="attn",
    )(q, k, v)


def _fourier_kernel(z_ref, wc_ref, ws_ref, tab_ref, o_ref, u_ref, *, seq):
    @pl.when(pl.program_id(1) == 0)
    def _():
        z = z_ref[0]
        u_ref[:seq, :] = jnp.dot(z, wc_ref[...], preferred_element_type=F32).astype(BF16)
        u_ref[seq:, :] = jnp.dot(z, ws_ref[...], preferred_element_type=F32).astype(BF16)

    o_ref[0] = jnp.dot(tab_ref[...], u_ref[...], preferred_element_type=F32).astype(BF16)


def _fourier_tables(seq, fw):
    g = FOURIER_GROUP
    n = np.arange(seq, dtype=np.int64)
    ang = 2.0 * np.pi * ((n[:, None] * n[None, :]) % seq).astype(np.float64) / seq
    tab = np.concatenate([np.cos(ang), -np.sin(ang)], axis=1)
    c = np.arange(g, dtype=np.int64)
    angc = 2.0 * np.pi * ((c[:, None] * c[None, :]) % g).astype(np.float64) / g
    scale = 1.0 / math.sqrt(seq * g)
    eye = np.eye(fw // g)
    wc = np.kron(eye, np.cos(angc) * scale)
    ws = np.kron(eye, np.sin(angc) * scale)
    return (jnp.asarray(tab, F32).astype(BF16), jnp.asarray(wc, F32).astype(BF16),
            jnp.asarray(ws, F32).astype(BF16))


def _fourier(zf):
    bsz, seq, fw = zf.shape
    tab, wc, ws = _fourier_tables(seq, fw)
    tr = _tile(seq, 256)
    return pl.pallas_call(
        functools.partial(_fourier_kernel, seq=seq),
        grid=(bsz, seq // tr),
        in_specs=[pl.BlockSpec((1, seq, fw), lambda b, j: (b, 0, 0)),
                  pl.BlockSpec((fw, fw), lambda b, j: (0, 0)),
                  pl.BlockSpec((fw, fw), lambda b, j: (0, 0)),
                  pl.BlockSpec((tr, 2 * seq), lambda b, j: (j, 0))],
        out_specs=pl.BlockSpec((1, tr, fw), lambda b, j: (b, j, 0)),
        out_shape=jax.ShapeDtypeStruct((bsz, seq, fw), BF16),
        scratch_shapes=[pltpu.VMEM((2 * seq, fw), BF16)],
        compiler_params=_cparams("arbitrary", "arbitrary"),
        name="fourier",
    )(zf, wc, ws, tab)


def _merge_kernel(a_ref, f_ref, sa_ref, sf_ref, x_ref, mod_ref, wpa_ref, wpf_ref, wo_ref,
                  g2_ref, wrh_ref, wrl_ref, x1_ref, h2_ref, sc_ref):
    ya = jnp.dot(a_ref[...], wpa_ref[...], preferred_element_type=F32)
    yf = jnp.dot(f_ref[...], wpf_ref[...], preferred_element_type=F32)
    merged = sa_ref[...].astype(F32) * ya + sf_ref[...].astype(F32) * yf
    mod = mod_ref[0]
    g1, sh2, sc2 = mod[2:3], mod[3:4], mod[4:5]
    x1 = x_ref[...] + g1 * jnp.dot(merged.astype(BF16), wo_ref[...], preferred_element_type=F32)
    x1_ref[...] = x1
    r = lax.rsqrt(jnp.mean(x1 * x1, axis=-1, keepdims=True) + EPS)
    h2 = (x1 * r * g2_ref[...]) * (1.0 + sc2) + sh2
    h2_ref[0], h2_ref[1] = _pack_halves(h2)
    hh = h2.astype(BF16)
    hl = (h2 - hh.astype(F32)).astype(BF16)
    nt = (((1,), (1,)), ((), ()))
    lt = (lax.dot_general(wrh_ref[...], hh, nt, preferred_element_type=F32)
          + lax.dot_general(wrh_ref[...], hl, nt, preferred_element_type=F32)
          + lax.dot_general(wrl_ref[...], hh, nt, preferred_element_type=F32))
    sc_ref[...] = jax.nn.sigmoid(lt)


def _merge(attn, four, sa, sf, x2, mod, w_pa, w_pf, w_out, norm2_g, w_router, seq):
    t, d = x2.shape
    e = w_router.shape[1]
    wrt = w_router.T
    wrh = wrt.astype(BF16)
    wrl = (wrt - wrh.astype(F32)).astype(BF16)
    tm = _tile(seq, 512)
    tpb = seq // tm
    full = lambda shp: pl.BlockSpec(shp, lambda i: (0,) * len(shp))
    tok = lambda w: pl.BlockSpec((tm, w), lambda i: (i, 0))
    return pl.pallas_call(
        _merge_kernel,
        grid=(t // tm,),
        in_specs=[tok(attn.shape[1]), tok(four.shape[1]), tok(d), tok(d), tok(d),
                  pl.BlockSpec((1, N_ADA, d), lambda i: (i // tpb, 0, 0)),
                  full(w_pa.shape), full(w_pf.shape), full(w_out.shape), full((1, d)),
                  full((e, d)), full((e, d))],
        out_specs=[tok(d), pl.BlockSpec((2, tm, d // 4), lambda i: (0, i, 0)),
                   pl.BlockSpec((e, tm), lambda i: (0, i))],
        out_shape=[jax.ShapeDtypeStruct((t, d), F32),
                   jax.ShapeDtypeStruct((2, t, d // 4), jnp.uint32),
                   jax.ShapeDtypeStruct((e, t), F32)],
        compiler_params=_cparams("arbitrary"),
        name="merge",
    )(attn, four, sa, sf, x2, mod, w_pa.astype(BF16), w_pf.astype(BF16), w_out.astype(BF16),
      norm2_g.reshape(1, d), wrh, wrl)


def _route_kernel(s_ref, b_ref, tri_ref, idx_ref, w_ref, rank_ref, cnt_ref, carry_ref):
    @pl.when(pl.program_id(0) == 0)
    def _():
        carry_ref[...] = jnp.zeros_like(carry_ref)

    sc = s_ref[...]
    e, tr = sc.shape
    row = lax.broadcasted_iota(jnp.int32, (e, tr), 0)
    v = sc + b_ref[...]
    sel = jnp.zeros((e, tr), F32)
    idxs, ws = [], []
    for _ in range(TOP_K):
        m = jnp.max(v, axis=0, keepdims=True)
        idx = jnp.min(jnp.where(v == m, row, e), axis=0, keepdims=True)
        oh = row == idx
        ws.append(jnp.sum(jnp.where(oh, sc, 0.0), axis=0, keepdims=True))
        idxs.append(idx)
        v = jnp.where(oh, -jnp.inf, v)
        sel = sel + oh.astype(F32)
    wsum = ws[0]
    for w in ws[1:]:
        wsum = wsum + w
    selb = sel.astype(BF16)
    cum = jnp.dot(selb, tri_ref[...], preferred_element_type=F32) + carry_ref[...]
    for kk in range(TOP_K):
        oh = row == idxs[kk]
        rk = jnp.sum(jnp.where(oh, cum, 0.0), axis=0, keepdims=True)
        idx_ref[kk:kk + 1, :] = idxs[kk]
        rank_ref[kk:kk + 1, :] = rk.astype(jnp.int32)
        w_ref[kk:kk + 1, :] = ws[kk] / wsum * ROUTED_SCALE
    tot = carry_ref[...] + jnp.dot(selb, jnp.ones((tr, tr), BF16), preferred_element_type=F32)
    carry_ref[...] = tot
    cnt_ref[...] = tot


def _route(scores_t, router_bias):
    e, t = scores_t.shape
    tr = _tile(t, 256)
    tri = jnp.asarray(np.triu(np.ones((tr, tr), np.float32), 1), BF16)
    bias = jnp.broadcast_to(router_bias.reshape(e, 1), (e, tr)).astype(F32)
    blk = pl.BlockSpec((TOP_K, tr), lambda i: (0, i))
    return pl.pallas_call(
        _route_kernel,
        grid=(t // tr,),
        in_specs=[pl.BlockSpec((e, tr), lambda i: (0, i)),
                  pl.BlockSpec((e, tr), lambda i: (0, 0)),
                  pl.BlockSpec((tr, tr), lambda i: (0, 0))],
        out_specs=[blk, blk, blk, pl.BlockSpec((e, tr), lambda i: (0, 0))],
        out_shape=[jax.ShapeDtypeStruct((TOP_K, t), jnp.int32),
                   jax.ShapeDtypeStruct((TOP_K, t), F32),
                   jax.ShapeDtypeStruct((TOP_K, t), jnp.int32),
                   jax.ShapeDtypeStruct((e, tr), F32)],
        scratch_shapes=[pltpu.VMEM((e, tr), F32)],
        compiler_params=_cparams("arbitrary"),
        name="route",
    )(scores_t, bias, tri)


def _slots_kernel(idx_ref, rank_ref, ps_ref, slot_ref):
    ps = ps_ref[...]
    row = lax.broadcasted_iota(jnp.int32, ps.shape, 0)
    for kk in range(TOP_K):
        oh = row == idx_ref[kk:kk + 1, :]
        start = jnp.sum(jnp.where(oh, ps, 0), axis=0, keepdims=True)
        slot_ref[kk:kk + 1, :] = start + rank_ref[kk:kk + 1, :]


def _slots(idx_t, rank_t, p_start):
    _, t = idx_t.shape
    e = p_start.shape[0]
    ts = _tile(t, 512)
    ps = jnp.broadcast_to(p_start.reshape(e, 1), (e, ts))
    blk = pl.BlockSpec((TOP_K, ts), lambda i: (0, i))
    return pl.pallas_call(
        _slots_kernel,
        grid=(t // ts,),
        in_specs=[blk, blk, pl.BlockSpec((e, ts), lambda i: (0, 0))],
        out_specs=blk,
        out_shape=jax.ShapeDtypeStruct((TOP_K, t), jnp.int32),
        compiler_params=_cparams("arbitrary"),
        name="slots",
    )(idx_t, rank_t, ps)


def _dispatch(h2p, slot_t, n_slots):
    _, t, c = h2p.shape
    k = slot_t.shape[0]
    win = _tile(2 * t, SC_WINDOW)
    rows = h2p.reshape(2 * t, c)
    dest = jnp.concatenate([slot_t, slot_t + n_slots], axis=1)
    mesh = plsc.VectorSubcoreMesh(core_axis_name="core", subcore_axis_name="subcore")

    @pl.kernel(out_type=jax.ShapeDtypeStruct((2 * n_slots, c), h2p.dtype), mesh=mesh,
               scratch_types=[])
    def scatter_rows(x_hbm, s_hbm, o_hbm):
        def body(x_vmem, s_vmem):
            pltpu.sync_copy(x_vmem, o_hbm.at[s_vmem.at[0]])

        pltpu.emit_pipeline(
            body, grid=(2 * t // win, k),
            in_specs=[pl.BlockSpec((win, c), lambda i, j: (i, 0)),
                      pl.BlockSpec((1, win), lambda i, j: (j, i))],
            out_specs=[], core_axis_name=("core", "subcore"),
            dimension_semantics=(pltpu.PARALLEL, pltpu.ARBITRARY))(x_hbm, s_hbm)

    return scatter_rows(rows, dest).reshape(2, n_slots, c)


def _expert_kernel(be_ref, nxt_ref, par_ref, nu_ref, xs_hbm, wg_hbm, wu_hbm, wd_hbm, ys_hbm,
                   xbuf, ybuf, wgf, wuf, wdf, wgb, wub, wdb, xsem, ysem, wsem, *, nblk):
    nu = nu_ref[0]
    rows = EXPERT_ROWS

    def x_copies(blk, slot):
        r0 = pl.multiple_of(blk * rows, rows)
        return [pltpu.make_async_copy(xs_hbm.at[h, pl.ds(r0, rows)], xbuf.at[slot, h],
                                      xsem.at[slot]) for h in range(2)]

    def y_copies(blk, slot):
        r0 = pl.multiple_of(blk * rows, rows)
        return [pltpu.make_async_copy(ybuf.at[slot, h], ys_hbm.at[h, pl.ds(r0, rows)],
                                      ysem.at[slot]) for h in range(2)]

    def weight_copies(e, which):
        return (pltpu.make_async_copy(wg_hbm.at[e], wgf.at[which], wsem.at[which]),
                pltpu.make_async_copy(wu_hbm.at[e], wuf.at[which], wsem.at[which]),
                pltpu.make_async_copy(wd_hbm.at[e], wdf.at[which], wsem.at[which]))

    for cp in weight_copies(be_ref[0], 0):
        cp.start(priority=1)
    for j in range(X_RING - 1):
        @pl.when(j < nu)
        def _():
            for cp in x_copies(j, j):
                cp.start()

    def step(i, carry):
        ahead = i + X_RING - 1

        @pl.when(ahead < nu)
        def _():
            for cp in x_copies(ahead, ahead % X_RING):
                cp.start()

        prev = be_ref[jnp.maximum(i - 1, 0)]

        @pl.when(jnp.logical_or(i == 0, be_ref[i] != prev))
        def _():
            par = par_ref[i]
            for cp in weight_copies(be_ref[i], par):
                cp.wait()
            wgb[...] = wgf[par].astype(BF16)
            wub[...] = wuf[par].astype(BF16)
            wdb[...] = wdf[par].astype(BF16)

            @pl.when(nxt_ref[i] >= 0)
            def _():
                for cp in weight_copies(nxt_ref[i], 1 - par):
                    cp.start(priority=1)

        slot = i % X_RING
        for cp in x_copies(i, slot):
            cp.wait()
        x = _unpack_halves(xbuf[slot, 0], xbuf[slot, 1])
        g = jnp.dot(x, wgb[...], preferred_element_type=F32)
        u = jnp.dot(x, wub[...], preferred_element_type=F32)
        a = (_silu(g) * u).astype(BF16)
        y0, y1 = _pack_halves(jnp.dot(a, wdb[...], preferred_element_type=F32))

        out = i % 2

        @pl.when(i >= 2)
        def _():
            for cp in y_copies(i - 2, out):
                cp.wait()
        ybuf[out, 0] = y0
        ybuf[out, 1] = y1
        for cp in y_copies(i, out):
            cp.start()
        return carry

    lax.fori_loop(0, nu, step, 0)

    @pl.when(nu >= 2)
    def _():
        for cp in y_copies(nu - 2, nu % 2):
            cp.wait()
    for cp in y_copies(nu - 1, (nu - 1) % 2):
        cp.wait()

    ybuf[0] = jnp.zeros_like(ybuf[0])

    def zero_start(j, carry):
        for cp in y_copies(j, 0):
            cp.start()
        return carry

    def zero_wait(j, carry):
        for cp in y_copies(j, 0):
            cp.wait()
        return carry
    lax.fori_loop(nu, nblk, zero_start, 0)
    lax.fori_loop(nu, nblk, zero_wait, 0)


def _experts(blk_expert, blk_next, blk_par, nblk_used, xs, w_g, w_u, w_d):
    d, f = w_g.shape[1], w_g.shape[2]
    rows = EXPERT_ROWS
    nblk = blk_expert.shape[0]
    c = xs.shape[2]
    hbm = pl.BlockSpec(memory_space=pl.ANY)
    return pl.pallas_call(
        functools.partial(_expert_kernel, nblk=nblk),
        grid_spec=pltpu.PrefetchScalarGridSpec(
            num_scalar_prefetch=4,
            grid=(1,),
            in_specs=[hbm, hbm, hbm, hbm],
            out_specs=hbm,
            scratch_shapes=[pltpu.VMEM((X_RING, 2, rows, c), jnp.uint32),
                            pltpu.VMEM((2, 2, rows, c), jnp.uint32),
                            pltpu.VMEM((2, d, f), F32), pltpu.VMEM((2, d, f), F32),
                            pltpu.VMEM((2, f, d), F32),
                            pltpu.VMEM((d, f), BF16), pltpu.VMEM((d, f), BF16),
                            pltpu.VMEM((f, d), BF16),
                            pltpu.SemaphoreType.DMA((X_RING,)), pltpu.SemaphoreType.DMA((2,)),
                            pltpu.SemaphoreType.DMA((2,))]),
        out_shape=jax.ShapeDtypeStruct((2, nblk * rows, c), jnp.uint32),
        compiler_params=_cparams("arbitrary"),
        name="experts",
    )(blk_expert, blk_next, blk_par, nblk_used, xs, w_g, w_u, w_d)


def _gather(ys, slot_t):
    _, n_slots, c = ys.shape
    k, t = slot_t.shape
    p = 2 * k * t
    win = _tile(p, SC_WINDOW)
    src = jnp.concatenate([slot_t, slot_t + n_slots], axis=0).reshape(1, p)
    mesh = plsc.VectorSubcoreMesh(core_axis_name="core", subcore_axis_name="subcore")

    @pl.kernel(out_type=jax.ShapeDtypeStruct((p, c), ys.dtype), mesh=mesh, scratch_types=[])
    def gather_rows(y_hbm, s_hbm, o_hbm):
        def body(s_vmem, o_vmem):
            pltpu.sync_copy(y_hbm.at[s_vmem.at[0]], o_vmem)

        pltpu.emit_pipeline(
            body, grid=(p // win,),
            in_specs=[pl.BlockSpec((1, win), lambda i: (0, i))],
            out_specs=[pl.BlockSpec((win, c), lambda i: (i, 0))],
            core_axis_name=("core", "subcore"),
            dimension_semantics=(pltpu.PARALLEL,))(s_hbm, o_hbm)

    return gather_rows(ys.reshape(2 * n_slots, c), src).reshape(2, k, t, c)


def _combine_kernel(y_ref, w_ref, x1_ref, h2_ref, mod_ref, wsg_ref, wsu_ref, wsd_ref, o_ref):
    hb = _unpack_halves(h2_ref[0], h2_ref[1])
    g = jnp.dot(hb, wsg_ref[...], preferred_element_type=F32)
    u = jnp.dot(hb, wsu_ref[...], preferred_element_type=F32)
    acc = jnp.dot((_silu(g) * u).astype(BF16), wsd_ref[...], preferred_element_type=F32)
    w = w_ref[...]
    for kk in range(TOP_K):
        acc = acc + w[:, kk:kk + 1] * _unpack_halves(y_ref[0, kk], y_ref[1, kk]).astype(F32)
    g2 = mod_ref[0][5:6]
    o_ref[...] = x1_ref[...] + g2 * acc


def _combine(y_tok, w_tk, x1, h2p, mod, w_sg, w_su, w_sd, seq):
    t, d = x1.shape
    c = d // 4
    tc = _tile(seq, 256)
    tpb = seq // tc
    full = lambda shp: pl.BlockSpec(shp, lambda i: (0,) * len(shp))
    tok = lambda w: pl.BlockSpec((tc, w), lambda i: (i, 0))
    return pl.pallas_call(
        _combine_kernel,
        grid=(t // tc,),
        in_specs=[pl.BlockSpec((2, TOP_K, tc, c), lambda i: (0, 0, i, 0)),
                  tok(TOP_K), tok(d),
                  pl.BlockSpec((2, tc, c), lambda i: (0, i, 0)),
                  pl.BlockSpec((1, N_ADA, d), lambda i: (i // tpb, 0, 0)),
                  full(w_sg.shape), full(w_su.shape), full(w_sd.shape)],
        out_specs=tok(d),
        out_shape=jax.ShapeDtypeStruct((t, d), F32),
        compiler_params=_cparams("arbitrary"),
        name="combine",
    )(y_tok, w_tk, x1, h2p, mod, w_sg.astype(BF16), w_su.astype(BF16), w_sd.astype(BF16))


def _layer(x, c, w_ada, b_ada, norm1_g, w_in, q_a_norm_g, w_uq, kv_a_norm_g, w_ukv,
           q_norm_g, k_norm_g, w_proj_attn, w_proj_fourier, w_out, norm2_g,
           w_router, router_bias, w_exp_gate, w_exp_up, w_exp_down,
           w_sh_gate, w_sh_up, w_sh_down):
    bsz, seq, d = x.shape
    t = bsz * seq
    e = w_router.shape[1]
    x2 = x.reshape(t, d)

    mod = _ada(c, w_ada, b_ada).reshape(bsz, N_ADA, d)
    q, k, v, zf, sa, sf = _inproj(x2, mod, norm1_g, w_in, q_a_norm_g, w_uq, kv_a_norm_g,
                                  w_ukv, q_norm_g, k_norm_g, bsz, seq)
    attn = _attention(q, k, v).reshape(t, N_HEADS * V_DIM)
    four = _fourier(zf.reshape(bsz, seq, zf.shape[1])).reshape(t, zf.shape[1])
    x1, h2, scores_t = _merge(attn, four, sa, sf, x2, mod, w_proj_attn, w_proj_fourier,
                              w_out, norm2_g, w_router, seq)

    idx_t, w_t, rank_t, cnt = _route(scores_t, router_bias)
    counts = cnt[:, 0].astype(jnp.int32)
    rows = EXPERT_ROWS
    nblk = -(-(t * TOP_K) // rows) + e
    padded = ((counts + rows - 1) // rows) * rows
    p_end = jnp.cumsum(padded)
    p_start = p_end - padded
    nblk_used = (p_end[-1] // rows).astype(jnp.int32)
    blk_start = jnp.arange(nblk, dtype=jnp.int32) * rows
    blk_first = jnp.minimum(blk_start, p_end[-1] - 1)
    hit = p_end[None, :] <= blk_first[:, None]
    blk_expert = jnp.clip(jnp.sum(hit.astype(jnp.int32), axis=1), 0, e - 1)

    slot_t = _slots(idx_t, rank_t, p_start.astype(jnp.int32))
    xs = _dispatch(h2, slot_t, nblk * rows)
    eid = jnp.arange(e, dtype=jnp.int32)
    cand = jnp.where(counts > 0, eid, e)
    nxt = jnp.concatenate([lax.cummin(cand[::-1])[::-1][1:], jnp.full((1,), e, jnp.int32)])
    nxt = jnp.where(nxt >= e, -1, nxt)
    run = jnp.cumsum((counts > 0).astype(jnp.int32)) - 1
    onehot = blk_expert[:, None] == eid[None, :]
    blk_next = jnp.sum(jnp.where(onehot, nxt[None, :], 0), axis=1).astype(jnp.int32)
    blk_par = (jnp.sum(jnp.where(onehot, run[None, :], 0), axis=1) % 2).astype(jnp.int32)
    ys = _experts(blk_expert, blk_next, blk_par, nblk_used.reshape(1), xs,
                  w_exp_gate, w_exp_up, w_exp_down)
    out = _combine(_gather(ys, slot_t), w_t.T, x1, h2, mod, w_sh_gate, w_sh_up, w_sh_down, seq)
    return out.reshape(bsz, seq, d)


def kernel(x, c, w_ada, b_ada, norm1_g, w_in, q_a_norm_g, w_uq, kv_a_norm_g, w_ukv, q_norm_g,
           k_norm_g, w_proj_attn, w_proj_fourier, w_out, norm2_g, w_router, router_bias,
           w_exp_gate, w_exp_up, w_exp_down, w_sh_gate, w_sh_up, w_sh_down):
    for l in range(w_ada.shape[0]):
        x = _layer(x, c, w_ada[l], b_ada[l], norm1_g[l], w_in[l], q_a_norm_g[l], w_uq[l],
                   kv_a_norm_g[l], w_ukv[l], q_norm_g[l], k_norm_g[l], w_proj_attn[l],
                   w_proj_fourier[l], w_out[l], norm2_g[l], w_router[l], router_bias[l],
                   w_exp_gate[l], w_exp_up[l], w_exp_down[l], w_sh_gate[l], w_sh_up[l],
                   w_sh_down[l])
    return x
```

```python
import functools
import math

import numpy as np
import jax
import jax.numpy as jnp
from jax import lax
from jax.experimental import pallas as pl
from jax.experimental.pallas import tpu as pltpu
from jax.experimental.pallas import tpu_sc as plsc

N_HEADS = 8
QK_NOPE = 64
QK_ROPE = 32
V_DIM = 64
FOURIER_GROUP = 64
TOP_K = 8
ROUTED_SCALE = 2.5
EPS = 1e-6
ROPE_THETA = 10000.0
N_ADA = 6

LANES = 128
EXPERT_ROWS = 256
X_RING = 4
W_RING = 3
SC_WINDOW = 128
VMEM_LIMIT = 48 * 1024 * 1024

F32 = jnp.float32
BF16 = jnp.bfloat16


def _cparams(*sem):
    return pltpu.CompilerParams(dimension_semantics=sem, vmem_limit_bytes=VMEM_LIMIT)


def _tile(n, pref):
    t = min(n, pref)
    assert n % t == 0, (n, pref)
    return t


def _silu(v):
    return v * jax.nn.sigmoid(v)


def _pack_halves(m):
    d = m.shape[1]
    lo = lax.bitcast_convert_type(m[:, :d // 2].astype(BF16).astype(F32), jnp.uint32)
    hi = lax.bitcast_convert_type(m[:, d // 2:].astype(BF16).astype(F32), jnp.uint32)
    w = (lo >> 16) | (hi & jnp.uint32(0xFFFF0000))
    return w[:, :d // 4], w[:, d // 4:]


def _unpack_halves(w0, w1):
    def lo(w):
        return lax.bitcast_convert_type(w << 16, F32)

    def hi(w):
        return lax.bitcast_convert_type(w & jnp.uint32(0xFFFF0000), F32)
    return jnp.concatenate([lo(w0), lo(w1), hi(w0), hi(w1)], axis=1).astype(BF16)


def _ada_kernel(c_ref, w_ref, b_ref, o_ref):
    a = _silu(c_ref[...])
    o_ref[...] = jnp.dot(a, w_ref[...], preferred_element_type=F32,
                         precision=lax.Precision.HIGHEST) + b_ref[...]


def _ada(c, w_ada, b_ada):
    bsz, d = c.shape
    n = w_ada.shape[1]
    tn = _tile(n, d)
    return pl.pallas_call(
        _ada_kernel,
        grid=(n // tn,),
        in_specs=[pl.BlockSpec((bsz, d), lambda j: (0, 0)),
                  pl.BlockSpec((d, tn), lambda j: (0, j)),
                  pl.BlockSpec((1, tn), lambda j: (0, j))],
        out_specs=pl.BlockSpec((bsz, tn), lambda j: (0, j)),
        out_shape=jax.ShapeDtypeStruct((bsz, n), F32),
        compiler_params=_cparams("arbitrary"),
        name="ada",
    )(c, w_ada, b_ada.reshape(1, n))


def _head_norm_rope(t, trot, a, b):
    ms = jnp.sum(t * t, axis=-1, keepdims=True) * (1.0 / (QK_NOPE + QK_ROPE))
    return (t * a + trot * b) * lax.rsqrt(ms + EPS)


def _inproj_kernel(x_ref, mod_ref, g1_ref, wa_ref, wf_ref, wga_ref, wgf_ref,
                   gq_ref, gkv_ref, wuq_ref, wkv_ref,
                   aq_ref, bq_ref, ak_ref, bk_ref,
                   q_ref, k_ref, v_ref, zf_ref, sa_ref, sf_ref, *, ql, kvl):
    x = x_ref[...]
    mod = mod_ref[0]
    sh1, sc1 = mod[0:1], mod[1:2]
    r = lax.rsqrt(jnp.mean(x * x, axis=-1, keepdims=True) + EPS)
    h = (x * r * g1_ref[...]) * (1.0 + sc1) + sh1
    hb = h.astype(BF16)

    zf_ref[...] = jnp.dot(hb, wf_ref[...], preferred_element_type=F32).astype(BF16)
    sa_ref[...] = jax.nn.sigmoid(
        jnp.dot(hb, wga_ref[...], preferred_element_type=F32)).astype(BF16)
    sf_ref[...] = jax.nn.sigmoid(
        jnp.dot(hb, wgf_ref[...], preferred_element_type=F32)).astype(BF16)

    za = jnp.dot(hb, wa_ref[...], preferred_element_type=F32)
    zq = za[:, :ql]
    cq = zq * lax.rsqrt(jnp.mean(zq * zq, axis=-1, keepdims=True) + EPS) * gq_ref[...]
    qall = jnp.dot(cq.astype(BF16), wuq_ref[...], preferred_element_type=F32)

    zk = za[:, ql:]
    kvn = zk[:, :kvl]
    rk = lax.rsqrt(jnp.mean(kvn * kvn, axis=-1, keepdims=True) + EPS)
    lane = lax.broadcasted_iota(jnp.int32, zk.shape, 1)
    u = zk * jnp.where(lane < kvl, rk, 1.0) * gkv_ref[...]
    kvall = jnp.dot(u.astype(BF16), wkv_ref[...], preferred_element_type=F32)

    aq, bq, ak, bk = aq_ref[...], bq_ref[...], ak_ref[...], bk_ref[...]
    hw = N_HEADS * LANES
    for hd in range(N_HEADS):
        lo, hi = hd * LANES, (hd + 1) * LANES
        q_ref[0, hd] = _head_norm_rope(qall[:, lo:hi], qall[:, hw + lo:hw + hi],
                                       aq, bq).astype(BF16)
        k_ref[0, hd] = _head_norm_rope(kvall[:, lo:hi], kvall[:, hw + lo:hw + hi],
                                       ak, bk).astype(BF16)
        v_ref[0, hd] = kvall[:, 2 * hw + hd * V_DIM: 2 * hw + (hd + 1) * V_DIM].astype(BF16)


def _rope_tables(seq):
    half = QK_ROPE // 2
    pos = np.arange(seq, dtype=np.float64)
    inv = ROPE_THETA ** (-np.arange(0, QK_ROPE, 2, dtype=np.float64) / QK_ROPE)
    ang = pos[:, None] * inv[None, :]
    c, s = np.cos(ang), np.sin(ang)
    cos = np.ones((seq, LANES)); sin = np.zeros((seq, LANES))
    cos[:, QK_NOPE:QK_NOPE + half] = c
    cos[:, QK_NOPE + half:QK_NOPE + QK_ROPE] = c
    sin[:, QK_NOPE:QK_NOPE + half] = -s
    sin[:, QK_NOPE + half:QK_NOPE + QK_ROPE] = s
    return jnp.asarray(cos, F32), jnp.asarray(sin, F32)


def _partner_columns(w):
    half = QK_ROPE // 2
    lo, mid, hi = QK_NOPE, QK_NOPE + half, QK_NOPE + QK_ROPE
    z = jnp.zeros_like(w)
    return jnp.concatenate([z[..., :lo], w[..., mid:hi], w[..., lo:mid], z[..., hi:]], axis=-1)


def _inproj(x2, mod, norm1_g, w_in, q_a_g, w_uq, kv_a_g, w_ukv, q_g, k_g, bsz, seq):
    t, d = x2.shape
    ql, kvl = q_a_g.shape[0], kv_a_g.shape[0]
    hq = QK_NOPE + QK_ROPE
    fw = w_in.shape[1] - ql - kvl - QK_ROPE - 2 * d
    o1, o2, o3, o4, o5 = ql, ql + kvl, ql + kvl + QK_ROPE, ql + kvl + QK_ROPE + fw, \
        ql + kvl + QK_ROPE + fw + d
    assert ql % LANES == 0 and kvl % LANES == 0

    wa = jnp.concatenate([w_in[:, :o3], jnp.zeros((d, LANES - QK_ROPE), F32)], axis=1).astype(BF16)
    wf = w_in[:, o3:o4].astype(BF16)
    wga = w_in[:, o4:o5].astype(BF16)
    wgf = w_in[:, o5:].astype(BF16)

    wuq = w_uq.reshape(ql, N_HEADS, hq)
    wuq = jnp.pad(wuq, ((0, 0), (0, 0), (0, LANES - hq)))
    wuq = jnp.concatenate([wuq.reshape(ql, N_HEADS * LANES),
                           _partner_columns(wuq).reshape(ql, N_HEADS * LANES)], axis=1).astype(BF16)
    wukv = w_ukv.reshape(kvl, N_HEADS, QK_NOPE + V_DIM)
    wk = jnp.pad(wukv[:, :, :QK_NOPE], ((0, 0), (0, 0), (0, LANES - QK_NOPE)))
    place = jnp.zeros((QK_ROPE, N_HEADS, LANES), F32)
    place = place.at[jnp.arange(QK_ROPE), :, QK_NOPE + jnp.arange(QK_ROPE)].set(1.0)
    wk = jnp.concatenate([wk, place, jnp.zeros((LANES - QK_ROPE, N_HEADS, LANES), F32)], axis=0)
    wv = jnp.concatenate([wukv[:, :, QK_NOPE:], jnp.zeros((LANES, N_HEADS, V_DIM), F32)], axis=0)
    wkv = jnp.concatenate([wk.reshape(kvl + LANES, N_HEADS * LANES),
                           _partner_columns(wk).reshape(kvl + LANES, N_HEADS * LANES),
                           wv.reshape(kvl + LANES, N_HEADS * V_DIM)], axis=1).astype(BF16)

    gkv = jnp.concatenate([kv_a_g, jnp.ones((LANES,), F32)]).reshape(1, kvl + LANES)
    pad = jnp.zeros((LANES - hq,), F32)
    qg = jnp.concatenate([q_g * (hq ** -0.5), pad])
    kg = jnp.concatenate([k_g, pad])
    cos, sin = _rope_tables(seq)
    aq, bq = qg[None, :] * cos, _partner_columns(qg)[None, :] * sin
    ak, bk = kg[None, :] * cos, _partner_columns(kg)[None, :] * sin

    tm = _tile(seq, 512)
    tpb = seq // tm
    full = lambda shp: pl.BlockSpec(shp, lambda i: (0,) * len(shp))
    tok = lambda w: pl.BlockSpec((tm, w), lambda i: (i, 0))
    head = lambda w: pl.BlockSpec((1, N_HEADS, tm, w), lambda i: (i // tpb, 0, i % tpb, 0))
    rope = pl.BlockSpec((tm, LANES), lambda i: (i % tpb, 0))
    return pl.pallas_call(
        functools.partial(_inproj_kernel, ql=ql, kvl=kvl),
        grid=(t // tm,),
        in_specs=[tok(d),
                  pl.BlockSpec((1, N_ADA, d), lambda i: (i // tpb, 0, 0)),
                  full((1, d)), full(wa.shape), full(wf.shape), full(wga.shape), full(wgf.shape),
                  full((1, ql)), full(gkv.shape), full(wuq.shape), full(wkv.shape),
                  rope, rope, rope, rope],
        out_specs=[head(LANES), head(LANES), head(V_DIM), tok(fw), tok(d), tok(d)],
        out_shape=[jax.ShapeDtypeStruct((bsz, N_HEADS, seq, LANES), BF16),
                   jax.ShapeDtypeStruct((bsz, N_HEADS, seq, LANES), BF16),
                   jax.ShapeDtypeStruct((bsz, N_HEADS, seq, V_DIM), BF16),
                   jax.ShapeDtypeStruct((t, fw), BF16),
                   jax.ShapeDtypeStruct((t, d), BF16),
                   jax.ShapeDtypeStruct((t, d), BF16)],
        compiler_params=_cparams("arbitrary"),
        name="inproj",
    )(x2, mod, norm1_g.reshape(1, d), wa, wf, wga, wgf, q_a_g.reshape(1, ql), gkv, wuq, wkv,
      aq, bq, ak, bk)


def _attn_kernel(q_ref, k_ref, v_ref, o_ref):
    for hd in range(N_HEADS):
        s = lax.dot_general(q_ref[0, hd], k_ref[0, hd], (((1,), (1,)), ((), ())),
                            preferred_element_type=F32)
        m = jnp.max(s, axis=-1, keepdims=True)
        p = jnp.exp(s - m)
        l = jnp.sum(p, axis=-1, keepdims=True)
        o = jnp.dot(p.astype(BF16), v_ref[0, hd], preferred_element_type=F32)
        o_ref[0, :, hd * V_DIM:(hd + 1) * V_DIM] = (o / l).astype(BF16)


def _attention(q, k, v):
    bsz, _, seq, _ = q.shape
    tq = _tile(seq, 512)
    return pl.pallas_call(
        _attn_kernel,
        grid=(bsz, seq // tq),
        in_specs=[pl.BlockSpec((1, N_HEADS, tq, LANES), lambda b, j: (b, 0, j, 0)),
                  pl.BlockSpec((1, N_HEADS, seq, LANES), lambda b, j: (b, 0, 0, 0)),
                  pl.BlockSpec((1, N_HEADS, seq, V_DIM), lambda b, j: (b, 0, 0, 0))],
        out_specs=pl.BlockSpec((1, tq, N_HEADS * V_DIM), lambda b, j: (b, j, 0)),
        out_shape=jax.ShapeDtypeStruct((bsz, seq, N_HEADS * V_DIM), BF16),
        compiler_params=_cparams("arbitrary", "arbitrary"),
        name="attn",
    )(q, k, v)


def _fourier_kernel(z_ref, wc_ref, ws_ref, tab_ref, o_ref, u_ref, *, seq):
    @pl.when(pl.program_id(1) == 0)
    def _():
        z = z_ref[0]
        u_ref[:seq, :] = jnp.dot(z, wc_ref[...], preferred_element_type=F32).astype(BF16)
        u_ref[seq:, :] = jnp.dot(z, ws_ref[...], preferred_element_type=F32).astype(BF16)

    o_ref[0] = jnp.dot(tab_ref[...], u_ref[...], preferred_element_type=F32).astype(BF16)


def _fourier_tables(seq, fw):
    g = FOURIER_GROUP
    n = np.arange(seq, dtype=np.int64)
    ang = 2.0 * np.pi * ((n[:, None] * n[None, :]) % seq).astype(np.float64) / seq
    tab = np.concatenate([np.cos(ang), -np.sin(ang)], axis=1)
    c = np.arange(g, dtype=np.int64)
    angc = 2.0 * np.pi * ((c[:, None] * c[None, :]) % g).astype(np.float64) / g
    scale = 1.0 / math.sqrt(seq * g)
    eye = np.eye(fw // g)
    wc = np.kron(eye, np.cos(angc) * scale)
    ws = np.kron(eye, np.sin(angc) * scale)
    return (jnp.asarray(tab, F32).astype(BF16), jnp.asarray(wc, F32).astype(BF16),
            jnp.asarray(ws, F32).astype(BF16))


def _fourier(zf):
    bsz, seq, fw = zf.shape
    tab, wc, ws = _fourier_tables(seq, fw)
    tr = _tile(seq, 256)
    return pl.pallas_call(
        functools.partial(_fourier_kernel, seq=seq),
        grid=(bsz, seq // tr),
        in_specs=[pl.BlockSpec((1, seq, fw), lambda b, j: (b, 0, 0)),
                  pl.BlockSpec((fw, fw), lambda b, j: (0, 0)),
                  pl.BlockSpec((fw, fw), lambda b, j: (0, 0)),
                  pl.BlockSpec((tr, 2 * seq), lambda b, j: (j, 0))],
        out_specs=pl.BlockSpec((1, tr, fw), lambda b, j: (b, j, 0)),
        out_shape=jax.ShapeDtypeStruct((bsz, seq, fw), BF16),
        scratch_shapes=[pltpu.VMEM((2 * seq, fw), BF16)],
        compiler_params=_cparams("arbitrary", "arbitrary"),
        name="fourier",
    )(zf, wc, ws, tab)


def _merge_kernel(a_ref, f_ref, sa_ref, sf_ref, x_ref, mod_ref, wpa_ref, wpf_ref, wo_ref,
                  g2_ref, wrh_ref, wrl_ref, x1_ref, h2_ref, sc_ref):
    ya = jnp.dot(a_ref[...], wpa_ref[...], preferred_element_type=F32)
    yf = jnp.dot(f_ref[...], wpf_ref[...], preferred_element_type=F32)
    merged = sa_ref[...].astype(F32) * ya + sf_ref[...].astype(F32) * yf
    mod = mod_ref[0]
    g1, sh2, sc2 = mod[2:3], mod[3:4], mod[4:5]
    x1 = x_ref[...] + g1 * jnp.dot(merged.astype(BF16), wo_ref[...], preferred_element_type=F32)
    x1_ref[...] = x1
    r = lax.rsqrt(jnp.mean(x1 * x1, axis=-1, keepdims=True) + EPS)
    h2 = (x1 * r * g2_ref[...]) * (1.0 + sc2) + sh2
    h2_ref[0], h2_ref[1] = _pack_halves(h2)
    hh = h2.astype(BF16)
    hl = (h2 - hh.astype(F32)).astype(BF16)
    nt = (((1,), (1,)), ((), ()))
    lt = (lax.dot_general(wrh_ref[...], hh, nt, preferred_element_type=F32)
          + lax.dot_general(wrh_ref[...], hl, nt, preferred_element_type=F32)
          + lax.dot_general(wrl_ref[...], hh, nt, preferred_element_type=F32))
    sc_ref[...] = jax.nn.sigmoid(lt)


def _merge(attn, four, sa, sf, x2, mod, w_pa, w_pf, w_out, norm2_g, w_router, seq):
    t, d = x2.shape
    e = w_router.shape[1]
    wrt = w_router.T
    wrh = wrt.astype(BF16)
    wrl = (wrt - wrh.astype(F32)).astype(BF16)
    tm = _tile(seq, 512)
    tpb = seq // tm
    full = lambda shp: pl.BlockSpec(shp, lambda i: (0,) * len(shp))
    tok = lambda w: pl.BlockSpec((tm, w), lambda i: (i, 0))
    return pl.pallas_call(
        _merge_kernel,
        grid=(t // tm,),
        in_specs=[tok(attn.shape[1]), tok(four.shape[1]), tok(d), tok(d), tok(d),
                  pl.BlockSpec((1, N_ADA, d), lambda i: (i // tpb, 0, 0)),
                  full(w_pa.shape), full(w_pf.shape), full(w_out.shape), full((1, d)),
                  full((e, d)), full((e, d))],
        out_specs=[tok(d), pl.BlockSpec((2, tm, d // 4), lambda i: (0, i, 0)),
                   pl.BlockSpec((e, tm), lambda i: (0, i))],
        out_shape=[jax.ShapeDtypeStruct((t, d), F32),
                   jax.ShapeDtypeStruct((2, t, d // 4), jnp.uint32),
                   jax.ShapeDtypeStruct((e, t), F32)],
        compiler_params=_cparams("arbitrary"),
        name="merge",
    )(attn, four, sa, sf, x2, mod, w_pa.astype(BF16), w_pf.astype(BF16), w_out.astype(BF16),
      norm2_g.reshape(1, d), wrh, wrl)


def _route_kernel(s_ref, b_ref, tri_ref, idx_ref, w_ref, rank_ref, cnt_ref, carry_ref):
    @pl.when(pl.program_id(0) == 0)
    def _():
        carry_ref[...] = jnp.zeros_like(carry_ref)

    sc = s_ref[...]
    e, tr = sc.shape
    row = lax.broadcasted_iota(jnp.int32, (e, tr), 0)
    v = sc + b_ref[...]
    sel = jnp.zeros((e, tr), F32)
    idxs, ws = [], []
    for _ in range(TOP_K):
        m = jnp.max(v, axis=0, keepdims=True)
        idx = jnp.min(jnp.where(v == m, row, e), axis=0, keepdims=True)
        oh = row == idx
        ws.append(jnp.sum(jnp.where(oh, sc, 0.0), axis=0, keepdims=True))
        idxs.append(idx)
        v = jnp.where(oh, -jnp.inf, v)
        sel = sel + oh.astype(F32)
    wsum = ws[0]
    for w in ws[1:]:
        wsum = wsum + w
    selb = sel.astype(BF16)
    cum = jnp.dot(selb, tri_ref[...], preferred_element_type=F32) + carry_ref[...]
    for kk in range(TOP_K):
        oh = row == idxs[kk]
        rk = jnp.sum(jnp.where(oh, cum, 0.0), axis=0, keepdims=True)
        idx_ref[kk:kk + 1, :] = idxs[kk]
        rank_ref[kk:kk + 1, :] = rk.astype(jnp.int32)
        w_ref[kk:kk + 1, :] = ws[kk] / wsum * ROUTED_SCALE
    tot = carry_ref[...] + jnp.dot(selb, jnp.ones((tr, tr), BF16), preferred_element_type=F32)
    carry_ref[...] = tot
    cnt_ref[...] = tot


def _route(scores_t, router_bias):
    e, t = scores_t.shape
    tr = _tile(t, 256)
    tri = jnp.asarray(np.triu(np.ones((tr, tr), np.float32), 1), BF16)
    bias = jnp.broadcast_to(router_bias.reshape(e, 1), (e, tr)).astype(F32)
    blk = pl.BlockSpec((TOP_K, tr), lambda i: (0, i))
    return pl.pallas_call(
        _route_kernel,
        grid=(t // tr,),
        in_specs=[pl.BlockSpec((e, tr), lambda i: (0, i)),
                  pl.BlockSpec((e, tr), lambda i: (0, 0)),
                  pl.BlockSpec((tr, tr), lambda i: (0, 0))],
        out_specs=[blk, blk, blk, pl.BlockSpec((e, tr), lambda i: (0, 0))],
        out_shape=[jax.ShapeDtypeStruct((TOP_K, t), jnp.int32),
                   jax.ShapeDtypeStruct((TOP_K, t), F32),
                   jax.ShapeDtypeStruct((TOP_K, t), jnp.int32),
                   jax.ShapeDtypeStruct((e, tr), F32)],
        scratch_shapes=[pltpu.VMEM((e, tr), F32)],
        compiler_params=_cparams("arbitrary"),
        name="route",
    )(scores_t, bias, tri)


def _slots_kernel(idx_ref, rank_ref, ps_ref, slot_ref):
    ps = ps_ref[...]
    row = lax.broadcasted_iota(jnp.int32, ps.shape, 0)
    for kk in range(TOP_K):
        oh = row == idx_ref[kk:kk + 1, :]
        start = jnp.sum(jnp.where(oh, ps, 0), axis=0, keepdims=True)
        slot_ref[kk:kk + 1, :] = start + rank_ref[kk:kk + 1, :]


def _slots(idx_t, rank_t, p_start):
    _, t = idx_t.shape
    e = p_start.shape[0]
    ts = _tile(t, 512)
    ps = jnp.broadcast_to(p_start.reshape(e, 1), (e, ts))
    blk = pl.BlockSpec((TOP_K, ts), lambda i: (0, i))
    return pl.pallas_call(
        _slots_kernel,
        grid=(t // ts,),
        in_specs=[blk, blk, pl.BlockSpec((e, ts), lambda i: (0, 0))],
        out_specs=blk,
        out_shape=jax.ShapeDtypeStruct((TOP_K, t), jnp.int32),
        compiler_params=_cparams("arbitrary"),
        name="slots",
    )(idx_t, rank_t, ps)


def _dispatch(h2p, slot_t, n_slots):
    _, t, c = h2p.shape
    k = slot_t.shape[0]
    win = _tile(2 * t, SC_WINDOW)
    rows = h2p.reshape(2 * t, c)
    dest = jnp.concatenate([slot_t, slot_t + n_slots], axis=1)
    mesh = plsc.VectorSubcoreMesh(core_axis_name="core", subcore_axis_name="subcore")

    @pl.kernel(out_type=jax.ShapeDtypeStruct((2 * n_slots, c), h2p.dtype), mesh=mesh,
               scratch_types=[])
    def scatter_rows(x_hbm, s_hbm, o_hbm):
        def body(x_vmem, s_vmem):
            pltpu.sync_copy(x_vmem, o_hbm.at[s_vmem.at[0]])

        pltpu.emit_pipeline(
            body, grid=(2 * t // win, k),
            in_specs=[pl.BlockSpec((win, c), lambda i, j: (i, 0)),
                      pl.BlockSpec((1, win), lambda i, j: (j, i))],
            out_specs=[], core_axis_name=("core", "subcore"),
            dimension_semantics=(pltpu.PARALLEL, pltpu.ARBITRARY))(x_hbm, s_hbm)

    return scatter_rows(rows, dest).reshape(2, n_slots, c)


def _expert_kernel(be_ref, run_ref, rex_ref, nu_ref, xs_hbm, wg_hbm, wu_hbm, wd_hbm, ys_hbm,
                   xbuf, ybuf, wgf, wuf, wdf, wgb, wub, wdb, xsem, ysem, wsem, *, nblk):
    nu = nu_ref[0]
    rows = EXPERT_ROWS

    def x_copies(blk, slot):
        r0 = pl.multiple_of(blk * rows, rows)
        return [pltpu.make_async_copy(xs_hbm.at[h, pl.ds(r0, rows)], xbuf.at[slot, h],
                                      xsem.at[slot]) for h in range(2)]

    def y_copies(blk, slot):
        r0 = pl.multiple_of(blk * rows, rows)
        return [pltpu.make_async_copy(ybuf.at[slot, h], ys_hbm.at[h, pl.ds(r0, rows)],
                                      ysem.at[slot]) for h in range(2)]

    def weight_copies(e, which):
        return (pltpu.make_async_copy(wg_hbm.at[e], wgf.at[which], wsem.at[which]),
                pltpu.make_async_copy(wu_hbm.at[e], wuf.at[which], wsem.at[which]),
                pltpu.make_async_copy(wd_hbm.at[e], wdf.at[which], wsem.at[which]))

    for j in range(W_RING - 1):
        @pl.when(rex_ref[j] >= 0)
        def _():
            for cp in weight_copies(rex_ref[j], j):
                cp.start(priority=1)
    for j in range(X_RING - 1):
        @pl.when(j < nu)
        def _():
            for cp in x_copies(j, j):
                cp.start()

    def step(i, carry):
        ahead = i + X_RING - 1

        @pl.when(ahead < nu)
        def _():
            for cp in x_copies(ahead, ahead % X_RING):
                cp.start()

        prev = be_ref[jnp.maximum(i - 1, 0)]

        @pl.when(jnp.logical_or(i == 0, be_ref[i] != prev))
        def _():
            r = run_ref[i]
            par = r % W_RING
            for cp in weight_copies(be_ref[i], par):
                cp.wait()
            wgb[...] = wgf[par].astype(BF16)
            wub[...] = wuf[par].astype(BF16)
            wdb[...] = wdf[par].astype(BF16)
            later = rex_ref[r + W_RING - 1]

            @pl.when(later >= 0)
            def _():
                for cp in weight_copies(later, (r + W_RING - 1) % W_RING):
                    cp.start(priority=1)

        slot = i % X_RING
        for cp in x_copies(i, slot):
            cp.wait()
        x = _unpack_halves(xbuf[slot, 0], xbuf[slot, 1])
        g = jnp.dot(x, wgb[...], preferred_element_type=F32)
        u = jnp.dot(x, wub[...], preferred_element_type=F32)
        a = (_silu(g) * u).astype(BF16)
        y0, y1 = _pack_halves(jnp.dot(a, wdb[...], preferred_element_type=F32))

        out = i % 2

        @pl.when(i >= 2)
        def _():
            for cp in y_copies(i - 2, out):
                cp.wait()
        ybuf[out, 0] = y0
        ybuf[out, 1] = y1
        for cp in y_copies(i, out):
            cp.start()
        return carry

    lax.fori_loop(0, nu, step, 0)

    @pl.when(nu >= 2)
    def _():
        for cp in y_copies(nu - 2, nu % 2):
            cp.wait()
    for cp in y_copies(nu - 1, (nu - 1) % 2):
        cp.wait()

    ybuf[0] = jnp.zeros_like(ybuf[0])

    def zero_start(j, carry):
        for cp in y_copies(j, 0):
            cp.start()
        return carry

    def zero_wait(j, carry):
        for cp in y_copies(j, 0):
            cp.wait()
        return carry
    lax.fori_loop(nu, nblk, zero_start, 0)
    lax.fori_loop(nu, nblk, zero_wait, 0)


def _experts(blk_expert, blk_run, run_expert, nblk_used, xs, w_g, w_u, w_d):
    d, f = w_g.shape[1], w_g.shape[2]
    rows = EXPERT_ROWS
    nblk = blk_expert.shape[0]
    c = xs.shape[2]
    hbm = pl.BlockSpec(memory_space=pl.ANY)
    return pl.pallas_call(
        functools.partial(_expert_kernel, nblk=nblk),
        grid_spec=pltpu.PrefetchScalarGridSpec(
            num_scalar_prefetch=4,
            grid=(1,),
            in_specs=[hbm, hbm, hbm, hbm],
            out_specs=hbm,
            scratch_shapes=[pltpu.VMEM((X_RING, 2, rows, c), jnp.uint32),
                            pltpu.VMEM((2, 2, rows, c), jnp.uint32),
                            pltpu.VMEM((W_RING, d, f), F32), pltpu.VMEM((W_RING, d, f), F32),
                            pltpu.VMEM((W_RING, f, d), F32),
                            pltpu.VMEM((d, f), BF16), pltpu.VMEM((d, f), BF16),
                            pltpu.VMEM((f, d), BF16),
                            pltpu.SemaphoreType.DMA((X_RING,)), pltpu.SemaphoreType.DMA((2,)),
                            pltpu.SemaphoreType.DMA((W_RING,))]),
        out_shape=jax.ShapeDtypeStruct((2, nblk * rows, c), jnp.uint32),
        compiler_params=_cparams("arbitrary"),
        name="experts",
    )(blk_expert, blk_run, run_expert, nblk_used, xs, w_g, w_u, w_d)


def _gather(ys, slot_t):
    _, n_slots, c = ys.shape
    k, t = slot_t.shape
    p = 2 * k * t
    win = _tile(p, SC_WINDOW)
    src = jnp.concatenate([slot_t, slot_t + n_slots], axis=0).reshape(1, p)
    mesh = plsc.VectorSubcoreMesh(core_axis_name="core", subcore_axis_name="subcore")

    @pl.kernel(out_type=jax.ShapeDtypeStruct((p, c), ys.dtype), mesh=mesh, scratch_types=[])
    def gather_rows(y_hbm, s_hbm, o_hbm):
        def body(s_vmem, o_vmem):
            pltpu.sync_copy(y_hbm.at[s_vmem.at[0]], o_vmem)

        pltpu.emit_pipeline(
            body, grid=(p // win,),
            in_specs=[pl.BlockSpec((1, win), lambda i: (0, i))],
            out_specs=[pl.BlockSpec((win, c), lambda i: (i, 0))],
            core_axis_name=("core", "subcore"),
            dimension_semantics=(pltpu.PARALLEL,))(s_hbm, o_hbm)

    return gather_rows(ys.reshape(2 * n_slots, c), src).reshape(2, k, t, c)


def _combine_kernel(y_ref, w_ref, x1_ref, h2_ref, mod_ref, wsg_ref, wsu_ref, wsd_ref, o_ref):
    hb = _unpack_halves(h2_ref[0], h2_ref[1])
    g = jnp.dot(hb, wsg_ref[...], preferred_element_type=F32)
    u = jnp.dot(hb, wsu_ref[...], preferred_element_type=F32)
    acc = jnp.dot((_silu(g) * u).astype(BF16), wsd_ref[...], preferred_element_type=F32)
    w = w_ref[...]
    for kk in range(TOP_K):
        acc = acc + w[:, kk:kk + 1] * _unpack_halves(y_ref[0, kk], y_ref[1, kk]).astype(F32)
    g2 = mod_ref[0][5:6]
    o_ref[...] = x1_ref[...] + g2 * acc


def _combine(y_tok, w_tk, x1, h2p, mod, w_sg, w_su, w_sd, seq):
    t, d = x1.shape
    c = d // 4
    tc = _tile(seq, 256)
    tpb = seq // tc
    full = lambda shp: pl.BlockSpec(shp, lambda i: (0,) * len(shp))
    tok = lambda w: pl.BlockSpec((tc, w), lambda i: (i, 0))
    return pl.pallas_call(
        _combine_kernel,
        grid=(t // tc,),
        in_specs=[pl.BlockSpec((2, TOP_K, tc, c), lambda i: (0, 0, i, 0)),
                  tok(TOP_K), tok(d),
                  pl.BlockSpec((2, tc, c), lambda i: (0, i, 0)),
                  pl.BlockSpec((1, N_ADA, d), lambda i: (i // tpb, 0, 0)),
                  full(w_sg.shape), full(w_su.shape), full(w_sd.shape)],
        out_specs=tok(d),
        out_shape=jax.ShapeDtypeStruct((t, d), F32),
        compiler_params=_cparams("arbitrary"),
        name="combine",
    )(y_tok, w_tk, x1, h2p, mod, w_sg.astype(BF16), w_su.astype(BF16), w_sd.astype(BF16))


def _layer(x, c, w_ada, b_ada, norm1_g, w_in, q_a_norm_g, w_uq, kv_a_norm_g, w_ukv,
           q_norm_g, k_norm_g, w_proj_attn, w_proj_fourier, w_out, norm2_g,
           w_router, router_bias, w_exp_gate, w_exp_up, w_exp_down,
           w_sh_gate, w_sh_up, w_sh_down):
    bsz, seq, d = x.shape
    t = bsz * seq
    e = w_router.shape[1]
    x2 = x.reshape(t, d)

    mod = _ada(c, w_ada, b_ada).reshape(bsz, N_ADA, d)
    q, k, v, zf, sa, sf = _inproj(x2, mod, norm1_g, w_in, q_a_norm_g, w_uq, kv_a_norm_g,
                                  w_ukv, q_norm_g, k_norm_g, bsz, seq)
    attn = _attention(q, k, v).reshape(t, N_HEADS * V_DIM)
    four = _fourier(zf.reshape(bsz, seq, zf.shape[1])).reshape(t, zf.shape[1])
    x1, h2, scores_t = _merge(attn, four, sa, sf, x2, mod, w_proj_attn, w_proj_fourier,
                              w_out, norm2_g, w_router, seq)

    idx_t, w_t, rank_t, cnt = _route(scores_t, router_bias)
    counts = cnt[:, 0].astype(jnp.int32)
    rows = EXPERT_ROWS
    nblk = -(-(t * TOP_K) // rows) + e
    padded = ((counts + rows - 1) // rows) * rows
    p_end = jnp.cumsum(padded)
    p_start = p_end - padded
    nblk_used = (p_end[-1] // rows).astype(jnp.int32)
    blk_start = jnp.arange(nblk, dtype=jnp.int32) * rows
    blk_first = jnp.minimum(blk_start, p_end[-1] - 1)
    hit = p_end[None, :] <= blk_first[:, None]
    blk_expert = jnp.clip(jnp.sum(hit.astype(jnp.int32), axis=1), 0, e - 1)

    slot_t = _slots(idx_t, rank_t, p_start.astype(jnp.int32))
    xs = _dispatch(h2, slot_t, nblk * rows)
    eid = jnp.arange(e, dtype=jnp.int32)
    used = counts > 0
    run = jnp.cumsum(used.astype(jnp.int32)) - 1
    blk_run = jnp.sum(jnp.where(blk_expert[:, None] == eid[None, :], run[None, :], 0),
                      axis=1).astype(jnp.int32)
    rid = jnp.arange(e + W_RING, dtype=jnp.int32)
    match = jnp.logical_and(used[None, :], run[None, :] == rid[:, None])
    run_expert = jnp.where(jnp.any(match, axis=1),
                           jnp.sum(jnp.where(match, eid[None, :], 0), axis=1), -1).astype(jnp.int32)
    ys = _experts(blk_expert, blk_run, run_expert, nblk_used.reshape(1), xs,
                  w_exp_gate, w_exp_up, w_exp_down)
    out = _combine(_gather(ys, slot_t), w_t.T, x1, h2, mod, w_sh_gate, w_sh_up, w_sh_down, seq)
    return out.reshape(bsz, seq, d)


def kernel(x, c, w_ada, b_ada, norm1_g, w_in, q_a_norm_g, w_uq, kv_a_norm_g, w_ukv, q_norm_g,
           k_norm_g, w_proj_attn, w_proj_fourier, w_out, norm2_g, w_router, router_bias,
           w_exp_gate, w_exp_up, w_exp_down, w_sh_gate, w_sh_up, w_sh_down):
    for l in range(w_ada.shape[0]):
        x = _layer(x, c, w_ada[l], b_ada[l], norm1_g[l], w_in[l], q_a_norm_g[l], w_uq[l],
                   kv_a_norm_g[l], w_ukv[l], q_norm_g[l], k_norm_g[l], w_proj_attn[l],
                   w_proj_fourier[l], w_out[l], norm2_g[l], w_router[l], router_bias[l],
                   w_exp_gate[l], w_exp_up[l], w_exp_down[l], w_sh_gate[l], w_sh_up[l],
                   w_sh_down[l])
    return x
```

```python
import functools
import math

import numpy as np
import jax
import jax.numpy as jnp
from jax import lax
from jax.experimental import pallas as pl
from jax.experimental.pallas import tpu as pltpu
from jax.experimental.pallas import tpu_sc as plsc

N_HEADS = 8
QK_NOPE = 64
QK_ROPE = 32
V_DIM = 64
FOURIER_GROUP = 64
TOP_K = 8
ROUTED_SCALE = 2.5
EPS = 1e-6
ROPE_THETA = 10000.0
N_ADA = 6

LANES = 128
EXPERT_ROWS = 256
X_RING = 4
W_RING = 3
COMBINE_CHUNKS = 4
SC_WINDOW = 128
VMEM_LIMIT = 48 * 1024 * 1024

F32 = jnp.float32
BF16 = jnp.bfloat16


def _cparams(*sem):
    return pltpu.CompilerParams(dimension_semantics=sem, vmem_limit_bytes=VMEM_LIMIT)


def _tile(n, pref):
    t = min(n, pref)
    assert n % t == 0, (n, pref)
    return t


def _silu(v):
    return v * jax.nn.sigmoid(v)


def _pack_halves(m):
    d = m.shape[1]
    lo = lax.bitcast_convert_type(m[:, :d // 2].astype(BF16).astype(F32), jnp.uint32)
    hi = lax.bitcast_convert_type(m[:, d // 2:].astype(BF16).astype(F32), jnp.uint32)
    w = (lo >> 16) | (hi & jnp.uint32(0xFFFF0000))
    return w[:, :d // 4], w[:, d // 4:]


def _unpack_halves(w0, w1):
    def lo(w):
        return lax.bitcast_convert_type(w << 16, F32)

    def hi(w):
        return lax.bitcast_convert_type(w & jnp.uint32(0xFFFF0000), F32)
    return jnp.concatenate([lo(w0), lo(w1), hi(w0), hi(w1)], axis=1).astype(BF16)


def _ada_kernel(c_ref, w_ref, b_ref, o_ref):
    a = _silu(c_ref[...])
    o_ref[...] = jnp.dot(a, w_ref[...], preferred_element_type=F32,
                         precision=lax.Precision.HIGHEST) + b_ref[...]


def _ada(c, w_ada, b_ada):
    bsz, d = c.shape
    n = w_ada.shape[1]
    tn = _tile(n, d)
    return pl.pallas_call(
        _ada_kernel,
        grid=(n // tn,),
        in_specs=[pl.BlockSpec((bsz, d), lambda j: (0, 0)),
                  pl.BlockSpec((d, tn), lambda j: (0, j)),
                  pl.BlockSpec((1, tn), lambda j: (0, j))],
        out_specs=pl.BlockSpec((bsz, tn), lambda j: (0, j)),
        out_shape=jax.ShapeDtypeStruct((bsz, n), F32),
        compiler_params=_cparams("arbitrary"),
        name="ada",
    )(c, w_ada, b_ada.reshape(1, n))


def _head_norm_rope(t, trot, a, b):
    ms = jnp.sum(t * t, axis=-1, keepdims=True) * (1.0 / (QK_NOPE + QK_ROPE))
    return (t * a + trot * b) * lax.rsqrt(ms + EPS)


def _inproj_kernel(x_ref, mod_ref, g1_ref, wa_ref, wf_ref, wga_ref, wgf_ref,
                   gq_ref, gkv_ref, wuq_ref, wkv_ref,
                   aq_ref, bq_ref, ak_ref, bk_ref,
                   q_ref, k_ref, v_ref, zf_ref, sa_ref, sf_ref, *, ql, kvl):
    x = x_ref[...]
    mod = mod_ref[0]
    sh1, sc1 = mod[0:1], mod[1:2]
    r = lax.rsqrt(jnp.mean(x * x, axis=-1, keepdims=True) + EPS)
    h = (x * r * g1_ref[...]) * (1.0 + sc1) + sh1
    hb = h.astype(BF16)

    zf_ref[...] = jnp.dot(hb, wf_ref[...], preferred_element_type=F32).astype(BF16)
    sa_ref[...] = jax.nn.sigmoid(
        jnp.dot(hb, wga_ref[...], preferred_element_type=F32)).astype(BF16)
    sf_ref[...] = jax.nn.sigmoid(
        jnp.dot(hb, wgf_ref[...], preferred_element_type=F32)).astype(BF16)

    za = jnp.dot(hb, wa_ref[...], preferred_element_type=F32)
    zq = za[:, :ql]
    cq = zq * lax.rsqrt(jnp.mean(zq * zq, axis=-1, keepdims=True) + EPS) * gq_ref[...]
    qall = jnp.dot(cq.astype(BF16), wuq_ref[...], preferred_element_type=F32)

    zk = za[:, ql:]
    kvn = zk[:, :kvl]
    rk = lax.rsqrt(jnp.mean(kvn * kvn, axis=-1, keepdims=True) + EPS)
    lane = lax.broadcasted_iota(jnp.int32, zk.shape, 1)
    u = zk * jnp.where(lane < kvl, rk, 1.0) * gkv_ref[...]
    kvall = jnp.dot(u.astype(BF16), wkv_ref[...], preferred_element_type=F32)

    aq, bq, ak, bk = aq_ref[...], bq_ref[...], ak_ref[...], bk_ref[...]
    hw = N_HEADS * LANES
    for hd in range(N_HEADS):
        lo, hi = hd * LANES, (hd + 1) * LANES
        q_ref[0, hd] = _head_norm_rope(qall[:, lo:hi], qall[:, hw + lo:hw + hi],
                                       aq, bq).astype(BF16)
        k_ref[0, hd] = _head_norm_rope(kvall[:, lo:hi], kvall[:, hw + lo:hw + hi],
                                       ak, bk).astype(BF16)
        v_ref[0, hd] = kvall[:, 2 * hw + hd * V_DIM: 2 * hw + (hd + 1) * V_DIM].astype(BF16)


def _rope_tables(seq):
    half = QK_ROPE // 2
    pos = np.arange(seq, dtype=np.float64)
    inv = ROPE_THETA ** (-np.arange(0, QK_ROPE, 2, dtype=np.float64) / QK_ROPE)
    ang = pos[:, None] * inv[None, :]
    c, s = np.cos(ang), np.sin(ang)
    cos = np.ones((seq, LANES)); sin = np.zeros((seq, LANES))
    cos[:, QK_NOPE:QK_NOPE + half] = c
    cos[:, QK_NOPE + half:QK_NOPE + QK_ROPE] = c
    sin[:, QK_NOPE:QK_NOPE + half] = -s
    sin[:, QK_NOPE + half:QK_NOPE + QK_ROPE] = s
    return jnp.asarray(cos, F32), jnp.asarray(sin, F32)


def _partner_columns(w):
    half = QK_ROPE // 2
    lo, mid, hi = QK_NOPE, QK_NOPE + half, QK_NOPE + QK_ROPE
    z = jnp.zeros_like(w)
    return jnp.concatenate([z[..., :lo], w[..., mid:hi], w[..., lo:mid], z[..., hi:]], axis=-1)


def _inproj(x2, mod, norm1_g, w_in, q_a_g, w_uq, kv_a_g, w_ukv, q_g, k_g, bsz, seq):
    t, d = x2.shape
    ql, kvl = q_a_g.shape[0], kv_a_g.shape[0]
    hq = QK_NOPE + QK_ROPE
    fw = w_in.shape[1] - ql - kvl - QK_ROPE - 2 * d
    o1, o2, o3, o4, o5 = ql, ql + kvl, ql + kvl + QK_ROPE, ql + kvl + QK_ROPE + fw, \
        ql + kvl + QK_ROPE + fw + d
    assert ql % LANES == 0 and kvl % LANES == 0

    wa = jnp.concatenate([w_in[:, :o3], jnp.zeros((d, LANES - QK_ROPE), F32)], axis=1).astype(BF16)
    wf = w_in[:, o3:o4].astype(BF16)
    wga = w_in[:, o4:o5].astype(BF16)
    wgf = w_in[:, o5:].astype(BF16)

    wuq = w_uq.reshape(ql, N_HEADS, hq)
    wuq = jnp.pad(wuq, ((0, 0), (0, 0), (0, LANES - hq)))
    wuq = jnp.concatenate([wuq.reshape(ql, N_HEADS * LANES),
                           _partner_columns(wuq).reshape(ql, N_HEADS * LANES)], axis=1).astype(BF16)
    wukv = w_ukv.reshape(kvl, N_HEADS, QK_NOPE + V_DIM)
    wk = jnp.pad(wukv[:, :, :QK_NOPE], ((0, 0), (0, 0), (0, LANES - QK_NOPE)))
    place = jnp.zeros((QK_ROPE, N_HEADS, LANES), F32)
    place = place.at[jnp.arange(QK_ROPE), :, QK_NOPE + jnp.arange(QK_ROPE)].set(1.0)
    wk = jnp.concatenate([wk, place, jnp.zeros((LANES - QK_ROPE, N_HEADS, LANES), F32)], axis=0)
    wv = jnp.concatenate([wukv[:, :, QK_NOPE:], jnp.zeros((LANES, N_HEADS, V_DIM), F32)], axis=0)
    wkv = jnp.concatenate([wk.reshape(kvl + LANES, N_HEADS * LANES),
                           _partner_columns(wk).reshape(kvl + LANES, N_HEADS * LANES),
                           wv.reshape(kvl + LANES, N_HEADS * V_DIM)], axis=1).astype(BF16)

    gkv = jnp.concatenate([kv_a_g, jnp.ones((LANES,), F32)]).reshape(1, kvl + LANES)
    pad = jnp.zeros((LANES - hq,), F32)
    qg = jnp.concatenate([q_g * (hq ** -0.5), pad])
    kg = jnp.concatenate([k_g, pad])
    cos, sin = _rope_tables(seq)
    aq, bq = qg[None, :] * cos, _partner_columns(qg)[None, :] * sin
    ak, bk = kg[None, :] * cos, _partner_columns(kg)[None, :] * sin

    tm = _tile(seq, 512)
    tpb = seq // tm
    full = lambda shp: pl.BlockSpec(shp, lambda i: (0,) * len(shp))
    tok = lambda w: pl.BlockSpec((tm, w), lambda i: (i, 0))
    head = lambda w: pl.BlockSpec((1, N_HEADS, tm, w), lambda i: (i // tpb, 0, i % tpb, 0))
    rope = pl.BlockSpec((tm, LANES), lambda i: (i % tpb, 0))
    return pl.pallas_call(
        functools.partial(_inproj_kernel, ql=ql, kvl=kvl),
        grid=(t // tm,),
        in_specs=[tok(d),
                  pl.BlockSpec((1, N_ADA, d), lambda i: (i // tpb, 0, 0)),
                  full((1, d)), full(wa.shape), full(wf.shape), full(wga.shape), full(wgf.shape),
                  full((1, ql)), full(gkv.shape), full(wuq.shape), full(wkv.shape),
                  rope, rope, rope, rope],
        out_specs=[head(LANES), head(LANES), head(V_DIM), tok(fw), tok(d), tok(d)],
        out_shape=[jax.ShapeDtypeStruct((bsz, N_HEADS, seq, LANES), BF16),
                   jax.ShapeDtypeStruct((bsz, N_HEADS, seq, LANES), BF16),
                   jax.ShapeDtypeStruct((bsz, N_HEADS, seq, V_DIM), BF16),
                   jax.ShapeDtypeStruct((t, fw), BF16),
                   jax.ShapeDtypeStruct((t, d), BF16),
                   jax.ShapeDtypeStruct((t, d), BF16)],
        compiler_params=_cparams("arbitrary"),
        name="inproj",
    )(x2, mod, norm1_g.reshape(1, d), wa, wf, wga, wgf, q_a_g.reshape(1, ql), gkv, wuq, wkv,
      aq, bq, ak, bk)


def _attn_kernel(q_ref, k_ref, v_ref, o_ref):
    for hd in range(N_HEADS):
        s = lax.dot_general(q_ref[0, hd], k_ref[0, hd], (((1,), (1,)), ((), ())),
                            preferred_element_type=F32)
        m = jnp.max(s, axis=-1, keepdims=True)
        p = jnp.exp(s - m)
        l = jnp.sum(p, axis=-1, keepdims=True)
        o = jnp.dot(p.astype(BF16), v_ref[0, hd], preferred_element_type=F32)
        o_ref[0, :, hd * V_DIM:(hd + 1) * V_DIM] = (o / l).astype(BF16)


def _attention(q, k, v):
    bsz, _, seq, _ = q.shape
    tq = _tile(seq, 512)
    return pl.pallas_call(
        _attn_kernel,
        grid=(bsz, seq // tq),
        in_specs=[pl.BlockSpec((1, N_HEADS, tq, LANES), lambda b, j: (b, 0, j, 0)),
                  pl.BlockSpec((1, N_HEADS, seq, LANES), lambda b, j: (b, 0, 0, 0)),
                  pl.BlockSpec((1, N_HEADS, seq, V_DIM), lambda b, j: (b, 0, 0, 0))],
        out_specs=pl.BlockSpec((1, tq, N_HEADS * V_DIM), lambda b, j: (b, j, 0)),
        out_shape=jax.ShapeDtypeStruct((bsz, seq, N_HEADS * V_DIM), BF16),
        compiler_params=_cparams("arbitrary", "arbitrary"),
        name="attn",
    )(q, k, v)


def _fourier_kernel(z_ref, wc_ref, ws_ref, tab_ref, o_ref, u_ref, *, seq):
    @pl.when(pl.program_id(1) == 0)
    def _():
        z = z_ref[0]
        u_ref[:seq, :] = jnp.dot(z, wc_ref[...], preferred_element_type=F32).astype(BF16)
        u_ref[seq:, :] = jnp.dot(z, ws_ref[...], preferred_element_type=F32).astype(BF16)

    o_ref[0] = jnp.dot(tab_ref[...], u_ref[...], preferred_element_type=F32).astype(BF16)


def _fourier_tables(seq, fw):
    g = FOURIER_GROUP
    n = np.arange(seq, dtype=np.int64)
    ang = 2.0 * np.pi * ((n[:, None] * n[None, :]) % seq).astype(np.float64) / seq
    tab = np.concatenate([np.cos(ang), -np.sin(ang)], axis=1)
    c = np.arange(g, dtype=np.int64)
    angc = 2.0 * np.pi * ((c[:, None] * c[None, :]) % g).astype(np.float64) / g
    scale = 1.0 / math.sqrt(seq * g)
    eye = np.eye(fw // g)
    wc = np.kron(eye, np.cos(angc) * scale)
    ws = np.kron(eye, np.sin(angc) * scale)
    return (jnp.asarray(tab, F32).astype(BF16), jnp.asarray(wc, F32).astype(BF16),
            jnp.asarray(ws, F32).astype(BF16))


def _fourier(zf):
    bsz, seq, fw = zf.shape
    tab, wc, ws = _fourier_tables(seq, fw)
    tr = _tile(seq, 256)
    return pl.pallas_call(
        functools.partial(_fourier_kernel, seq=seq),
        grid=(bsz, seq // tr),
        in_specs=[pl.BlockSpec((1, seq, fw), lambda b, j: (b, 0, 0)),
                  pl.BlockSpec((fw, fw), lambda b, j: (0, 0)),
                  pl.BlockSpec((fw, fw), lambda b, j: (0, 0)),
                  pl.BlockSpec((tr, 2 * seq), lambda b, j: (j, 0))],
        out_specs=pl.BlockSpec((1, tr, fw), lambda b, j: (b, j, 0)),
        out_shape=jax.ShapeDtypeStruct((bsz, seq, fw), BF16),
        scratch_shapes=[pltpu.VMEM((2 * seq, fw), BF16)],
        compiler_params=_cparams("arbitrary", "arbitrary"),
        name="fourier",
    )(zf, wc, ws, tab)


def _merge_kernel(a_ref, f_ref, sa_ref, sf_ref, x_ref, mod_ref, wpa_ref, wpf_ref, wo_ref,
                  g2_ref, wrh_ref, wrl_ref, x1_ref, h2_ref, sc_ref):
    ya = jnp.dot(a_ref[...], wpa_ref[...], preferred_element_type=F32)
    yf = jnp.dot(f_ref[...], wpf_ref[...], preferred_element_type=F32)
    merged = sa_ref[...].astype(F32) * ya + sf_ref[...].astype(F32) * yf
    mod = mod_ref[0]
    g1, sh2, sc2 = mod[2:3], mod[3:4], mod[4:5]
    x1 = x_ref[...] + g1 * jnp.dot(merged.astype(BF16), wo_ref[...], preferred_element_type=F32)
    x1_ref[...] = x1
    r = lax.rsqrt(jnp.mean(x1 * x1, axis=-1, keepdims=True) + EPS)
    h2 = (x1 * r * g2_ref[...]) * (1.0 + sc2) + sh2
    h2_ref[0], h2_ref[1] = _pack_halves(h2)
    hh = h2.astype(BF16)
    hl = (h2 - hh.astype(F32)).astype(BF16)
    nt = (((1,), (1,)), ((), ()))
    lt = (lax.dot_general(wrh_ref[...], hh, nt, preferred_element_type=F32)
          + lax.dot_general(wrh_ref[...], hl, nt, preferred_element_type=F32)
          + lax.dot_general(wrl_ref[...], hh, nt, preferred_element_type=F32))
    sc_ref[...] = jax.nn.sigmoid(lt)


def _merge(attn, four, sa, sf, x2, mod, w_pa, w_pf, w_out, norm2_g, w_router, seq):
    t, d = x2.shape
    e = w_router.shape[1]
    wrt = w_router.T
    wrh = wrt.astype(BF16)
    wrl = (wrt - wrh.astype(F32)).astype(BF16)
    tm = _tile(seq, 512)
    tpb = seq // tm
    full = lambda shp: pl.BlockSpec(shp, lambda i: (0,) * len(shp))
    tok = lambda w: pl.BlockSpec((tm, w), lambda i: (i, 0))
    return pl.pallas_call(
        _merge_kernel,
        grid=(t // tm,),
        in_specs=[tok(attn.shape[1]), tok(four.shape[1]), tok(d), tok(d), tok(d),
                  pl.BlockSpec((1, N_ADA, d), lambda i: (i // tpb, 0, 0)),
                  full(w_pa.shape), full(w_pf.shape), full(w_out.shape), full((1, d)),
                  full((e, d)), full((e, d))],
        out_specs=[tok(d), pl.BlockSpec((2, tm, d // 4), lambda i: (0, i, 0)),
                   pl.BlockSpec((e, tm), lambda i: (0, i))],
        out_shape=[jax.ShapeDtypeStruct((t, d), F32),
                   jax.ShapeDtypeStruct((2, t, d // 4), jnp.uint32),
                   jax.ShapeDtypeStruct((e, t), F32)],
        compiler_params=_cparams("arbitrary"),
        name="merge",
    )(attn, four, sa, sf, x2, mod, w_pa.astype(BF16), w_pf.astype(BF16), w_out.astype(BF16),
      norm2_g.reshape(1, d), wrh, wrl)


def _route_kernel(s_ref, b_ref, tri_ref, idx_ref, w_ref, rank_ref, cnt_ref, carry_ref):
    @pl.when(pl.program_id(0) == 0)
    def _():
        carry_ref[...] = jnp.zeros_like(carry_ref)

    sc = s_ref[...]
    e, tr = sc.shape
    row = lax.broadcasted_iota(jnp.int32, (e, tr), 0)
    v = sc + b_ref[...]
    sel = jnp.zeros((e, tr), F32)
    idxs, ws = [], []
    for _ in range(TOP_K):
        m = jnp.max(v, axis=0, keepdims=True)
        idx = jnp.min(jnp.where(v == m, row, e), axis=0, keepdims=True)
        oh = row == idx
        ws.append(jnp.sum(jnp.where(oh, sc, 0.0), axis=0, keepdims=True))
        idxs.append(idx)
        v = jnp.where(oh, -jnp.inf, v)
        sel = sel + oh.astype(F32)
    wsum = ws[0]
    for w in ws[1:]:
        wsum = wsum + w
    selb = sel.astype(BF16)
    cum = jnp.dot(selb, tri_ref[...], preferred_element_type=F32) + carry_ref[...]
    for kk in range(TOP_K):
        oh = row == idxs[kk]
        rk = jnp.sum(jnp.where(oh, cum, 0.0), axis=0, keepdims=True)
        idx_ref[kk:kk + 1, :] = idxs[kk]
        rank_ref[kk:kk + 1, :] = rk.astype(jnp.int32)
        w_ref[kk:kk + 1, :] = ws[kk] / wsum * ROUTED_SCALE
    tot = carry_ref[...] + jnp.dot(selb, jnp.ones((tr, tr), BF16), preferred_element_type=F32)
    carry_ref[...] = tot
    cnt_ref[...] = tot


def _route(scores_t, router_bias):
    e, t = scores_t.shape
    tr = _tile(t, 256)
    tri = jnp.asarray(np.triu(np.ones((tr, tr), np.float32), 1), BF16)
    bias = jnp.broadcast_to(router_bias.reshape(e, 1), (e, tr)).astype(F32)
    blk = pl.BlockSpec((TOP_K, tr), lambda i: (0, i))
    return pl.pallas_call(
        _route_kernel,
        grid=(t // tr,),
        in_specs=[pl.BlockSpec((e, tr), lambda i: (0, i)),
                  pl.BlockSpec((e, tr), lambda i: (0, 0)),
                  pl.BlockSpec((tr, tr), lambda i: (0, 0))],
        out_specs=[blk, blk, blk, pl.BlockSpec((e, tr), lambda i: (0, 0))],
        out_shape=[jax.ShapeDtypeStruct((TOP_K, t), jnp.int32),
                   jax.ShapeDtypeStruct((TOP_K, t), F32),
                   jax.ShapeDtypeStruct((TOP_K, t), jnp.int32),
                   jax.ShapeDtypeStruct((e, tr), F32)],
        scratch_shapes=[pltpu.VMEM((e, tr), F32)],
        compiler_params=_cparams("arbitrary"),
        name="route",
    )(scores_t, bias, tri)


def _slots_kernel(idx_ref, rank_ref, ps_ref, slot_ref):
    ps = ps_ref[...]
    row = lax.broadcasted_iota(jnp.int32, ps.shape, 0)
    for kk in range(TOP_K):
        oh = row == idx_ref[kk:kk + 1, :]
        start = jnp.sum(jnp.where(oh, ps, 0), axis=0, keepdims=True)
        slot_ref[kk:kk + 1, :] = start + rank_ref[kk:kk + 1, :]


def _slots(idx_t, rank_t, p_start):
    _, t = idx_t.shape
    e = p_start.shape[0]
    ts = _tile(t, 512)
    ps = jnp.broadcast_to(p_start.reshape(e, 1), (e, ts))
    blk = pl.BlockSpec((TOP_K, ts), lambda i: (0, i))
    return pl.pallas_call(
        _slots_kernel,
        grid=(t // ts,),
        in_specs=[blk, blk, pl.BlockSpec((e, ts), lambda i: (0, 0))],
        out_specs=blk,
        out_shape=jax.ShapeDtypeStruct((TOP_K, t), jnp.int32),
        compiler_params=_cparams("arbitrary"),
        name="slots",
    )(idx_t, rank_t, ps)


def _dispatch(h2p, slot_t, n_slots):
    _, t, c = h2p.shape
    k = slot_t.shape[0]
    win = _tile(2 * t, SC_WINDOW)
    rows = h2p.reshape(2 * t, c)
    dest = jnp.concatenate([slot_t, slot_t + n_slots], axis=1)
    mesh = plsc.VectorSubcoreMesh(core_axis_name="core", subcore_axis_name="subcore")

    @pl.kernel(out_type=jax.ShapeDtypeStruct((2 * n_slots, c), h2p.dtype), mesh=mesh,
               scratch_types=[])
    def scatter_rows(x_hbm, s_hbm, o_hbm):
        def body(x_vmem, s_vmem):
            pltpu.sync_copy(x_vmem, o_hbm.at[s_vmem.at[0]])

        pltpu.emit_pipeline(
            body, grid=(2 * t // win, k),
            in_specs=[pl.BlockSpec((win, c), lambda i, j: (i, 0)),
                      pl.BlockSpec((1, win), lambda i, j: (j, i))],
            out_specs=[], core_axis_name=("core", "subcore"),
            dimension_semantics=(pltpu.PARALLEL, pltpu.ARBITRARY))(x_hbm, s_hbm)

    return scatter_rows(rows, dest).reshape(2, n_slots, c)


def _expert_kernel(be_ref, run_ref, rex_ref, nu_ref, xs_hbm, wg_hbm, wu_hbm, wd_hbm, ys_hbm,
                   xbuf, ybuf, wgf, wuf, wdf, wgb, wub, wdb, xsem, ysem, wsem, *, nblk):
    nu = nu_ref[0]
    rows = EXPERT_ROWS

    def x_copies(blk, slot):
        r0 = pl.multiple_of(blk * rows, rows)
        return [pltpu.make_async_copy(xs_hbm.at[h, pl.ds(r0, rows)], xbuf.at[slot, h],
                                      xsem.at[slot]) for h in range(2)]

    def y_copies(blk, slot):
        r0 = pl.multiple_of(blk * rows, rows)
        return [pltpu.make_async_copy(ybuf.at[slot, h], ys_hbm.at[h, pl.ds(r0, rows)],
                                      ysem.at[slot]) for h in range(2)]

    def weight_copies(e, which):
        return (pltpu.make_async_copy(wg_hbm.at[e], wgf.at[which], wsem.at[which]),
                pltpu.make_async_copy(wu_hbm.at[e], wuf.at[which], wsem.at[which]),
                pltpu.make_async_copy(wd_hbm.at[e], wdf.at[which], wsem.at[which]))

    for j in range(W_RING - 1):
        @pl.when(rex_ref[j] >= 0)
        def _():
            for cp in weight_copies(rex_ref[j], j):
                cp.start(priority=1)
    for j in range(X_RING - 1):
        @pl.when(j < nu)
        def _():
            for cp in x_copies(j, j):
                cp.start()

    def step(i, carry):
        ahead = i + X_RING - 1

        @pl.when(ahead < nu)
        def _():
            for cp in x_copies(ahead, ahead % X_RING):
                cp.start()

        prev = be_ref[jnp.maximum(i - 1, 0)]

        @pl.when(jnp.logical_or(i == 0, be_ref[i] != prev))
        def _():
            r = run_ref[i]
            par = r % W_RING
            for cp in weight_copies(be_ref[i], par):
                cp.wait()
            wgb[...] = wgf[par].astype(BF16)
            wub[...] = wuf[par].astype(BF16)
            wdb[...] = wdf[par].astype(BF16)
            later = rex_ref[r + W_RING - 1]

            @pl.when(later >= 0)
            def _():
                for cp in weight_copies(later, (r + W_RING - 1) % W_RING):
                    cp.start(priority=1)

        slot = i % X_RING
        for cp in x_copies(i, slot):
            cp.wait()
        x = _unpack_halves(xbuf[slot, 0], xbuf[slot, 1])
        g = jnp.dot(x, wgb[...], preferred_element_type=F32)
        u = jnp.dot(x, wub[...], preferred_element_type=F32)
        a = (_silu(g) * u).astype(BF16)
        y0, y1 = _pack_halves(jnp.dot(a, wdb[...], preferred_element_type=F32))

        out = i % 2

        @pl.when(i >= 2)
        def _():
            for cp in y_copies(i - 2, out):
                cp.wait()
        ybuf[out, 0] = y0
        ybuf[out, 1] = y1
        for cp in y_copies(i, out):
            cp.start()
        return carry

    lax.fori_loop(0, nu, step, 0)

    @pl.when(nu >= 2)
    def _():
        for cp in y_copies(nu - 2, nu % 2):
            cp.wait()
    for cp in y_copies(nu - 1, (nu - 1) % 2):
        cp.wait()

    ybuf[0] = jnp.zeros_like(ybuf[0])

    def zero_start(j, carry):
        for cp in y_copies(j, 0):
            cp.start()
        return carry

    def zero_wait(j, carry):
        for cp in y_copies(j, 0):
            cp.wait()
        return carry
    lax.fori_loop(nu, nblk, zero_start, 0)
    lax.fori_loop(nu, nblk, zero_wait, 0)


def _experts(blk_expert, blk_run, run_expert, nblk_used, xs, w_g, w_u, w_d):
    d, f = w_g.shape[1], w_g.shape[2]
    rows = EXPERT_ROWS
    nblk = blk_expert.shape[0]
    c = xs.shape[2]
    hbm = pl.BlockSpec(memory_space=pl.ANY)
    return pl.pallas_call(
        functools.partial(_expert_kernel, nblk=nblk),
        grid_spec=pltpu.PrefetchScalarGridSpec(
            num_scalar_prefetch=4,
            grid=(1,),
            in_specs=[hbm, hbm, hbm, hbm],
            out_specs=hbm,
            scratch_shapes=[pltpu.VMEM((X_RING, 2, rows, c), jnp.uint32),
                            pltpu.VMEM((2, 2, rows, c), jnp.uint32),
                            pltpu.VMEM((W_RING, d, f), F32), pltpu.VMEM((W_RING, d, f), F32),
                            pltpu.VMEM((W_RING, f, d), F32),
                            pltpu.VMEM((d, f), BF16), pltpu.VMEM((d, f), BF16),
                            pltpu.VMEM((f, d), BF16),
                            pltpu.SemaphoreType.DMA((X_RING,)), pltpu.SemaphoreType.DMA((2,)),
                            pltpu.SemaphoreType.DMA((W_RING,))]),
        out_shape=jax.ShapeDtypeStruct((2, nblk * rows, c), jnp.uint32),
        compiler_params=_cparams("arbitrary"),
        name="experts",
    )(blk_expert, blk_run, run_expert, nblk_used, xs, w_g, w_u, w_d)


def _gather(ys, slot_t):
    _, n_slots, c = ys.shape
    k, t = slot_t.shape
    p = 2 * k * t
    win = _tile(p, SC_WINDOW)
    src = jnp.concatenate([slot_t, slot_t + n_slots], axis=0).reshape(1, p)
    mesh = plsc.VectorSubcoreMesh(core_axis_name="core", subcore_axis_name="subcore")

    @pl.kernel(out_type=jax.ShapeDtypeStruct((p, c), ys.dtype), mesh=mesh, scratch_types=[])
    def gather_rows(y_hbm, s_hbm, o_hbm):
        def body(s_vmem, o_vmem):
            pltpu.sync_copy(y_hbm.at[s_vmem.at[0]], o_vmem)

        pltpu.emit_pipeline(
            body, grid=(p // win,),
            in_specs=[pl.BlockSpec((1, win), lambda i: (0, i))],
            out_specs=[pl.BlockSpec((win, c), lambda i: (i, 0))],
            core_axis_name=("core", "subcore"),
            dimension_semantics=(pltpu.PARALLEL,))(s_hbm, o_hbm)

    return gather_rows(ys.reshape(2 * n_slots, c), src).reshape(2, k, t, c)


def _combine_kernel(y_ref, w_ref, x1_ref, h2_ref, mod_ref, wsg_ref, wsu_ref, wsd_ref, *rest):
    o_ref = rest[-1]
    hb = _unpack_halves(h2_ref[0], h2_ref[1])
    g = jnp.dot(hb, wsg_ref[...], preferred_element_type=F32)
    u = jnp.dot(hb, wsu_ref[...], preferred_element_type=F32)
    acc = jnp.dot((_silu(g) * u).astype(BF16), wsd_ref[...], preferred_element_type=F32)
    w = w_ref[...]
    for kk in range(TOP_K):
        acc = acc + w[:, kk:kk + 1] * _unpack_halves(y_ref[0, kk], y_ref[1, kk]).astype(F32)
    g2 = mod_ref[0][5:6]
    o_ref[...] = x1_ref[...] + g2 * acc


def _combine(y_tok, w_tk, x1, h2p, mod, w_sg, w_su, w_sd, seq, chunk, out_prev):
    t, d = x1.shape
    c = d // 4
    tchunk = y_tok.shape[2]
    tc = _tile(min(seq, tchunk), 256)
    tpb = seq // tc
    off = chunk * (tchunk // tc)
    full = lambda shp: pl.BlockSpec(shp, lambda i: (0,) * len(shp))
    tok = lambda w: pl.BlockSpec((tc, w), lambda i: (i + off, 0))
    in_specs = [pl.BlockSpec((2, TOP_K, tc, c), lambda i: (0, 0, i, 0)),
                tok(TOP_K), tok(d),
                pl.BlockSpec((2, tc, c), lambda i: (0, i + off, 0)),
                pl.BlockSpec((1, N_ADA, d), lambda i: ((i + off) // tpb, 0, 0)),
                full(w_sg.shape), full(w_su.shape), full(w_sd.shape)]
    args = [y_tok, w_tk, x1, h2p, mod, w_sg, w_su, w_sd]
    aliases = {}
    if out_prev is not None:
        in_specs.append(pl.BlockSpec(memory_space=pl.ANY))
        args.append(out_prev)
        aliases = {len(args) - 1: 0}
    return pl.pallas_call(
        _combine_kernel,
        grid=(tchunk // tc,),
        in_specs=in_specs,
        out_specs=tok(d),
        out_shape=jax.ShapeDtypeStruct((t, d), F32),
        input_output_aliases=aliases,
        compiler_params=_cparams("arbitrary"),
        name="combine",
    )(*args)


def _layer(x, c, w_ada, b_ada, norm1_g, w_in, q_a_norm_g, w_uq, kv_a_norm_g, w_ukv,
           q_norm_g, k_norm_g, w_proj_attn, w_proj_fourier, w_out, norm2_g,
           w_router, router_bias, w_exp_gate, w_exp_up, w_exp_down,
           w_sh_gate, w_sh_up, w_sh_down):
    bsz, seq, d = x.shape
    t = bsz * seq
    e = w_router.shape[1]
    x2 = x.reshape(t, d)

    mod = _ada(c, w_ada, b_ada).reshape(bsz, N_ADA, d)
    q, k, v, zf, sa, sf = _inproj(x2, mod, norm1_g, w_in, q_a_norm_g, w_uq, kv_a_norm_g,
                                  w_ukv, q_norm_g, k_norm_g, bsz, seq)
    attn = _attention(q, k, v).reshape(t, N_HEADS * V_DIM)
    four = _fourier(zf.reshape(bsz, seq, zf.shape[1])).reshape(t, zf.shape[1])
    x1, h2, scores_t = _merge(attn, four, sa, sf, x2, mod, w_proj_attn, w_proj_fourier,
                              w_out, norm2_g, w_router, seq)

    idx_t, w_t, rank_t, cnt = _route(scores_t, router_bias)
    counts = cnt[:, 0].astype(jnp.int32)
    rows = EXPERT_ROWS
    nblk = -(-(t * TOP_K) // rows) + e
    padded = ((counts + rows - 1) // rows) * rows
    p_end = jnp.cumsum(padded)
    p_start = p_end - padded
    nblk_used = (p_end[-1] // rows).astype(jnp.int32)
    blk_start = jnp.arange(nblk, dtype=jnp.int32) * rows
    blk_first = jnp.minimum(blk_start, p_end[-1] - 1)
    hit = p_end[None, :] <= blk_first[:, None]
    blk_expert = jnp.clip(jnp.sum(hit.astype(jnp.int32), axis=1), 0, e - 1)

    slot_t = _slots(idx_t, rank_t, p_start.astype(jnp.int32))
    xs = _dispatch(h2, slot_t, nblk * rows)
    eid = jnp.arange(e, dtype=jnp.int32)
    used = counts > 0
    run = jnp.cumsum(used.astype(jnp.int32)) - 1
    blk_run = jnp.sum(jnp.where(blk_expert[:, None] == eid[None, :], run[None, :], 0),
                      axis=1).astype(jnp.int32)
    rid = jnp.arange(e + W_RING, dtype=jnp.int32)
    match = jnp.logical_and(used[None, :], run[None, :] == rid[:, None])
    run_expert = jnp.where(jnp.any(match, axis=1),
                           jnp.sum(jnp.where(match, eid[None, :], 0), axis=1), -1).astype(jnp.int32)
    ys = _experts(blk_expert, blk_run, run_expert, nblk_used.reshape(1), xs,
                  w_exp_gate, w_exp_up, w_exp_down)
    nchunk = COMBINE_CHUNKS if t % (COMBINE_CHUNKS * SC_WINDOW) == 0 else 1
    tchunk = t // nchunk
    w_tk = w_t.T
    wsg, wsu, wsd = w_sh_gate.astype(BF16), w_sh_up.astype(BF16), w_sh_down.astype(BF16)
    out = None
    for ch in range(nchunk):
        y_tok = _gather(ys, slot_t[:, ch * tchunk:(ch + 1) * tchunk])
        out = _combine(y_tok, w_tk, x1, h2, mod, wsg, wsu, wsd, seq, ch, out)
    return out.reshape(bsz, seq, d)


def kernel(x, c, w_ada, b_ada, norm1_g, w_in, q_a_norm_g, w_uq, kv_a_norm_g, w_ukv, q_norm_g,
           k_norm_g, w_proj_attn, w_proj_fourier, w_out, norm2_g, w_router, router_bias,
           w_exp_gate, w_exp_up, w_exp_down, w_sh_gate, w_sh_up, w_sh_down):
    for l in range(w_ada.shape[0]):
        x = _layer(x, c, w_ada[l], b_ada[l], norm1_g[l], w_in[l], q_a_norm_g[l], w_uq[l],
                   kv_a_norm_g[l], w_ukv[l], q_norm_g[l], k_norm_g[l], w_proj_attn[l],
                   w_proj_fourier[l], w_out[l], norm2_g[l], w_router[l], router_bias[l],
                   w_exp_gate[l], w_exp_up[l], w_exp_down[l], w_sh_gate[l], w_sh_up[l],
                   w_sh_down[l])
    return x
```

```python
import functools
import math

import numpy as np
import jax
import jax.numpy as jnp
from jax import lax
from jax.experimental import pallas as pl
from jax.experimental.pallas import tpu as pltpu
from jax.experimental.pallas import tpu_sc as plsc

N_HEADS = 8
QK_NOPE = 64
QK_ROPE = 32
V_DIM = 64
FOURIER_GROUP = 64
TOP_K = 8
ROUTED_SCALE = 2.5
EPS = 1e-6
ROPE_THETA = 10000.0
N_ADA = 6

LANES = 128
EXPERT_ROWS = 256
X_RING = 4
W_RING = 3
SC_WINDOW = 128
VMEM_LIMIT = 48 * 1024 * 1024

F32 = jnp.float32
BF16 = jnp.bfloat16


def _cparams(*sem):
    return pltpu.CompilerParams(dimension_semantics=sem, vmem_limit_bytes=VMEM_LIMIT)


def _tile(n, pref):
    t = min(n, pref)
    assert n % t == 0, (n, pref)
    return t


def _silu(v):
    return v * jax.nn.sigmoid(v)


def _pack_halves(m):
    d = m.shape[1]
    lo = lax.bitcast_convert_type(m[:, :d // 2].astype(BF16).astype(F32), jnp.uint32)
    hi = lax.bitcast_convert_type(m[:, d // 2:].astype(BF16).astype(F32), jnp.uint32)
    w = (lo >> 16) | (hi & jnp.uint32(0xFFFF0000))
    return w[:, :d // 4], w[:, d // 4:]


def _unpack_halves(w0, w1):
    def lo(w):
        return lax.bitcast_convert_type(w << 16, F32)

    def hi(w):
        return lax.bitcast_convert_type(w & jnp.uint32(0xFFFF0000), F32)
    return jnp.concatenate([lo(w0), lo(w1), hi(w0), hi(w1)], axis=1).astype(BF16)


def _ada_kernel(c_ref, w_ref, b_ref, o_ref):
    a = _silu(c_ref[...])
    o_ref[...] = jnp.dot(a, w_ref[...], preferred_element_type=F32,
                         precision=lax.Precision.HIGHEST) + b_ref[...]


def _ada(c, w_ada, b_ada):
    bsz, d = c.shape
    n = w_ada.shape[1]
    tn = _tile(n, d)
    return pl.pallas_call(
        _ada_kernel,
        grid=(n // tn,),
        in_specs=[pl.BlockSpec((bsz, d), lambda j: (0, 0)),
                  pl.BlockSpec((d, tn), lambda j: (0, j)),
                  pl.BlockSpec((1, tn), lambda j: (0, j))],
        out_specs=pl.BlockSpec((bsz, tn), lambda j: (0, j)),
        out_shape=jax.ShapeDtypeStruct((bsz, n), F32),
        compiler_params=_cparams("arbitrary"),
        name="ada",
    )(c, w_ada, b_ada.reshape(1, n))


def _head_norm_rope(t, trot, a, b):
    ms = jnp.sum(t * t, axis=-1, keepdims=True) * (1.0 / (QK_NOPE + QK_ROPE))
    return (t * a + trot * b) * lax.rsqrt(ms + EPS)


def _inproj_kernel(x_ref, mod_ref, g1_ref, wa_ref, wf_ref, wga_ref, wgf_ref,
                   gq_ref, gkv_ref, wuq_ref, wkv_ref,
                   aq_ref, bq_ref, ak_ref, bk_ref,
                   q_ref, k_ref, v_ref, zf_ref, sa_ref, sf_ref, *, ql, kvl):
    x = x_ref[...]
    mod = mod_ref[0]
    sh1, sc1 = mod[0:1], mod[1:2]
    r = lax.rsqrt(jnp.mean(x * x, axis=-1, keepdims=True) + EPS)
    h = (x * r * g1_ref[...]) * (1.0 + sc1) + sh1
    hb = h.astype(BF16)

    zf_ref[...] = jnp.dot(hb, wf_ref[...], preferred_element_type=F32).astype(BF16)
    sa_ref[...] = jax.nn.sigmoid(
        jnp.dot(hb, wga_ref[...], preferred_element_type=F32)).astype(BF16)
    sf_ref[...] = jax.nn.sigmoid(
        jnp.dot(hb, wgf_ref[...], preferred_element_type=F32)).astype(BF16)

    za = jnp.dot(hb, wa_ref[...], preferred_element_type=F32)
    zq = za[:, :ql]
    cq = zq * lax.rsqrt(jnp.mean(zq * zq, axis=-1, keepdims=True) + EPS) * gq_ref[...]
    qall = jnp.dot(cq.astype(BF16), wuq_ref[...], preferred_element_type=F32)

    zk = za[:, ql:]
    kvn = zk[:, :kvl]
    rk = lax.rsqrt(jnp.mean(kvn * kvn, axis=-1, keepdims=True) + EPS)
    lane = lax.broadcasted_iota(jnp.int32, zk.shape, 1)
    u = zk * jnp.where(lane < kvl, rk, 1.0) * gkv_ref[...]
    kvall = jnp.dot(u.astype(BF16), wkv_ref[...], preferred_element_type=F32)

    aq, bq, ak, bk = aq_ref[...], bq_ref[...], ak_ref[...], bk_ref[...]
    hw = N_HEADS * LANES
    for hd in range(N_HEADS):
        lo, hi = hd * LANES, (hd + 1) * LANES
        q_ref[0, hd] = _head_norm_rope(qall[:, lo:hi], qall[:, hw + lo:hw + hi],
                                       aq, bq).astype(BF16)
        k_ref[0, hd] = _head_norm_rope(kvall[:, lo:hi], kvall[:, hw + lo:hw + hi],
                                       ak, bk).astype(BF16)
        v_ref[0, hd] = kvall[:, 2 * hw + hd * V_DIM: 2 * hw + (hd + 1) * V_DIM].astype(BF16)


def _rope_tables(seq):
    half = QK_ROPE // 2
    pos = np.arange(seq, dtype=np.float64)
    inv = ROPE_THETA ** (-np.arange(0, QK_ROPE, 2, dtype=np.float64) / QK_ROPE)
    ang = pos[:, None] * inv[None, :]
    c, s = np.cos(ang), np.sin(ang)
    cos = np.ones((seq, LANES)); sin = np.zeros((seq, LANES))
    cos[:, QK_NOPE:QK_NOPE + half] = c
    cos[:, QK_NOPE + half:QK_NOPE + QK_ROPE] = c
    sin[:, QK_NOPE:QK_NOPE + half] = -s
    sin[:, QK_NOPE + half:QK_NOPE + QK_ROPE] = s
    return jnp.asarray(cos, F32), jnp.asarray(sin, F32)


def _partner_columns(w):
    half = QK_ROPE // 2
    lo, mid, hi = QK_NOPE, QK_NOPE + half, QK_NOPE + QK_ROPE
    z = jnp.zeros_like(w)
    return jnp.concatenate([z[..., :lo], w[..., mid:hi], w[..., lo:mid], z[..., hi:]], axis=-1)


def _inproj(x2, mod, norm1_g, w_in, q_a_g, w_uq, kv_a_g, w_ukv, q_g, k_g, bsz, seq):
    t, d = x2.shape
    ql, kvl = q_a_g.shape[0], kv_a_g.shape[0]
    hq = QK_NOPE + QK_ROPE
    fw = w_in.shape[1] - ql - kvl - QK_ROPE - 2 * d
    o1, o2, o3, o4, o5 = ql, ql + kvl, ql + kvl + QK_ROPE, ql + kvl + QK_ROPE + fw, \
        ql + kvl + QK_ROPE + fw + d
    assert ql % LANES == 0 and kvl % LANES == 0

    wa = jnp.concatenate([w_in[:, :o3], jnp.zeros((d, LANES - QK_ROPE), F32)], axis=1).astype(BF16)
    wf = w_in[:, o3:o4].astype(BF16)
    wga = w_in[:, o4:o5].astype(BF16)
    wgf = w_in[:, o5:].astype(BF16)

    wuq = w_uq.reshape(ql, N_HEADS, hq)
    wuq = jnp.pad(wuq, ((0, 0), (0, 0), (0, LANES - hq)))
    wuq = jnp.concatenate([wuq.reshape(ql, N_HEADS * LANES),
                           _partner_columns(wuq).reshape(ql, N_HEADS * LANES)], axis=1).astype(BF16)
    wukv = w_ukv.reshape(kvl, N_HEADS, QK_NOPE + V_DIM)
    wk = jnp.pad(wukv[:, :, :QK_NOPE], ((0, 0), (0, 0), (0, LANES - QK_NOPE)))
    place = jnp.zeros((QK_ROPE, N_HEADS, LANES), F32)
    place = place.at[jnp.arange(QK_ROPE), :, QK_NOPE + jnp.arange(QK_ROPE)].set(1.0)
    wk = jnp.concatenate([wk, place, jnp.zeros((LANES - QK_ROPE, N_HEADS, LANES), F32)], axis=0)
    wv = jnp.concatenate([wukv[:, :, QK_NOPE:], jnp.zeros((LANES, N_HEADS, V_DIM), F32)], axis=0)
    wkv = jnp.concatenate([wk.reshape(kvl + LANES, N_HEADS * LANES),
                           _partner_columns(wk).reshape(kvl + LANES, N_HEADS * LANES),
                           wv.reshape(kvl + LANES, N_HEADS * V_DIM)], axis=1).astype(BF16)

    gkv = jnp.concatenate([kv_a_g, jnp.ones((LANES,), F32)]).reshape(1, kvl + LANES)
    pad = jnp.zeros((LANES - hq,), F32)
    qg = jnp.concatenate([q_g * (hq ** -0.5), pad])
    kg = jnp.concatenate([k_g, pad])
    cos, sin = _rope_tables(seq)
    aq, bq = qg[None, :] * cos, _partner_columns(qg)[None, :] * sin
    ak, bk = kg[None, :] * cos, _partner_columns(kg)[None, :] * sin

    tm = _tile(seq, 512)
    tpb = seq // tm
    full = lambda shp: pl.BlockSpec(shp, lambda i: (0,) * len(shp))
    tok = lambda w: pl.BlockSpec((tm, w), lambda i: (i, 0))
    head = lambda w: pl.BlockSpec((1, N_HEADS, tm, w), lambda i: (i // tpb, 0, i % tpb, 0))
    rope = pl.BlockSpec((tm, LANES), lambda i: (i % tpb, 0))
    return pl.pallas_call(
        functools.partial(_inproj_kernel, ql=ql, kvl=kvl),
        grid=(t // tm,),
        in_specs=[tok(d),
                  pl.BlockSpec((1, N_ADA, d), lambda i: (i // tpb, 0, 0)),
                  full((1, d)), full(wa.shape), full(wf.shape), full(wga.shape), full(wgf.shape),
                  full((1, ql)), full(gkv.shape), full(wuq.shape), full(wkv.shape),
                  rope, rope, rope, rope],
        out_specs=[head(LANES), head(LANES), head(V_DIM), tok(fw), tok(d), tok(d)],
        out_shape=[jax.ShapeDtypeStruct((bsz, N_HEADS, seq, LANES), BF16),
                   jax.ShapeDtypeStruct((bsz, N_HEADS, seq, LANES), BF16),
                   jax.ShapeDtypeStruct((bsz, N_HEADS, seq, V_DIM), BF16),
                   jax.ShapeDtypeStruct((t, fw), BF16),
                   jax.ShapeDtypeStruct((t, d), BF16),
                   jax.ShapeDtypeStruct((t, d), BF16)],
        compiler_params=_cparams("arbitrary"),
        name="inproj",
    )(x2, mod, norm1_g.reshape(1, d), wa, wf, wga, wgf, q_a_g.reshape(1, ql), gkv, wuq, wkv,
      aq, bq, ak, bk)


def _attn_kernel(q_ref, k_ref, v_ref, o_ref):
    for hd in range(N_HEADS):
        s = lax.dot_general(q_ref[0, hd], k_ref[0, hd], (((1,), (1,)), ((), ())),
                            preferred_element_type=F32)
        m = jnp.max(s, axis=-1, keepdims=True)
        p = jnp.exp(s - m)
        l = jnp.sum(p, axis=-1, keepdims=True)
        o = jnp.dot(p.astype(BF16), v_ref[0, hd], preferred_element_type=F32)
        o_ref[0, :, hd * V_DIM:(hd + 1) * V_DIM] = (o / l).astype(BF16)


def _attention(q, k, v):
    bsz, _, seq, _ = q.shape
    tq = _tile(seq, 512)
    return pl.pallas_call(
        _attn_kernel,
        grid=(bsz, seq // tq),
        in_specs=[pl.BlockSpec((1, N_HEADS, tq, LANES), lambda b, j: (b, 0, j, 0)),
                  pl.BlockSpec((1, N_HEADS, seq, LANES), lambda b, j: (b, 0, 0, 0)),
                  pl.BlockSpec((1, N_HEADS, seq, V_DIM), lambda b, j: (b, 0, 0, 0))],
        out_specs=pl.BlockSpec((1, tq, N_HEADS * V_DIM), lambda b, j: (b, j, 0)),
        out_shape=jax.ShapeDtypeStruct((bsz, seq, N_HEADS * V_DIM), BF16),
        compiler_params=_cparams("arbitrary", "arbitrary"),
        name="attn",
    )(q, k, v)


def _fourier_kernel(z_ref, wc_ref, ws_ref, tab_ref, o_ref, u_ref, *, seq):
    @pl.when(pl.program_id(1) == 0)
    def _():
        z = z_ref[0]
        u_ref[:seq, :] = jnp.dot(z, wc_ref[...], preferred_element_type=F32).astype(BF16)
        u_ref[seq:, :] = jnp.dot(z, ws_ref[...], preferred_element_type=F32).astype(BF16)

    o_ref[0] = jnp.dot(tab_ref[...], u_ref[...], preferred_element_type=F32).astype(BF16)


def _fourier_tables(seq, fw):
    g = FOURIER_GROUP
    n = np.arange(seq, dtype=np.int64)
    ang = 2.0 * np.pi * ((n[:, None] * n[None, :]) % seq).astype(np.float64) / seq
    tab = np.concatenate([np.cos(ang), -np.sin(ang)], axis=1)
    c = np.arange(g, dtype=np.int64)
    angc = 2.0 * np.pi * ((c[:, None] * c[None, :]) % g).astype(np.float64) / g
    scale = 1.0 / math.sqrt(seq * g)
    eye = np.eye(fw // g)
    wc = np.kron(eye, np.cos(angc) * scale)
    ws = np.kron(eye, np.sin(angc) * scale)
    return (jnp.asarray(tab, F32).astype(BF16), jnp.asarray(wc, F32).astype(BF16),
            jnp.asarray(ws, F32).astype(BF16))


def _fourier(zf):
    bsz, seq, fw = zf.shape
    tab, wc, ws = _fourier_tables(seq, fw)
    tr = _tile(seq, 512)
    return pl.pallas_call(
        functools.partial(_fourier_kernel, seq=seq),
        grid=(bsz, seq // tr),
        in_specs=[pl.BlockSpec((1, seq, fw), lambda b, j: (b, 0, 0)),
                  pl.BlockSpec((fw, fw), lambda b, j: (0, 0)),
                  pl.BlockSpec((fw, fw), lambda b, j: (0, 0)),
                  pl.BlockSpec((tr, 2 * seq), lambda b, j: (j, 0))],
        out_specs=pl.BlockSpec((1, tr, fw), lambda b, j: (b, j, 0)),
        out_shape=jax.ShapeDtypeStruct((bsz, seq, fw), BF16),
        scratch_shapes=[pltpu.VMEM((2 * seq, fw), BF16)],
        compiler_params=_cparams("arbitrary", "arbitrary"),
        name="fourier",
    )(zf, wc, ws, tab)


def _merge_kernel(a_ref, f_ref, sa_ref, sf_ref, x_ref, mod_ref, wpa_ref, wpf_ref, wo_ref,
                  g2_ref, wrh_ref, wrl_ref, x1_ref, h2_ref, sc_ref):
    ya = jnp.dot(a_ref[...], wpa_ref[...], preferred_element_type=F32)
    yf = jnp.dot(f_ref[...], wpf_ref[...], preferred_element_type=F32)
    merged = sa_ref[...].astype(F32) * ya + sf_ref[...].astype(F32) * yf
    mod = mod_ref[0]
    g1, sh2, sc2 = mod[2:3], mod[3:4], mod[4:5]
    x1 = x_ref[...] + g1 * jnp.dot(merged.astype(BF16), wo_ref[...], preferred_element_type=F32)
    x1_ref[...] = x1
    r = lax.rsqrt(jnp.mean(x1 * x1, axis=-1, keepdims=True) + EPS)
    h2 = (x1 * r * g2_ref[...]) * (1.0 + sc2) + sh2
    h2_ref[0], h2_ref[1] = _pack_halves(h2)
    hh = h2.astype(BF16)
    hl = (h2 - hh.astype(F32)).astype(BF16)
    nt = (((1,), (1,)), ((), ()))
    lt = (lax.dot_general(wrh_ref[...], hh, nt, preferred_element_type=F32)
          + lax.dot_general(wrh_ref[...], hl, nt, preferred_element_type=F32)
          + lax.dot_general(wrl_ref[...], hh, nt, preferred_element_type=F32))
    sc_ref[...] = jax.nn.sigmoid(lt)


def _merge(attn, four, sa, sf, x2, mod, w_pa, w_pf, w_out, norm2_g, w_router, seq):
    t, d = x2.shape
    e = w_router.shape[1]
    wrt = w_router.T
    wrh = wrt.astype(BF16)
    wrl = (wrt - wrh.astype(F32)).astype(BF16)
    tm = _tile(seq, 512)
    tpb = seq // tm
    full = lambda shp: pl.BlockSpec(shp, lambda i: (0,) * len(shp))
    tok = lambda w: pl.BlockSpec((tm, w), lambda i: (i, 0))
    return pl.pallas_call(
        _merge_kernel,
        grid=(t // tm,),
        in_specs=[tok(attn.shape[1]), tok(four.shape[1]), tok(d), tok(d), tok(d),
                  pl.BlockSpec((1, N_ADA, d), lambda i: (i // tpb, 0, 0)),
                  full(w_pa.shape), full(w_pf.shape), full(w_out.shape), full((1, d)),
                  full((e, d)), full((e, d))],
        out_specs=[tok(d), pl.BlockSpec((2, tm, d // 4), lambda i: (0, i, 0)),
                   pl.BlockSpec((e, tm), lambda i: (0, i))],
        out_shape=[jax.ShapeDtypeStruct((t, d), F32),
                   jax.ShapeDtypeStruct((2, t, d // 4), jnp.uint32),
                   jax.ShapeDtypeStruct((e, t), F32)],
        compiler_params=_cparams("arbitrary"),
        name="merge",
    )(attn, four, sa, sf, x2, mod, w_pa.astype(BF16), w_pf.astype(BF16), w_out.astype(BF16),
      norm2_g.reshape(1, d), wrh, wrl)


def _route_kernel(s_ref, b_ref, tri_ref, idx_ref, w_ref, rank_ref, cnt_ref, carry_ref):
    @pl.when(pl.program_id(0) == 0)
    def _():
        carry_ref[...] = jnp.zeros_like(carry_ref)

    sc = s_ref[...]
    e, tr = sc.shape
    row = lax.broadcasted_iota(jnp.int32, (e, tr), 0)
    v = sc + b_ref[...]
    sel = jnp.zeros((e, tr), F32)
    idxs, ws = [], []
    for _ in range(TOP_K):
        m = jnp.max(v, axis=0, keepdims=True)
        idx = jnp.min(jnp.where(v == m, row, e), axis=0, keepdims=True)
        oh = row == idx
        ws.append(jnp.sum(jnp.where(oh, sc, 0.0), axis=0, keepdims=True))
        idxs.append(idx)
        v = jnp.where(oh, -jnp.inf, v)
        sel = sel + oh.astype(F32)
    wsum = ws[0]
    for w in ws[1:]:
        wsum = wsum + w
    selb = sel.astype(BF16)
    cum = jnp.dot(selb, tri_ref[...], preferred_element_type=F32) + carry_ref[...]
    for kk in range(TOP_K):
        oh = row == idxs[kk]
        rk = jnp.sum(jnp.where(oh, cum, 0.0), axis=0, keepdims=True)
        idx_ref[kk:kk + 1, :] = idxs[kk]
        rank_ref[kk:kk + 1, :] = rk.astype(jnp.int32)
        w_ref[kk:kk + 1, :] = ws[kk] / wsum * ROUTED_SCALE
    tot = carry_ref[...] + jnp.dot(selb, jnp.ones((tr, tr), BF16), preferred_element_type=F32)
    carry_ref[...] = tot
    cnt_ref[...] = tot


def _route(scores_t, router_bias):
    e, t = scores_t.shape
    tr = _tile(t, 256)
    tri = jnp.asarray(np.triu(np.ones((tr, tr), np.float32), 1), BF16)
    bias = jnp.broadcast_to(router_bias.reshape(e, 1), (e, tr)).astype(F32)
    blk = pl.BlockSpec((TOP_K, tr), lambda i: (0, i))
    return pl.pallas_call(
        _route_kernel,
        grid=(t // tr,),
        in_specs=[pl.BlockSpec((e, tr), lambda i: (0, i)),
                  pl.BlockSpec((e, tr), lambda i: (0, 0)),
                  pl.BlockSpec((tr, tr), lambda i: (0, 0))],
        out_specs=[blk, blk, blk, pl.BlockSpec((e, tr), lambda i: (0, 0))],
        out_shape=[jax.ShapeDtypeStruct((TOP_K, t), jnp.int32),
                   jax.ShapeDtypeStruct((TOP_K, t), F32),
                   jax.ShapeDtypeStruct((TOP_K, t), jnp.int32),
                   jax.ShapeDtypeStruct((e, tr), F32)],
        scratch_shapes=[pltpu.VMEM((e, tr), F32)],
        compiler_params=_cparams("arbitrary"),
        name="route",
    )(scores_t, bias, tri)


def _slots_kernel(idx_ref, rank_ref, ps_ref, slot_ref):
    ps = ps_ref[...]
    row = lax.broadcasted_iota(jnp.int32, ps.shape, 0)
    for kk in range(TOP_K):
        oh = row == idx_ref[kk:kk + 1, :]
        start = jnp.sum(jnp.where(oh, ps, 0), axis=0, keepdims=True)
        slot_ref[kk:kk + 1, :] = start + rank_ref[kk:kk + 1, :]


def _slots(idx_t, rank_t, p_start):
    _, t = idx_t.shape
    e = p_start.shape[0]
    ts = _tile(t, 512)
    ps = jnp.broadcast_to(p_start.reshape(e, 1), (e, ts))
    blk = pl.BlockSpec((TOP_K, ts), lambda i: (0, i))
    return pl.pallas_call(
        _slots_kernel,
        grid=(t // ts,),
        in_specs=[blk, blk, pl.BlockSpec((e, ts), lambda i: (0, 0))],
        out_specs=blk,
        out_shape=jax.ShapeDtypeStruct((TOP_K, t), jnp.int32),
        compiler_params=_cparams("arbitrary"),
        name="slots",
    )(idx_t, rank_t, ps)


def _dispatch(h2p, slot_t, n_slots):
    _, t, c = h2p.shape
    k = slot_t.shape[0]
    win = _tile(2 * t, SC_WINDOW)
    rows = h2p.reshape(2 * t, c)
    dest = jnp.concatenate([slot_t, slot_t + n_slots], axis=1)
    mesh = plsc.VectorSubcoreMesh(core_axis_name="core", subcore_axis_name="subcore")

    @pl.kernel(out_type=jax.ShapeDtypeStruct((2 * n_slots, c), h2p.dtype), mesh=mesh,
               scratch_types=[])
    def scatter_rows(x_hbm, s_hbm, o_hbm):
        def body(x_vmem, s_vmem):
            pltpu.sync_copy(x_vmem, o_hbm.at[s_vmem.at[0]])

        pltpu.emit_pipeline(
            body, grid=(2 * t // win, k),
            in_specs=[pl.BlockSpec((win, c), lambda i, j: (i, 0)),
                      pl.BlockSpec((1, win), lambda i, j: (j, i))],
            out_specs=[], core_axis_name=("core", "subcore"),
            dimension_semantics=(pltpu.PARALLEL, pltpu.ARBITRARY))(x_hbm, s_hbm)

    return scatter_rows(rows, dest).reshape(2, n_slots, c)


def _expert_kernel(be_ref, run_ref, rex_ref, nu_ref, xs_hbm, wg_hbm, wu_hbm, wd_hbm, ys_hbm,
                   xbuf, ybuf, wgf, wuf, wdf, wgb, wub, wdb, xsem, ysem, wsem, *, nblk):
    nu = nu_ref[0]
    rows = EXPERT_ROWS

    def x_copies(blk, slot):
        r0 = pl.multiple_of(blk * rows, rows)
        return [pltpu.make_async_copy(xs_hbm.at[h, pl.ds(r0, rows)], xbuf.at[slot, h],
                                      xsem.at[slot]) for h in range(2)]

    def y_copies(blk, slot):
        r0 = pl.multiple_of(blk * rows, rows)
        return [pltpu.make_async_copy(ybuf.at[slot, h], ys_hbm.at[h, pl.ds(r0, rows)],
                                      ysem.at[slot]) for h in range(2)]

    def weight_copies(e, which):
        return (pltpu.make_async_copy(wg_hbm.at[e], wgf.at[which], wsem.at[which]),
                pltpu.make_async_copy(wu_hbm.at[e], wuf.at[which], wsem.at[which]),
                pltpu.make_async_copy(wd_hbm.at[e], wdf.at[which], wsem.at[which]))

    for j in range(W_RING - 1):
        @pl.when(rex_ref[j] >= 0)
        def _():
            for cp in weight_copies(rex_ref[j], j):
                cp.start(priority=1)
    for j in range(X_RING - 1):
        @pl.when(j < nu)
        def _():
            for cp in x_copies(j, j):
                cp.start()

    def step(i, carry):
        ahead = i + X_RING - 1

        @pl.when(ahead < nu)
        def _():
            for cp in x_copies(ahead, ahead % X_RING):
                cp.start()

        prev = be_ref[jnp.maximum(i - 1, 0)]

        @pl.when(jnp.logical_or(i == 0, be_ref[i] != prev))
        def _():
            r = run_ref[i]
            par = r % W_RING
            for cp in weight_copies(be_ref[i], par):
                cp.wait()
            wgb[...] = wgf[par].astype(BF16)
            wub[...] = wuf[par].astype(BF16)
            wdb[...] = wdf[par].astype(BF16)
            later = rex_ref[r + W_RING - 1]

            @pl.when(later >= 0)
            def _():
                for cp in weight_copies(later, (r + W_RING - 1) % W_RING):
                    cp.start(priority=1)

        slot = i % X_RING
        for cp in x_copies(i, slot):
            cp.wait()
        x = _unpack_halves(xbuf[slot, 0], xbuf[slot, 1])
        g = jnp.dot(x, wgb[...], preferred_element_type=F32)
        u = jnp.dot(x, wub[...], preferred_element_type=F32)
        a = (_silu(g) * u).astype(BF16)
        y0, y1 = _pack_halves(jnp.dot(a, wdb[...], preferred_element_type=F32))

        out = i % 2

        @pl.when(i >= 2)
        def _():
            for cp in y_copies(i - 2, out):
                cp.wait()
        ybuf[out, 0] = y0
        ybuf[out, 1] = y1
        for cp in y_copies(i, out):
            cp.start()
        return carry

    lax.fori_loop(0, nu, step, 0)

    @pl.when(nu >= 2)
    def _():
        for cp in y_copies(nu - 2, nu % 2):
            cp.wait()
    for cp in y_copies(nu - 1, (nu - 1) % 2):
        cp.wait()

    ybuf[0] = jnp.zeros_like(ybuf[0])

    def zero_start(j, carry):
        for cp in y_copies(j, 0):
            cp.start()
        return carry

    def zero_wait(j, carry):
        for cp in y_copies(j, 0):
            cp.wait()
        return carry
    lax.fori_loop(nu, nblk, zero_start, 0)
    lax.fori_loop(nu, nblk, zero_wait, 0)


def _experts(blk_expert, blk_run, run_expert, nblk_used, xs, w_g, w_u, w_d):
    d, f = w_g.shape[1], w_g.shape[2]
    rows = EXPERT_ROWS
    nblk = blk_expert.shape[0]
    c = xs.shape[2]
    hbm = pl.BlockSpec(memory_space=pl.ANY)
    return pl.pallas_call(
        functools.partial(_expert_kernel, nblk=nblk),
        grid_spec=pltpu.PrefetchScalarGridSpec(
            num_scalar_prefetch=4,
            grid=(1,),
            in_specs=[hbm, hbm, hbm, hbm],
            out_specs=hbm,
            scratch_shapes=[pltpu.VMEM((X_RING, 2, rows, c), jnp.uint32),
                            pltpu.VMEM((2, 2, rows, c), jnp.uint32),
                            pltpu.VMEM((W_RING, d, f), F32), pltpu.VMEM((W_RING, d, f), F32),
                            pltpu.VMEM((W_RING, f, d), F32),
                            pltpu.VMEM((d, f), BF16), pltpu.VMEM((d, f), BF16),
                            pltpu.VMEM((f, d), BF16),
                            pltpu.SemaphoreType.DMA((X_RING,)), pltpu.SemaphoreType.DMA((2,)),
                            pltpu.SemaphoreType.DMA((W_RING,))]),
        out_shape=jax.ShapeDtypeStruct((2, nblk * rows, c), jnp.uint32),
        compiler_params=_cparams("arbitrary"),
        name="experts",
    )(blk_expert, blk_run, run_expert, nblk_used, xs, w_g, w_u, w_d)


def _gather(ys, slot_t):
    _, n_slots, c = ys.shape
    k, t = slot_t.shape
    p = 2 * k * t
    win = _tile(p, SC_WINDOW)
    src = jnp.concatenate([slot_t, slot_t + n_slots], axis=0).reshape(1, p)
    mesh = plsc.VectorSubcoreMesh(core_axis_name="core", subcore_axis_name="subcore")

    @pl.kernel(out_type=jax.ShapeDtypeStruct((p, c), ys.dtype), mesh=mesh, scratch_types=[])
    def gather_rows(y_hbm, s_hbm, o_hbm):
        def body(s_vmem, o_vmem):
            pltpu.sync_copy(y_hbm.at[s_vmem.at[0]], o_vmem)

        pltpu.emit_pipeline(
            body, grid=(p // win,),
            in_specs=[pl.BlockSpec((1, win), lambda i: (0, i))],
            out_specs=[pl.BlockSpec((win, c), lambda i: (i, 0))],
            core_axis_name=("core", "subcore"),
            dimension_semantics=(pltpu.PARALLEL,))(s_hbm, o_hbm)

    return gather_rows(ys.reshape(2 * n_slots, c), src).reshape(2, k, t, c)


def _combine_kernel(y_ref, w_ref, x1_ref, h2_ref, mod_ref, wsg_ref, wsu_ref, wsd_ref, o_ref):
    hb = _unpack_halves(h2_ref[0], h2_ref[1])
    g = jnp.dot(hb, wsg_ref[...], preferred_element_type=F32)
    u = jnp.dot(hb, wsu_ref[...], preferred_element_type=F32)
    acc = jnp.dot((_silu(g) * u).astype(BF16), wsd_ref[...], preferred_element_type=F32)
    w = w_ref[...].T
    for kk in range(TOP_K):
        acc = acc + w[:, kk:kk + 1] * _unpack_halves(y_ref[0, kk], y_ref[1, kk]).astype(F32)
    g2 = mod_ref[0][5:6]
    o_ref[...] = x1_ref[...] + g2 * acc


def _combine(y_tok, w_tk, x1, h2p, mod, w_sg, w_su, w_sd, seq):
    t, d = x1.shape
    c = d // 4
    tc = _tile(seq, 256)
    tpb = seq // tc
    full = lambda shp: pl.BlockSpec(shp, lambda i: (0,) * len(shp))
    tok = lambda w: pl.BlockSpec((tc, w), lambda i: (i, 0))
    return pl.pallas_call(
        _combine_kernel,
        grid=(t // tc,),
        in_specs=[pl.BlockSpec((2, TOP_K, tc, c), lambda i: (0, 0, i, 0)),
                  pl.BlockSpec((TOP_K, tc), lambda i: (0, i)), tok(d),
                  pl.BlockSpec((2, tc, c), lambda i: (0, i, 0)),
                  pl.BlockSpec((1, N_ADA, d), lambda i: (i // tpb, 0, 0)),
                  full(w_sg.shape), full(w_su.shape), full(w_sd.shape)],
        out_specs=tok(d),
        out_shape=jax.ShapeDtypeStruct((t, d), F32),
        compiler_params=_cparams("arbitrary"),
        name="combine",
    )(y_tok, w_tk, x1, h2p, mod, w_sg.astype(BF16), w_su.astype(BF16), w_sd.astype(BF16))


def _layer(x, c, w_ada, b_ada, norm1_g, w_in, q_a_norm_g, w_uq, kv_a_norm_g, w_ukv,
           q_norm_g, k_norm_g, w_proj_attn, w_proj_fourier, w_out, norm2_g,
           w_router, router_bias, w_exp_gate, w_exp_up, w_exp_down,
           w_sh_gate, w_sh_up, w_sh_down):
    bsz, seq, d = x.shape
    t = bsz * seq
    e = w_router.shape[1]
    x2 = x.reshape(t, d)

    mod = _ada(c, w_ada, b_ada).reshape(bsz, N_ADA, d)
    q, k, v, zf, sa, sf = _inproj(x2, mod, norm1_g, w_in, q_a_norm_g, w_uq, kv_a_norm_g,
                                  w_ukv, q_norm_g, k_norm_g, bsz, seq)
    attn = _attention(q, k, v).reshape(t, N_HEADS * V_DIM)
    four = _fourier(zf.reshape(bsz, seq, zf.shape[1])).reshape(t, zf.shape[1])
    x1, h2, scores_t = _merge(attn, four, sa, sf, x2, mod, w_proj_attn, w_proj_fourier,
                              w_out, norm2_g, w_router, seq)

    idx_t, w_t, rank_t, cnt = _route(scores_t, router_bias)
    counts = cnt[:, 0].astype(jnp.int32)
    rows = EXPERT_ROWS
    nblk = -(-(t * TOP_K) // rows) + e
    padded = ((counts + rows - 1) // rows) * rows
    p_end = jnp.cumsum(padded)
    p_start = p_end - padded
    nblk_used = (p_end[-1] // rows).astype(jnp.int32)
    blk_start = jnp.arange(nblk, dtype=jnp.int32) * rows
    blk_first = jnp.minimum(blk_start, p_end[-1] - 1)
    hit = p_end[None, :] <= blk_first[:, None]
    blk_expert = jnp.clip(jnp.sum(hit.astype(jnp.int32), axis=1), 0, e - 1)

    slot_t = _slots(idx_t, rank_t, p_start.astype(jnp.int32))
    xs = _dispatch(h2, slot_t, nblk * rows)
    eid = jnp.arange(e, dtype=jnp.int32)
    used = counts > 0
    run = jnp.cumsum(used.astype(jnp.int32)) - 1
    blk_run = jnp.sum(jnp.where(blk_expert[:, None] == eid[None, :], run[None, :], 0),
                      axis=1).astype(jnp.int32)
    rid = jnp.arange(e + W_RING, dtype=jnp.int32)
    match = jnp.logical_and(used[None, :], run[None, :] == rid[:, None])
    run_expert = jnp.where(jnp.any(match, axis=1),
                           jnp.sum(jnp.where(match, eid[None, :], 0), axis=1), -1).astype(jnp.int32)
    ys = _experts(blk_expert, blk_run, run_expert, nblk_used.reshape(1), xs,
                  w_exp_gate, w_exp_up, w_exp_down)
    out = _combine(_gather(ys, slot_t), w_t, x1, h2, mod, w_sh_gate, w_sh_up, w_sh_down, seq)
    return out.reshape(bsz, seq, d)


def kernel(x, c, w_ada, b_ada, norm1_g, w_in, q_a_norm_g, w_uq, kv_a_norm_g, w_ukv, q_norm_g,
           k_norm_g, w_proj_attn, w_proj_fourier, w_out, norm2_g, w_router, router_bias,
           w_exp_gate, w_exp_up, w_exp_down, w_sh_gate, w_sh_up, w_sh_down):
    for l in range(w_ada.shape[0]):
        x = _layer(x, c, w_ada[l], b_ada[l], norm1_g[l], w_in[l], q_a_norm_g[l], w_uq[l],
                   kv_a_norm_g[l], w_ukv[l], q_norm_g[l], k_norm_g[l], w_proj_attn[l],
                   w_proj_fourier[l], w_out[l], norm2_g[l], w_router[l], router_bias[l],
                   w_exp_gate[l], w_exp_up[l], w_exp_down[l], w_sh_gate[l], w_sh_up[l],
                   w_sh_down[l])
    return x
```

```python
import functools
import math

import numpy as np
import jax
import jax.numpy as jnp
from jax import lax
from jax.experimental import pallas as pl
from jax.experimental.pallas import tpu as pltpu
from jax.experimental.pallas import tpu_sc as plsc

N_HEADS = 8
QK_NOPE = 64
QK_ROPE = 32
V_DIM = 64
FOURIER_GROUP = 64
TOP_K = 8
ROUTED_SCALE = 2.5
EPS = 1e-6
ROPE_THETA = 10000.0
N_ADA = 6

LANES = 128
EXPERT_ROWS = 512
X_RING = 4
W_RING = 3
SC_WINDOW = 128
VMEM_LIMIT = 48 * 1024 * 1024

F32 = jnp.float32
BF16 = jnp.bfloat16


def _cparams(*sem):
    return pltpu.CompilerParams(dimension_semantics=sem, vmem_limit_bytes=VMEM_LIMIT)


def _tile(n, pref):
    t = min(n, pref)
    assert n % t == 0, (n, pref)
    return t


def _silu(v):
    return v * jax.nn.sigmoid(v)


def _pack_halves(m):
    d = m.shape[1]
    lo = lax.bitcast_convert_type(m[:, :d // 2].astype(BF16).astype(F32), jnp.uint32)
    hi = lax.bitcast_convert_type(m[:, d // 2:].astype(BF16).astype(F32), jnp.uint32)
    w = (lo >> 16) | (hi & jnp.uint32(0xFFFF0000))
    return w[:, :d // 4], w[:, d // 4:]


def _unpack_halves(w0, w1):
    def lo(w):
        return lax.bitcast_convert_type(w << 16, F32)

    def hi(w):
        return lax.bitcast_convert_type(w & jnp.uint32(0xFFFF0000), F32)
    return jnp.concatenate([lo(w0), lo(w1), hi(w0), hi(w1)], axis=1).astype(BF16)


def _ada_kernel(c_ref, w_ref, b_ref, o_ref):
    a = _silu(c_ref[...])
    o_ref[...] = jnp.dot(a, w_ref[...], preferred_element_type=F32,
                         precision=lax.Precision.HIGHEST) + b_ref[...]


def _ada(c, w_ada, b_ada):
    bsz, d = c.shape
    n = w_ada.shape[1]
    tn = _tile(n, d)
    return pl.pallas_call(
        _ada_kernel,
        grid=(n // tn,),
        in_specs=[pl.BlockSpec((bsz, d), lambda j: (0, 0)),
                  pl.BlockSpec((d, tn), lambda j: (0, j)),
                  pl.BlockSpec((1, tn), lambda j: (0, j))],
        out_specs=pl.BlockSpec((bsz, tn), lambda j: (0, j)),
        out_shape=jax.ShapeDtypeStruct((bsz, n), F32),
        compiler_params=_cparams("arbitrary"),
        name="ada",
    )(c, w_ada, b_ada.reshape(1, n))


def _head_norm_rope(t, trot, a, b):
    ms = jnp.sum(t * t, axis=-1, keepdims=True) * (1.0 / (QK_NOPE + QK_ROPE))
    return (t * a + trot * b) * lax.rsqrt(ms + EPS)


def _inproj_kernel(x_ref, mod_ref, g1_ref, wa_ref, wf_ref, wga_ref, wgf_ref,
                   gq_ref, gkv_ref, wuq_ref, wkv_ref,
                   aq_ref, bq_ref, ak_ref, bk_ref,
                   q_ref, k_ref, v_ref, zf_ref, sa_ref, sf_ref, *, ql, kvl):
    x = x_ref[...]
    mod = mod_ref[0]
    sh1, sc1 = mod[0:1], mod[1:2]
    r = lax.rsqrt(jnp.mean(x * x, axis=-1, keepdims=True) + EPS)
    h = (x * r * g1_ref[...]) * (1.0 + sc1) + sh1
    hb = h.astype(BF16)

    zf_ref[...] = jnp.dot(hb, wf_ref[...], preferred_element_type=F32).astype(BF16)
    sa_ref[...] = jax.nn.sigmoid(
        jnp.dot(hb, wga_ref[...], preferred_element_type=F32)).astype(BF16)
    sf_ref[...] = jax.nn.sigmoid(
        jnp.dot(hb, wgf_ref[...], preferred_element_type=F32)).astype(BF16)

    za = jnp.dot(hb, wa_ref[...], preferred_element_type=F32)
    zq = za[:, :ql]
    cq = zq * lax.rsqrt(jnp.mean(zq * zq, axis=-1, keepdims=True) + EPS) * gq_ref[...]
    qall = jnp.dot(cq.astype(BF16), wuq_ref[...], preferred_element_type=F32)

    zk = za[:, ql:]
    kvn = zk[:, :kvl]
    rk = lax.rsqrt(jnp.mean(kvn * kvn, axis=-1, keepdims=True) + EPS)
    lane = lax.broadcasted_iota(jnp.int32, zk.shape, 1)
    u = zk * jnp.where(lane < kvl, rk, 1.0) * gkv_ref[...]
    kvall = jnp.dot(u.astype(BF16), wkv_ref[...], preferred_element_type=F32)

    aq, bq, ak, bk = aq_ref[...], bq_ref[...], ak_ref[...], bk_ref[...]
    hw = N_HEADS * LANES
    for hd in range(N_HEADS):
        lo, hi = hd * LANES, (hd + 1) * LANES
        q_ref[0, hd] = _head_norm_rope(qall[:, lo:hi], qall[:, hw + lo:hw + hi],
                                       aq, bq).astype(BF16)
        k_ref[0, hd] = _head_norm_rope(kvall[:, lo:hi], kvall[:, hw + lo:hw + hi],
                                       ak, bk).astype(BF16)
        v_ref[0, hd] = kvall[:, 2 * hw + hd * V_DIM: 2 * hw + (hd + 1) * V_DIM].astype(BF16)


def _rope_tables(seq):
    half = QK_ROPE // 2
    pos = np.arange(seq, dtype=np.float64)
    inv = ROPE_THETA ** (-np.arange(0, QK_ROPE, 2, dtype=np.float64) / QK_ROPE)
    ang = pos[:, None] * inv[None, :]
    c, s = np.cos(ang), np.sin(ang)
    cos = np.ones((seq, LANES)); sin = np.zeros((seq, LANES))
    cos[:, QK_NOPE:QK_NOPE + half] = c
    cos[:, QK_NOPE + half:QK_NOPE + QK_ROPE] = c
    sin[:, QK_NOPE:QK_NOPE + half] = -s
    sin[:, QK_NOPE + half:QK_NOPE + QK_ROPE] = s
    return jnp.asarray(cos, F32), jnp.asarray(sin, F32)


def _partner_columns(w):
    half = QK_ROPE // 2
    lo, mid, hi = QK_NOPE, QK_NOPE + half, QK_NOPE + QK_ROPE
    z = jnp.zeros_like(w)
    return jnp.concatenate([z[..., :lo], w[..., mid:hi], w[..., lo:mid], z[..., hi:]], axis=-1)


def _inproj(x2, mod, norm1_g, w_in, q_a_g, w_uq, kv_a_g, w_ukv, q_g, k_g, bsz, seq):
    t, d = x2.shape
    ql, kvl = q_a_g.shape[0], kv_a_g.shape[0]
    hq = QK_NOPE + QK_ROPE
    fw = w_in.shape[1] - ql - kvl - QK_ROPE - 2 * d
    o1, o2, o3, o4, o5 = ql, ql + kvl, ql + kvl + QK_ROPE, ql + kvl + QK_ROPE + fw, \
        ql + kvl + QK_ROPE + fw + d
    assert ql % LANES == 0 and kvl % LANES == 0

    wa = jnp.concatenate([w_in[:, :o3], jnp.zeros((d, LANES - QK_ROPE), F32)], axis=1).astype(BF16)
    wf = w_in[:, o3:o4].astype(BF16)
    wga = w_in[:, o4:o5].astype(BF16)
    wgf = w_in[:, o5:].astype(BF16)

    wuq = w_uq.reshape(ql, N_HEADS, hq)
    wuq = jnp.pad(wuq, ((0, 0), (0, 0), (0, LANES - hq)))
    wuq = jnp.concatenate([wuq.reshape(ql, N_HEADS * LANES),
                           _partner_columns(wuq).reshape(ql, N_HEADS * LANES)], axis=1).astype(BF16)
    wukv = w_ukv.reshape(kvl, N_HEADS, QK_NOPE + V_DIM)
    wk = jnp.pad(wukv[:, :, :QK_NOPE], ((0, 0), (0, 0), (0, LANES - QK_NOPE)))
    place = jnp.zeros((QK_ROPE, N_HEADS, LANES), F32)
    place = place.at[jnp.arange(QK_ROPE), :, QK_NOPE + jnp.arange(QK_ROPE)].set(1.0)
    wk = jnp.concatenate([wk, place, jnp.zeros((LANES - QK_ROPE, N_HEADS, LANES), F32)], axis=0)
    wv = jnp.concatenate([wukv[:, :, QK_NOPE:], jnp.zeros((LANES, N_HEADS, V_DIM), F32)], axis=0)
    wkv = jnp.concatenate([wk.reshape(kvl + LANES, N_HEADS * LANES),
                           _partner_columns(wk).reshape(kvl + LANES, N_HEADS * LANES),
                           wv.reshape(kvl + LANES, N_HEADS * V_DIM)], axis=1).astype(BF16)

    gkv = jnp.concatenate([kv_a_g, jnp.ones((LANES,), F32)]).reshape(1, kvl + LANES)
    pad = jnp.zeros((LANES - hq,), F32)
    qg = jnp.concatenate([q_g * (hq ** -0.5), pad])
    kg = jnp.concatenate([k_g, pad])
    cos, sin = _rope_tables(seq)
    aq, bq = qg[None, :] * cos, _partner_columns(qg)[None, :] * sin
    ak, bk = kg[None, :] * cos, _partner_columns(kg)[None, :] * sin

    tm = _tile(seq, 512)
    tpb = seq // tm
    full = lambda shp: pl.BlockSpec(shp, lambda i: (0,) * len(shp))
    tok = lambda w: pl.BlockSpec((tm, w), lambda i: (i, 0))
    head = lambda w: pl.BlockSpec((1, N_HEADS, tm, w), lambda i: (i // tpb, 0, i % tpb, 0))
    rope = pl.BlockSpec((tm, LANES), lambda i: (i % tpb, 0))
    return pl.pallas_call(
        functools.partial(_inproj_kernel, ql=ql, kvl=kvl),
        grid=(t // tm,),
        in_specs=[tok(d),
                  pl.BlockSpec((1, N_ADA, d), lambda i: (i // tpb, 0, 0)),
                  full((1, d)), full(wa.shape), full(wf.shape), full(wga.shape), full(wgf.shape),
                  full((1, ql)), full(gkv.shape), full(wuq.shape), full(wkv.shape),
                  rope, rope, rope, rope],
        out_specs=[head(LANES), head(LANES), head(V_DIM), tok(fw), tok(d), tok(d)],
        out_shape=[jax.ShapeDtypeStruct((bsz, N_HEADS, seq, LANES), BF16),
                   jax.ShapeDtypeStruct((bsz, N_HEADS, seq, LANES), BF16),
                   jax.ShapeDtypeStruct((bsz, N_HEADS, seq, V_DIM), BF16),
                   jax.ShapeDtypeStruct((t, fw), BF16),
                   jax.ShapeDtypeStruct((t, d), BF16),
                   jax.ShapeDtypeStruct((t, d), BF16)],
        compiler_params=_cparams("arbitrary"),
        name="inproj",
    )(x2, mod, norm1_g.reshape(1, d), wa, wf, wga, wgf, q_a_g.reshape(1, ql), gkv, wuq, wkv,
      aq, bq, ak, bk)


def _attn_kernel(q_ref, k_ref, v_ref, o_ref):
    for hd in range(N_HEADS):
        s = lax.dot_general(q_ref[0, hd], k_ref[0, hd], (((1,), (1,)), ((), ())),
                            preferred_element_type=F32)
        m = jnp.max(s, axis=-1, keepdims=True)
        p = jnp.exp(s - m)
        l = jnp.sum(p, axis=-1, keepdims=True)
        o = jnp.dot(p.astype(BF16), v_ref[0, hd], preferred_element_type=F32)
        o_ref[0, :, hd * V_DIM:(hd + 1) * V_DIM] = (o / l).astype(BF16)


def _attention(q, k, v):
    bsz, _, seq, _ = q.shape
    tq = _tile(seq, 512)
    return pl.pallas_call(
        _attn_kernel,
        grid=(bsz, seq // tq),
        in_specs=[pl.BlockSpec((1, N_HEADS, tq, LANES), lambda b, j: (b, 0, j, 0)),
                  pl.BlockSpec((1, N_HEADS, seq, LANES), lambda b, j: (b, 0, 0, 0)),
                  pl.BlockSpec((1, N_HEADS, seq, V_DIM), lambda b, j: (b, 0, 0, 0))],
        out_specs=pl.BlockSpec((1, tq, N_HEADS * V_DIM), lambda b, j: (b, j, 0)),
        out_shape=jax.ShapeDtypeStruct((bsz, seq, N_HEADS * V_DIM), BF16),
        compiler_params=_cparams("arbitrary", "arbitrary"),
        name="attn",
    )(q, k, v)


def _fourier_kernel(z_ref, wc_ref, ws_ref, tab_ref, o_ref, u_ref, *, seq):
    @pl.when(pl.program_id(1) == 0)
    def _():
        z = z_ref[0]
        u_ref[:seq, :] = jnp.dot(z, wc_ref[...], preferred_element_type=F32).astype(BF16)
        u_ref[seq:, :] = jnp.dot(z, ws_ref[...], preferred_element_type=F32).astype(BF16)

    o_ref[0] = jnp.dot(tab_ref[...], u_ref[...], preferred_element_type=F32).astype(BF16)


def _fourier_tables(seq, fw):
    g = FOURIER_GROUP
    n = np.arange(seq, dtype=np.int64)
    ang = 2.0 * np.pi * ((n[:, None] * n[None, :]) % seq).astype(np.float64) / seq
    tab = np.concatenate([np.cos(ang), -np.sin(ang)], axis=1)
    c = np.arange(g, dtype=np.int64)
    angc = 2.0 * np.pi * ((c[:, None] * c[None, :]) % g).astype(np.float64) / g
    scale = 1.0 / math.sqrt(seq * g)
    eye = np.eye(fw // g)
    wc = np.kron(eye, np.cos(angc) * scale)
    ws = np.kron(eye, np.sin(angc) * scale)
    return (jnp.asarray(tab, F32).astype(BF16), jnp.asarray(wc, F32).astype(BF16),
            jnp.asarray(ws, F32).astype(BF16))


def _fourier(zf):
    bsz, seq, fw = zf.shape
    tab, wc, ws = _fourier_tables(seq, fw)
    tr = _tile(seq, 512)
    return pl.pallas_call(
        functools.partial(_fourier_kernel, seq=seq),
        grid=(bsz, seq // tr),
        in_specs=[pl.BlockSpec((1, seq, fw), lambda b, j: (b, 0, 0)),
                  pl.BlockSpec((fw, fw), lambda b, j: (0, 0)),
                  pl.BlockSpec((fw, fw), lambda b, j: (0, 0)),
                  pl.BlockSpec((tr, 2 * seq), lambda b, j: (j, 0))],
        out_specs=pl.BlockSpec((1, tr, fw), lambda b, j: (b, j, 0)),
        out_shape=jax.ShapeDtypeStruct((bsz, seq, fw), BF16),
        scratch_shapes=[pltpu.VMEM((2 * seq, fw), BF16)],
        compiler_params=_cparams("arbitrary", "arbitrary"),
        name="fourier",
    )(zf, wc, ws, tab)


def _merge_kernel(a_ref, f_ref, sa_ref, sf_ref, x_ref, mod_ref, wpa_ref, wpf_ref, wo_ref,
                  g2_ref, wrh_ref, wrl_ref, x1_ref, h2_ref, sc_ref):
    ya = jnp.dot(a_ref[...], wpa_ref[...], preferred_element_type=F32)
    yf = jnp.dot(f_ref[...], wpf_ref[...], preferred_element_type=F32)
    merged = sa_ref[...].astype(F32) * ya + sf_ref[...].astype(F32) * yf
    mod = mod_ref[0]
    g1, sh2, sc2 = mod[2:3], mod[3:4], mod[4:5]
    x1 = x_ref[...] + g1 * jnp.dot(merged.astype(BF16), wo_ref[...], preferred_element_type=F32)
    x1_ref[...] = x1
    r = lax.rsqrt(jnp.mean(x1 * x1, axis=-1, keepdims=True) + EPS)
    h2 = (x1 * r * g2_ref[...]) * (1.0 + sc2) + sh2
    h2_ref[0], h2_ref[1] = _pack_halves(h2)
    hh = h2.astype(BF16)
    hl = (h2 - hh.astype(F32)).astype(BF16)
    nt = (((1,), (1,)), ((), ()))
    lt = (lax.dot_general(wrh_ref[...], hh, nt, preferred_element_type=F32)
          + lax.dot_general(wrh_ref[...], hl, nt, preferred_element_type=F32)
          + lax.dot_general(wrl_ref[...], hh, nt, preferred_element_type=F32))
    sc_ref[...] = jax.nn.sigmoid(lt)


def _merge(attn, four, sa, sf, x2, mod, w_pa, w_pf, w_out, norm2_g, w_router, seq):
    t, d = x2.shape
    e = w_router.shape[1]
    wrt = w_router.T
    wrh = wrt.astype(BF16)
    wrl = (wrt - wrh.astype(F32)).astype(BF16)
    tm = _tile(seq, 512)
    tpb = seq // tm
    full = lambda shp: pl.BlockSpec(shp, lambda i: (0,) * len(shp))
    tok = lambda w: pl.BlockSpec((tm, w), lambda i: (i, 0))
    return pl.pallas_call(
        _merge_kernel,
        grid=(t // tm,),
        in_specs=[tok(attn.shape[1]), tok(four.shape[1]), tok(d), tok(d), tok(d),
                  pl.BlockSpec((1, N_ADA, d), lambda i: (i // tpb, 0, 0)),
                  full(w_pa.shape), full(w_pf.shape), full(w_out.shape), full((1, d)),
                  full((e, d)), full((e, d))],
        out_specs=[tok(d), pl.BlockSpec((2, tm, d // 4), lambda i: (0, i, 0)),
                   pl.BlockSpec((e, tm), lambda i: (0, i))],
        out_shape=[jax.ShapeDtypeStruct((t, d), F32),
                   jax.ShapeDtypeStruct((2, t, d // 4), jnp.uint32),
                   jax.ShapeDtypeStruct((e, t), F32)],
        compiler_params=_cparams("arbitrary"),
        name="merge",
    )(attn, four, sa, sf, x2, mod, w_pa.astype(BF16), w_pf.astype(BF16), w_out.astype(BF16),
      norm2_g.reshape(1, d), wrh, wrl)


def _route_kernel(s_ref, b_ref, tri_ref, idx_ref, w_ref, rank_ref, cnt_ref, carry_ref):
    @pl.when(pl.program_id(0) == 0)
    def _():
        carry_ref[...] = jnp.zeros_like(carry_ref)

    sc = s_ref[...]
    e, tr = sc.shape
    row = lax.broadcasted_iota(jnp.int32, (e, tr), 0)
    v = sc + b_ref[...]
    sel = jnp.zeros((e, tr), F32)
    idxs, ws = [], []
    for _ in range(TOP_K):
        m = jnp.max(v, axis=0, keepdims=True)
        idx = jnp.min(jnp.where(v == m, row, e), axis=0, keepdims=True)
        oh = row == idx
        ws.append(jnp.sum(jnp.where(oh, sc, 0.0), axis=0, keepdims=True))
        idxs.append(idx)
        v = jnp.where(oh, -jnp.inf, v)
        sel = sel + oh.astype(F32)
    wsum = ws[0]
    for w in ws[1:]:
        wsum = wsum + w
    selb = sel.astype(BF16)
    cum = jnp.dot(selb, tri_ref[...], preferred_element_type=F32) + carry_ref[...]
    for kk in range(TOP_K):
        oh = row == idxs[kk]
        rk = jnp.sum(jnp.where(oh, cum, 0.0), axis=0, keepdims=True)
        idx_ref[kk:kk + 1, :] = idxs[kk]
        rank_ref[kk:kk + 1, :] = rk.astype(jnp.int32)
        w_ref[kk:kk + 1, :] = ws[kk] / wsum * ROUTED_SCALE
    tot = carry_ref[...] + jnp.dot(selb, jnp.ones((tr, tr), BF16), preferred_element_type=F32)
    carry_ref[...] = tot
    cnt_ref[...] = tot


def _route(scores_t, router_bias):
    e, t = scores_t.shape
    tr = _tile(t, 256)
    tri = jnp.asarray(np.triu(np.ones((tr, tr), np.float32), 1), BF16)
    bias = jnp.broadcast_to(router_bias.reshape(e, 1), (e, tr)).astype(F32)
    blk = pl.BlockSpec((TOP_K, tr), lambda i: (0, i))
    return pl.pallas_call(
        _route_kernel,
        grid=(t // tr,),
        in_specs=[pl.BlockSpec((e, tr), lambda i: (0, i)),
                  pl.BlockSpec((e, tr), lambda i: (0, 0)),
                  pl.BlockSpec((tr, tr), lambda i: (0, 0))],
        out_specs=[blk, blk, blk, pl.BlockSpec((e, tr), lambda i: (0, 0))],
        out_shape=[jax.ShapeDtypeStruct((TOP_K, t), jnp.int32),
                   jax.ShapeDtypeStruct((TOP_K, t), F32),
                   jax.ShapeDtypeStruct((TOP_K, t), jnp.int32),
                   jax.ShapeDtypeStruct((e, tr), F32)],
        scratch_shapes=[pltpu.VMEM((e, tr), F32)],
        compiler_params=_cparams("arbitrary"),
        name="route",
    )(scores_t, bias, tri)


def _slots_kernel(idx_ref, rank_ref, ps_ref, slot_ref):
    ps = ps_ref[...]
    row = lax.broadcasted_iota(jnp.int32, ps.shape, 0)
    for kk in range(TOP_K):
        oh = row == idx_ref[kk:kk + 1, :]
        start = jnp.sum(jnp.where(oh, ps, 0), axis=0, keepdims=True)
        slot_ref[kk:kk + 1, :] = start + rank_ref[kk:kk + 1, :]


def _slots(idx_t, rank_t, p_start):
    _, t = idx_t.shape
    e = p_start.shape[0]
    ts = _tile(t, 512)
    ps = jnp.broadcast_to(p_start.reshape(e, 1), (e, ts))
    blk = pl.BlockSpec((TOP_K, ts), lambda i: (0, i))
    return pl.pallas_call(
        _slots_kernel,
        grid=(t // ts,),
        in_specs=[blk, blk, pl.BlockSpec((e, ts), lambda i: (0, 0))],
        out_specs=blk,
        out_shape=jax.ShapeDtypeStruct((TOP_K, t), jnp.int32),
        compiler_params=_cparams("arbitrary"),
        name="slots",
    )(idx_t, rank_t, ps)


def _dispatch(h2p, slot_t, n_slots):
    _, t, c = h2p.shape
    k = slot_t.shape[0]
    win = _tile(2 * t, SC_WINDOW)
    rows = h2p.reshape(2 * t, c)
    dest = jnp.concatenate([slot_t, slot_t + n_slots], axis=1)
    mesh = plsc.VectorSubcoreMesh(core_axis_name="core", subcore_axis_name="subcore")

    @pl.kernel(out_type=jax.ShapeDtypeStruct((2 * n_slots, c), h2p.dtype), mesh=mesh,
               scratch_types=[])
    def scatter_rows(x_hbm, s_hbm, o_hbm):
        def body(x_vmem, s_vmem):
            pltpu.sync_copy(x_vmem, o_hbm.at[s_vmem.at[0]])

        pltpu.emit_pipeline(
            body, grid=(2 * t // win, k),
            in_specs=[pl.BlockSpec((win, c), lambda i, j: (i, 0)),
                      pl.BlockSpec((1, win), lambda i, j: (j, i))],
            out_specs=[], core_axis_name=("core", "subcore"),
            dimension_semantics=(pltpu.PARALLEL, pltpu.ARBITRARY))(x_hbm, s_hbm)

    return scatter_rows(rows, dest).reshape(2, n_slots, c)


def _expert_kernel(be_ref, run_ref, rex_ref, nu_ref, xs_hbm, wg_hbm, wu_hbm, wd_hbm, ys_hbm,
                   xbuf, ybuf, wgf, wuf, wdf, wgb, wub, wdb, xsem, ysem, wsem, *, nblk):
    nu = nu_ref[0]
    rows = EXPERT_ROWS

    def x_copies(blk, slot):
        r0 = pl.multiple_of(blk * rows, rows)
        return [pltpu.make_async_copy(xs_hbm.at[h, pl.ds(r0, rows)], xbuf.at[slot, h],
                                      xsem.at[slot]) for h in range(2)]

    def y_copies(blk, slot):
        r0 = pl.multiple_of(blk * rows, rows)
        return [pltpu.make_async_copy(ybuf.at[slot, h], ys_hbm.at[h, pl.ds(r0, rows)],
                                      ysem.at[slot]) for h in range(2)]

    def weight_copies(e, which):
        return (pltpu.make_async_copy(wg_hbm.at[e], wgf.at[which], wsem.at[which]),
                pltpu.make_async_copy(wu_hbm.at[e], wuf.at[which], wsem.at[which]),
                pltpu.make_async_copy(wd_hbm.at[e], wdf.at[which], wsem.at[which]))

    for j in range(W_RING - 1):
        @pl.when(rex_ref[j] >= 0)
        def _():
            for cp in weight_copies(rex_ref[j], j):
                cp.start(priority=1)
    for j in range(X_RING - 1):
        @pl.when(j < nu)
        def _():
            for cp in x_copies(j, j):
                cp.start()

    def step(i, carry):
        ahead = i + X_RING - 1

        @pl.when(ahead < nu)
        def _():
            for cp in x_copies(ahead, ahead % X_RING):
                cp.start()

        prev = be_ref[jnp.maximum(i - 1, 0)]

        @pl.when(jnp.logical_or(i == 0, be_ref[i] != prev))
        def _():
            r = run_ref[i]
            par = r % W_RING
            for cp in weight_copies(be_ref[i], par):
                cp.wait()
            wgb[...] = wgf[par].astype(BF16)
            wub[...] = wuf[par].astype(BF16)
            wdb[...] = wdf[par].astype(BF16)
            later = rex_ref[r + W_RING - 1]

            @pl.when(later >= 0)
            def _():
                for cp in weight_copies(later, (r + W_RING - 1) % W_RING):
                    cp.start(priority=1)

        slot = i % X_RING
        for cp in x_copies(i, slot):
            cp.wait()
        x = _unpack_halves(xbuf[slot, 0], xbuf[slot, 1])
        g = jnp.dot(x, wgb[...], preferred_element_type=F32)
        u = jnp.dot(x, wub[...], preferred_element_type=F32)
        a = (_silu(g) * u).astype(BF16)
        y0, y1 = _pack_halves(jnp.dot(a, wdb[...], preferred_element_type=F32))

        out = i % 2

        @pl.when(i >= 2)
        def _():
            for cp in y_copies(i - 2, out):
                cp.wait()
        ybuf[out, 0] = y0
        ybuf[out, 1] = y1
        for cp in y_copies(i, out):
            cp.start()
        return carry

    lax.fori_loop(0, nu, step, 0)

    @pl.when(nu >= 2)
    def _():
        for cp in y_copies(nu - 2, nu % 2):
            cp.wait()
    for cp in y_copies(nu - 1, (nu - 1) % 2):
        cp.wait()

    ybuf[0] = jnp.zeros_like(ybuf[0])

    def zero_start(j, carry):
        for cp in y_copies(j, 0):
            cp.start()
        return carry

    def zero_wait(j, carry):
        for cp in y_copies(j, 0):
            cp.wait()
        return carry
    lax.fori_loop(nu, nblk, zero_start, 0)
    lax.fori_loop(nu, nblk, zero_wait, 0)


def _experts(blk_expert, blk_run, run_expert, nblk_used, xs, w_g, w_u, w_d):
    d, f = w_g.shape[1], w_g.shape[2]
    rows = EXPERT_ROWS
    nblk = blk_expert.shape[0]
    c = xs.shape[2]
    hbm = pl.BlockSpec(memory_space=pl.ANY)
    return pl.pallas_call(
        functools.partial(_expert_kernel, nblk=nblk),
        grid_spec=pltpu.PrefetchScalarGridSpec(
            num_scalar_prefetch=4,
            grid=(1,),
            in_specs=[hbm, hbm, hbm, hbm],
            out_specs=hbm,
            scratch_shapes=[pltpu.VMEM((X_RING, 2, rows, c), jnp.uint32),
                            pltpu.VMEM((2, 2, rows, c), jnp.uint32),
                            pltpu.VMEM((W_RING, d, f), F32), pltpu.VMEM((W_RING, d, f), F32),
                            pltpu.VMEM((W_RING, f, d), F32),
                            pltpu.VMEM((d, f), BF16), pltpu.VMEM((d, f), BF16),
                            pltpu.VMEM((f, d), BF16),
                            pltpu.SemaphoreType.DMA((X_RING,)), pltpu.SemaphoreType.DMA((2,)),
                            pltpu.SemaphoreType.DMA((W_RING,))]),
        out_shape=jax.ShapeDtypeStruct((2, nblk * rows, c), jnp.uint32),
        compiler_params=_cparams("arbitrary"),
        name="experts",
    )(blk_expert, blk_run, run_expert, nblk_used, xs, w_g, w_u, w_d)


def _gather(ys, slot_t):
    _, n_slots, c = ys.shape
    k, t = slot_t.shape
    p = 2 * k * t
    win = _tile(p, SC_WINDOW)
    src = jnp.concatenate([slot_t, slot_t + n_slots], axis=0).reshape(1, p)
    mesh = plsc.VectorSubcoreMesh(core_axis_name="core", subcore_axis_name="subcore")

    @pl.kernel(out_type=jax.ShapeDtypeStruct((p, c), ys.dtype), mesh=mesh, scratch_types=[])
    def gather_rows(y_hbm, s_hbm, o_hbm):
        def body(s_vmem, o_vmem):
            pltpu.sync_copy(y_hbm.at[s_vmem.at[0]], o_vmem)

        pltpu.emit_pipeline(
            body, grid=(p // win,),
            in_specs=[pl.BlockSpec((1, win), lambda i: (0, i))],
            out_specs=[pl.BlockSpec((win, c), lambda i: (i, 0))],
            core_axis_name=("core", "subcore"),
            dimension_semantics=(pltpu.PARALLEL,))(s_hbm, o_hbm)

    return gather_rows(ys.reshape(2 * n_slots, c), src).reshape(2, k, t, c)


def _combine_kernel(y_ref, w_ref, x1_ref, h2_ref, mod_ref, wsg_ref, wsu_ref, wsd_ref, o_ref):
    hb = _unpack_halves(h2_ref[0], h2_ref[1])
    g = jnp.dot(hb, wsg_ref[...], preferred_element_type=F32)
    u = jnp.dot(hb, wsu_ref[...], preferred_element_type=F32)
    acc = jnp.dot((_silu(g) * u).astype(BF16), wsd_ref[...], preferred_element_type=F32)
    w = w_ref[...].T
    for kk in range(TOP_K):
        acc = acc + w[:, kk:kk + 1] * _unpack_halves(y_ref[0, kk], y_ref[1, kk]).astype(F32)
    g2 = mod_ref[0][5:6]
    o_ref[...] = x1_ref[...] + g2 * acc


def _combine(y_tok, w_tk, x1, h2p, mod, w_sg, w_su, w_sd, seq):
    t, d = x1.shape
    c = d // 4
    tc = _tile(seq, 256)
    tpb = seq // tc
    full = lambda shp: pl.BlockSpec(shp, lambda i: (0,) * len(shp))
    tok = lambda w: pl.BlockSpec((tc, w), lambda i: (i, 0))
    return pl.pallas_call(
        _combine_kernel,
        grid=(t // tc,),
        in_specs=[pl.BlockSpec((2, TOP_K, tc, c), lambda i: (0, 0, i, 0)),
                  pl.BlockSpec((TOP_K, tc), lambda i: (0, i)), tok(d),
                  pl.BlockSpec((2, tc, c), lambda i: (0, i, 0)),
                  pl.BlockSpec((1, N_ADA, d), lambda i: (i // tpb, 0, 0)),
                  full(w_sg.shape), full(w_su.shape), full(w_sd.shape)],
        out_specs=tok(d),
        out_shape=jax.ShapeDtypeStruct((t, d), F32),
        compiler_params=_cparams("arbitrary"),
        name="combine",
    )(y_tok, w_tk, x1, h2p, mod, w_sg.astype(BF16), w_su.astype(BF16), w_sd.astype(BF16))


def _layer(x, c, w_ada, b_ada, norm1_g, w_in, q_a_norm_g, w_uq, kv_a_norm_g, w_ukv,
           q_norm_g, k_norm_g, w_proj_attn, w_proj_fourier, w_out, norm2_g,
           w_router, router_bias, w_exp_gate, w_exp_up, w_exp_down,
           w_sh_gate, w_sh_up, w_sh_down):
    bsz, seq, d = x.shape
    t = bsz * seq
    e = w_router.shape[1]
    x2 = x.reshape(t, d)

    mod = _ada(c, w_ada, b_ada).reshape(bsz, N_ADA, d)
    q, k, v, zf, sa, sf = _inproj(x2, mod, norm1_g, w_in, q_a_norm_g, w_uq, kv_a_norm_g,
                                  w_ukv, q_norm_g, k_norm_g, bsz, seq)
    attn = _attention(q, k, v).reshape(t, N_HEADS * V_DIM)
    four = _fourier(zf.reshape(bsz, seq, zf.shape[1])).reshape(t, zf.shape[1])
    x1, h2, scores_t = _merge(attn, four, sa, sf, x2, mod, w_proj_attn, w_proj_fourier,
                              w_out, norm2_g, w_router, seq)

    idx_t, w_t, rank_t, cnt = _route(scores_t, router_bias)
    counts = cnt[:, 0].astype(jnp.int32)
    rows = EXPERT_ROWS
    nblk = -(-(t * TOP_K) // rows) + e
    padded = ((counts + rows - 1) // rows) * rows
    p_end = jnp.cumsum(padded)
    p_start = p_end - padded
    nblk_used = (p_end[-1] // rows).astype(jnp.int32)
    blk_start = jnp.arange(nblk, dtype=jnp.int32) * rows
    blk_first = jnp.minimum(blk_start, p_end[-1] - 1)
    hit = p_end[None, :] <= blk_first[:, None]
    blk_expert = jnp.clip(jnp.sum(hit.astype(jnp.int32), axis=1), 0, e - 1)

    slot_t = _slots(idx_t, rank_t, p_start.astype(jnp.int32))
    xs = _dispatch(h2, slot_t, nblk * rows)
    eid = jnp.arange(e, dtype=jnp.int32)
    used = counts > 0
    run = jnp.cumsum(used.astype(jnp.int32)) - 1
    blk_run = jnp.sum(jnp.where(blk_expert[:, None] == eid[None, :], run[None, :], 0),
                      axis=1).astype(jnp.int32)
    rid = jnp.arange(e + W_RING, dtype=jnp.int32)
    match = jnp.logical_and(used[None, :], run[None, :] == rid[:, None])
    run_expert = jnp.where(jnp.any(match, axis=1),
                           jnp.sum(jnp.where(match, eid[None, :], 0), axis=1), -1).astype(jnp.int32)
    ys = _experts(blk_expert, blk_run, run_expert, nblk_used.reshape(1), xs,
                  w_exp_gate, w_exp_up, w_exp_down)
    out = _combine(_gather(ys, slot_t), w_t, x1, h2, mod, w_sh_gate, w_sh_up, w_sh_down, seq)
    return out.reshape(bsz, seq, d)


def kernel(x, c, w_ada, b_ada, norm1_g, w_in, q_a_norm_g, w_uq, kv_a_norm_g, w_ukv, q_norm_g,
           k_norm_g, w_proj_attn, w_proj_fourier, w_out, norm2_g, w_router, router_bias,
           w_exp_gate, w_exp_up, w_exp_down, w_sh_gate, w_sh_up, w_sh_down):
    for l in range(w_ada.shape[0]):
        x = _layer(x, c, w_ada[l], b_ada[l], norm1_g[l], w_in[l], q_a_norm_g[l], w_uq[l],
                   kv_a_norm_g[l], w_ukv[l], q_norm_g[l], k_norm_g[l], w_proj_attn[l],
                   w_proj_fourier[l], w_out[l], norm2_g[l], w_router[l], router_bias[l],
                   w_exp_gate[l], w_exp_up[l], w_exp_down[l], w_sh_gate[l], w_sh_up[l],
                   w_sh_down[l])
    return x
```

```python
import functools
import math

import numpy as np
import jax
import jax.numpy as jnp
from jax import lax
from jax.experimental import pallas as pl
from jax.experimental.pallas import tpu as pltpu
from jax.experimental.pallas import tpu_sc as plsc

N_HEADS = 8
QK_NOPE = 64
QK_ROPE = 32
V_DIM = 64
FOURIER_GROUP = 64
TOP_K = 8
ROUTED_SCALE = 2.5
EPS = 1e-6
ROPE_THETA = 10000.0
N_ADA = 6

LANES = 128
EXPERT_ROWS = 512
X_RING = 4
W_RING = 3
SC_WINDOW = 128
VMEM_LIMIT = 48 * 1024 * 1024

F32 = jnp.float32
BF16 = jnp.bfloat16


def _cparams(*sem):
    return pltpu.CompilerParams(dimension_semantics=sem, vmem_limit_bytes=VMEM_LIMIT)


def _tile(n, pref):
    t = min(n, pref)
    assert n % t == 0, (n, pref)
    return t


def _silu(v):
    return v * jax.nn.sigmoid(v)


def _pack_halves(m):
    d = m.shape[1]
    lo = lax.bitcast_convert_type(m[:, :d // 2].astype(BF16).astype(F32), jnp.uint32)
    hi = lax.bitcast_convert_type(m[:, d // 2:].astype(BF16).astype(F32), jnp.uint32)
    w = (lo >> 16) | (hi & jnp.uint32(0xFFFF0000))
    return w[:, :d // 4], w[:, d // 4:]


def _unpack_halves(w0, w1):
    def lo(w):
        return lax.bitcast_convert_type(w << 16, F32)

    def hi(w):
        return lax.bitcast_convert_type(w & jnp.uint32(0xFFFF0000), F32)
    return jnp.concatenate([lo(w0), lo(w1), hi(w0), hi(w1)], axis=1).astype(BF16)


def _ada_kernel(c_ref, w_ref, b_ref, o_ref):
    a = _silu(c_ref[...])
    o_ref[...] = jnp.dot(a, w_ref[...], preferred_element_type=F32,
                         precision=lax.Precision.HIGHEST) + b_ref[...]


def _ada(c, w_ada, b_ada):
    bsz, d = c.shape
    n = w_ada.shape[1]
    tn = _tile(n, d)
    return pl.pallas_call(
        _ada_kernel,
        grid=(n // tn,),
        in_specs=[pl.BlockSpec((bsz, d), lambda j: (0, 0)),
                  pl.BlockSpec((d, tn), lambda j: (0, j)),
                  pl.BlockSpec((1, tn), lambda j: (0, j))],
        out_specs=pl.BlockSpec((bsz, tn), lambda j: (0, j)),
        out_shape=jax.ShapeDtypeStruct((bsz, n), F32),
        compiler_params=_cparams("arbitrary"),
        name="ada",
    )(c, w_ada, b_ada.reshape(1, n))


def _head_norm_rope(t, trot, a, b):
    ms = jnp.sum(t * t, axis=-1, keepdims=True) * (1.0 / (QK_NOPE + QK_ROPE))
    return (t * a + trot * b) * lax.rsqrt(ms + EPS)


def _inproj_kernel(x_ref, mod_ref, g1_ref, wa_ref, wf_ref, wga_ref, wgf_ref,
                   gq_ref, gkv_ref, wuq_ref, wkv_ref,
                   aq_ref, bq_ref, ak_ref, bk_ref,
                   q_ref, k_ref, v_ref, zf_ref, sa_ref, sf_ref, *, ql, kvl):
    x = x_ref[...]
    mod = mod_ref[0]
    sh1, sc1 = mod[0:1], mod[1:2]
    r = lax.rsqrt(jnp.mean(x * x, axis=-1, keepdims=True) + EPS)
    h = (x * r * g1_ref[...]) * (1.0 + sc1) + sh1
    hb = h.astype(BF16)

    zf_ref[...] = jnp.dot(hb, wf_ref[...], preferred_element_type=F32).astype(BF16)
    sa_ref[...] = jax.nn.sigmoid(
        jnp.dot(hb, wga_ref[...], preferred_element_type=F32)).astype(BF16)
    sf_ref[...] = jax.nn.sigmoid(
        jnp.dot(hb, wgf_ref[...], preferred_element_type=F32)).astype(BF16)

    za = jnp.dot(hb, wa_ref[...], preferred_element_type=F32)
    zq = za[:, :ql]
    cq = zq * lax.rsqrt(jnp.mean(zq * zq, axis=-1, keepdims=True) + EPS) * gq_ref[...]
    qall = jnp.dot(cq.astype(BF16), wuq_ref[...], preferred_element_type=F32)

    zk = za[:, ql:]
    kvn = zk[:, :kvl]
    rk = lax.rsqrt(jnp.mean(kvn * kvn, axis=-1, keepdims=True) + EPS)
    lane = lax.broadcasted_iota(jnp.int32, zk.shape, 1)
    u = zk * jnp.where(lane < kvl, rk, 1.0) * gkv_ref[...]
    kvall = jnp.dot(u.astype(BF16), wkv_ref[...], preferred_element_type=F32)

    aq, bq, ak, bk = aq_ref[...], bq_ref[...], ak_ref[...], bk_ref[...]
    hw = N_HEADS * LANES
    for hd in range(N_HEADS):
        lo, hi = hd * LANES, (hd + 1) * LANES
        q_ref[:, lo:hi] = _head_norm_rope(qall[:, lo:hi], qall[:, hw + lo:hw + hi],
                                       aq, bq).astype(BF16)
        k_ref[:, lo:hi] = _head_norm_rope(kvall[:, lo:hi], kvall[:, hw + lo:hw + hi],
                                       ak, bk).astype(BF16)
    v_ref[...] = kvall[:, 2 * hw:].astype(BF16)


def _rope_tables(seq):
    half = QK_ROPE // 2
    pos = np.arange(seq, dtype=np.float64)
    inv = ROPE_THETA ** (-np.arange(0, QK_ROPE, 2, dtype=np.float64) / QK_ROPE)
    ang = pos[:, None] * inv[None, :]
    c, s = np.cos(ang), np.sin(ang)
    cos = np.ones((seq, LANES)); sin = np.zeros((seq, LANES))
    cos[:, QK_NOPE:QK_NOPE + half] = c
    cos[:, QK_NOPE + half:QK_NOPE + QK_ROPE] = c
    sin[:, QK_NOPE:QK_NOPE + half] = -s
    sin[:, QK_NOPE + half:QK_NOPE + QK_ROPE] = s
    return jnp.asarray(cos, F32), jnp.asarray(sin, F32)


def _partner_columns(w):
    half = QK_ROPE // 2
    lo, mid, hi = QK_NOPE, QK_NOPE + half, QK_NOPE + QK_ROPE
    z = jnp.zeros_like(w)
    return jnp.concatenate([z[..., :lo], w[..., mid:hi], w[..., lo:mid], z[..., hi:]], axis=-1)


def _inproj(x2, mod, norm1_g, w_in, q_a_g, w_uq, kv_a_g, w_ukv, q_g, k_g, bsz, seq):
    t, d = x2.shape
    ql, kvl = q_a_g.shape[0], kv_a_g.shape[0]
    hq = QK_NOPE + QK_ROPE
    fw = w_in.shape[1] - ql - kvl - QK_ROPE - 2 * d
    o1, o2, o3, o4, o5 = ql, ql + kvl, ql + kvl + QK_ROPE, ql + kvl + QK_ROPE + fw, \
        ql + kvl + QK_ROPE + fw + d
    assert ql % LANES == 0 and kvl % LANES == 0

    wa = jnp.concatenate([w_in[:, :o3], jnp.zeros((d, LANES - QK_ROPE), F32)], axis=1).astype(BF16)
    wf = w_in[:, o3:o4].astype(BF16)
    wga = w_in[:, o4:o5].astype(BF16)
    wgf = w_in[:, o5:].astype(BF16)

    wuq = w_uq.reshape(ql, N_HEADS, hq)
    wuq = jnp.pad(wuq, ((0, 0), (0, 0), (0, LANES - hq)))
    wuq = jnp.concatenate([wuq.reshape(ql, N_HEADS * LANES),
                           _partner_columns(wuq).reshape(ql, N_HEADS * LANES)], axis=1).astype(BF16)
    wukv = w_ukv.reshape(kvl, N_HEADS, QK_NOPE + V_DIM)
    wk = jnp.pad(wukv[:, :, :QK_NOPE], ((0, 0), (0, 0), (0, LANES - QK_NOPE)))
    place = jnp.zeros((QK_ROPE, N_HEADS, LANES), F32)
    place = place.at[jnp.arange(QK_ROPE), :, QK_NOPE + jnp.arange(QK_ROPE)].set(1.0)
    wk = jnp.concatenate([wk, place, jnp.zeros((LANES - QK_ROPE, N_HEADS, LANES), F32)], axis=0)
    wv = jnp.concatenate([wukv[:, :, QK_NOPE:], jnp.zeros((LANES, N_HEADS, V_DIM), F32)], axis=0)
    wkv = jnp.concatenate([wk.reshape(kvl + LANES, N_HEADS * LANES),
                           _partner_columns(wk).reshape(kvl + LANES, N_HEADS * LANES),
                           wv.reshape(kvl + LANES, N_HEADS * V_DIM)], axis=1).astype(BF16)

    gkv = jnp.concatenate([kv_a_g, jnp.ones((LANES,), F32)]).reshape(1, kvl + LANES)
    pad = jnp.zeros((LANES - hq,), F32)
    qg = jnp.concatenate([q_g * (hq ** -0.5), pad])
    kg = jnp.concatenate([k_g, pad])
    cos, sin = _rope_tables(seq)
    aq, bq = qg[None, :] * cos, _partner_columns(qg)[None, :] * sin
    ak, bk = kg[None, :] * cos, _partner_columns(kg)[None, :] * sin

    tm = _tile(seq, 512)
    tpb = seq // tm
    full = lambda shp: pl.BlockSpec(shp, lambda i: (0,) * len(shp))
    tok = lambda w: pl.BlockSpec((tm, w), lambda i: (i, 0))
    rope = pl.BlockSpec((tm, LANES), lambda i: (i % tpb, 0))
    return pl.pallas_call(
        functools.partial(_inproj_kernel, ql=ql, kvl=kvl),
        grid=(t // tm,),
        in_specs=[tok(d),
                  pl.BlockSpec((1, N_ADA, d), lambda i: (i // tpb, 0, 0)),
                  full((1, d)), full(wa.shape), full(wf.shape), full(wga.shape), full(wgf.shape),
                  full((1, ql)), full(gkv.shape), full(wuq.shape), full(wkv.shape),
                  rope, rope, rope, rope],
        out_specs=[tok(N_HEADS * LANES), tok(N_HEADS * LANES), tok(N_HEADS * V_DIM), tok(fw), tok(d), tok(d)],
        out_shape=[jax.ShapeDtypeStruct((t, N_HEADS * LANES), BF16),
                   jax.ShapeDtypeStruct((t, N_HEADS * LANES), BF16),
                   jax.ShapeDtypeStruct((t, N_HEADS * V_DIM), BF16),
                   jax.ShapeDtypeStruct((t, fw), BF16),
                   jax.ShapeDtypeStruct((t, d), BF16),
                   jax.ShapeDtypeStruct((t, d), BF16)],
        compiler_params=_cparams("arbitrary"),
        name="inproj",
    )(x2, mod, norm1_g.reshape(1, d), wa, wf, wga, wgf, q_a_g.reshape(1, ql), gkv, wuq, wkv,
      aq, bq, ak, bk)


def _attn_kernel(q_ref, k_ref, v_ref, o_ref):
    for hd in range(N_HEADS):
        s = lax.dot_general(q_ref[0, :, hd * LANES:(hd + 1) * LANES],
                            k_ref[0, :, hd * LANES:(hd + 1) * LANES], (((1,), (1,)), ((), ())),
                            preferred_element_type=F32)
        m = jnp.max(s, axis=-1, keepdims=True)
        p = jnp.exp(s - m)
        l = jnp.sum(p, axis=-1, keepdims=True)
        o = jnp.dot(p.astype(BF16), v_ref[0, :, hd * V_DIM:(hd + 1) * V_DIM],
                    preferred_element_type=F32)
        o_ref[0, :, hd * V_DIM:(hd + 1) * V_DIM] = (o / l).astype(BF16)


def _attention(q, k, v):
    bsz, seq, _ = q.shape
    tq = _tile(seq, 512)
    return pl.pallas_call(
        _attn_kernel,
        grid=(bsz, seq // tq),
        in_specs=[pl.BlockSpec((1, tq, N_HEADS * LANES), lambda b, j: (b, j, 0)),
                  pl.BlockSpec((1, seq, N_HEADS * LANES), lambda b, j: (b, 0, 0)),
                  pl.BlockSpec((1, seq, N_HEADS * V_DIM), lambda b, j: (b, 0, 0))],
        out_specs=pl.BlockSpec((1, tq, N_HEADS * V_DIM), lambda b, j: (b, j, 0)),
        out_shape=jax.ShapeDtypeStruct((bsz, seq, N_HEADS * V_DIM), BF16),
        compiler_params=_cparams("arbitrary", "arbitrary"),
        name="attn",
    )(q, k, v)


def _fourier_kernel(z_ref, wc_ref, ws_ref, tab_ref, o_ref, u_ref, *, seq):
    @pl.when(pl.program_id(1) == 0)
    def _():
        z = z_ref[0]
        u_ref[:seq, :] = jnp.dot(z, wc_ref[...], preferred_element_type=F32).astype(BF16)
        u_ref[seq:, :] = jnp.dot(z, ws_ref[...], preferred_element_type=F32).astype(BF16)

    o_ref[0] = jnp.dot(tab_ref[...], u_ref[...], preferred_element_type=F32).astype(BF16)


def _fourier_tables(seq, fw):
    g = FOURIER_GROUP
    n = np.arange(seq, dtype=np.int64)
    ang = 2.0 * np.pi * ((n[:, None] * n[None, :]) % seq).astype(np.float64) / seq
    tab = np.concatenate([np.cos(ang), -np.sin(ang)], axis=1)
    c = np.arange(g, dtype=np.int64)
    angc = 2.0 * np.pi * ((c[:, None] * c[None, :]) % g).astype(np.float64) / g
    scale = 1.0 / math.sqrt(seq * g)
    eye = np.eye(fw // g)
    wc = np.kron(eye, np.cos(angc) * scale)
    ws = np.kron(eye, np.sin(angc) * scale)
    return (jnp.asarray(tab, F32).astype(BF16), jnp.asarray(wc, F32).astype(BF16),
            jnp.asarray(ws, F32).astype(BF16))


def _fourier(zf):
    bsz, seq, fw = zf.shape
    tab, wc, ws = _fourier_tables(seq, fw)
    tr = _tile(seq, 512)
    return pl.pallas_call(
        functools.partial(_fourier_kernel, seq=seq),
        grid=(bsz, seq // tr),
        in_specs=[pl.BlockSpec((1, seq, fw), lambda b, j: (b, 0, 0)),
                  pl.BlockSpec((fw, fw), lambda b, j: (0, 0)),
                  pl.BlockSpec((fw, fw), lambda b, j: (0, 0)),
                  pl.BlockSpec((tr, 2 * seq), lambda b, j: (j, 0))],
        out_specs=pl.BlockSpec((1, tr, fw), lambda b, j: (b, j, 0)),
        out_shape=jax.ShapeDtypeStruct((bsz, seq, fw), BF16),
        scratch_shapes=[pltpu.VMEM((2 * seq, fw), BF16)],
        compiler_params=_cparams("arbitrary", "arbitrary"),
        name="fourier",
    )(zf, wc, ws, tab)


def _merge_kernel(a_ref, f_ref, sa_ref, sf_ref, x_ref, mod_ref, wpa_ref, wpf_ref, wo_ref,
                  g2_ref, wrh_ref, wrl_ref, x1_ref, h2_ref, sc_ref):
    ya = jnp.dot(a_ref[...], wpa_ref[...], preferred_element_type=F32)
    yf = jnp.dot(f_ref[...], wpf_ref[...], preferred_element_type=F32)
    merged = sa_ref[...].astype(F32) * ya + sf_ref[...].astype(F32) * yf
    mod = mod_ref[0]
    g1, sh2, sc2 = mod[2:3], mod[3:4], mod[4:5]
    x1 = x_ref[...] + g1 * jnp.dot(merged.astype(BF16), wo_ref[...], preferred_element_type=F32)
    x1_ref[...] = x1
    r = lax.rsqrt(jnp.mean(x1 * x1, axis=-1, keepdims=True) + EPS)
    h2 = (x1 * r * g2_ref[...]) * (1.0 + sc2) + sh2
    h2_ref[0], h2_ref[1] = _pack_halves(h2)
    hh = h2.astype(BF16)
    hl = (h2 - hh.astype(F32)).astype(BF16)
    nt = (((1,), (1,)), ((), ()))
    lt = (lax.dot_general(wrh_ref[...], hh, nt, preferred_element_type=F32)
          + lax.dot_general(wrh_ref[...], hl, nt, preferred_element_type=F32)
          + lax.dot_general(wrl_ref[...], hh, nt, preferred_element_type=F32))
    sc_ref[...] = jax.nn.sigmoid(lt)


def _merge(attn, four, sa, sf, x2, mod, w_pa, w_pf, w_out, norm2_g, w_router, seq):
    t, d = x2.shape
    e = w_router.shape[1]
    wrt = w_router.T
    wrh = wrt.astype(BF16)
    wrl = (wrt - wrh.astype(F32)).astype(BF16)
    tm = _tile(seq, 512)
    tpb = seq // tm
    full = lambda shp: pl.BlockSpec(shp, lambda i: (0,) * len(shp))
    tok = lambda w: pl.BlockSpec((tm, w), lambda i: (i, 0))
    return pl.pallas_call(
        _merge_kernel,
        grid=(t // tm,),
        in_specs=[tok(attn.shape[1]), tok(four.shape[1]), tok(d), tok(d), tok(d),
                  pl.BlockSpec((1, N_ADA, d), lambda i: (i // tpb, 0, 0)),
                  full(w_pa.shape), full(w_pf.shape), full(w_out.shape), full((1, d)),
                  full((e, d)), full((e, d))],
        out_specs=[tok(d), pl.BlockSpec((2, tm, d // 4), lambda i: (0, i, 0)),
                   pl.BlockSpec((e, tm), lambda i: (0, i))],
        out_shape=[jax.ShapeDtypeStruct((t, d), F32),
                   jax.ShapeDtypeStruct((2, t, d // 4), jnp.uint32),
                   jax.ShapeDtypeStruct((e, t), F32)],
        compiler_params=_cparams("arbitrary"),
        name="merge",
    )(attn, four, sa, sf, x2, mod, w_pa.astype(BF16), w_pf.astype(BF16), w_out.astype(BF16),
      norm2_g.reshape(1, d), wrh, wrl)


def _route_kernel(s_ref, b_ref, tri_ref, idx_ref, w_ref, rank_ref, cnt_ref, carry_ref):
    @pl.when(pl.program_id(0) == 0)
    def _():
        carry_ref[...] = jnp.zeros_like(carry_ref)

    sc = s_ref[...]
    e, tr = sc.shape
    row = lax.broadcasted_iota(jnp.int32, (e, tr), 0)
    v = sc + b_ref[...]
    sel = jnp.zeros((e, tr), F32)
    idxs, ws = [], []
    for _ in range(TOP_K):
        m = jnp.max(v, axis=0, keepdims=True)
        idx = jnp.min(jnp.where(v == m, row, e), axis=0, keepdims=True)
        oh = row == idx
        ws.append(jnp.sum(jnp.where(oh, sc, 0.0), axis=0, keepdims=True))
        idxs.append(idx)
        v = jnp.where(oh, -jnp.inf, v)
        sel = sel + oh.astype(F32)
    wsum = ws[0]
    for w in ws[1:]:
        wsum = wsum + w
    selb = sel.astype(BF16)
    cum = jnp.dot(selb, tri_ref[...], preferred_element_type=F32) + carry_ref[...]
    for kk in range(TOP_K):
        oh = row == idxs[kk]
        rk = jnp.sum(jnp.where(oh, cum, 0.0), axis=0, keepdims=True)
        idx_ref[kk:kk + 1, :] = idxs[kk]
        rank_ref[kk:kk + 1, :] = rk.astype(jnp.int32)
        w_ref[kk:kk + 1, :] = ws[kk] / wsum * ROUTED_SCALE
    tot = carry_ref[...] + jnp.dot(selb, jnp.ones((tr, tr), BF16), preferred_element_type=F32)
    carry_ref[...] = tot
    cnt_ref[...] = tot


def _route(scores_t, router_bias):
    e, t = scores_t.shape
    tr = _tile(t, 256)
    tri = jnp.asarray(np.triu(np.ones((tr, tr), np.float32), 1), BF16)
    bias = jnp.broadcast_to(router_bias.reshape(e, 1), (e, tr)).astype(F32)
    blk = pl.BlockSpec((TOP_K, tr), lambda i: (0, i))
    return pl.pallas_call(
        _route_kernel,
        grid=(t // tr,),
        in_specs=[pl.BlockSpec((e, tr), lambda i: (0, i)),
                  pl.BlockSpec((e, tr), lambda i: (0, 0)),
                  pl.BlockSpec((tr, tr), lambda i: (0, 0))],
        out_specs=[blk, blk, blk, pl.BlockSpec((e, tr), lambda i: (0, 0))],
        out_shape=[jax.ShapeDtypeStruct((TOP_K, t), jnp.int32),
                   jax.ShapeDtypeStruct((TOP_K, t), F32),
                   jax.ShapeDtypeStruct((TOP_K, t), jnp.int32),
                   jax.ShapeDtypeStruct((e, tr), F32)],
        scratch_shapes=[pltpu.VMEM((e, tr), F32)],
        compiler_params=_cparams("arbitrary"),
        name="route",
    )(scores_t, bias, tri)


def _slots_kernel(idx_ref, rank_ref, ps_ref, slot_ref):
    ps = ps_ref[...]
    row = lax.broadcasted_iota(jnp.int32, ps.shape, 0)
    for kk in range(TOP_K):
        oh = row == idx_ref[kk:kk + 1, :]
        start = jnp.sum(jnp.where(oh, ps, 0), axis=0, keepdims=True)
        slot_ref[kk:kk + 1, :] = start + rank_ref[kk:kk + 1, :]


def _slots(idx_t, rank_t, p_start):
    _, t = idx_t.shape
    e = p_start.shape[0]
    ts = _tile(t, 512)
    ps = jnp.broadcast_to(p_start.reshape(e, 1), (e, ts))
    blk = pl.BlockSpec((TOP_K, ts), lambda i: (0, i))
    return pl.pallas_call(
        _slots_kernel,
        grid=(t // ts,),
        in_specs=[blk, blk, pl.BlockSpec((e, ts), lambda i: (0, 0))],
        out_specs=blk,
        out_shape=jax.ShapeDtypeStruct((TOP_K, t), jnp.int32),
        compiler_params=_cparams("arbitrary"),
        name="slots",
    )(idx_t, rank_t, ps)


def _dispatch(h2p, slot_t, n_slots):
    _, t, c = h2p.shape
    k = slot_t.shape[0]
    win = _tile(2 * t, SC_WINDOW)
    rows = h2p.reshape(2 * t, c)
    dest = jnp.concatenate([slot_t, slot_t + n_slots], axis=1)
    mesh = plsc.VectorSubcoreMesh(core_axis_name="core", subcore_axis_name="subcore")

    @pl.kernel(out_type=jax.ShapeDtypeStruct((2 * n_slots, c), h2p.dtype), mesh=mesh,
               scratch_types=[])
    def scatter_rows(x_hbm, s_hbm, o_hbm):
        def body(x_vmem, s_vmem):
            pltpu.sync_copy(x_vmem, o_hbm.at[s_vmem.at[0]])

        pltpu.emit_pipeline(
            body, grid=(2 * t // win, k),
            in_specs=[pl.BlockSpec((win, c), lambda i, j: (i, 0)),
                      pl.BlockSpec((1, win), lambda i, j: (j, i))],
            out_specs=[], core_axis_name=("core", "subcore"),
            dimension_semantics=(pltpu.PARALLEL, pltpu.ARBITRARY))(x_hbm, s_hbm)

    return scatter_rows(rows, dest).reshape(2, n_slots, c)


def _expert_kernel(be_ref, run_ref, rex_ref, nu_ref, xs_hbm, wg_hbm, wu_hbm, wd_hbm, ys_hbm,
                   xbuf, ybuf, wgf, wuf, wdf, wgb, wub, wdb, xsem, ysem, wsem, *, nblk):
    nu = nu_ref[0]
    rows = EXPERT_ROWS

    def x_copies(blk, slot):
        r0 = pl.multiple_of(blk * rows, rows)
        return [pltpu.make_async_copy(xs_hbm.at[h, pl.ds(r0, rows)], xbuf.at[slot, h],
                                      xsem.at[slot]) for h in range(2)]

    def y_copies(blk, slot):
        r0 = pl.multiple_of(blk * rows, rows)
        return [pltpu.make_async_copy(ybuf.at[slot, h], ys_hbm.at[h, pl.ds(r0, rows)],
                                      ysem.at[slot]) for h in range(2)]

    def weight_copies(e, which):
        return (pltpu.make_async_copy(wg_hbm.at[e], wgf.at[which], wsem.at[which]),
                pltpu.make_async_copy(wu_hbm.at[e], wuf.at[which], wsem.at[which]),
                pltpu.make_async_copy(wd_hbm.at[e], wdf.at[which], wsem.at[which]))

    for j in range(W_RING - 1):
        @pl.when(rex_ref[j] >= 0)
        def _():
            for cp in weight_copies(rex_ref[j], j):
                cp.start(priority=1)
    for j in range(X_RING - 1):
        @pl.when(j < nu)
        def _():
            for cp in x_copies(j, j):
                cp.start()

    def step(i, carry):
        ahead = i + X_RING - 1

        @pl.when(ahead < nu)
        def _():
            for cp in x_copies(ahead, ahead % X_RING):
                cp.start()

        prev = be_ref[jnp.maximum(i - 1, 0)]

        @pl.when(jnp.logical_or(i == 0, be_ref[i] != prev))
        def _():
            r = run_ref[i]
            par = r % W_RING
            for cp in weight_copies(be_ref[i], par):
                cp.wait()
            wgb[...] = wgf[par].astype(BF16)
            wub[...] = wuf[par].astype(BF16)
            wdb[...] = wdf[par].astype(BF16)
            later = rex_ref[r + W_RING - 1]

            @pl.when(later >= 0)
            def _():
                for cp in weight_copies(later, (r + W_RING - 1) % W_RING):
                    cp.start(priority=1)

        slot = i % X_RING
        for cp in x_copies(i, slot):
            cp.wait()
        x = _unpack_halves(xbuf[slot, 0], xbuf[slot, 1])
        g = jnp.dot(x, wgb[...], preferred_element_type=F32)
        u = jnp.dot(x, wub[...], preferred_element_type=F32)
        a = (_silu(g) * u).astype(BF16)
        y0, y1 = _pack_halves(jnp.dot(a, wdb[...], preferred_element_type=F32))

        out = i % 2

        @pl.when(i >= 2)
        def _():
            for cp in y_copies(i - 2, out):
                cp.wait()
        ybuf[out, 0] = y0
        ybuf[out, 1] = y1
        for cp in y_copies(i, out):
            cp.start()
        return carry

    lax.fori_loop(0, nu, step, 0)

    @pl.when(nu >= 2)
    def _():
        for cp in y_copies(nu - 2, nu % 2):
            cp.wait()
    for cp in y_copies(nu - 1, (nu - 1) % 2):
        cp.wait()

    ybuf[0] = jnp.zeros_like(ybuf[0])

    def zero_start(j, carry):
        for cp in y_copies(j, 0):
            cp.start()
        return carry

    def zero_wait(j, carry):
        for cp in y_copies(j, 0):
            cp.wait()
        return carry
    lax.fori_loop(nu, nblk, zero_start, 0)
    lax.fori_loop(nu, nblk, zero_wait, 0)


def _experts(blk_expert, blk_run, run_expert, nblk_used, xs, w_g, w_u, w_d):
    d, f = w_g.shape[1], w_g.shape[2]
    rows = EXPERT_ROWS
    nblk = blk_expert.shape[0]
    c = xs.shape[2]
    hbm = pl.BlockSpec(memory_space=pl.ANY)
    return pl.pallas_call(
        functools.partial(_expert_kernel, nblk=nblk),
        grid_spec=pltpu.PrefetchScalarGridSpec(
            num_scalar_prefetch=4,
            grid=(1,),
            in_specs=[hbm, hbm, hbm, hbm],
            out_specs=hbm,
            scratch_shapes=[pltpu.VMEM((X_RING, 2, rows, c), jnp.uint32),
                            pltpu.VMEM((2, 2, rows, c), jnp.uint32),
                            pltpu.VMEM((W_RING, d, f), F32), pltpu.VMEM((W_RING, d, f), F32),
                            pltpu.VMEM((W_RING, f, d), F32),
                            pltpu.VMEM((d, f), BF16), pltpu.VMEM((d, f), BF16),
                            pltpu.VMEM((f, d), BF16),
                            pltpu.SemaphoreType.DMA((X_RING,)), pltpu.SemaphoreType.DMA((2,)),
                            pltpu.SemaphoreType.DMA((W_RING,))]),
        out_shape=jax.ShapeDtypeStruct((2, nblk * rows, c), jnp.uint32),
        compiler_params=_cparams("arbitrary"),
        name="experts",
    )(blk_expert, blk_run, run_expert, nblk_used, xs, w_g, w_u, w_d)


def _gather(ys, slot_t):
    _, n_slots, c = ys.shape
    k, t = slot_t.shape
    p = 2 * k * t
    win = _tile(p, SC_WINDOW)
    src = jnp.concatenate([slot_t, slot_t + n_slots], axis=0).reshape(1, p)
    mesh = plsc.VectorSubcoreMesh(core_axis_name="core", subcore_axis_name="subcore")

    @pl.kernel(out_type=jax.ShapeDtypeStruct((p, c), ys.dtype), mesh=mesh, scratch_types=[])
    def gather_rows(y_hbm, s_hbm, o_hbm):
        def body(s_vmem, o_vmem):
            pltpu.sync_copy(y_hbm.at[s_vmem.at[0]], o_vmem)

        pltpu.emit_pipeline(
            body, grid=(p // win,),
            in_specs=[pl.BlockSpec((1, win), lambda i: (0, i))],
            out_specs=[pl.BlockSpec((win, c), lambda i: (i, 0))],
            core_axis_name=("core", "subcore"),
            dimension_semantics=(pltpu.PARALLEL,))(s_hbm, o_hbm)

    return gather_rows(ys.reshape(2 * n_slots, c), src).reshape(2, k, t, c)


def _combine_kernel(y_ref, w_ref, x1_ref, h2_ref, mod_ref, wsg_ref, wsu_ref, wsd_ref, o_ref):
    hb = _unpack_halves(h2_ref[0], h2_ref[1])
    g = jnp.dot(hb, wsg_ref[...], preferred_element_type=F32)
    u = jnp.dot(hb, wsu_ref[...], preferred_element_type=F32)
    acc = jnp.dot((_silu(g) * u).astype(BF16), wsd_ref[...], preferred_element_type=F32)
    w = w_ref[...].T
    for kk in range(TOP_K):
        acc = acc + w[:, kk:kk + 1] * _unpack_halves(y_ref[0, kk], y_ref[1, kk]).astype(F32)
    g2 = mod_ref[0][5:6]
    o_ref[...] = x1_ref[...] + g2 * acc


def _combine(y_tok, w_tk, x1, h2p, mod, w_sg, w_su, w_sd, seq):
    t, d = x1.shape
    c = d // 4
    tc = _tile(seq, 256)
    tpb = seq // tc
    full = lambda shp: pl.BlockSpec(shp, lambda i: (0,) * len(shp))
    tok = lambda w: pl.BlockSpec((tc, w), lambda i: (i, 0))
    return pl.pallas_call(
        _combine_kernel,
        grid=(t // tc,),
        in_specs=[pl.BlockSpec((2, TOP_K, tc, c), lambda i: (0, 0, i, 0)),
                  pl.BlockSpec((TOP_K, tc), lambda i: (0, i)), tok(d),
                  pl.BlockSpec((2, tc, c), lambda i: (0, i, 0)),
                  pl.BlockSpec((1, N_ADA, d), lambda i: (i // tpb, 0, 0)),
                  full(w_sg.shape), full(w_su.shape), full(w_sd.shape)],
        out_specs=tok(d),
        out_shape=jax.ShapeDtypeStruct((t, d), F32),
        compiler_params=_cparams("arbitrary"),
        name="combine",
    )(y_tok, w_tk, x1, h2p, mod, w_sg.astype(BF16), w_su.astype(BF16), w_sd.astype(BF16))


def _layer(x, c, w_ada, b_ada, norm1_g, w_in, q_a_norm_g, w_uq, kv_a_norm_g, w_ukv,
           q_norm_g, k_norm_g, w_proj_attn, w_proj_fourier, w_out, norm2_g,
           w_router, router_bias, w_exp_gate, w_exp_up, w_exp_down,
           w_sh_gate, w_sh_up, w_sh_down):
    bsz, seq, d = x.shape
    t = bsz * seq
    e = w_router.shape[1]
    x2 = x.reshape(t, d)

    mod = _ada(c, w_ada, b_ada).reshape(bsz, N_ADA, d)
    q, k, v, zf, sa, sf = _inproj(x2, mod, norm1_g, w_in, q_a_norm_g, w_uq, kv_a_norm_g,
                                  w_ukv, q_norm_g, k_norm_g, bsz, seq)
    per_batch = lambda a: a.reshape(bsz, seq, a.shape[1])
    attn = _attention(per_batch(q), per_batch(k), per_batch(v)).reshape(t, N_HEADS * V_DIM)
    four = _fourier(zf.reshape(bsz, seq, zf.shape[1])).reshape(t, zf.shape[1])
    x1, h2, scores_t = _merge(attn, four, sa, sf, x2, mod, w_proj_attn, w_proj_fourier,
                              w_out, norm2_g, w_router, seq)

    idx_t, w_t, rank_t, cnt = _route(scores_t, router_bias)
    counts = cnt[:, 0].astype(jnp.int32)
    rows = EXPERT_ROWS
    nblk = -(-(t * TOP_K) // rows) + e
    padded = ((counts + rows - 1) // rows) * rows
    p_end = jnp.cumsum(padded)
    p_start = p_end - padded
    nblk_used = (p_end[-1] // rows).astype(jnp.int32)
    blk_start = jnp.arange(nblk, dtype=jnp.int32) * rows
    blk_first = jnp.minimum(blk_start, p_end[-1] - 1)
    hit = p_end[None, :] <= blk_first[:, None]
    blk_expert = jnp.clip(jnp.sum(hit.astype(jnp.int32), axis=1), 0, e - 1)

    slot_t = _slots(idx_t, rank_t, p_start.astype(jnp.int32))
    xs = _dispatch(h2, slot_t, nblk * rows)
    eid = jnp.arange(e, dtype=jnp.int32)
    used = counts > 0
    run = jnp.cumsum(used.astype(jnp.int32)) - 1
    blk_run = jnp.sum(jnp.where(blk_expert[:, None] == eid[None, :], run[None, :], 0),
                      axis=1).astype(jnp.int32)
    rid = jnp.arange(e + W_RING, dtype=jnp.int32)
    match = jnp.logical_and(used[None, :], run[None, :] == rid[:, None])
    run_expert = jnp.where(jnp.any(match, axis=1),
                           jnp.sum(jnp.where(match, eid[None, :], 0), axis=1), -1).astype(jnp.int32)
    ys = _experts(blk_expert, blk_run, run_expert, nblk_used.reshape(1), xs,
                  w_exp_gate, w_exp_up, w_exp_down)
    out = _combine(_gather(ys, slot_t), w_t, x1, h2, mod, w_sh_gate, w_sh_up, w_sh_down, seq)
    return out.reshape(bsz, seq, d)


def kernel(x, c, w_ada, b_ada, norm1_g, w_in, q_a_norm_g, w_uq, kv_a_norm_g, w_ukv, q_norm_g,
           k_norm_g, w_proj_attn, w_proj_fourier, w_out, norm2_g, w_router, router_bias,
           w_exp_gate, w_exp_up, w_exp_down, w_sh_gate, w_sh_up, w_sh_down):
    for l in range(w_ada.shape[0]):
        x = _layer(x, c, w_ada[l], b_ada[l], norm1_g[l], w_in[l], q_a_norm_g[l], w_uq[l],
                   kv_a_norm_g[l], w_ukv[l], q_norm_g[l], k_norm_g[l], w_proj_attn[l],
                   w_proj_fourier[l], w_out[l], norm2_g[l], w_router[l], router_bias[l],
                   w_exp_gate[l], w_exp_up[l], w_exp_down[l], w_sh_gate[l], w_sh_up[l],
                   w_sh_down[l])
    return x
```

```python
import functools
import math

import numpy as np
import jax
import jax.numpy as jnp
from jax import lax
from jax.experimental import pallas as pl
from jax.experimental.pallas import tpu as pltpu
from jax.experimental.pallas import tpu_sc as plsc

N_HEADS = 8
QK_NOPE = 64
QK_ROPE = 32
V_DIM = 64
FOURIER_GROUP = 64
TOP_K = 8
ROUTED_SCALE = 2.5
EPS = 1e-6
ROPE_THETA = 10000.0
N_ADA = 6

LANES = 128
EXPERT_ROWS = 512
X_RING = 4
W_RING = 3
SC_WINDOW = 128
VMEM_LIMIT = 48 * 1024 * 1024

F32 = jnp.float32
BF16 = jnp.bfloat16


def _cparams(*sem):
    return pltpu.CompilerParams(dimension_semantics=sem, vmem_limit_bytes=VMEM_LIMIT)


def _tile(n, pref):
    t = min(n, pref)
    assert n % t == 0, (n, pref)
    return t


def _silu(v):
    return v * jax.nn.sigmoid(v)


def _pack_halves(m):
    d = m.shape[1]
    lo = lax.bitcast_convert_type(m[:, :d // 2].astype(BF16).astype(F32), jnp.uint32)
    hi = lax.bitcast_convert_type(m[:, d // 2:].astype(BF16).astype(F32), jnp.uint32)
    w = (lo >> 16) | (hi & jnp.uint32(0xFFFF0000))
    return w[:, :d // 4], w[:, d // 4:]


def _unpack_halves(w0, w1):
    def lo(w):
        return lax.bitcast_convert_type(w << 16, F32)

    def hi(w):
        return lax.bitcast_convert_type(w & jnp.uint32(0xFFFF0000), F32)
    return jnp.concatenate([lo(w0), lo(w1), hi(w0), hi(w1)], axis=1).astype(BF16)


def _ada_kernel(c_ref, w_ref, b_ref, o_ref):
    a = _silu(c_ref[...])
    o_ref[...] = jnp.dot(a, w_ref[...], preferred_element_type=F32,
                         precision=lax.Precision.HIGHEST) + b_ref[...]


def _ada(c, w_ada, b_ada):
    bsz, d = c.shape
    n = w_ada.shape[1]
    tn = _tile(n, d)
    return pl.pallas_call(
        _ada_kernel,
        grid=(n // tn,),
        in_specs=[pl.BlockSpec((bsz, d), lambda j: (0, 0)),
                  pl.BlockSpec((d, tn), lambda j: (0, j)),
                  pl.BlockSpec((1, tn), lambda j: (0, j))],
        out_specs=pl.BlockSpec((bsz, tn), lambda j: (0, j)),
        out_shape=jax.ShapeDtypeStruct((bsz, n), F32),
        compiler_params=_cparams("arbitrary"),
        name="ada",
    )(c, w_ada, b_ada.reshape(1, n))


def _head_norm_rope(t, trot, a, b):
    ms = jnp.sum(t * t, axis=-1, keepdims=True) * (1.0 / (QK_NOPE + QK_ROPE))
    return (t * a + trot * b) * lax.rsqrt(ms + EPS)


def _inproj_kernel(x_ref, mod_ref, g1_ref, wa_ref, wf_ref, wga_ref, wgf_ref,
                   gq_ref, gkv_ref, wuq_ref, wkv_ref, wvt_ref,
                   aq_ref, bq_ref, ak_ref, bk_ref,
                   q_ref, k_ref, v_ref, zf_ref, sa_ref, sf_ref, *, ql, kvl):
    x = x_ref[...]
    mod = mod_ref[0]
    sh1, sc1 = mod[0:1], mod[1:2]
    r = lax.rsqrt(jnp.mean(x * x, axis=-1, keepdims=True) + EPS)
    h = (x * r * g1_ref[...]) * (1.0 + sc1) + sh1
    hb = h.astype(BF16)

    zf_ref[...] = jnp.dot(hb, wf_ref[...], preferred_element_type=F32).astype(BF16)
    sa_ref[...] = jax.nn.sigmoid(
        jnp.dot(hb, wga_ref[...], preferred_element_type=F32)).astype(BF16)
    sf_ref[...] = jax.nn.sigmoid(
        jnp.dot(hb, wgf_ref[...], preferred_element_type=F32)).astype(BF16)

    za = jnp.dot(hb, wa_ref[...], preferred_element_type=F32)
    zq = za[:, :ql]
    cq = zq * lax.rsqrt(jnp.mean(zq * zq, axis=-1, keepdims=True) + EPS) * gq_ref[...]
    qall = jnp.dot(cq.astype(BF16), wuq_ref[...], preferred_element_type=F32)

    zk = za[:, ql:]
    kvn = zk[:, :kvl]
    rk = lax.rsqrt(jnp.mean(kvn * kvn, axis=-1, keepdims=True) + EPS)
    lane = lax.broadcasted_iota(jnp.int32, zk.shape, 1)
    u = zk * jnp.where(lane < kvl, rk, 1.0) * gkv_ref[...]
    ub = u.astype(BF16)
    kvall = jnp.dot(ub, wkv_ref[...], preferred_element_type=F32)

    aq, bq, ak, bk = aq_ref[...], bq_ref[...], ak_ref[...], bk_ref[...]
    hw = N_HEADS * LANES
    for hd in range(N_HEADS):
        lo, hi = hd * LANES, (hd + 1) * LANES
        q_ref[:, lo:hi] = _head_norm_rope(qall[:, lo:hi], qall[:, hw + lo:hw + hi],
                                       aq, bq).astype(BF16)
        k_ref[:, lo:hi] = _head_norm_rope(kvall[:, lo:hi], kvall[:, hw + lo:hw + hi],
                                       ak, bk).astype(BF16)
    v_ref[0] = lax.dot_general(wvt_ref[...], ub, (((1,), (1,)), ((), ())),
                               preferred_element_type=F32).astype(BF16)


def _rope_tables(seq):
    half = QK_ROPE // 2
    pos = np.arange(seq, dtype=np.float64)
    inv = ROPE_THETA ** (-np.arange(0, QK_ROPE, 2, dtype=np.float64) / QK_ROPE)
    ang = pos[:, None] * inv[None, :]
    c, s = np.cos(ang), np.sin(ang)
    cos = np.ones((seq, LANES)); sin = np.zeros((seq, LANES))
    cos[:, QK_NOPE:QK_NOPE + half] = c
    cos[:, QK_NOPE + half:QK_NOPE + QK_ROPE] = c
    sin[:, QK_NOPE:QK_NOPE + half] = -s
    sin[:, QK_NOPE + half:QK_NOPE + QK_ROPE] = s
    return jnp.asarray(cos, F32), jnp.asarray(sin, F32)


def _partner_columns(w):
    half = QK_ROPE // 2
    lo, mid, hi = QK_NOPE, QK_NOPE + half, QK_NOPE + QK_ROPE
    z = jnp.zeros_like(w)
    return jnp.concatenate([z[..., :lo], w[..., mid:hi], w[..., lo:mid], z[..., hi:]], axis=-1)


def _inproj(x2, mod, norm1_g, w_in, q_a_g, w_uq, kv_a_g, w_ukv, q_g, k_g, bsz, seq):
    t, d = x2.shape
    ql, kvl = q_a_g.shape[0], kv_a_g.shape[0]
    hq = QK_NOPE + QK_ROPE
    fw = w_in.shape[1] - ql - kvl - QK_ROPE - 2 * d
    o1, o2, o3, o4, o5 = ql, ql + kvl, ql + kvl + QK_ROPE, ql + kvl + QK_ROPE + fw, \
        ql + kvl + QK_ROPE + fw + d
    assert ql % LANES == 0 and kvl % LANES == 0

    wa = jnp.concatenate([w_in[:, :o3], jnp.zeros((d, LANES - QK_ROPE), F32)], axis=1).astype(BF16)
    wf = w_in[:, o3:o4].astype(BF16)
    wga = w_in[:, o4:o5].astype(BF16)
    wgf = w_in[:, o5:].astype(BF16)

    wuq = w_uq.reshape(ql, N_HEADS, hq)
    wuq = jnp.pad(wuq, ((0, 0), (0, 0), (0, LANES - hq)))
    wuq = jnp.concatenate([wuq.reshape(ql, N_HEADS * LANES),
                           _partner_columns(wuq).reshape(ql, N_HEADS * LANES)], axis=1).astype(BF16)
    wukv = w_ukv.reshape(kvl, N_HEADS, QK_NOPE + V_DIM)
    wk = jnp.pad(wukv[:, :, :QK_NOPE], ((0, 0), (0, 0), (0, LANES - QK_NOPE)))
    place = jnp.zeros((QK_ROPE, N_HEADS, LANES), F32)
    place = place.at[jnp.arange(QK_ROPE), :, QK_NOPE + jnp.arange(QK_ROPE)].set(1.0)
    wk = jnp.concatenate([wk, place, jnp.zeros((LANES - QK_ROPE, N_HEADS, LANES), F32)], axis=0)
    wv = jnp.concatenate([wukv[:, :, QK_NOPE:], jnp.zeros((LANES, N_HEADS, V_DIM), F32)], axis=0)
    wkv = jnp.concatenate([wk.reshape(kvl + LANES, N_HEADS * LANES),
                           _partner_columns(wk).reshape(kvl + LANES, N_HEADS * LANES)],
                          axis=1).astype(BF16)
    wvt = wv.reshape(kvl + LANES, N_HEADS * V_DIM).T.astype(BF16)

    gkv = jnp.concatenate([kv_a_g, jnp.ones((LANES,), F32)]).reshape(1, kvl + LANES)
    pad = jnp.zeros((LANES - hq,), F32)
    qg = jnp.concatenate([q_g * (hq ** -0.5 * math.log2(math.e)), pad])
    kg = jnp.concatenate([k_g, pad])
    cos, sin = _rope_tables(seq)
    aq, bq = qg[None, :] * cos, _partner_columns(qg)[None, :] * sin
    ak, bk = kg[None, :] * cos, _partner_columns(kg)[None, :] * sin

    tm = _tile(seq, 512)
    tpb = seq // tm
    full = lambda shp: pl.BlockSpec(shp, lambda i: (0,) * len(shp))
    tok = lambda w: pl.BlockSpec((tm, w), lambda i: (i, 0))
    rope = pl.BlockSpec((tm, LANES), lambda i: (i % tpb, 0))
    return pl.pallas_call(
        functools.partial(_inproj_kernel, ql=ql, kvl=kvl),
        grid=(t // tm,),
        in_specs=[tok(d),
                  pl.BlockSpec((1, N_ADA, d), lambda i: (i // tpb, 0, 0)),
                  full((1, d)), full(wa.shape), full(wf.shape), full(wga.shape), full(wgf.shape),
                  full((1, ql)), full(gkv.shape), full(wuq.shape), full(wkv.shape),
                  full(wvt.shape), rope, rope, rope, rope],
        out_specs=[tok(N_HEADS * LANES), tok(N_HEADS * LANES),
                   pl.BlockSpec((1, N_HEADS * V_DIM, tm), lambda i: (i // tpb, 0, i % tpb)),
                   tok(fw), tok(d), tok(d)],
        out_shape=[jax.ShapeDtypeStruct((t, N_HEADS * LANES), BF16),
                   jax.ShapeDtypeStruct((t, N_HEADS * LANES), BF16),
                   jax.ShapeDtypeStruct((bsz, N_HEADS * V_DIM, seq), BF16),
                   jax.ShapeDtypeStruct((t, fw), BF16),
                   jax.ShapeDtypeStruct((t, d), BF16),
                   jax.ShapeDtypeStruct((t, d), BF16)],
        compiler_params=_cparams("arbitrary"),
        name="inproj",
    )(x2, mod, norm1_g.reshape(1, d), wa, wf, wga, wgf, q_a_g.reshape(1, ql), gkv, wuq, wkv, wvt,
      aq, bq, ak, bk)


def _attn_kernel(q_ref, k_ref, v_ref, o_ref, s_ref):
    def scores(hd):
        s_ref[hd % 2] = lax.dot_general(
            k_ref[0, :, hd * LANES:(hd + 1) * LANES], q_ref[0, :, hd * LANES:(hd + 1) * LANES],
            (((1,), (1,)), ((), ())), preferred_element_type=F32)

    scores(0)
    for hd in range(N_HEADS):
        if hd + 1 < N_HEADS:
            scores(hd + 1)
        st = s_ref[hd % 2]
        m = jnp.max(st, axis=0, keepdims=True)
        pt = jnp.exp2(st - m)
        l = jnp.sum(pt, axis=0, keepdims=True)
        ot = jnp.dot(v_ref[0, hd * V_DIM:(hd + 1) * V_DIM, :], pt.astype(BF16),
                     preferred_element_type=F32)
        o_ref[0, :, hd * V_DIM:(hd + 1) * V_DIM] = (ot / l).T.astype(BF16)


def _attention(q, k, v):
    bsz, seq, _ = q.shape
    tq = _tile(seq, 512)
    return pl.pallas_call(
        _attn_kernel,
        grid=(bsz, seq // tq),
        in_specs=[pl.BlockSpec((1, tq, N_HEADS * LANES), lambda b, j: (b, j, 0)),
                  pl.BlockSpec((1, seq, N_HEADS * LANES), lambda b, j: (b, 0, 0)),
                  pl.BlockSpec((1, N_HEADS * V_DIM, seq), lambda b, j: (b, 0, 0))],
        out_specs=pl.BlockSpec((1, tq, N_HEADS * V_DIM), lambda b, j: (b, j, 0)),
        out_shape=jax.ShapeDtypeStruct((bsz, seq, N_HEADS * V_DIM), BF16),
        scratch_shapes=[pltpu.VMEM((2, seq, tq), F32)],
        compiler_params=_cparams("arbitrary", "arbitrary"),
        name="attn",
    )(q, k, v)


def _fourier_kernel(z_ref, wc_ref, ws_ref, tab_ref, o_ref, u_ref, *, seq):
    @pl.when(pl.program_id(1) == 0)
    def _():
        z = z_ref[0]
        u_ref[:seq, :] = jnp.dot(z, wc_ref[...], preferred_element_type=F32).astype(BF16)
        u_ref[seq:, :] = jnp.dot(z, ws_ref[...], preferred_element_type=F32).astype(BF16)

    o_ref[0] = jnp.dot(tab_ref[...], u_ref[...], preferred_element_type=F32).astype(BF16)


def _fourier_tables(seq, fw):
    g = FOURIER_GROUP
    n = np.arange(seq, dtype=np.int64)
    ang = 2.0 * np.pi * ((n[:, None] * n[None, :]) % seq).astype(np.float64) / seq
    tab = np.concatenate([np.cos(ang), -np.sin(ang)], axis=1)
    c = np.arange(g, dtype=np.int64)
    angc = 2.0 * np.pi * ((c[:, None] * c[None, :]) % g).astype(np.float64) / g
    scale = 1.0 / math.sqrt(seq * g)
    eye = np.eye(fw // g)
    wc = np.kron(eye, np.cos(angc) * scale)
    ws = np.kron(eye, np.sin(angc) * scale)
    return (jnp.asarray(tab, F32).astype(BF16), jnp.asarray(wc, F32).astype(BF16),
            jnp.asarray(ws, F32).astype(BF16))


def _fourier(zf):
    bsz, seq, fw = zf.shape
    tab, wc, ws = _fourier_tables(seq, fw)
    tr = _tile(seq, 512)
    return pl.pallas_call(
        functools.partial(_fourier_kernel, seq=seq),
        grid=(bsz, seq // tr),
        in_specs=[pl.BlockSpec((1, seq, fw), lambda b, j: (b, 0, 0)),
                  pl.BlockSpec((fw, fw), lambda b, j: (0, 0)),
                  pl.BlockSpec((fw, fw), lambda b, j: (0, 0)),
                  pl.BlockSpec((tr, 2 * seq), lambda b, j: (j, 0))],
        out_specs=pl.BlockSpec((1, tr, fw), lambda b, j: (b, j, 0)),
        out_shape=jax.ShapeDtypeStruct((bsz, seq, fw), BF16),
        scratch_shapes=[pltpu.VMEM((2 * seq, fw), BF16)],
        compiler_params=_cparams("arbitrary", "arbitrary"),
        name="fourier",
    )(zf, wc, ws, tab)


def _merge_kernel(a_ref, f_ref, sa_ref, sf_ref, x_ref, mod_ref, wpa_ref, wpf_ref, wo_ref,
                  g2_ref, wrh_ref, wrl_ref, x1_ref, h2_ref, sc_ref):
    ya = jnp.dot(a_ref[...], wpa_ref[...], preferred_element_type=F32)
    yf = jnp.dot(f_ref[...], wpf_ref[...], preferred_element_type=F32)
    merged = sa_ref[...].astype(F32) * ya + sf_ref[...].astype(F32) * yf
    mod = mod_ref[0]
    g1, sh2, sc2 = mod[2:3], mod[3:4], mod[4:5]
    x1 = x_ref[...] + g1 * jnp.dot(merged.astype(BF16), wo_ref[...], preferred_element_type=F32)
    x1_ref[...] = x1
    r = lax.rsqrt(jnp.mean(x1 * x1, axis=-1, keepdims=True) + EPS)
    h2 = (x1 * r * g2_ref[...]) * (1.0 + sc2) + sh2
    h2_ref[0], h2_ref[1] = _pack_halves(h2)
    hh = h2.astype(BF16)
    hl = (h2 - hh.astype(F32)).astype(BF16)
    nt = (((1,), (1,)), ((), ()))
    lt = (lax.dot_general(wrh_ref[...], hh, nt, preferred_element_type=F32)
          + lax.dot_general(wrh_ref[...], hl, nt, preferred_element_type=F32)
          + lax.dot_general(wrl_ref[...], hh, nt, preferred_element_type=F32))
    sc_ref[...] = jax.nn.sigmoid(lt)


def _merge(attn, four, sa, sf, x2, mod, w_pa, w_pf, w_out, norm2_g, w_router, seq):
    t, d = x2.shape
    e = w_router.shape[1]
    wrt = w_router.T
    wrh = wrt.astype(BF16)
    wrl = (wrt - wrh.astype(F32)).astype(BF16)
    tm = _tile(seq, 512)
    tpb = seq // tm
    full = lambda shp: pl.BlockSpec(shp, lambda i: (0,) * len(shp))
    tok = lambda w: pl.BlockSpec((tm, w), lambda i: (i, 0))
    return pl.pallas_call(
        _merge_kernel,
        grid=(t // tm,),
        in_specs=[tok(attn.shape[1]), tok(four.shape[1]), tok(d), tok(d), tok(d),
                  pl.BlockSpec((1, N_ADA, d), lambda i: (i // tpb, 0, 0)),
                  full(w_pa.shape), full(w_pf.shape), full(w_out.shape), full((1, d)),
                  full((e, d)), full((e, d))],
        out_specs=[tok(d), pl.BlockSpec((2, tm, d // 4), lambda i: (0, i, 0)),
                   pl.BlockSpec((e, tm), lambda i: (0, i))],
        out_shape=[jax.ShapeDtypeStruct((t, d), F32),
                   jax.ShapeDtypeStruct((2, t, d // 4), jnp.uint32),
                   jax.ShapeDtypeStruct((e, t), F32)],
        compiler_params=_cparams("arbitrary"),
        name="merge",
    )(attn, four, sa, sf, x2, mod, w_pa.astype(BF16), w_pf.astype(BF16), w_out.astype(BF16),
      norm2_g.reshape(1, d), wrh, wrl)


def _route_kernel(s_ref, b_ref, tri_ref, idx_ref, w_ref, rank_ref, cnt_ref, carry_ref):
    @pl.when(pl.program_id(0) == 0)
    def _():
        carry_ref[...] = jnp.zeros_like(carry_ref)

    sc = s_ref[...]
    e, tr = sc.shape
    row = lax.broadcasted_iota(jnp.int32, (e, tr), 0)
    v = sc + b_ref[...]
    sel = jnp.zeros((e, tr), F32)
    idxs, ws = [], []
    for _ in range(TOP_K):
        m = jnp.max(v, axis=0, keepdims=True)
        idx = jnp.min(jnp.where(v == m, row, e), axis=0, keepdims=True)
        oh = row == idx
        ws.append(jnp.sum(jnp.where(oh, sc, 0.0), axis=0, keepdims=True))
        idxs.append(idx)
        v = jnp.where(oh, -jnp.inf, v)
        sel = sel + oh.astype(F32)
    wsum = ws[0]
    for w in ws[1:]:
        wsum = wsum + w
    selb = sel.astype(BF16)
    cum = jnp.dot(selb, tri_ref[...], preferred_element_type=F32) + carry_ref[...]
    for kk in range(TOP_K):
        oh = row == idxs[kk]
        rk = jnp.sum(jnp.where(oh, cum, 0.0), axis=0, keepdims=True)
        idx_ref[kk:kk + 1, :] = idxs[kk]
        rank_ref[kk:kk + 1, :] = rk.astype(jnp.int32)
        w_ref[kk:kk + 1, :] = ws[kk] / wsum * ROUTED_SCALE
    tot = carry_ref[...] + jnp.dot(selb, jnp.ones((tr, tr), BF16), preferred_element_type=F32)
    carry_ref[...] = tot
    cnt_ref[...] = tot


def _route(scores_t, router_bias):
    e, t = scores_t.shape
    tr = _tile(t, 256)
    tri = jnp.asarray(np.triu(np.ones((tr, tr), np.float32), 1), BF16)
    bias = jnp.broadcast_to(router_bias.reshape(e, 1), (e, tr)).astype(F32)
    blk = pl.BlockSpec((TOP_K, tr), lambda i: (0, i))
    return pl.pallas_call(
        _route_kernel,
        grid=(t // tr,),
        in_specs=[pl.BlockSpec((e, tr), lambda i: (0, i)),
                  pl.BlockSpec((e, tr), lambda i: (0, 0)),
                  pl.BlockSpec((tr, tr), lambda i: (0, 0))],
        out_specs=[blk, blk, blk, pl.BlockSpec((e, tr), lambda i: (0, 0))],
        out_shape=[jax.ShapeDtypeStruct((TOP_K, t), jnp.int32),
                   jax.ShapeDtypeStruct((TOP_K, t), F32),
                   jax.ShapeDtypeStruct((TOP_K, t), jnp.int32),
                   jax.ShapeDtypeStruct((e, tr), F32)],
        scratch_shapes=[pltpu.VMEM((e, tr), F32)],
        compiler_params=_cparams("arbitrary"),
        name="route",
    )(scores_t, bias, tri)


def _slots_kernel(idx_ref, rank_ref, ps_ref, slot_ref):
    ps = ps_ref[...]
    row = lax.broadcasted_iota(jnp.int32, ps.shape, 0)
    for kk in range(TOP_K):
        oh = row == idx_ref[kk:kk + 1, :]
        start = jnp.sum(jnp.where(oh, ps, 0), axis=0, keepdims=True)
        slot_ref[kk:kk + 1, :] = start + rank_ref[kk:kk + 1, :]


def _slots(idx_t, rank_t, p_start):
    _, t = idx_t.shape
    e = p_start.shape[0]
    ts = _tile(t, 512)
    ps = jnp.broadcast_to(p_start.reshape(e, 1), (e, ts))
    blk = pl.BlockSpec((TOP_K, ts), lambda i: (0, i))
    return pl.pallas_call(
        _slots_kernel,
        grid=(t // ts,),
        in_specs=[blk, blk, pl.BlockSpec((e, ts), lambda i: (0, 0))],
        out_specs=blk,
        out_shape=jax.ShapeDtypeStruct((TOP_K, t), jnp.int32),
        compiler_params=_cparams("arbitrary"),
        name="slots",
    )(idx_t, rank_t, ps)


def _dispatch(h2p, slot_t, n_slots):
    _, t, c = h2p.shape
    k = slot_t.shape[0]
    win = _tile(2 * t, SC_WINDOW)
    rows = h2p.reshape(2 * t, c)
    dest = jnp.concatenate([slot_t, slot_t + n_slots], axis=1)
    mesh = plsc.VectorSubcoreMesh(core_axis_name="core", subcore_axis_name="subcore")

    @pl.kernel(out_type=jax.ShapeDtypeStruct((2 * n_slots, c), h2p.dtype), mesh=mesh,
               scratch_types=[])
    def scatter_rows(x_hbm, s_hbm, o_hbm):
        def body(x_vmem, s_vmem):
            pltpu.sync_copy(x_vmem, o_hbm.at[s_vmem.at[0]])

        pltpu.emit_pipeline(
            body, grid=(2 * t // win, k),
            in_specs=[pl.BlockSpec((win, c), lambda i, j: (i, 0)),
                      pl.BlockSpec((1, win), lambda i, j: (j, i))],
            out_specs=[], core_axis_name=("core", "subcore"),
            dimension_semantics=(pltpu.PARALLEL, pltpu.ARBITRARY))(x_hbm, s_hbm)

    return scatter_rows(rows, dest).reshape(2, n_slots, c)


def _expert_kernel(be_ref, run_ref, rex_ref, nu_ref, xs_hbm, wg_hbm, wu_hbm, wd_hbm, ys_hbm,
                   xbuf, ybuf, wgf, wuf, wdf, wgb, wub, wdb, xsem, ysem, wsem, *, nblk):
    nu = nu_ref[0]
    rows = EXPERT_ROWS

    def x_copies(blk, slot):
        r0 = pl.multiple_of(blk * rows, rows)
        return [pltpu.make_async_copy(xs_hbm.at[h, pl.ds(r0, rows)], xbuf.at[slot, h],
                                      xsem.at[slot]) for h in range(2)]

    def y_copies(blk, slot):
        r0 = pl.multiple_of(blk * rows, rows)
        return [pltpu.make_async_copy(ybuf.at[slot, h], ys_hbm.at[h, pl.ds(r0, rows)],
                                      ysem.at[slot]) for h in range(2)]

    def weight_copies(e, which):
        return (pltpu.make_async_copy(wg_hbm.at[e], wgf.at[which], wsem.at[which]),
                pltpu.make_async_copy(wu_hbm.at[e], wuf.at[which], wsem.at[which]),
                pltpu.make_async_copy(wd_hbm.at[e], wdf.at[which], wsem.at[which]))

    for j in range(W_RING - 1):
        @pl.when(rex_ref[j] >= 0)
        def _():
            for cp in weight_copies(rex_ref[j], j):
                cp.start(priority=1)
    for j in range(X_RING - 1):
        @pl.when(j < nu)
        def _():
            for cp in x_copies(j, j):
                cp.start()

    def step(i, carry):
        ahead = i + X_RING - 1

        @pl.when(ahead < nu)
        def _():
            for cp in x_copies(ahead, ahead % X_RING):
                cp.start()

        prev = be_ref[jnp.maximum(i - 1, 0)]

        @pl.when(jnp.logical_or(i == 0, be_ref[i] != prev))
        def _():
            r = run_ref[i]
            par = r % W_RING
            for cp in weight_copies(be_ref[i], par):
                cp.wait()
            wgb[...] = wgf[par].astype(BF16)
            wub[...] = wuf[par].astype(BF16)
            wdb[...] = wdf[par].astype(BF16)
            later = rex_ref[r + W_RING - 1]

            @pl.when(later >= 0)
            def _():
                for cp in weight_copies(later, (r + W_RING - 1) % W_RING):
                    cp.start(priority=1)

        slot = i % X_RING
        for cp in x_copies(i, slot):
            cp.wait()
        x = _unpack_halves(xbuf[slot, 0], xbuf[slot, 1])
        g = jnp.dot(x, wgb[...], preferred_element_type=F32)
        u = jnp.dot(x, wub[...], preferred_element_type=F32)
        a = (_silu(g) * u).astype(BF16)
        y0, y1 = _pack_halves(jnp.dot(a, wdb[...], preferred_element_type=F32))

        out = i % 2

        @pl.when(i >= 2)
        def _():
            for cp in y_copies(i - 2, out):
                cp.wait()
        ybuf[out, 0] = y0
        ybuf[out, 1] = y1
        for cp in y_copies(i, out):
            cp.start()
        return carry

    lax.fori_loop(0, nu, step, 0)

    @pl.when(nu >= 2)
    def _():
        for cp in y_copies(nu - 2, nu % 2):
            cp.wait()
    for cp in y_copies(nu - 1, (nu - 1) % 2):
        cp.wait()

    ybuf[0] = jnp.zeros_like(ybuf[0])

    def zero_start(j, carry):
        for cp in y_copies(j, 0):
            cp.start()
        return carry

    def zero_wait(j, carry):
        for cp in y_copies(j, 0):
            cp.wait()
        return carry
    lax.fori_loop(nu, nblk, zero_start, 0)
    lax.fori_loop(nu, nblk, zero_wait, 0)


def _experts(blk_expert, blk_run, run_expert, nblk_used, xs, w_g, w_u, w_d):
    d, f = w_g.shape[1], w_g.shape[2]
    rows = EXPERT_ROWS
    nblk = blk_expert.shape[0]
    c = xs.shape[2]
    hbm = pl.BlockSpec(memory_space=pl.ANY)
    return pl.pallas_call(
        functools.partial(_expert_kernel, nblk=nblk),
        grid_spec=pltpu.PrefetchScalarGridSpec(
            num_scalar_prefetch=4,
            grid=(1,),
            in_specs=[hbm, hbm, hbm, hbm],
            out_specs=hbm,
            scratch_shapes=[pltpu.VMEM((X_RING, 2, rows, c), jnp.uint32),
                            pltpu.VMEM((2, 2, rows, c), jnp.uint32),
                            pltpu.VMEM((W_RING, d, f), F32), pltpu.VMEM((W_RING, d, f), F32),
                            pltpu.VMEM((W_RING, f, d), F32),
                            pltpu.VMEM((d, f), BF16), pltpu.VMEM((d, f), BF16),
                            pltpu.VMEM((f, d), BF16),
                            pltpu.SemaphoreType.DMA((X_RING,)), pltpu.SemaphoreType.DMA((2,)),
                            pltpu.SemaphoreType.DMA((W_RING,))]),
        out_shape=jax.ShapeDtypeStruct((2, nblk * rows, c), jnp.uint32),
        compiler_params=_cparams("arbitrary"),
        name="experts",
    )(blk_expert, blk_run, run_expert, nblk_used, xs, w_g, w_u, w_d)


def _gather(ys, slot_t):
    _, n_slots, c = ys.shape
    k, t = slot_t.shape
    p = 2 * k * t
    win = _tile(p, SC_WINDOW)
    src = jnp.concatenate([slot_t, slot_t + n_slots], axis=0).reshape(1, p)
    mesh = plsc.VectorSubcoreMesh(core_axis_name="core", subcore_axis_name="subcore")

    @pl.kernel(out_type=jax.ShapeDtypeStruct((p, c), ys.dtype), mesh=mesh, scratch_types=[])
    def gather_rows(y_hbm, s_hbm, o_hbm):
        def body(s_vmem, o_vmem):
            pltpu.sync_copy(y_hbm.at[s_vmem.at[0]], o_vmem)

        pltpu.emit_pipeline(
            body, grid=(p // win,),
            in_specs=[pl.BlockSpec((1, win), lambda i: (0, i))],
            out_specs=[pl.BlockSpec((win, c), lambda i: (i, 0))],
            core_axis_name=("core", "subcore"),
            dimension_semantics=(pltpu.PARALLEL,))(s_hbm, o_hbm)

    return gather_rows(ys.reshape(2 * n_slots, c), src).reshape(2, k, t, c)


def _combine_kernel(y_ref, w_ref, x1_ref, h2_ref, mod_ref, wsg_ref, wsu_ref, wsd_ref, o_ref):
    hb = _unpack_halves(h2_ref[0], h2_ref[1])
    g = jnp.dot(hb, wsg_ref[...], preferred_element_type=F32)
    u = jnp.dot(hb, wsu_ref[...], preferred_element_type=F32)
    acc = jnp.dot((_silu(g) * u).astype(BF16), wsd_ref[...], preferred_element_type=F32)
    w = w_ref[...].T
    for kk in range(TOP_K):
        acc = acc + w[:, kk:kk + 1] * _unpack_halves(y_ref[0, kk], y_ref[1, kk]).astype(F32)
    g2 = mod_ref[0][5:6]
    o_ref[...] = x1_ref[...] + g2 * acc


def _combine(y_tok, w_tk, x1, h2p, mod, w_sg, w_su, w_sd, seq):
    t, d = x1.shape
    c = d // 4
    tc = _tile(seq, 256)
    tpb = seq // tc
    full = lambda shp: pl.BlockSpec(shp, lambda i: (0,) * len(shp))
    tok = lambda w: pl.BlockSpec((tc, w), lambda i: (i, 0))
    return pl.pallas_call(
        _combine_kernel,
        grid=(t // tc,),
        in_specs=[pl.BlockSpec((2, TOP_K, tc, c), lambda i: (0, 0, i, 0)),
                  pl.BlockSpec((TOP_K, tc), lambda i: (0, i)), tok(d),
                  pl.BlockSpec((2, tc, c), lambda i: (0, i, 0)),
                  pl.BlockSpec((1, N_ADA, d), lambda i: (i // tpb, 0, 0)),
                  full(w_sg.shape), full(w_su.shape), full(w_sd.shape)],
        out_specs=tok(d),
        out_shape=jax.ShapeDtypeStruct((t, d), F32),
        compiler_params=_cparams("arbitrary"),
        name="combine",
    )(y_tok, w_tk, x1, h2p, mod, w_sg.astype(BF16), w_su.astype(BF16), w_sd.astype(BF16))


def _layer(x, c, w_ada, b_ada, norm1_g, w_in, q_a_norm_g, w_uq, kv_a_norm_g, w_ukv,
           q_norm_g, k_norm_g, w_proj_attn, w_proj_fourier, w_out, norm2_g,
           w_router, router_bias, w_exp_gate, w_exp_up, w_exp_down,
           w_sh_gate, w_sh_up, w_sh_down):
    bsz, seq, d = x.shape
    t = bsz * seq
    e = w_router.shape[1]
    x2 = x.reshape(t, d)

    mod = _ada(c, w_ada, b_ada).reshape(bsz, N_ADA, d)
    q, k, v, zf, sa, sf = _inproj(x2, mod, norm1_g, w_in, q_a_norm_g, w_uq, kv_a_norm_g,
                                  w_ukv, q_norm_g, k_norm_g, bsz, seq)
    per_batch = lambda a: a.reshape(bsz, seq, a.shape[1])
    attn = _attention(per_batch(q), per_batch(k), v).reshape(t, N_HEADS * V_DIM)
    four = _fourier(zf.reshape(bsz, seq, zf.shape[1])).reshape(t, zf.shape[1])
    x1, h2, scores_t = _merge(attn, four, sa, sf, x2, mod, w_proj_attn, w_proj_fourier,
                              w_out, norm2_g, w_router, seq)

    idx_t, w_t, rank_t, cnt = _route(scores_t, router_bias)
    counts = cnt[:, 0].astype(jnp.int32)
    rows = EXPERT_ROWS
    nblk = -(-(t * TOP_K) // rows) + e
    padded = ((counts + rows - 1) // rows) * rows
    p_end = jnp.cumsum(padded)
    p_start = p_end - padded
    nblk_used = (p_end[-1] // rows).astype(jnp.int32)
    blk_start = jnp.arange(nblk, dtype=jnp.int32) * rows
    blk_first = jnp.minimum(blk_start, p_end[-1] - 1)
    hit = p_end[None, :] <= blk_first[:, None]
    blk_expert = jnp.clip(jnp.sum(hit.astype(jnp.int32), axis=1), 0, e - 1)

    slot_t = _slots(idx_t, rank_t, p_start.astype(jnp.int32))
    xs = _dispatch(h2, slot_t, nblk * rows)
    eid = jnp.arange(e, dtype=jnp.int32)
    used = counts > 0
    run = jnp.cumsum(used.astype(jnp.int32)) - 1
    blk_run = jnp.sum(jnp.where(blk_expert[:, None] == eid[None, :], run[None, :], 0),
                      axis=1).astype(jnp.int32)
    rid = jnp.arange(e + W_RING, dtype=jnp.int32)
    match = jnp.logical_and(used[None, :], run[None, :] == rid[:, None])
    run_expert = jnp.where(jnp.any(match, axis=1),
                           jnp.sum(jnp.where(match, eid[None, :], 0), axis=1), -1).astype(jnp.int32)
    ys = _experts(blk_expert, blk_run, run_expert, nblk_used.reshape(1), xs,
                  w_exp_gate, w_exp_up, w_exp_down)
    out = _combine(_gather(ys, slot_t), w_t, x1, h2, mod, w_sh_gate, w_sh_up, w_sh_down, seq)
    return out.reshape(bsz, seq, d)


def kernel(x, c, w_ada, b_ada, norm1_g, w_in, q_a_norm_g, w_uq, kv_a_norm_g, w_ukv, q_norm_g,
           k_norm_g, w_proj_attn, w_proj_fourier, w_out, norm2_g, w_router, router_bias,
           w_exp_gate, w_exp_up, w_exp_down, w_sh_gate, w_sh_up, w_sh_down):
    for l in range(w_ada.shape[0]):
        x = _layer(x, c, w_ada[l], b_ada[l], norm1_g[l], w_in[l], q_a_norm_g[l], w_uq[l],
                   kv_a_norm_g[l], w_ukv[l], q_norm_g[l], k_norm_g[l], w_proj_attn[l],
                   w_proj_fourier[l], w_out[l], norm2_g[l], w_router[l], router_bias[l],
                   w_exp_gate[l], w_exp_up[l], w_exp_down[l], w_sh_gate[l], w_sh_up[l],
                   w_sh_down[l])
    return x
```

```python
import functools
import math

import numpy as np
import jax
import jax.numpy as jnp
from jax import lax
from jax.experimental import pallas as pl
from jax.experimental.pallas import tpu as pltpu
from jax.experimental.pallas import tpu_sc as plsc

N_HEADS = 8
QK_NOPE = 64
QK_ROPE = 32
V_DIM = 64
FOURIER_GROUP = 64
TOP_K = 8
ROUTED_SCALE = 2.5
EPS = 1e-6
ROPE_THETA = 10000.0
N_ADA = 6

LANES = 128
EXPERT_ROWS = 512
X_RING = 4
W_RING = 3
X_CHUNKS = 8
SC_WINDOW = 128
VMEM_LIMIT = 48 * 1024 * 1024

F32 = jnp.float32
BF16 = jnp.bfloat16


def _cparams(*sem):
    return pltpu.CompilerParams(dimension_semantics=sem, vmem_limit_bytes=VMEM_LIMIT)


def _tile(n, pref):
    t = min(n, pref)
    assert n % t == 0, (n, pref)
    return t


def _silu(v):
    return v * jax.nn.sigmoid(v)


def _pack_halves(m):
    d = m.shape[1]
    lo = lax.bitcast_convert_type(m[:, :d // 2].astype(BF16).astype(F32), jnp.uint32)
    hi = lax.bitcast_convert_type(m[:, d // 2:].astype(BF16).astype(F32), jnp.uint32)
    w = (lo >> 16) | (hi & jnp.uint32(0xFFFF0000))
    return w[:, :d // 4], w[:, d // 4:]


def _unpack_halves(w0, w1):
    def lo(w):
        return lax.bitcast_convert_type(w << 16, F32)

    def hi(w):
        return lax.bitcast_convert_type(w & jnp.uint32(0xFFFF0000), F32)
    return jnp.concatenate([lo(w0), lo(w1), hi(w0), hi(w1)], axis=1).astype(BF16)


def _ada_kernel(c_ref, w_ref, b_ref, o_ref):
    a = _silu(c_ref[...])
    o_ref[...] = jnp.dot(a, w_ref[...], preferred_element_type=F32,
                         precision=lax.Precision.HIGHEST) + b_ref[...]


def _ada(c, w_ada, b_ada):
    bsz, d = c.shape
    n = w_ada.shape[1]
    tn = _tile(n, d)
    return pl.pallas_call(
        _ada_kernel,
        grid=(n // tn,),
        in_specs=[pl.BlockSpec((bsz, d), lambda j: (0, 0)),
                  pl.BlockSpec((d, tn), lambda j: (0, j)),
                  pl.BlockSpec((1, tn), lambda j: (0, j))],
        out_specs=pl.BlockSpec((bsz, tn), lambda j: (0, j)),
        out_shape=jax.ShapeDtypeStruct((bsz, n), F32),
        compiler_params=_cparams("arbitrary"),
        name="ada",
    )(c, w_ada, b_ada.reshape(1, n))


def _head_norm_rope(t, trot, a, b):
    ms = jnp.sum(t * t, axis=-1, keepdims=True) * (1.0 / (QK_NOPE + QK_ROPE))
    return (t * a + trot * b) * lax.rsqrt(ms + EPS)


def _inproj_kernel(x_ref, mod_ref, g1_ref, wa_ref, wf_ref, wga_ref, wgf_ref,
                   gq_ref, gkv_ref, wuq_ref, wkv_ref,
                   aq_ref, bq_ref, ak_ref, bk_ref,
                   q_ref, k_ref, v_ref, zf_ref, sa_ref, sf_ref, *, ql, kvl):
    x = x_ref[...]
    mod = mod_ref[0]
    sh1, sc1 = mod[0:1], mod[1:2]
    r = lax.rsqrt(jnp.mean(x * x, axis=-1, keepdims=True) + EPS)
    h = (x * r * g1_ref[...]) * (1.0 + sc1) + sh1
    hb = h.astype(BF16)

    zf_ref[...] = jnp.dot(hb, wf_ref[...], preferred_element_type=F32).astype(BF16)
    sa_ref[...] = jax.nn.sigmoid(
        jnp.dot(hb, wga_ref[...], preferred_element_type=F32)).astype(BF16)
    sf_ref[...] = jax.nn.sigmoid(
        jnp.dot(hb, wgf_ref[...], preferred_element_type=F32)).astype(BF16)

    za = jnp.dot(hb, wa_ref[...], preferred_element_type=F32)
    zq = za[:, :ql]
    cq = zq * lax.rsqrt(jnp.mean(zq * zq, axis=-1, keepdims=True) + EPS) * gq_ref[...]
    qall = jnp.dot(cq.astype(BF16), wuq_ref[...], preferred_element_type=F32)

    zk = za[:, ql:]
    kvn = zk[:, :kvl]
    rk = lax.rsqrt(jnp.mean(kvn * kvn, axis=-1, keepdims=True) + EPS)
    lane = lax.broadcasted_iota(jnp.int32, zk.shape, 1)
    u = zk * jnp.where(lane < kvl, rk, 1.0) * gkv_ref[...]
    kvall = jnp.dot(u.astype(BF16), wkv_ref[...], preferred_element_type=F32)

    aq, bq, ak, bk = aq_ref[...], bq_ref[...], ak_ref[...], bk_ref[...]
    hw = N_HEADS * LANES
    for hd in range(N_HEADS):
        lo, hi = hd * LANES, (hd + 1) * LANES
        q_ref[:, lo:hi] = _head_norm_rope(qall[:, lo:hi], qall[:, hw + lo:hw + hi],
                                       aq, bq).astype(BF16)
        k_ref[:, lo:hi] = _head_norm_rope(kvall[:, lo:hi], kvall[:, hw + lo:hw + hi],
                                       ak, bk).astype(BF16)
    v_ref[...] = kvall[:, 2 * hw:].astype(BF16)


def _rope_tables(seq):
    half = QK_ROPE // 2
    pos = np.arange(seq, dtype=np.float64)
    inv = ROPE_THETA ** (-np.arange(0, QK_ROPE, 2, dtype=np.float64) / QK_ROPE)
    ang = pos[:, None] * inv[None, :]
    c, s = np.cos(ang), np.sin(ang)
    cos = np.ones((seq, LANES)); sin = np.zeros((seq, LANES))
    cos[:, QK_NOPE:QK_NOPE + half] = c
    cos[:, QK_NOPE + half:QK_NOPE + QK_ROPE] = c
    sin[:, QK_NOPE:QK_NOPE + half] = -s
    sin[:, QK_NOPE + half:QK_NOPE + QK_ROPE] = s
    return jnp.asarray(cos, F32), jnp.asarray(sin, F32)


def _partner_columns(w):
    half = QK_ROPE // 2
    lo, mid, hi = QK_NOPE, QK_NOPE + half, QK_NOPE + QK_ROPE
    z = jnp.zeros_like(w)
    return jnp.concatenate([z[..., :lo], w[..., mid:hi], w[..., lo:mid], z[..., hi:]], axis=-1)


def _inproj(x2, mod, norm1_g, w_in, q_a_g, w_uq, kv_a_g, w_ukv, q_g, k_g, bsz, seq):
    t, d = x2.shape
    ql, kvl = q_a_g.shape[0], kv_a_g.shape[0]
    hq = QK_NOPE + QK_ROPE
    fw = w_in.shape[1] - ql - kvl - QK_ROPE - 2 * d
    o1, o2, o3, o4, o5 = ql, ql + kvl, ql + kvl + QK_ROPE, ql + kvl + QK_ROPE + fw, \
        ql + kvl + QK_ROPE + fw + d
    assert ql % LANES == 0 and kvl % LANES == 0

    wa = jnp.concatenate([w_in[:, :o3], jnp.zeros((d, LANES - QK_ROPE), F32)], axis=1).astype(BF16)
    wf = w_in[:, o3:o4].astype(BF16)
    wga = w_in[:, o4:o5].astype(BF16)
    wgf = w_in[:, o5:].astype(BF16)

    wuq = w_uq.reshape(ql, N_HEADS, hq)
    wuq = jnp.pad(wuq, ((0, 0), (0, 0), (0, LANES - hq)))
    wuq = jnp.concatenate([wuq.reshape(ql, N_HEADS * LANES),
                           _partner_columns(wuq).reshape(ql, N_HEADS * LANES)], axis=1).astype(BF16)
    wukv = w_ukv.reshape(kvl, N_HEADS, QK_NOPE + V_DIM)
    wk = jnp.pad(wukv[:, :, :QK_NOPE], ((0, 0), (0, 0), (0, LANES - QK_NOPE)))
    place = jnp.zeros((QK_ROPE, N_HEADS, LANES), F32)
    place = place.at[jnp.arange(QK_ROPE), :, QK_NOPE + jnp.arange(QK_ROPE)].set(1.0)
    wk = jnp.concatenate([wk, place, jnp.zeros((LANES - QK_ROPE, N_HEADS, LANES), F32)], axis=0)
    wv = jnp.concatenate([wukv[:, :, QK_NOPE:], jnp.zeros((LANES, N_HEADS, V_DIM), F32)], axis=0)
    wkv = jnp.concatenate([wk.reshape(kvl + LANES, N_HEADS * LANES),
                           _partner_columns(wk).reshape(kvl + LANES, N_HEADS * LANES),
                           wv.reshape(kvl + LANES, N_HEADS * V_DIM)], axis=1).astype(BF16)

    gkv = jnp.concatenate([kv_a_g, jnp.ones((LANES,), F32)]).reshape(1, kvl + LANES)
    pad = jnp.zeros((LANES - hq,), F32)
    qg = jnp.concatenate([q_g * (hq ** -0.5), pad])
    kg = jnp.concatenate([k_g, pad])
    cos, sin = _rope_tables(seq)
    aq, bq = qg[None, :] * cos, _partner_columns(qg)[None, :] * sin
    ak, bk = kg[None, :] * cos, _partner_columns(kg)[None, :] * sin

    tm = _tile(seq, 512)
    tpb = seq // tm
    full = lambda shp: pl.BlockSpec(shp, lambda i: (0,) * len(shp))
    tok = lambda w: pl.BlockSpec((tm, w), lambda i: (i, 0))
    rope = pl.BlockSpec((tm, LANES), lambda i: (i % tpb, 0))
    return pl.pallas_call(
        functools.partial(_inproj_kernel, ql=ql, kvl=kvl),
        grid=(t // tm,),
        in_specs=[tok(d),
                  pl.BlockSpec((1, N_ADA, d), lambda i: (i // tpb, 0, 0)),
                  full((1, d)), full(wa.shape), full(wf.shape), full(wga.shape), full(wgf.shape),
                  full((1, ql)), full(gkv.shape), full(wuq.shape), full(wkv.shape),
                  rope, rope, rope, rope],
        out_specs=[tok(N_HEADS * LANES), tok(N_HEADS * LANES), tok(N_HEADS * V_DIM), tok(fw), tok(d), tok(d)],
        out_shape=[jax.ShapeDtypeStruct((t, N_HEADS * LANES), BF16),
                   jax.ShapeDtypeStruct((t, N_HEADS * LANES), BF16),
                   jax.ShapeDtypeStruct((t, N_HEADS * V_DIM), BF16),
                   jax.ShapeDtypeStruct((t, fw), BF16),
                   jax.ShapeDtypeStruct((t, d), BF16),
                   jax.ShapeDtypeStruct((t, d), BF16)],
        compiler_params=_cparams("arbitrary"),
        name="inproj",
    )(x2, mod, norm1_g.reshape(1, d), wa, wf, wga, wgf, q_a_g.reshape(1, ql), gkv, wuq, wkv,
      aq, bq, ak, bk)


def _attn_kernel(q_ref, k_ref, v_ref, o_ref):
    for hd in range(N_HEADS):
        s = lax.dot_general(q_ref[0, :, hd * LANES:(hd + 1) * LANES],
                            k_ref[0, :, hd * LANES:(hd + 1) * LANES], (((1,), (1,)), ((), ())),
                            preferred_element_type=F32)
        m = jnp.max(s, axis=-1, keepdims=True)
        p = jnp.exp(s - m)
        l = jnp.sum(p, axis=-1, keepdims=True)
        o = jnp.dot(p.astype(BF16), v_ref[0, :, hd * V_DIM:(hd + 1) * V_DIM],
                    preferred_element_type=F32)
        o_ref[0, :, hd * V_DIM:(hd + 1) * V_DIM] = (o / l).astype(BF16)


def _attention(q, k, v):
    bsz, seq, _ = q.shape
    tq = _tile(seq, 512)
    return pl.pallas_call(
        _attn_kernel,
        grid=(bsz, seq // tq),
        in_specs=[pl.BlockSpec((1, tq, N_HEADS * LANES), lambda b, j: (b, j, 0)),
                  pl.BlockSpec((1, seq, N_HEADS * LANES), lambda b, j: (b, 0, 0)),
                  pl.BlockSpec((1, seq, N_HEADS * V_DIM), lambda b, j: (b, 0, 0))],
        out_specs=pl.BlockSpec((1, tq, N_HEADS * V_DIM), lambda b, j: (b, j, 0)),
        out_shape=jax.ShapeDtypeStruct((bsz, seq, N_HEADS * V_DIM), BF16),
        compiler_params=_cparams("arbitrary", "arbitrary"),
        name="attn",
    )(q, k, v)


def _fourier_kernel(z_ref, wc_ref, ws_ref, tab_ref, o_ref, u_ref, *, seq):
    @pl.when(pl.program_id(1) == 0)
    def _():
        z = z_ref[0]
        u_ref[:seq, :] = jnp.dot(z, wc_ref[...], preferred_element_type=F32).astype(BF16)
        u_ref[seq:, :] = jnp.dot(z, ws_ref[...], preferred_element_type=F32).astype(BF16)

    o_ref[0] = jnp.dot(tab_ref[...], u_ref[...], preferred_element_type=F32).astype(BF16)


def _fourier_tables(seq, fw):
    g = FOURIER_GROUP
    n = np.arange(seq, dtype=np.int64)
    ang = 2.0 * np.pi * ((n[:, None] * n[None, :]) % seq).astype(np.float64) / seq
    tab = np.concatenate([np.cos(ang), -np.sin(ang)], axis=1)
    c = np.arange(g, dtype=np.int64)
    angc = 2.0 * np.pi * ((c[:, None] * c[None, :]) % g).astype(np.float64) / g
    scale = 1.0 / math.sqrt(seq * g)
    eye = np.eye(fw // g)
    wc = np.kron(eye, np.cos(angc) * scale)
    ws = np.kron(eye, np.sin(angc) * scale)
    return (jnp.asarray(tab, F32).astype(BF16), jnp.asarray(wc, F32).astype(BF16),
            jnp.asarray(ws, F32).astype(BF16))


def _fourier(zf):
    bsz, seq, fw = zf.shape
    tab, wc, ws = _fourier_tables(seq, fw)
    tr = _tile(seq, 512)
    return pl.pallas_call(
        functools.partial(_fourier_kernel, seq=seq),
        grid=(bsz, seq // tr),
        in_specs=[pl.BlockSpec((1, seq, fw), lambda b, j: (b, 0, 0)),
                  pl.BlockSpec((fw, fw), lambda b, j: (0, 0)),
                  pl.BlockSpec((fw, fw), lambda b, j: (0, 0)),
                  pl.BlockSpec((tr, 2 * seq), lambda b, j: (j, 0))],
        out_specs=pl.BlockSpec((1, tr, fw), lambda b, j: (b, j, 0)),
        out_shape=jax.ShapeDtypeStruct((bsz, seq, fw), BF16),
        scratch_shapes=[pltpu.VMEM((2 * seq, fw), BF16)],
        compiler_params=_cparams("arbitrary", "arbitrary"),
        name="fourier",
    )(zf, wc, ws, tab)


def _merge_kernel(a_ref, f_ref, sa_ref, sf_ref, x_ref, mod_ref, wpa_ref, wpf_ref, wo_ref,
                  g2_ref, wrh_ref, wrl_ref, x1_ref, h2_ref, sc_ref):
    ya = jnp.dot(a_ref[...], wpa_ref[...], preferred_element_type=F32)
    yf = jnp.dot(f_ref[...], wpf_ref[...], preferred_element_type=F32)
    merged = sa_ref[...].astype(F32) * ya + sf_ref[...].astype(F32) * yf
    mod = mod_ref[0]
    g1, sh2, sc2 = mod[2:3], mod[3:4], mod[4:5]
    x1 = x_ref[...] + g1 * jnp.dot(merged.astype(BF16), wo_ref[...], preferred_element_type=F32)
    x1_ref[...] = x1
    r = lax.rsqrt(jnp.mean(x1 * x1, axis=-1, keepdims=True) + EPS)
    h2 = (x1 * r * g2_ref[...]) * (1.0 + sc2) + sh2
    h2_ref[0], h2_ref[1] = _pack_halves(h2)
    hh = h2.astype(BF16)
    hl = (h2 - hh.astype(F32)).astype(BF16)
    nt = (((1,), (1,)), ((), ()))
    lt = (lax.dot_general(wrh_ref[...], hh, nt, preferred_element_type=F32)
          + lax.dot_general(wrh_ref[...], hl, nt, preferred_element_type=F32)
          + lax.dot_general(wrl_ref[...], hh, nt, preferred_element_type=F32))
    sc_ref[...] = jax.nn.sigmoid(lt)


def _merge(attn, four, sa, sf, x2, mod, w_pa, w_pf, w_out, norm2_g, w_router, seq):
    t, d = x2.shape
    e = w_router.shape[1]
    wrt = w_router.T
    wrh = wrt.astype(BF16)
    wrl = (wrt - wrh.astype(F32)).astype(BF16)
    tm = _tile(seq, 512)
    tpb = seq // tm
    full = lambda shp: pl.BlockSpec(shp, lambda i: (0,) * len(shp))
    tok = lambda w: pl.BlockSpec((tm, w), lambda i: (i, 0))
    return pl.pallas_call(
        _merge_kernel,
        grid=(t // tm,),
        in_specs=[tok(attn.shape[1]), tok(four.shape[1]), tok(d), tok(d), tok(d),
                  pl.BlockSpec((1, N_ADA, d), lambda i: (i // tpb, 0, 0)),
                  full(w_pa.shape), full(w_pf.shape), full(w_out.shape), full((1, d)),
                  full((e, d)), full((e, d))],
        out_specs=[tok(d), pl.BlockSpec((2, tm, d // 4), lambda i: (0, i, 0)),
                   pl.BlockSpec((e, tm), lambda i: (0, i))],
        out_shape=[jax.ShapeDtypeStruct((t, d), F32),
                   jax.ShapeDtypeStruct((2, t, d // 4), jnp.uint32),
                   jax.ShapeDtypeStruct((e, t), F32)],
        compiler_params=_cparams("arbitrary"),
        name="merge",
    )(attn, four, sa, sf, x2, mod, w_pa.astype(BF16), w_pf.astype(BF16), w_out.astype(BF16),
      norm2_g.reshape(1, d), wrh, wrl)


def _route_kernel(s_ref, b_ref, tri_ref, idx_ref, w_ref, rank_ref, cnt_ref, carry_ref):
    @pl.when(pl.program_id(0) == 0)
    def _():
        carry_ref[...] = jnp.zeros_like(carry_ref)

    sc = s_ref[...]
    e, tr = sc.shape
    row = lax.broadcasted_iota(jnp.int32, (e, tr), 0)
    v = sc + b_ref[...]
    sel = jnp.zeros((e, tr), F32)
    idxs, ws = [], []
    for _ in range(TOP_K):
        m = jnp.max(v, axis=0, keepdims=True)
        idx = jnp.min(jnp.where(v == m, row, e), axis=0, keepdims=True)
        oh = row == idx
        ws.append(jnp.sum(jnp.where(oh, sc, 0.0), axis=0, keepdims=True))
        idxs.append(idx)
        v = jnp.where(oh, -jnp.inf, v)
        sel = sel + oh.astype(F32)
    wsum = ws[0]
    for w in ws[1:]:
        wsum = wsum + w
    selb = sel.astype(BF16)
    cum = jnp.dot(selb, tri_ref[...], preferred_element_type=F32) + carry_ref[...]
    for kk in range(TOP_K):
        oh = row == idxs[kk]
        rk = jnp.sum(jnp.where(oh, cum, 0.0), axis=0, keepdims=True)
        idx_ref[kk:kk + 1, :] = idxs[kk]
        rank_ref[kk:kk + 1, :] = rk.astype(jnp.int32)
        w_ref[kk:kk + 1, :] = ws[kk] / wsum * ROUTED_SCALE
    tot = carry_ref[...] + jnp.dot(selb, jnp.ones((tr, tr), BF16), preferred_element_type=F32)
    carry_ref[...] = tot
    cnt_ref[...] = tot


def _route(scores_t, router_bias):
    e, t = scores_t.shape
    tr = _tile(t, 256)
    tri = jnp.asarray(np.triu(np.ones((tr, tr), np.float32), 1), BF16)
    bias = jnp.broadcast_to(router_bias.reshape(e, 1), (e, tr)).astype(F32)
    blk = pl.BlockSpec((TOP_K, tr), lambda i: (0, i))
    return pl.pallas_call(
        _route_kernel,
        grid=(t // tr,),
        in_specs=[pl.BlockSpec((e, tr), lambda i: (0, i)),
                  pl.BlockSpec((e, tr), lambda i: (0, 0)),
                  pl.BlockSpec((tr, tr), lambda i: (0, 0))],
        out_specs=[blk, blk, blk, pl.BlockSpec((e, tr), lambda i: (0, 0))],
        out_shape=[jax.ShapeDtypeStruct((TOP_K, t), jnp.int32),
                   jax.ShapeDtypeStruct((TOP_K, t), F32),
                   jax.ShapeDtypeStruct((TOP_K, t), jnp.int32),
                   jax.ShapeDtypeStruct((e, tr), F32)],
        scratch_shapes=[pltpu.VMEM((e, tr), F32)],
        compiler_params=_cparams("arbitrary"),
        name="route",
    )(scores_t, bias, tri)


def _slots_kernel(idx_ref, rank_ref, ps_ref, slot_ref):
    ps = ps_ref[...]
    row = lax.broadcasted_iota(jnp.int32, ps.shape, 0)
    for kk in range(TOP_K):
        oh = row == idx_ref[kk:kk + 1, :]
        start = jnp.sum(jnp.where(oh, ps, 0), axis=0, keepdims=True)
        slot_ref[kk:kk + 1, :] = start + rank_ref[kk:kk + 1, :]


def _slots(idx_t, rank_t, p_start):
    _, t = idx_t.shape
    e = p_start.shape[0]
    ts = _tile(t, 512)
    ps = jnp.broadcast_to(p_start.reshape(e, 1), (e, ts))
    blk = pl.BlockSpec((TOP_K, ts), lambda i: (0, i))
    return pl.pallas_call(
        _slots_kernel,
        grid=(t // ts,),
        in_specs=[blk, blk, pl.BlockSpec((e, ts), lambda i: (0, 0))],
        out_specs=blk,
        out_shape=jax.ShapeDtypeStruct((TOP_K, t), jnp.int32),
        compiler_params=_cparams("arbitrary"),
        name="slots",
    )(idx_t, rank_t, ps)


def _dispatch(h2p, slot_t, n_slots):
    _, t, c = h2p.shape
    k = slot_t.shape[0]
    win = _tile(2 * t, SC_WINDOW)
    rows = h2p.reshape(2 * t, c)
    dest = jnp.concatenate([slot_t, slot_t + n_slots], axis=1)
    mesh = plsc.VectorSubcoreMesh(core_axis_name="core", subcore_axis_name="subcore")

    @pl.kernel(out_type=jax.ShapeDtypeStruct((2 * n_slots, c), h2p.dtype), mesh=mesh,
               scratch_types=[])
    def scatter_rows(x_hbm, s_hbm, o_hbm):
        def body(x_vmem, s_vmem):
            pltpu.sync_copy(x_vmem, o_hbm.at[s_vmem.at[0]])

        pltpu.emit_pipeline(
            body, grid=(2 * t // win, k),
            in_specs=[pl.BlockSpec((win, c), lambda i, j: (i, 0)),
                      pl.BlockSpec((1, win), lambda i, j: (j, i))],
            out_specs=[], core_axis_name=("core", "subcore"),
            dimension_semantics=(pltpu.PARALLEL, pltpu.ARBITRARY))(x_hbm, s_hbm)

    return scatter_rows(rows, dest).reshape(2, n_slots, c)


def _expert_kernel(be_ref, run_ref, rex_ref, nv_ref, nu_ref, xs_hbm, wg_hbm, wu_hbm, wd_hbm, ys_hbm,
                   xbuf, ybuf, wgf, wuf, wdf, wgb, wub, wdb, xsem, ysem, wsem, *, nblk):
    nu = nu_ref[0]
    rows = EXPERT_ROWS

    def x_stream(blk, slot, start):
        r0 = pl.multiple_of(blk * rows, rows)
        cr = rows // X_CHUNKS
        for ch in range(X_CHUNKS):
            @pl.when(ch * cr < nv_ref[blk])
            def _():
                for h in range(2):
                    cp = pltpu.make_async_copy(xs_hbm.at[h, pl.ds(r0 + ch * cr, cr)],
                                               xbuf.at[slot, h, pl.ds(ch * cr, cr)],
                                               xsem.at[slot])
                    cp.start() if start else cp.wait()

    def y_copies(blk, slot):
        r0 = pl.multiple_of(blk * rows, rows)
        return [pltpu.make_async_copy(ybuf.at[slot, h], ys_hbm.at[h, pl.ds(r0, rows)],
                                      ysem.at[slot]) for h in range(2)]

    def weight_copies(e, which):
        return (pltpu.make_async_copy(wg_hbm.at[e], wgf.at[which], wsem.at[which]),
                pltpu.make_async_copy(wu_hbm.at[e], wuf.at[which], wsem.at[which]),
                pltpu.make_async_copy(wd_hbm.at[e], wdf.at[which], wsem.at[which]))

    for j in range(W_RING - 1):
        @pl.when(rex_ref[j] >= 0)
        def _():
            for cp in weight_copies(rex_ref[j], j):
                cp.start(priority=1)
    xbuf[...] = jnp.zeros_like(xbuf)
    for j in range(X_RING - 1):
        @pl.when(j < nu)
        def _():
            x_stream(j, j, True)

    def step(i, carry):
        ahead = i + X_RING - 1

        @pl.when(ahead < nu)
        def _():
            x_stream(ahead, ahead % X_RING, True)

        prev = be_ref[jnp.maximum(i - 1, 0)]

        @pl.when(jnp.logical_or(i == 0, be_ref[i] != prev))
        def _():
            r = run_ref[i]
            par = r % W_RING
            for cp in weight_copies(be_ref[i], par):
                cp.wait()
            wgb[...] = wgf[par].astype(BF16)
            wub[...] = wuf[par].astype(BF16)
            wdb[...] = wdf[par].astype(BF16)
            later = rex_ref[r + W_RING - 1]

            @pl.when(later >= 0)
            def _():
                for cp in weight_copies(later, (r + W_RING - 1) % W_RING):
                    cp.start(priority=1)

        slot = i % X_RING
        x_stream(i, slot, False)
        x = _unpack_halves(xbuf[slot, 0], xbuf[slot, 1])
        g = jnp.dot(x, wgb[...], preferred_element_type=F32)
        u = jnp.dot(x, wub[...], preferred_element_type=F32)
        a = (_silu(g) * u).astype(BF16)
        y0, y1 = _pack_halves(jnp.dot(a, wdb[...], preferred_element_type=F32))

        out = i % 2

        @pl.when(i >= 2)
        def _():
            for cp in y_copies(i - 2, out):
                cp.wait()
        ybuf[out, 0] = y0
        ybuf[out, 1] = y1
        for cp in y_copies(i, out):
            cp.start()
        return carry

    lax.fori_loop(0, nu, step, 0)

    @pl.when(nu >= 2)
    def _():
        for cp in y_copies(nu - 2, nu % 2):
            cp.wait()
    for cp in y_copies(nu - 1, (nu - 1) % 2):
        cp.wait()

    ybuf[0] = jnp.zeros_like(ybuf[0])

    def zero_start(j, carry):
        for cp in y_copies(j, 0):
            cp.start()
        return carry

    def zero_wait(j, carry):
        for cp in y_copies(j, 0):
            cp.wait()
        return carry
    lax.fori_loop(nu, nblk, zero_start, 0)
    lax.fori_loop(nu, nblk, zero_wait, 0)


def _experts(blk_expert, blk_run, run_expert, blk_valid, nblk_used, xs, w_g, w_u, w_d):
    d, f = w_g.shape[1], w_g.shape[2]
    rows = EXPERT_ROWS
    nblk = blk_expert.shape[0]
    c = xs.shape[2]
    hbm = pl.BlockSpec(memory_space=pl.ANY)
    return pl.pallas_call(
        functools.partial(_expert_kernel, nblk=nblk),
        grid_spec=pltpu.PrefetchScalarGridSpec(
            num_scalar_prefetch=5,
            grid=(1,),
            in_specs=[hbm, hbm, hbm, hbm],
            out_specs=hbm,
            scratch_shapes=[pltpu.VMEM((X_RING, 2, rows, c), jnp.uint32),
                            pltpu.VMEM((2, 2, rows, c), jnp.uint32),
                            pltpu.VMEM((W_RING, d, f), F32), pltpu.VMEM((W_RING, d, f), F32),
                            pltpu.VMEM((W_RING, f, d), F32),
                            pltpu.VMEM((d, f), BF16), pltpu.VMEM((d, f), BF16),
                            pltpu.VMEM((f, d), BF16),
                            pltpu.SemaphoreType.DMA((X_RING,)), pltpu.SemaphoreType.DMA((2,)),
                            pltpu.SemaphoreType.DMA((W_RING,))]),
        out_shape=jax.ShapeDtypeStruct((2, nblk * rows, c), jnp.uint32),
        compiler_params=_cparams("arbitrary"),
        name="experts",
    )(blk_expert, blk_run, run_expert, blk_valid, nblk_used, xs, w_g, w_u, w_d)


def _gather(ys, slot_t):
    _, n_slots, c = ys.shape
    k, t = slot_t.shape
    p = 2 * k * t
    win = _tile(p, SC_WINDOW)
    src = jnp.concatenate([slot_t, slot_t + n_slots], axis=0).reshape(1, p)
    mesh = plsc.VectorSubcoreMesh(core_axis_name="core", subcore_axis_name="subcore")

    @pl.kernel(out_type=jax.ShapeDtypeStruct((p, c), ys.dtype), mesh=mesh, scratch_types=[])
    def gather_rows(y_hbm, s_hbm, o_hbm):
        def body(s_vmem, o_vmem):
            pltpu.sync_copy(y_hbm.at[s_vmem.at[0]], o_vmem)

        pltpu.emit_pipeline(
            body, grid=(p // win,),
            in_specs=[pl.BlockSpec((1, win), lambda i: (0, i))],
            out_specs=[pl.BlockSpec((win, c), lambda i: (i, 0))],
            core_axis_name=("core", "subcore"),
            dimension_semantics=(pltpu.PARALLEL,))(s_hbm, o_hbm)

    return gather_rows(ys.reshape(2 * n_slots, c), src).reshape(2, k, t, c)


def _combine_kernel(y_ref, w_ref, x1_ref, h2_ref, mod_ref, wsg_ref, wsu_ref, wsd_ref, o_ref):
    hb = _unpack_halves(h2_ref[0], h2_ref[1])
    g = jnp.dot(hb, wsg_ref[...], preferred_element_type=F32)
    u = jnp.dot(hb, wsu_ref[...], preferred_element_type=F32)
    acc = jnp.dot((_silu(g) * u).astype(BF16), wsd_ref[...], preferred_element_type=F32)
    w = w_ref[...].T
    for kk in range(TOP_K):
        acc = acc + w[:, kk:kk + 1] * _unpack_halves(y_ref[0, kk], y_ref[1, kk]).astype(F32)
    g2 = mod_ref[0][5:6]
    o_ref[...] = x1_ref[...] + g2 * acc


def _combine(y_tok, w_tk, x1, h2p, mod, w_sg, w_su, w_sd, seq):
    t, d = x1.shape
    c = d // 4
    tc = _tile(seq, 256)
    tpb = seq // tc
    full = lambda shp: pl.BlockSpec(shp, lambda i: (0,) * len(shp))
    tok = lambda w: pl.BlockSpec((tc, w), lambda i: (i, 0))
    return pl.pallas_call(
        _combine_kernel,
        grid=(t // tc,),
        in_specs=[pl.BlockSpec((2, TOP_K, tc, c), lambda i: (0, 0, i, 0)),
                  pl.BlockSpec((TOP_K, tc), lambda i: (0, i)), tok(d),
                  pl.BlockSpec((2, tc, c), lambda i: (0, i, 0)),
                  pl.BlockSpec((1, N_ADA, d), lambda i: (i // tpb, 0, 0)),
                  full(w_sg.shape), full(w_su.shape), full(w_sd.shape)],
        out_specs=tok(d),
        out_shape=jax.ShapeDtypeStruct((t, d), F32),
        compiler_params=_cparams("arbitrary"),
        name="combine",
    )(y_tok, w_tk, x1, h2p, mod, w_sg.astype(BF16), w_su.astype(BF16), w_sd.astype(BF16))


def _layer(x, c, w_ada, b_ada, norm1_g, w_in, q_a_norm_g, w_uq, kv_a_norm_g, w_ukv,
           q_norm_g, k_norm_g, w_proj_attn, w_proj_fourier, w_out, norm2_g,
           w_router, router_bias, w_exp_gate, w_exp_up, w_exp_down,
           w_sh_gate, w_sh_up, w_sh_down):
    bsz, seq, d = x.shape
    t = bsz * seq
    e = w_router.shape[1]
    x2 = x.reshape(t, d)

    mod = _ada(c, w_ada, b_ada).reshape(bsz, N_ADA, d)
    q, k, v, zf, sa, sf = _inproj(x2, mod, norm1_g, w_in, q_a_norm_g, w_uq, kv_a_norm_g,
                                  w_ukv, q_norm_g, k_norm_g, bsz, seq)
    per_batch = lambda a: a.reshape(bsz, seq, a.shape[1])
    attn = _attention(per_batch(q), per_batch(k), per_batch(v)).reshape(t, N_HEADS * V_DIM)
    four = _fourier(zf.reshape(bsz, seq, zf.shape[1])).reshape(t, zf.shape[1])
    x1, h2, scores_t = _merge(attn, four, sa, sf, x2, mod, w_proj_attn, w_proj_fourier,
                              w_out, norm2_g, w_router, seq)

    idx_t, w_t, rank_t, cnt = _route(scores_t, router_bias)
    counts = cnt[:, 0].astype(jnp.int32)
    rows = EXPERT_ROWS
    nblk = -(-(t * TOP_K) // rows) + e
    padded = ((counts + rows - 1) // rows) * rows
    p_end = jnp.cumsum(padded)
    p_start = p_end - padded
    nblk_used = (p_end[-1] // rows).astype(jnp.int32)
    blk_start = jnp.arange(nblk, dtype=jnp.int32) * rows
    blk_first = jnp.minimum(blk_start, p_end[-1] - 1)
    hit = p_end[None, :] <= blk_first[:, None]
    blk_expert = jnp.clip(jnp.sum(hit.astype(jnp.int32), axis=1), 0, e - 1)

    slot_t = _slots(idx_t, rank_t, p_start.astype(jnp.int32))
    xs = _dispatch(h2, slot_t, nblk * rows)
    eid = jnp.arange(e, dtype=jnp.int32)
    used = counts > 0
    run = jnp.cumsum(used.astype(jnp.int32)) - 1
    blk_run = jnp.sum(jnp.where(blk_expert[:, None] == eid[None, :], run[None, :], 0),
                      axis=1).astype(jnp.int32)
    rid = jnp.arange(e + W_RING, dtype=jnp.int32)
    match = jnp.logical_and(used[None, :], run[None, :] == rid[:, None])
    run_expert = jnp.where(jnp.any(match, axis=1),
                           jnp.sum(jnp.where(match, eid[None, :], 0), axis=1), -1).astype(jnp.int32)
    seg_end = (p_start + counts).astype(jnp.int32)
    blk_end = jnp.sum(jnp.where(blk_expert[:, None] == eid[None, :], seg_end[None, :], 0), axis=1)
    blk_valid = jnp.clip(blk_end - blk_start, 0, rows).astype(jnp.int32)
    ys = _experts(blk_expert, blk_run, run_expert, blk_valid, nblk_used.reshape(1), xs,
                  w_exp_gate, w_exp_up, w_exp_down)
    out = _combine(_gather(ys, slot_t), w_t, x1, h2, mod, w_sh_gate, w_sh_up, w_sh_down, seq)
    return out.reshape(bsz, seq, d)


def kernel(x, c, w_ada, b_ada, norm1_g, w_in, q_a_norm_g, w_uq, kv_a_norm_g, w_ukv, q_norm_g,
           k_norm_g, w_proj_attn, w_proj_fourier, w_out, norm2_g, w_router, router_bias,
           w_exp_gate, w_exp_up, w_exp_down, w_sh_gate, w_sh_up, w_sh_down):
    for l in range(w_ada.shape[0]):
        x = _layer(x, c, w_ada[l], b_ada[l], norm1_g[l], w_in[l], q_a_norm_g[l], w_uq[l],
                   kv_a_norm_g[l], w_ukv[l], q_norm_g[l], k_norm_g[l], w_proj_attn[l],
                   w_proj_fourier[l], w_out[l], norm2_g[l], w_router[l], router_bias[l],
                   w_exp_gate[l], w_exp_up[l], w_exp_down[l], w_sh_gate[l], w_sh_up[l],
                   w_sh_down[l])
    return x
```

```python
import functools
import math

import numpy as np
import jax
import jax.numpy as jnp
from jax import lax
from jax.experimental import pallas as pl
from jax.experimental.pallas import tpu as pltpu
from jax.experimental.pallas import tpu_sc as plsc

N_HEADS = 8
QK_NOPE = 64
QK_ROPE = 32
V_DIM = 64
FOURIER_GROUP = 64
TOP_K = 8
ROUTED_SCALE = 2.5
EPS = 1e-6
ROPE_THETA = 10000.0
N_ADA = 6

LANES = 128
ROW_UNIT = 128
MAX_UNITS = 8
X_RING = 4
W_RING = 3
SC_WINDOW = 128
VMEM_LIMIT = 48 * 1024 * 1024

F32 = jnp.float32
BF16 = jnp.bfloat16


def _cparams(*sem):
    return pltpu.CompilerParams(dimension_semantics=sem, vmem_limit_bytes=VMEM_LIMIT)


def _tile(n, pref):
    t = min(n, pref)
    assert n % t == 0, (n, pref)
    return t


def _silu(v):
    return v * jax.nn.sigmoid(v)


def _pack_halves(m):
    d = m.shape[1]
    lo = lax.bitcast_convert_type(m[:, :d // 2].astype(BF16).astype(F32), jnp.uint32)
    hi = lax.bitcast_convert_type(m[:, d // 2:].astype(BF16).astype(F32), jnp.uint32)
    w = (lo >> 16) | (hi & jnp.uint32(0xFFFF0000))
    return w[:, :d // 4], w[:, d // 4:]


def _unpack_halves(w0, w1):
    def lo(w):
        return lax.bitcast_convert_type(w << 16, F32)

    def hi(w):
        return lax.bitcast_convert_type(w & jnp.uint32(0xFFFF0000), F32)
    return jnp.concatenate([lo(w0), lo(w1), hi(w0), hi(w1)], axis=1).astype(BF16)


def _ada_kernel(c_ref, w_ref, b_ref, o_ref):
    a = _silu(c_ref[...])
    o_ref[...] = jnp.dot(a, w_ref[...], preferred_element_type=F32,
                         precision=lax.Precision.HIGHEST) + b_ref[...]


def _ada(c, w_ada, b_ada):
    bsz, d = c.shape
    n = w_ada.shape[1]
    tn = _tile(n, d)
    return pl.pallas_call(
        _ada_kernel,
        grid=(n // tn,),
        in_specs=[pl.BlockSpec((bsz, d), lambda j: (0, 0)),
                  pl.BlockSpec((d, tn), lambda j: (0, j)),
                  pl.BlockSpec((1, tn), lambda j: (0, j))],
        out_specs=pl.BlockSpec((bsz, tn), lambda j: (0, j)),
        out_shape=jax.ShapeDtypeStruct((bsz, n), F32),
        compiler_params=_cparams("arbitrary"),
        name="ada",
    )(c, w_ada, b_ada.reshape(1, n))


def _head_norm_rope(t, trot, a, b):
    ms = jnp.sum(t * t, axis=-1, keepdims=True) * (1.0 / (QK_NOPE + QK_ROPE))
    return (t * a + trot * b) * lax.rsqrt(ms + EPS)


def _inproj_kernel(x_ref, mod_ref, g1_ref, wa_ref, wf_ref, wga_ref, wgf_ref,
                   gq_ref, gkv_ref, wuq_ref, wkv_ref,
                   aq_ref, bq_ref, ak_ref, bk_ref,
                   q_ref, k_ref, v_ref, zf_ref, sa_ref, sf_ref, *, ql, kvl):
    x = x_ref[...]
    mod = mod_ref[0]
    sh1, sc1 = mod[0:1], mod[1:2]
    r = lax.rsqrt(jnp.mean(x * x, axis=-1, keepdims=True) + EPS)
    h = (x * r * g1_ref[...]) * (1.0 + sc1) + sh1
    hb = h.astype(BF16)

    zf_ref[...] = jnp.dot(hb, wf_ref[...], preferred_element_type=F32).astype(BF16)
    sa_ref[...] = jax.nn.sigmoid(
        jnp.dot(hb, wga_ref[...], preferred_element_type=F32)).astype(BF16)
    sf_ref[...] = jax.nn.sigmoid(
        jnp.dot(hb, wgf_ref[...], preferred_element_type=F32)).astype(BF16)

    za = jnp.dot(hb, wa_ref[...], preferred_element_type=F32)
    zq = za[:, :ql]
    cq = zq * lax.rsqrt(jnp.mean(zq * zq, axis=-1, keepdims=True) + EPS) * gq_ref[...]
    qall = jnp.dot(cq.astype(BF16), wuq_ref[...], preferred_element_type=F32)

    zk = za[:, ql:]
    kvn = zk[:, :kvl]
    rk = lax.rsqrt(jnp.mean(kvn * kvn, axis=-1, keepdims=True) + EPS)
    lane = lax.broadcasted_iota(jnp.int32, zk.shape, 1)
    u = zk * jnp.where(lane < kvl, rk, 1.0) * gkv_ref[...]
    kvall = jnp.dot(u.astype(BF16), wkv_ref[...], preferred_element_type=F32)

    aq, bq, ak, bk = aq_ref[...], bq_ref[...], ak_ref[...], bk_ref[...]
    hw = N_HEADS * LANES
    for hd in range(N_HEADS):
        lo, hi = hd * LANES, (hd + 1) * LANES
        q_ref[:, lo:hi] = _head_norm_rope(qall[:, lo:hi], qall[:, hw + lo:hw + hi],
                                       aq, bq).astype(BF16)
        k_ref[:, lo:hi] = _head_norm_rope(kvall[:, lo:hi], kvall[:, hw + lo:hw + hi],
                                       ak, bk).astype(BF16)
    v_ref[...] = kvall[:, 2 * hw:].astype(BF16)


def _rope_tables(seq):
    half = QK_ROPE // 2
    pos = np.arange(seq, dtype=np.float64)
    inv = ROPE_THETA ** (-np.arange(0, QK_ROPE, 2, dtype=np.float64) / QK_ROPE)
    ang = pos[:, None] * inv[None, :]
    c, s = np.cos(ang), np.sin(ang)
    cos = np.ones((seq, LANES)); sin = np.zeros((seq, LANES))
    cos[:, QK_NOPE:QK_NOPE + half] = c
    cos[:, QK_NOPE + half:QK_NOPE + QK_ROPE] = c
    sin[:, QK_NOPE:QK_NOPE + half] = -s
    sin[:, QK_NOPE + half:QK_NOPE + QK_ROPE] = s
    return jnp.asarray(cos, F32), jnp.asarray(sin, F32)


def _partner_columns(w):
    half = QK_ROPE // 2
    lo, mid, hi = QK_NOPE, QK_NOPE + half, QK_NOPE + QK_ROPE
    z = jnp.zeros_like(w)
    return jnp.concatenate([z[..., :lo], w[..., mid:hi], w[..., lo:mid], z[..., hi:]], axis=-1)


def _inproj(x2, mod, norm1_g, w_in, q_a_g, w_uq, kv_a_g, w_ukv, q_g, k_g, bsz, seq):
    t, d = x2.shape
    ql, kvl = q_a_g.shape[0], kv_a_g.shape[0]
    hq = QK_NOPE + QK_ROPE
    fw = w_in.shape[1] - ql - kvl - QK_ROPE - 2 * d
    o1, o2, o3, o4, o5 = ql, ql + kvl, ql + kvl + QK_ROPE, ql + kvl + QK_ROPE + fw, \
        ql + kvl + QK_ROPE + fw + d
    assert ql % LANES == 0 and kvl % LANES == 0

    wa = jnp.concatenate([w_in[:, :o3], jnp.zeros((d, LANES - QK_ROPE), F32)], axis=1).astype(BF16)
    wf = w_in[:, o3:o4].astype(BF16)
    wga = w_in[:, o4:o5].astype(BF16)
    wgf = w_in[:, o5:].astype(BF16)

    wuq = w_uq.reshape(ql, N_HEADS, hq)
    wuq = jnp.pad(wuq, ((0, 0), (0, 0), (0, LANES - hq)))
    wuq = jnp.concatenate([wuq.reshape(ql, N_HEADS * LANES),
                           _partner_columns(wuq).reshape(ql, N_HEADS * LANES)], axis=1).astype(BF16)
    wukv = w_ukv.reshape(kvl, N_HEADS, QK_NOPE + V_DIM)
    wk = jnp.pad(wukv[:, :, :QK_NOPE], ((0, 0), (0, 0), (0, LANES - QK_NOPE)))
    place = jnp.zeros((QK_ROPE, N_HEADS, LANES), F32)
    place = place.at[jnp.arange(QK_ROPE), :, QK_NOPE + jnp.arange(QK_ROPE)].set(1.0)
    wk = jnp.concatenate([wk, place, jnp.zeros((LANES - QK_ROPE, N_HEADS, LANES), F32)], axis=0)
    wv = jnp.concatenate([wukv[:, :, QK_NOPE:], jnp.zeros((LANES, N_HEADS, V_DIM), F32)], axis=0)
    wkv = jnp.concatenate([wk.reshape(kvl + LANES, N_HEADS * LANES),
                           _partner_columns(wk).reshape(kvl + LANES, N_HEADS * LANES),
                           wv.reshape(kvl + LANES, N_HEADS * V_DIM)], axis=1).astype(BF16)

    gkv = jnp.concatenate([kv_a_g, jnp.ones((LANES,), F32)]).reshape(1, kvl + LANES)
    pad = jnp.zeros((LANES - hq,), F32)
    qg = jnp.concatenate([q_g * (hq ** -0.5), pad])
    kg = jnp.concatenate([k_g, pad])
    cos, sin = _rope_tables(seq)
    aq, bq = qg[None, :] * cos, _partner_columns(qg)[None, :] * sin
    ak, bk = kg[None, :] * cos, _partner_columns(kg)[None, :] * sin

    tm = _tile(seq, 512)
    tpb = seq // tm
    full = lambda shp: pl.BlockSpec(shp, lambda i: (0,) * len(shp))
    tok = lambda w: pl.BlockSpec((tm, w), lambda i: (i, 0))
    rope = pl.BlockSpec((tm, LANES), lambda i: (i % tpb, 0))
    return pl.pallas_call(
        functools.partial(_inproj_kernel, ql=ql, kvl=kvl),
        grid=(t // tm,),
        in_specs=[tok(d),
                  pl.BlockSpec((1, N_ADA, d), lambda i: (i // tpb, 0, 0)),
                  full((1, d)), full(wa.shape), full(wf.shape), full(wga.shape), full(wgf.shape),
                  full((1, ql)), full(gkv.shape), full(wuq.shape), full(wkv.shape),
                  rope, rope, rope, rope],
        out_specs=[tok(N_HEADS * LANES), tok(N_HEADS * LANES), tok(N_HEADS * V_DIM), tok(fw), tok(d), tok(d)],
        out_shape=[jax.ShapeDtypeStruct((t, N_HEADS * LANES), BF16),
                   jax.ShapeDtypeStruct((t, N_HEADS * LANES), BF16),
                   jax.ShapeDtypeStruct((t, N_HEADS * V_DIM), BF16),
                   jax.ShapeDtypeStruct((t, fw), BF16),
                   jax.ShapeDtypeStruct((t, d), BF16),
                   jax.ShapeDtypeStruct((t, d), BF16)],
        compiler_params=_cparams("arbitrary"),
        name="inproj",
    )(x2, mod, norm1_g.reshape(1, d), wa, wf, wga, wgf, q_a_g.reshape(1, ql), gkv, wuq, wkv,
      aq, bq, ak, bk)


def _attn_kernel(q_ref, k_ref, v_ref, o_ref):
    for hd in range(N_HEADS):
        s = lax.dot_general(q_ref[0, :, hd * LANES:(hd + 1) * LANES],
                            k_ref[0, :, hd * LANES:(hd + 1) * LANES], (((1,), (1,)), ((), ())),
                            preferred_element_type=F32)
        m = jnp.max(s, axis=-1, keepdims=True)
        p = jnp.exp(s - m)
        l = jnp.sum(p, axis=-1, keepdims=True)
        o = jnp.dot(p.astype(BF16), v_ref[0, :, hd * V_DIM:(hd + 1) * V_DIM],
                    preferred_element_type=F32)
        o_ref[0, :, hd * V_DIM:(hd + 1) * V_DIM] = (o / l).astype(BF16)


def _attention(q, k, v):
    bsz, seq, _ = q.shape
    tq = _tile(seq, 512)
    return pl.pallas_call(
        _attn_kernel,
        grid=(bsz, seq // tq),
        in_specs=[pl.BlockSpec((1, tq, N_HEADS * LANES), lambda b, j: (b, j, 0)),
                  pl.BlockSpec((1, seq, N_HEADS * LANES), lambda b, j: (b, 0, 0)),
                  pl.BlockSpec((1, seq, N_HEADS * V_DIM), lambda b, j: (b, 0, 0))],
        out_specs=pl.BlockSpec((1, tq, N_HEADS * V_DIM), lambda b, j: (b, j, 0)),
        out_shape=jax.ShapeDtypeStruct((bsz, seq, N_HEADS * V_DIM), BF16),
        compiler_params=_cparams("arbitrary", "arbitrary"),
        name="attn",
    )(q, k, v)


def _fourier_kernel(z_ref, wc_ref, ws_ref, tab_ref, o_ref, u_ref, *, seq):
    @pl.when(pl.program_id(1) == 0)
    def _():
        z = z_ref[0]
        u_ref[:seq, :] = jnp.dot(z, wc_ref[...], preferred_element_type=F32).astype(BF16)
        u_ref[seq:, :] = jnp.dot(z, ws_ref[...], preferred_element_type=F32).astype(BF16)

    o_ref[0] = jnp.dot(tab_ref[...], u_ref[...], preferred_element_type=F32).astype(BF16)


def _fourier_tables(seq, fw):
    g = FOURIER_GROUP
    n = np.arange(seq, dtype=np.int64)
    ang = 2.0 * np.pi * ((n[:, None] * n[None, :]) % seq).astype(np.float64) / seq
    tab = np.concatenate([np.cos(ang), -np.sin(ang)], axis=1)
    c = np.arange(g, dtype=np.int64)
    angc = 2.0 * np.pi * ((c[:, None] * c[None, :]) % g).astype(np.float64) / g
    scale = 1.0 / math.sqrt(seq * g)
    eye = np.eye(fw // g)
    wc = np.kron(eye, np.cos(angc) * scale)
    ws = np.kron(eye, np.sin(angc) * scale)
    return (jnp.asarray(tab, F32).astype(BF16), jnp.asarray(wc, F32).astype(BF16),
            jnp.asarray(ws, F32).astype(BF16))


def _fourier(zf):
    bsz, seq, fw = zf.shape
    tab, wc, ws = _fourier_tables(seq, fw)
    tr = _tile(seq, 512)
    return pl.pallas_call(
        functools.partial(_fourier_kernel, seq=seq),
        grid=(bsz, seq // tr),
        in_specs=[pl.BlockSpec((1, seq, fw), lambda b, j: (b, 0, 0)),
                  pl.BlockSpec((fw, fw), lambda b, j: (0, 0)),
                  pl.BlockSpec((fw, fw), lambda b, j: (0, 0)),
                  pl.BlockSpec((tr, 2 * seq), lambda b, j: (j, 0))],
        out_specs=pl.BlockSpec((1, tr, fw), lambda b, j: (b, j, 0)),
        out_shape=jax.ShapeDtypeStruct((bsz, seq, fw), BF16),
        scratch_shapes=[pltpu.VMEM((2 * seq, fw), BF16)],
        compiler_params=_cparams("arbitrary", "arbitrary"),
        name="fourier",
    )(zf, wc, ws, tab)


def _merge_kernel(a_ref, f_ref, sa_ref, sf_ref, x_ref, mod_ref, wpa_ref, wpf_ref, wo_ref,
                  g2_ref, wrh_ref, wrl_ref, x1_ref, h2_ref, sc_ref):
    ya = jnp.dot(a_ref[...], wpa_ref[...], preferred_element_type=F32)
    yf = jnp.dot(f_ref[...], wpf_ref[...], preferred_element_type=F32)
    merged = sa_ref[...].astype(F32) * ya + sf_ref[...].astype(F32) * yf
    mod = mod_ref[0]
    g1, sh2, sc2 = mod[2:3], mod[3:4], mod[4:5]
    x1 = x_ref[...] + g1 * jnp.dot(merged.astype(BF16), wo_ref[...], preferred_element_type=F32)
    x1_ref[...] = x1
    r = lax.rsqrt(jnp.mean(x1 * x1, axis=-1, keepdims=True) + EPS)
    h2 = (x1 * r * g2_ref[...]) * (1.0 + sc2) + sh2
    h2_ref[0], h2_ref[1] = _pack_halves(h2)
    hh = h2.astype(BF16)
    hl = (h2 - hh.astype(F32)).astype(BF16)
    nt = (((1,), (1,)), ((), ()))
    lt = (lax.dot_general(wrh_ref[...], hh, nt, preferred_element_type=F32)
          + lax.dot_general(wrh_ref[...], hl, nt, preferred_element_type=F32)
          + lax.dot_general(wrl_ref[...], hh, nt, preferred_element_type=F32))
    sc_ref[...] = jax.nn.sigmoid(lt)


def _merge(attn, four, sa, sf, x2, mod, w_pa, w_pf, w_out, norm2_g, w_router, seq):
    t, d = x2.shape
    e = w_router.shape[1]
    wrt = w_router.T
    wrh = wrt.astype(BF16)
    wrl = (wrt - wrh.astype(F32)).astype(BF16)
    tm = _tile(seq, 512)
    tpb = seq // tm
    full = lambda shp: pl.BlockSpec(shp, lambda i: (0,) * len(shp))
    tok = lambda w: pl.BlockSpec((tm, w), lambda i: (i, 0))
    return pl.pallas_call(
        _merge_kernel,
        grid=(t // tm,),
        in_specs=[tok(attn.shape[1]), tok(four.shape[1]), tok(d), tok(d), tok(d),
                  pl.BlockSpec((1, N_ADA, d), lambda i: (i // tpb, 0, 0)),
                  full(w_pa.shape), full(w_pf.shape), full(w_out.shape), full((1, d)),
                  full((e, d)), full((e, d))],
        out_specs=[tok(d), pl.BlockSpec((2, tm, d // 4), lambda i: (0, i, 0)),
                   pl.BlockSpec((e, tm), lambda i: (0, i))],
        out_shape=[jax.ShapeDtypeStruct((t, d), F32),
                   jax.ShapeDtypeStruct((2, t, d // 4), jnp.uint32),
                   jax.ShapeDtypeStruct((e, t), F32)],
        compiler_params=_cparams("arbitrary"),
        name="merge",
    )(attn, four, sa, sf, x2, mod, w_pa.astype(BF16), w_pf.astype(BF16), w_out.astype(BF16),
      norm2_g.reshape(1, d), wrh, wrl)


def _route_kernel(s_ref, b_ref, tri_ref, idx_ref, w_ref, rank_ref, cnt_ref, carry_ref):
    @pl.when(pl.program_id(0) == 0)
    def _():
        carry_ref[...] = jnp.zeros_like(carry_ref)

    sc = s_ref[...]
    e, tr = sc.shape
    row = lax.broadcasted_iota(jnp.int32, (e, tr), 0)
    v = sc + b_ref[...]
    sel = jnp.zeros((e, tr), F32)
    idxs, ws = [], []
    for _ in range(TOP_K):
        m = jnp.max(v, axis=0, keepdims=True)
        idx = jnp.min(jnp.where(v == m, row, e), axis=0, keepdims=True)
        oh = row == idx
        ws.append(jnp.sum(jnp.where(oh, sc, 0.0), axis=0, keepdims=True))
        idxs.append(idx)
        v = jnp.where(oh, -jnp.inf, v)
        sel = sel + oh.astype(F32)
    wsum = ws[0]
    for w in ws[1:]:
        wsum = wsum + w
    selb = sel.astype(BF16)
    cum = jnp.dot(selb, tri_ref[...], preferred_element_type=F32) + carry_ref[...]
    for kk in range(TOP_K):
        oh = row == idxs[kk]
        rk = jnp.sum(jnp.where(oh, cum, 0.0), axis=0, keepdims=True)
        idx_ref[kk:kk + 1, :] = idxs[kk]
        rank_ref[kk:kk + 1, :] = rk.astype(jnp.int32)
        w_ref[kk:kk + 1, :] = ws[kk] / wsum * ROUTED_SCALE
    tot = carry_ref[...] + jnp.dot(selb, jnp.ones((tr, tr), BF16), preferred_element_type=F32)
    carry_ref[...] = tot
    cnt_ref[...] = tot


def _route(scores_t, router_bias):
    e, t = scores_t.shape
    tr = _tile(t, 256)
    tri = jnp.asarray(np.triu(np.ones((tr, tr), np.float32), 1), BF16)
    bias = jnp.broadcast_to(router_bias.reshape(e, 1), (e, tr)).astype(F32)
    blk = pl.BlockSpec((TOP_K, tr), lambda i: (0, i))
    return pl.pallas_call(
        _route_kernel,
        grid=(t // tr,),
        in_specs=[pl.BlockSpec((e, tr), lambda i: (0, i)),
                  pl.BlockSpec((e, tr), lambda i: (0, 0)),
                  pl.BlockSpec((tr, tr), lambda i: (0, 0))],
        out_specs=[blk, blk, blk, pl.BlockSpec((e, tr), lambda i: (0, 0))],
        out_shape=[jax.ShapeDtypeStruct((TOP_K, t), jnp.int32),
                   jax.ShapeDtypeStruct((TOP_K, t), F32),
                   jax.ShapeDtypeStruct((TOP_K, t), jnp.int32),
                   jax.ShapeDtypeStruct((e, tr), F32)],
        scratch_shapes=[pltpu.VMEM((e, tr), F32)],
        compiler_params=_cparams("arbitrary"),
        name="route",
    )(scores_t, bias, tri)


def _slots_kernel(idx_ref, rank_ref, ps_ref, slot_ref):
    ps = ps_ref[...]
    row = lax.broadcasted_iota(jnp.int32, ps.shape, 0)
    for kk in range(TOP_K):
        oh = row == idx_ref[kk:kk + 1, :]
        start = jnp.sum(jnp.where(oh, ps, 0), axis=0, keepdims=True)
        slot_ref[kk:kk + 1, :] = start + rank_ref[kk:kk + 1, :]


def _slots(idx_t, rank_t, p_start):
    _, t = idx_t.shape
    e = p_start.shape[0]
    ts = _tile(t, 512)
    ps = jnp.broadcast_to(p_start.reshape(e, 1), (e, ts))
    blk = pl.BlockSpec((TOP_K, ts), lambda i: (0, i))
    return pl.pallas_call(
        _slots_kernel,
        grid=(t // ts,),
        in_specs=[blk, blk, pl.BlockSpec((e, ts), lambda i: (0, 0))],
        out_specs=blk,
        out_shape=jax.ShapeDtypeStruct((TOP_K, t), jnp.int32),
        compiler_params=_cparams("arbitrary"),
        name="slots",
    )(idx_t, rank_t, ps)


def _dispatch(h2p, slot_t, n_slots):
    _, t, c = h2p.shape
    k = slot_t.shape[0]
    win = _tile(2 * t, SC_WINDOW)
    rows = h2p.reshape(2 * t, c)
    dest = jnp.concatenate([slot_t, slot_t + n_slots], axis=1)
    mesh = plsc.VectorSubcoreMesh(core_axis_name="core", subcore_axis_name="subcore")

    @pl.kernel(out_type=jax.ShapeDtypeStruct((2 * n_slots, c), h2p.dtype), mesh=mesh,
               scratch_types=[])
    def scatter_rows(x_hbm, s_hbm, o_hbm):
        def body(x_vmem, s_vmem):
            pltpu.sync_copy(x_vmem, o_hbm.at[s_vmem.at[0]])

        pltpu.emit_pipeline(
            body, grid=(2 * t // win, k),
            in_specs=[pl.BlockSpec((win, c), lambda i, j: (i, 0)),
                      pl.BlockSpec((1, win), lambda i, j: (j, i))],
            out_specs=[], core_axis_name=("core", "subcore"),
            dimension_semantics=(pltpu.PARALLEL, pltpu.ARBITRARY))(x_hbm, s_hbm)

    return scatter_rows(rows, dest).reshape(2, n_slots, c)


def _expert_kernel(be_ref, run_ref, rex_ref, bst_ref, bsz_ref, nu_ref, xs_hbm, wg_hbm, wu_hbm,
                   wd_hbm, ys_hbm, xbuf, ybuf, wgf, wuf, wdf, wgb, wub, wdb, xsem, ysem, wsem,
                   *, n_slots):
    nu = nu_ref[0]
    unit = ROW_UNIT

    def x_stream(blk, slot, start):
        r0 = pl.multiple_of(bst_ref[blk], unit)
        for ch in range(MAX_UNITS):
            @pl.when(ch * unit < bsz_ref[blk])
            def _():
                for h in range(2):
                    cp = pltpu.make_async_copy(xs_hbm.at[h, pl.ds(r0 + ch * unit, unit)],
                                               xbuf.at[slot, h, pl.ds(ch * unit, unit)],
                                               xsem.at[slot])
                    cp.start() if start else cp.wait()

    def y_copies(r0, slot, m):
        return [pltpu.make_async_copy(ybuf.at[slot, h, pl.ds(0, m)],
                                      ys_hbm.at[h, pl.ds(pl.multiple_of(r0, unit), m)],
                                      ysem.at[slot]) for h in range(2)]

    def y_wait(blk, slot):
        for n in range(1, MAX_UNITS + 1):
            @pl.when(bsz_ref[blk] == n * unit)
            def _():
                for cp in y_copies(bst_ref[blk], slot, n * unit):
                    cp.wait()

    def weight_copies(e, which):
        return (pltpu.make_async_copy(wg_hbm.at[e], wgf.at[which], wsem.at[which]),
                pltpu.make_async_copy(wu_hbm.at[e], wuf.at[which], wsem.at[which]),
                pltpu.make_async_copy(wd_hbm.at[e], wdf.at[which], wsem.at[which]))

    for j in range(W_RING - 1):
        @pl.when(rex_ref[j] >= 0)
        def _():
            for cp in weight_copies(rex_ref[j], j):
                cp.start(priority=1)
    for j in range(X_RING - 1):
        @pl.when(j < nu)
        def _():
            x_stream(j, j, True)

    def step(i, carry):
        ahead = i + X_RING - 1

        @pl.when(ahead < nu)
        def _():
            x_stream(ahead, ahead % X_RING, True)

        prev = be_ref[jnp.maximum(i - 1, 0)]

        @pl.when(jnp.logical_or(i == 0, be_ref[i] != prev))
        def _():
            r = run_ref[i]
            par = r % W_RING
            for cp in weight_copies(be_ref[i], par):
                cp.wait()
            wgb[...] = wgf[par].astype(BF16)
            wub[...] = wuf[par].astype(BF16)
            wdb[...] = wdf[par].astype(BF16)
            later = rex_ref[r + W_RING - 1]

            @pl.when(later >= 0)
            def _():
                for cp in weight_copies(later, (r + W_RING - 1) % W_RING):
                    cp.start(priority=1)

        slot = i % X_RING
        x_stream(i, slot, False)
        out = i % 2

        @pl.when(i >= 2)
        def _():
            y_wait(i - 2, out)

        for n in range(1, MAX_UNITS + 1):
            @pl.when(bsz_ref[i] == n * unit)
            def _():
                m = n * unit
                x = _unpack_halves(xbuf[slot, 0, pl.ds(0, m)], xbuf[slot, 1, pl.ds(0, m)])
                g = jnp.dot(x, wgb[...], preferred_element_type=F32)
                u = jnp.dot(x, wub[...], preferred_element_type=F32)
                a = (_silu(g) * u).astype(BF16)
                y0, y1 = _pack_halves(jnp.dot(a, wdb[...], preferred_element_type=F32))
                ybuf[out, 0, pl.ds(0, m)] = y0
                ybuf[out, 1, pl.ds(0, m)] = y1
                for cp in y_copies(bst_ref[i], out, m):
                    cp.start()
        return carry

    lax.fori_loop(0, nu, step, 0)

    @pl.when(nu >= 2)
    def _():
        y_wait(nu - 2, nu % 2)
    y_wait(nu - 1, (nu - 1) % 2)

    ybuf[0, :, pl.ds(0, unit)] = jnp.zeros((2, unit, ybuf.shape[3]), ybuf.dtype)
    used = bst_ref[nu - 1] + bsz_ref[nu - 1]
    spare = (n_slots - used) // unit

    def zero_start(j, carry):
        for cp in y_copies(used + j * unit, 0, unit):
            cp.start()
        return carry

    def zero_wait(j, carry):
        for cp in y_copies(used + j * unit, 0, unit):
            cp.wait()
        return carry
    lax.fori_loop(0, spare, zero_start, 0)
    lax.fori_loop(0, spare, zero_wait, 0)


def _experts(blk_expert, blk_run, run_expert, blk_start, blk_size, nblk_used, xs, w_g, w_u, w_d):
    d, f = w_g.shape[1], w_g.shape[2]
    rows = ROW_UNIT * MAX_UNITS
    n_slots = xs.shape[1]
    c = xs.shape[2]
    hbm = pl.BlockSpec(memory_space=pl.ANY)
    return pl.pallas_call(
        functools.partial(_expert_kernel, n_slots=n_slots),
        grid_spec=pltpu.PrefetchScalarGridSpec(
            num_scalar_prefetch=6,
            grid=(1,),
            in_specs=[hbm, hbm, hbm, hbm],
            out_specs=hbm,
            scratch_shapes=[pltpu.VMEM((X_RING, 2, rows, c), jnp.uint32),
                            pltpu.VMEM((2, 2, rows, c), jnp.uint32),
                            pltpu.VMEM((W_RING, d, f), F32), pltpu.VMEM((W_RING, d, f), F32),
                            pltpu.VMEM((W_RING, f, d), F32),
                            pltpu.VMEM((d, f), BF16), pltpu.VMEM((d, f), BF16),
                            pltpu.VMEM((f, d), BF16),
                            pltpu.SemaphoreType.DMA((X_RING,)), pltpu.SemaphoreType.DMA((2,)),
                            pltpu.SemaphoreType.DMA((W_RING,))]),
        out_shape=jax.ShapeDtypeStruct((2, n_slots, c), jnp.uint32),
        compiler_params=_cparams("arbitrary"),
        name="experts",
    )(blk_expert, blk_run, run_expert, blk_start, blk_size, nblk_used, xs, w_g, w_u, w_d)


def _gather(ys, slot_t):
    _, n_slots, c = ys.shape
    k, t = slot_t.shape
    p = 2 * k * t
    win = _tile(p, SC_WINDOW)
    src = jnp.concatenate([slot_t, slot_t + n_slots], axis=0).reshape(1, p)
    mesh = plsc.VectorSubcoreMesh(core_axis_name="core", subcore_axis_name="subcore")

    @pl.kernel(out_type=jax.ShapeDtypeStruct((p, c), ys.dtype), mesh=mesh, scratch_types=[])
    def gather_rows(y_hbm, s_hbm, o_hbm):
        def body(s_vmem, o_vmem):
            pltpu.sync_copy(y_hbm.at[s_vmem.at[0]], o_vmem)

        pltpu.emit_pipeline(
            body, grid=(p // win,),
            in_specs=[pl.BlockSpec((1, win), lambda i: (0, i))],
            out_specs=[pl.BlockSpec((win, c), lambda i: (i, 0))],
            core_axis_name=("core", "subcore"),
            dimension_semantics=(pltpu.PARALLEL,))(s_hbm, o_hbm)

    return gather_rows(ys.reshape(2 * n_slots, c), src).reshape(2, k, t, c)


def _combine_kernel(y_ref, w_ref, x1_ref, h2_ref, mod_ref, wsg_ref, wsu_ref, wsd_ref, o_ref):
    hb = _unpack_halves(h2_ref[0], h2_ref[1])
    g = jnp.dot(hb, wsg_ref[...], preferred_element_type=F32)
    u = jnp.dot(hb, wsu_ref[...], preferred_element_type=F32)
    acc = jnp.dot((_silu(g) * u).astype(BF16), wsd_ref[...], preferred_element_type=F32)
    w = w_ref[...].T
    for kk in range(TOP_K):
        acc = acc + w[:, kk:kk + 1] * _unpack_halves(y_ref[0, kk], y_ref[1, kk]).astype(F32)
    g2 = mod_ref[0][5:6]
    o_ref[...] = x1_ref[...] + g2 * acc


def _combine(y_tok, w_tk, x1, h2p, mod, w_sg, w_su, w_sd, seq):
    t, d = x1.shape
    c = d // 4
    tc = _tile(seq, 256)
    tpb = seq // tc
    full = lambda shp: pl.BlockSpec(shp, lambda i: (0,) * len(shp))
    tok = lambda w: pl.BlockSpec((tc, w), lambda i: (i, 0))
    return pl.pallas_call(
        _combine_kernel,
        grid=(t // tc,),
        in_specs=[pl.BlockSpec((2, TOP_K, tc, c), lambda i: (0, 0, i, 0)),
                  pl.BlockSpec((TOP_K, tc), lambda i: (0, i)), tok(d),
                  pl.BlockSpec((2, tc, c), lambda i: (0, i, 0)),
                  pl.BlockSpec((1, N_ADA, d), lambda i: (i // tpb, 0, 0)),
                  full(w_sg.shape), full(w_su.shape), full(w_sd.shape)],
        out_specs=tok(d),
        out_shape=jax.ShapeDtypeStruct((t, d), F32),
        compiler_params=_cparams("arbitrary"),
        name="combine",
    )(y_tok, w_tk, x1, h2p, mod, w_sg.astype(BF16), w_su.astype(BF16), w_sd.astype(BF16))


def _layer(x, c, w_ada, b_ada, norm1_g, w_in, q_a_norm_g, w_uq, kv_a_norm_g, w_ukv,
           q_norm_g, k_norm_g, w_proj_attn, w_proj_fourier, w_out, norm2_g,
           w_router, router_bias, w_exp_gate, w_exp_up, w_exp_down,
           w_sh_gate, w_sh_up, w_sh_down):
    bsz, seq, d = x.shape
    t = bsz * seq
    e = w_router.shape[1]
    x2 = x.reshape(t, d)

    mod = _ada(c, w_ada, b_ada).reshape(bsz, N_ADA, d)
    q, k, v, zf, sa, sf = _inproj(x2, mod, norm1_g, w_in, q_a_norm_g, w_uq, kv_a_norm_g,
                                  w_ukv, q_norm_g, k_norm_g, bsz, seq)
    per_batch = lambda a: a.reshape(bsz, seq, a.shape[1])
    attn = _attention(per_batch(q), per_batch(k), per_batch(v)).reshape(t, N_HEADS * V_DIM)
    four = _fourier(zf.reshape(bsz, seq, zf.shape[1])).reshape(t, zf.shape[1])
    x1, h2, scores_t = _merge(attn, four, sa, sf, x2, mod, w_proj_attn, w_proj_fourier,
                              w_out, norm2_g, w_router, seq)

    idx_t, w_t, rank_t, cnt = _route(scores_t, router_bias)
    counts = cnt[:, 0].astype(jnp.int32)
    unit, bmax = ROW_UNIT, ROW_UNIT * MAX_UNITS
    n_slots = t * TOP_K + e * unit
    padded = ((counts + unit - 1) // unit) * unit
    p_end = jnp.cumsum(padded)
    p_start = p_end - padded
    eid = jnp.arange(e, dtype=jnp.int32)
    nb = (padded + bmax - 1) // bmax
    b_end = jnp.cumsum(nb)
    b_first = b_end - nb
    nblk_used = b_end[-1].astype(jnp.int32)
    nblk = -(-(t * TOP_K) // bmax) + e
    bid = jnp.minimum(jnp.arange(nblk, dtype=jnp.int32), nblk_used - 1)
    blk_expert = jnp.clip(jnp.sum((b_end[None, :] <= bid[:, None]).astype(jnp.int32), axis=1),
                          0, e - 1)
    onehot = blk_expert[:, None] == eid[None, :]
    look = lambda tab: jnp.sum(jnp.where(onehot, tab[None, :], 0), axis=1)
    piece = bid - look(b_first)
    blk_start = (look(p_start) + piece * bmax).astype(jnp.int32)
    blk_size = jnp.clip(look(padded) - piece * bmax, 0, bmax).astype(jnp.int32)

    slot_t = _slots(idx_t, rank_t, p_start.astype(jnp.int32))
    xs = _dispatch(h2, slot_t, n_slots)
    used = counts > 0
    run = jnp.cumsum(used.astype(jnp.int32)) - 1
    blk_run = look(run).astype(jnp.int32)
    rid = jnp.arange(e + W_RING, dtype=jnp.int32)
    match = jnp.logical_and(used[None, :], run[None, :] == rid[:, None])
    run_expert = jnp.where(jnp.any(match, axis=1),
                           jnp.sum(jnp.where(match, eid[None, :], 0), axis=1), -1).astype(jnp.int32)
    ys = _experts(blk_expert, blk_run, run_expert, blk_start, blk_size, nblk_used.reshape(1), xs,
                  w_exp_gate, w_exp_up, w_exp_down)
    out = _combine(_gather(ys, slot_t), w_t, x1, h2, mod, w_sh_gate, w_sh_up, w_sh_down, seq)
    return out.reshape(bsz, seq, d)


def kernel(x, c, w_ada, b_ada, norm1_g, w_in, q_a_norm_g, w_uq, kv_a_norm_g, w_ukv, q_norm_g,
           k_norm_g, w_proj_attn, w_proj_fourier, w_out, norm2_g, w_router, router_bias,
           w_exp_gate, w_exp_up, w_exp_down, w_sh_gate, w_sh_up, w_sh_down):
    for l in range(w_ada.shape[0]):
        x = _layer(x, c, w_ada[l], b_ada[l], norm1_g[l], w_in[l], q_a_norm_g[l], w_uq[l],
                   kv_a_norm_g[l], w_ukv[l], q_norm_g[l], k_norm_g[l], w_proj_attn[l],
                   w_proj_fourier[l], w_out[l], norm2_g[l], w_router[l], router_bias[l],
                   w_exp_gate[l], w_exp_up[l], w_exp_down[l], w_sh_gate[l], w_sh_up[l],
                   w_sh_down[l])
    return x
```

```python
import functools
import math

import numpy as np
import jax
import jax.numpy as jnp
from jax import lax
from jax.experimental import pallas as pl
from jax.experimental.pallas import tpu as pltpu
from jax.experimental.pallas import tpu_sc as plsc

N_HEADS = 8
QK_NOPE = 64
QK_ROPE = 32
V_DIM = 64
FOURIER_GROUP = 64
TOP_K = 8
ROUTED_SCALE = 2.5
EPS = 1e-6
ROPE_THETA = 10000.0
N_ADA = 6

LANES = 128
ROW_UNIT = 128
MAX_UNITS = 8
X_RING = 4
W_RING = 3
SC_WINDOW = 128
SC_PACK_ROWS = 16
SC_LANES = 16
VMEM_LIMIT = 48 * 1024 * 1024

F32 = jnp.float32
BF16 = jnp.bfloat16


def _cparams(*sem):
    return pltpu.CompilerParams(dimension_semantics=sem, vmem_limit_bytes=VMEM_LIMIT)


def _tile(n, pref):
    t = min(n, pref)
    assert n % t == 0, (n, pref)
    return t


def _silu(v):
    return v * jax.nn.sigmoid(v)


def _pack_halves(m):
    d = m.shape[1]
    lo = lax.bitcast_convert_type(m[:, :d // 2].astype(BF16).astype(F32), jnp.uint32)
    hi = lax.bitcast_convert_type(m[:, d // 2:].astype(BF16).astype(F32), jnp.uint32)
    w = (lo >> 16) | (hi & jnp.uint32(0xFFFF0000))
    return w[:, :d // 4], w[:, d // 4:]


def _unpack_halves(w0, w1):
    def lo(w):
        return lax.bitcast_convert_type(w << 16, F32)

    def hi(w):
        return lax.bitcast_convert_type(w & jnp.uint32(0xFFFF0000), F32)
    return jnp.concatenate([lo(w0), lo(w1), hi(w0), hi(w1)], axis=1).astype(BF16)


def _ada_kernel(c_ref, w_ref, b_ref, o_ref):
    a = _silu(c_ref[...])
    o_ref[...] = jnp.dot(a, w_ref[...], preferred_element_type=F32,
                         precision=lax.Precision.HIGHEST) + b_ref[...]


def _ada(c, w_ada, b_ada):
    bsz, d = c.shape
    n = w_ada.shape[1]
    tn = _tile(n, d)
    return pl.pallas_call(
        _ada_kernel,
        grid=(n // tn,),
        in_specs=[pl.BlockSpec((bsz, d), lambda j: (0, 0)),
                  pl.BlockSpec((d, tn), lambda j: (0, j)),
                  pl.BlockSpec((1, tn), lambda j: (0, j))],
        out_specs=pl.BlockSpec((bsz, tn), lambda j: (0, j)),
        out_shape=jax.ShapeDtypeStruct((bsz, n), F32),
        compiler_params=_cparams("arbitrary"),
        name="ada",
    )(c, w_ada, b_ada.reshape(1, n))


def _head_norm_rope(t, trot, a, b):
    ms = jnp.sum(t * t, axis=-1, keepdims=True) * (1.0 / (QK_NOPE + QK_ROPE))
    return (t * a + trot * b) * lax.rsqrt(ms + EPS)


def _inproj_kernel(x_ref, mod_ref, g1_ref, wa_ref, wf_ref, wga_ref, wgf_ref,
                   gq_ref, gkv_ref, wuq_ref, wkv_ref,
                   aq_ref, bq_ref, ak_ref, bk_ref,
                   q_ref, k_ref, v_ref, zf_ref, sa_ref, sf_ref, *, ql, kvl):
    x = x_ref[...]
    mod = mod_ref[0]
    sh1, sc1 = mod[0:1], mod[1:2]
    r = lax.rsqrt(jnp.mean(x * x, axis=-1, keepdims=True) + EPS)
    h = (x * r * g1_ref[...]) * (1.0 + sc1) + sh1
    hb = h.astype(BF16)

    zf_ref[...] = jnp.dot(hb, wf_ref[...], preferred_element_type=F32).astype(BF16)
    sa_ref[...] = jax.nn.sigmoid(
        jnp.dot(hb, wga_ref[...], preferred_element_type=F32)).astype(BF16)
    sf_ref[...] = jax.nn.sigmoid(
        jnp.dot(hb, wgf_ref[...], preferred_element_type=F32)).astype(BF16)

    za = jnp.dot(hb, wa_ref[...], preferred_element_type=F32)
    zq = za[:, :ql]
    cq = zq * lax.rsqrt(jnp.mean(zq * zq, axis=-1, keepdims=True) + EPS) * gq_ref[...]
    qall = jnp.dot(cq.astype(BF16), wuq_ref[...], preferred_element_type=F32)

    zk = za[:, ql:]
    kvn = zk[:, :kvl]
    rk = lax.rsqrt(jnp.mean(kvn * kvn, axis=-1, keepdims=True) + EPS)
    lane = lax.broadcasted_iota(jnp.int32, zk.shape, 1)
    u = zk * jnp.where(lane < kvl, rk, 1.0) * gkv_ref[...]
    kvall = jnp.dot(u.astype(BF16), wkv_ref[...], preferred_element_type=F32)

    aq, bq, ak, bk = aq_ref[...], bq_ref[...], ak_ref[...], bk_ref[...]
    hw = N_HEADS * LANES
    for hd in range(N_HEADS):
        lo, hi = hd * LANES, (hd + 1) * LANES
        q_ref[:, lo:hi] = _head_norm_rope(qall[:, lo:hi], qall[:, hw + lo:hw + hi],
                                       aq, bq).astype(BF16)
        k_ref[:, lo:hi] = _head_norm_rope(kvall[:, lo:hi], kvall[:, hw + lo:hw + hi],
                                       ak, bk).astype(BF16)
    v_ref[...] = kvall[:, 2 * hw:].astype(BF16)


def _rope_tables(seq):
    half = QK_ROPE // 2
    pos = np.arange(seq, dtype=np.float64)
    inv = ROPE_THETA ** (-np.arange(0, QK_ROPE, 2, dtype=np.float64) / QK_ROPE)
    ang = pos[:, None] * inv[None, :]
    c, s = np.cos(ang), np.sin(ang)
    cos = np.ones((seq, LANES)); sin = np.zeros((seq, LANES))
    cos[:, QK_NOPE:QK_NOPE + half] = c
    cos[:, QK_NOPE + half:QK_NOPE + QK_ROPE] = c
    sin[:, QK_NOPE:QK_NOPE + half] = -s
    sin[:, QK_NOPE + half:QK_NOPE + QK_ROPE] = s
    return jnp.asarray(cos, F32), jnp.asarray(sin, F32)


def _partner_columns(w):
    half = QK_ROPE // 2
    lo, mid, hi = QK_NOPE, QK_NOPE + half, QK_NOPE + QK_ROPE
    z = jnp.zeros_like(w)
    return jnp.concatenate([z[..., :lo], w[..., mid:hi], w[..., lo:mid], z[..., hi:]], axis=-1)


def _inproj(x2, mod, norm1_g, w_in, q_a_g, w_uq, kv_a_g, w_ukv, q_g, k_g, bsz, seq):
    t, d = x2.shape
    ql, kvl = q_a_g.shape[0], kv_a_g.shape[0]
    hq = QK_NOPE + QK_ROPE
    fw = w_in.shape[1] - ql - kvl - QK_ROPE - 2 * d
    o1, o2, o3, o4, o5 = ql, ql + kvl, ql + kvl + QK_ROPE, ql + kvl + QK_ROPE + fw, \
        ql + kvl + QK_ROPE + fw + d
    assert ql % LANES == 0 and kvl % LANES == 0

    wa = jnp.concatenate([w_in[:, :o3], jnp.zeros((d, LANES - QK_ROPE), F32)], axis=1).astype(BF16)
    wf = w_in[:, o3:o4].astype(BF16)
    wga = w_in[:, o4:o5].astype(BF16)
    wgf = w_in[:, o5:].astype(BF16)

    wuq = w_uq.reshape(ql, N_HEADS, hq)
    wuq = jnp.pad(wuq, ((0, 0), (0, 0), (0, LANES - hq)))
    wuq = jnp.concatenate([wuq.reshape(ql, N_HEADS * LANES),
                           _partner_columns(wuq).reshape(ql, N_HEADS * LANES)], axis=1).astype(BF16)
    wukv = w_ukv.reshape(kvl, N_HEADS, QK_NOPE + V_DIM)
    wk = jnp.pad(wukv[:, :, :QK_NOPE], ((0, 0), (0, 0), (0, LANES - QK_NOPE)))
    place = jnp.zeros((QK_ROPE, N_HEADS, LANES), F32)
    place = place.at[jnp.arange(QK_ROPE), :, QK_NOPE + jnp.arange(QK_ROPE)].set(1.0)
    wk = jnp.concatenate([wk, place, jnp.zeros((LANES - QK_ROPE, N_HEADS, LANES), F32)], axis=0)
    wv = jnp.concatenate([wukv[:, :, QK_NOPE:], jnp.zeros((LANES, N_HEADS, V_DIM), F32)], axis=0)
    wkv = jnp.concatenate([wk.reshape(kvl + LANES, N_HEADS * LANES),
                           _partner_columns(wk).reshape(kvl + LANES, N_HEADS * LANES),
                           wv.reshape(kvl + LANES, N_HEADS * V_DIM)], axis=1).astype(BF16)

    gkv = jnp.concatenate([kv_a_g, jnp.ones((LANES,), F32)]).reshape(1, kvl + LANES)
    pad = jnp.zeros((LANES - hq,), F32)
    qg = jnp.concatenate([q_g * (hq ** -0.5), pad])
    kg = jnp.concatenate([k_g, pad])
    cos, sin = _rope_tables(seq)
    aq, bq = qg[None, :] * cos, _partner_columns(qg)[None, :] * sin
    ak, bk = kg[None, :] * cos, _partner_columns(kg)[None, :] * sin

    tm = _tile(seq, 512)
    tpb = seq // tm
    full = lambda shp: pl.BlockSpec(shp, lambda i: (0,) * len(shp))
    tok = lambda w: pl.BlockSpec((tm, w), lambda i: (i, 0))
    rope = pl.BlockSpec((tm, LANES), lambda i: (i % tpb, 0))
    return pl.pallas_call(
        functools.partial(_inproj_kernel, ql=ql, kvl=kvl),
        grid=(t // tm,),
        in_specs=[tok(d),
                  pl.BlockSpec((1, N_ADA, d), lambda i: (i // tpb, 0, 0)),
                  full((1, d)), full(wa.shape), full(wf.shape), full(wga.shape), full(wgf.shape),
                  full((1, ql)), full(gkv.shape), full(wuq.shape), full(wkv.shape),
                  rope, rope, rope, rope],
        out_specs=[tok(N_HEADS * LANES), tok(N_HEADS * LANES), tok(N_HEADS * V_DIM), tok(fw), tok(d), tok(d)],
        out_shape=[jax.ShapeDtypeStruct((t, N_HEADS * LANES), BF16),
                   jax.ShapeDtypeStruct((t, N_HEADS * LANES), BF16),
                   jax.ShapeDtypeStruct((t, N_HEADS * V_DIM), BF16),
                   jax.ShapeDtypeStruct((t, fw), BF16),
                   jax.ShapeDtypeStruct((t, d), BF16),
                   jax.ShapeDtypeStruct((t, d), BF16)],
        compiler_params=_cparams("arbitrary"),
        name="inproj",
    )(x2, mod, norm1_g.reshape(1, d), wa, wf, wga, wgf, q_a_g.reshape(1, ql), gkv, wuq, wkv,
      aq, bq, ak, bk)


def _attn_kernel(q_ref, k_ref, v_ref, o_ref):
    for hd in range(N_HEADS):
        s = lax.dot_general(q_ref[0, :, hd * LANES:(hd + 1) * LANES],
                            k_ref[0, :, hd * LANES:(hd + 1) * LANES], (((1,), (1,)), ((), ())),
                            preferred_element_type=F32)
        m = jnp.max(s, axis=-1, keepdims=True)
        p = jnp.exp(s - m)
        l = jnp.sum(p, axis=-1, keepdims=True)
        o = jnp.dot(p.astype(BF16), v_ref[0, :, hd * V_DIM:(hd + 1) * V_DIM],
                    preferred_element_type=F32)
        o_ref[0, :, hd * V_DIM:(hd + 1) * V_DIM] = (o / l).astype(BF16)


def _attention(q, k, v):
    bsz, seq, _ = q.shape
    tq = _tile(seq, 512)
    return pl.pallas_call(
        _attn_kernel,
        grid=(bsz, seq // tq),
        in_specs=[pl.BlockSpec((1, tq, N_HEADS * LANES), lambda b, j: (b, j, 0)),
                  pl.BlockSpec((1, seq, N_HEADS * LANES), lambda b, j: (b, 0, 0)),
                  pl.BlockSpec((1, seq, N_HEADS * V_DIM), lambda b, j: (b, 0, 0))],
        out_specs=pl.BlockSpec((1, tq, N_HEADS * V_DIM), lambda b, j: (b, j, 0)),
        out_shape=jax.ShapeDtypeStruct((bsz, seq, N_HEADS * V_DIM), BF16),
        compiler_params=_cparams("arbitrary", "arbitrary"),
        name="attn",
    )(q, k, v)


def _fourier_kernel(z_ref, wc_ref, ws_ref, tab_ref, o_ref, u_ref, *, seq):
    @pl.when(pl.program_id(1) == 0)
    def _():
        z = z_ref[0]
        u_ref[:seq, :] = jnp.dot(z, wc_ref[...], preferred_element_type=F32).astype(BF16)
        u_ref[seq:, :] = jnp.dot(z, ws_ref[...], preferred_element_type=F32).astype(BF16)

    o_ref[0] = jnp.dot(tab_ref[...], u_ref[...], preferred_element_type=F32).astype(BF16)


def _fourier_tables(seq, fw):
    g = FOURIER_GROUP
    n = np.arange(seq, dtype=np.int64)
    ang = 2.0 * np.pi * ((n[:, None] * n[None, :]) % seq).astype(np.float64) / seq
    tab = np.concatenate([np.cos(ang), -np.sin(ang)], axis=1)
    c = np.arange(g, dtype=np.int64)
    angc = 2.0 * np.pi * ((c[:, None] * c[None, :]) % g).astype(np.float64) / g
    scale = 1.0 / math.sqrt(seq * g)
    eye = np.eye(fw // g)
    wc = np.kron(eye, np.cos(angc) * scale)
    ws = np.kron(eye, np.sin(angc) * scale)
    return (jnp.asarray(tab, F32).astype(BF16), jnp.asarray(wc, F32).astype(BF16),
            jnp.asarray(ws, F32).astype(BF16))


def _fourier(zf):
    bsz, seq, fw = zf.shape
    tab, wc, ws = _fourier_tables(seq, fw)
    tr = _tile(seq, 512)
    return pl.pallas_call(
        functools.partial(_fourier_kernel, seq=seq),
        grid=(bsz, seq // tr),
        in_specs=[pl.BlockSpec((1, seq, fw), lambda b, j: (b, 0, 0)),
                  pl.BlockSpec((fw, fw), lambda b, j: (0, 0)),
                  pl.BlockSpec((fw, fw), lambda b, j: (0, 0)),
                  pl.BlockSpec((tr, 2 * seq), lambda b, j: (j, 0))],
        out_specs=pl.BlockSpec((1, tr, fw), lambda b, j: (b, j, 0)),
        out_shape=jax.ShapeDtypeStruct((bsz, seq, fw), BF16),
        scratch_shapes=[pltpu.VMEM((2 * seq, fw), BF16)],
        compiler_params=_cparams("arbitrary", "arbitrary"),
        name="fourier",
    )(zf, wc, ws, tab)


def _merge_kernel(a_ref, f_ref, sa_ref, sf_ref, x_ref, mod_ref, wpa_ref, wpf_ref, wo_ref,
                  g2_ref, wrh_ref, wrl_ref, x1_ref, h2_ref, sc_ref):
    ya = jnp.dot(a_ref[...], wpa_ref[...], preferred_element_type=F32)
    yf = jnp.dot(f_ref[...], wpf_ref[...], preferred_element_type=F32)
    merged = sa_ref[...].astype(F32) * ya + sf_ref[...].astype(F32) * yf
    mod = mod_ref[0]
    g1, sh2, sc2 = mod[2:3], mod[3:4], mod[4:5]
    x1 = x_ref[...] + g1 * jnp.dot(merged.astype(BF16), wo_ref[...], preferred_element_type=F32)
    x1_ref[...] = x1
    r = lax.rsqrt(jnp.mean(x1 * x1, axis=-1, keepdims=True) + EPS)
    h2 = (x1 * r * g2_ref[...]) * (1.0 + sc2) + sh2
    h2_ref[0], h2_ref[1] = _pack_halves(h2)
    hh = h2.astype(BF16)
    hl = (h2 - hh.astype(F32)).astype(BF16)
    nt = (((1,), (1,)), ((), ()))
    lt = (lax.dot_general(wrh_ref[...], hh, nt, preferred_element_type=F32)
          + lax.dot_general(wrh_ref[...], hl, nt, preferred_element_type=F32)
          + lax.dot_general(wrl_ref[...], hh, nt, preferred_element_type=F32))
    sc_ref[...] = jax.nn.sigmoid(lt)


def _merge(attn, four, sa, sf, x2, mod, w_pa, w_pf, w_out, norm2_g, w_router, seq):
    t, d = x2.shape
    e = w_router.shape[1]
    wrt = w_router.T
    wrh = wrt.astype(BF16)
    wrl = (wrt - wrh.astype(F32)).astype(BF16)
    tm = _tile(seq, 512)
    tpb = seq // tm
    full = lambda shp: pl.BlockSpec(shp, lambda i: (0,) * len(shp))
    tok = lambda w: pl.BlockSpec((tm, w), lambda i: (i, 0))
    return pl.pallas_call(
        _merge_kernel,
        grid=(t // tm,),
        in_specs=[tok(attn.shape[1]), tok(four.shape[1]), tok(d), tok(d), tok(d),
                  pl.BlockSpec((1, N_ADA, d), lambda i: (i // tpb, 0, 0)),
                  full(w_pa.shape), full(w_pf.shape), full(w_out.shape), full((1, d)),
                  full((e, d)), full((e, d))],
        out_specs=[tok(d), pl.BlockSpec((2, tm, d // 4), lambda i: (0, i, 0)),
                   pl.BlockSpec((e, tm), lambda i: (0, i))],
        out_shape=[jax.ShapeDtypeStruct((t, d), F32),
                   jax.ShapeDtypeStruct((2, t, d // 4), jnp.uint32),
                   jax.ShapeDtypeStruct((e, t), F32)],
        compiler_params=_cparams("arbitrary"),
        name="merge",
    )(attn, four, sa, sf, x2, mod, w_pa.astype(BF16), w_pf.astype(BF16), w_out.astype(BF16),
      norm2_g.reshape(1, d), wrh, wrl)


def _route_kernel(s_ref, b_ref, tri_ref, idx_ref, w_ref, rank_ref, cnt_ref, carry_ref):
    @pl.when(pl.program_id(0) == 0)
    def _():
        carry_ref[...] = jnp.zeros_like(carry_ref)

    sc = s_ref[...]
    e, tr = sc.shape
    row = lax.broadcasted_iota(jnp.int32, (e, tr), 0)
    v = sc + b_ref[...]
    sel = jnp.zeros((e, tr), F32)
    idxs, ws = [], []
    for _ in range(TOP_K):
        m = jnp.max(v, axis=0, keepdims=True)
        idx = jnp.min(jnp.where(v == m, row, e), axis=0, keepdims=True)
        oh = row == idx
        ws.append(jnp.sum(jnp.where(oh, sc, 0.0), axis=0, keepdims=True))
        idxs.append(idx)
        v = jnp.where(oh, -jnp.inf, v)
        sel = sel + oh.astype(F32)
    wsum = ws[0]
    for w in ws[1:]:
        wsum = wsum + w
    selb = sel.astype(BF16)
    cum = jnp.dot(selb, tri_ref[...], preferred_element_type=F32) + carry_ref[...]
    for kk in range(TOP_K):
        oh = row == idxs[kk]
        rk = jnp.sum(jnp.where(oh, cum, 0.0), axis=0, keepdims=True)
        idx_ref[kk:kk + 1, :] = idxs[kk]
        rank_ref[kk:kk + 1, :] = rk.astype(jnp.int32)
        w_ref[kk:kk + 1, :] = ws[kk] / wsum * ROUTED_SCALE
    tot = carry_ref[...] + jnp.dot(selb, jnp.ones((tr, tr), BF16), preferred_element_type=F32)
    carry_ref[...] = tot
    cnt_ref[...] = tot


def _route(scores_t, router_bias):
    e, t = scores_t.shape
    tr = _tile(t, 256)
    tri = jnp.asarray(np.triu(np.ones((tr, tr), np.float32), 1), BF16)
    bias = jnp.broadcast_to(router_bias.reshape(e, 1), (e, tr)).astype(F32)
    blk = pl.BlockSpec((TOP_K, tr), lambda i: (0, i))
    return pl.pallas_call(
        _route_kernel,
        grid=(t // tr,),
        in_specs=[pl.BlockSpec((e, tr), lambda i: (0, i)),
                  pl.BlockSpec((e, tr), lambda i: (0, 0)),
                  pl.BlockSpec((tr, tr), lambda i: (0, 0))],
        out_specs=[blk, blk, blk, pl.BlockSpec((e, tr), lambda i: (0, 0))],
        out_shape=[jax.ShapeDtypeStruct((TOP_K, t), jnp.int32),
                   jax.ShapeDtypeStruct((TOP_K, t), F32),
                   jax.ShapeDtypeStruct((TOP_K, t), jnp.int32),
                   jax.ShapeDtypeStruct((e, tr), F32)],
        scratch_shapes=[pltpu.VMEM((e, tr), F32)],
        compiler_params=_cparams("arbitrary"),
        name="route",
    )(scores_t, bias, tri)


def _slots_kernel(idx_ref, rank_ref, ps_ref, slot_ref):
    ps = ps_ref[...]
    row = lax.broadcasted_iota(jnp.int32, ps.shape, 0)
    for kk in range(TOP_K):
        oh = row == idx_ref[kk:kk + 1, :]
        start = jnp.sum(jnp.where(oh, ps, 0), axis=0, keepdims=True)
        slot_ref[kk:kk + 1, :] = start + rank_ref[kk:kk + 1, :]


def _slots(idx_t, rank_t, p_start):
    _, t = idx_t.shape
    e = p_start.shape[0]
    ts = _tile(t, 512)
    ps = jnp.broadcast_to(p_start.reshape(e, 1), (e, ts))
    blk = pl.BlockSpec((TOP_K, ts), lambda i: (0, i))
    return pl.pallas_call(
        _slots_kernel,
        grid=(t // ts,),
        in_specs=[blk, blk, pl.BlockSpec((e, ts), lambda i: (0, 0))],
        out_specs=blk,
        out_shape=jax.ShapeDtypeStruct((TOP_K, t), jnp.int32),
        compiler_params=_cparams("arbitrary"),
        name="slots",
    )(idx_t, rank_t, ps)


def _dispatch(h2p, slot_t, n_slots):
    _, t, c = h2p.shape
    k = slot_t.shape[0]
    win = _tile(2 * t, SC_WINDOW)
    rows = h2p.reshape(2 * t, c)
    dest = jnp.concatenate([slot_t, slot_t + n_slots], axis=1)
    mesh = plsc.VectorSubcoreMesh(core_axis_name="core", subcore_axis_name="subcore")

    @pl.kernel(out_type=jax.ShapeDtypeStruct((2 * n_slots, c), h2p.dtype), mesh=mesh,
               scratch_types=[])
    def scatter_rows(x_hbm, s_hbm, o_hbm):
        def body(x_vmem, s_vmem):
            pltpu.sync_copy(x_vmem, o_hbm.at[s_vmem.at[0]])

        pltpu.emit_pipeline(
            body, grid=(2 * t // win, k),
            in_specs=[pl.BlockSpec((win, c), lambda i, j: (i, 0)),
                      pl.BlockSpec((1, win), lambda i, j: (j, i))],
            out_specs=[], core_axis_name=("core", "subcore"),
            dimension_semantics=(pltpu.PARALLEL, pltpu.ARBITRARY))(x_hbm, s_hbm)

    return scatter_rows(rows, dest).reshape(2, n_slots, c)


def _expert_kernel(be_ref, run_ref, rex_ref, bst_ref, bsz_ref, nu_ref, xs_hbm, wg_hbm, wu_hbm,
                   wd_hbm, ys_hbm, xbuf, ybuf, wgf, wuf, wdf, xsem, ysem, wsem, *, n_slots):
    nu = nu_ref[0]
    unit = ROW_UNIT

    def x_stream(blk, slot, start):
        r0 = pl.multiple_of(bst_ref[blk], unit)
        for ch in range(MAX_UNITS):
            @pl.when(ch * unit < bsz_ref[blk])
            def _():
                for h in range(2):
                    cp = pltpu.make_async_copy(xs_hbm.at[h, pl.ds(r0 + ch * unit, unit)],
                                               xbuf.at[slot, h, pl.ds(ch * unit, unit)],
                                               xsem.at[slot])
                    cp.start() if start else cp.wait()

    def y_copies(r0, slot, m):
        return [pltpu.make_async_copy(ybuf.at[slot, h, pl.ds(0, m)],
                                      ys_hbm.at[h, pl.ds(pl.multiple_of(r0, unit), m)],
                                      ysem.at[slot]) for h in range(2)]

    def y_wait(blk, slot):
        for n in range(1, MAX_UNITS + 1):
            @pl.when(bsz_ref[blk] == n * unit)
            def _():
                for cp in y_copies(bst_ref[blk], slot, n * unit):
                    cp.wait()

    def weight_copies(e, which):
        return (pltpu.make_async_copy(wg_hbm.at[e], wgf.at[which], wsem.at[which]),
                pltpu.make_async_copy(wu_hbm.at[e], wuf.at[which], wsem.at[which]),
                pltpu.make_async_copy(wd_hbm.at[e], wdf.at[which], wsem.at[which]))

    for j in range(W_RING - 1):
        @pl.when(rex_ref[j] >= 0)
        def _():
            for cp in weight_copies(rex_ref[j], j):
                cp.start(priority=1)
    for j in range(X_RING - 1):
        @pl.when(j < nu)
        def _():
            x_stream(j, j, True)

    def step(i, carry):
        ahead = i + X_RING - 1

        @pl.when(ahead < nu)
        def _():
            x_stream(ahead, ahead % X_RING, True)

        prev = be_ref[jnp.maximum(i - 1, 0)]

        @pl.when(jnp.logical_or(i == 0, be_ref[i] != prev))
        def _():
            r = run_ref[i]
            par = r % W_RING
            for cp in weight_copies(be_ref[i], par):
                cp.wait()
            later = rex_ref[r + W_RING - 1]

            @pl.when(later >= 0)
            def _():
                for cp in weight_copies(later, (r + W_RING - 1) % W_RING):
                    cp.start(priority=1)

        slot = i % X_RING
        x_stream(i, slot, False)
        out = i % 2
        wslot = run_ref[i] % W_RING

        @pl.when(i >= 2)
        def _():
            y_wait(i - 2, out)

        for n in range(1, MAX_UNITS + 1):
            @pl.when(bsz_ref[i] == n * unit)
            def _():
                m = n * unit
                x = _unpack_halves(xbuf[slot, 0, pl.ds(0, m)], xbuf[slot, 1, pl.ds(0, m)])
                wg = pltpu.bitcast(wgf[wslot], BF16)
                wu = pltpu.bitcast(wuf[wslot], BF16)
                wd = pltpu.bitcast(wdf[wslot], BF16)
                g = jnp.dot(x, wg, preferred_element_type=F32)
                u = jnp.dot(x, wu, preferred_element_type=F32)
                a = (_silu(g) * u).astype(BF16)
                y0, y1 = _pack_halves(jnp.dot(a, wd, preferred_element_type=F32))
                ybuf[out, 0, pl.ds(0, m)] = y0
                ybuf[out, 1, pl.ds(0, m)] = y1
                for cp in y_copies(bst_ref[i], out, m):
                    cp.start()
        return carry

    lax.fori_loop(0, nu, step, 0)

    @pl.when(nu >= 2)
    def _():
        y_wait(nu - 2, nu % 2)
    y_wait(nu - 1, (nu - 1) % 2)

    ybuf[0, :, pl.ds(0, unit)] = jnp.zeros((2, unit, ybuf.shape[3]), ybuf.dtype)
    used = bst_ref[nu - 1] + bsz_ref[nu - 1]
    spare = (n_slots - used) // unit

    def zero_start(j, carry):
        for cp in y_copies(used + j * unit, 0, unit):
            cp.start()
        return carry

    def zero_wait(j, carry):
        for cp in y_copies(used + j * unit, 0, unit):
            cp.wait()
        return carry
    lax.fori_loop(0, spare, zero_start, 0)
    lax.fori_loop(0, spare, zero_wait, 0)


def _pack_weights(w):
    e, r, c = w.shape
    n = e * r // 2
    rb = _tile(n, SC_PACK_ROWS)
    mesh = plsc.VectorSubcoreMesh(core_axis_name="core", subcore_axis_name="subcore")
    lanes = SC_LANES

    @pl.kernel(out_type=jax.ShapeDtypeStruct((n, c), jnp.uint32), mesh=mesh, scratch_types=[],
               compiler_params=pltpu.CompilerParams(needs_layout_passes=False))
    def pack_rows(w_hbm, o_hbm):
        def body(in_vmem, out_vmem):
            @pl.loop(0, rb)
            def _(rr):
                for cc in range(0, c, lanes):
                    lo = plsc.bitcast(in_vmem.at[rr, pl.ds(cc, lanes)][...], jnp.uint32)
                    hi = plsc.bitcast(in_vmem.at[rr, pl.ds(c + cc, lanes)][...], jnp.uint32)
                    lo = (lo + jnp.uint32(0x8000)) >> 16
                    hi = (hi + jnp.uint32(0x8000)) & jnp.uint32(0xFFFF0000)
                    out_vmem.at[rr, pl.ds(cc, lanes)][...] = lo | hi

        pltpu.emit_pipeline(
            body, grid=(n // rb,),
            in_specs=[pl.BlockSpec((rb, 2 * c), lambda i: (i, 0))],
            out_specs=[pl.BlockSpec((rb, c), lambda i: (i, 0))],
            core_axis_name=("core", "subcore"),
            dimension_semantics=(pltpu.PARALLEL,))(w_hbm, o_hbm)

    return pack_rows(w.reshape(n, 2 * c)).reshape(e, r // 2, c)


def _experts(blk_expert, blk_run, run_expert, blk_start, blk_size, nblk_used, xs, w_g, w_u, w_d):
    d, f = 2 * w_g.shape[1], w_g.shape[2]
    rows = ROW_UNIT * MAX_UNITS
    n_slots = xs.shape[1]
    c = xs.shape[2]
    hbm = pl.BlockSpec(memory_space=pl.ANY)
    return pl.pallas_call(
        functools.partial(_expert_kernel, n_slots=n_slots),
        grid_spec=pltpu.PrefetchScalarGridSpec(
            num_scalar_prefetch=6,
            grid=(1,),
            in_specs=[hbm, hbm, hbm, hbm],
            out_specs=hbm,
            scratch_shapes=[pltpu.VMEM((X_RING, 2, rows, c), jnp.uint32),
                            pltpu.VMEM((2, 2, rows, c), jnp.uint32),
                            pltpu.VMEM((W_RING, d // 2, f), jnp.uint32),
                            pltpu.VMEM((W_RING, d // 2, f), jnp.uint32),
                            pltpu.VMEM((W_RING, f // 2, d), jnp.uint32),
                            pltpu.SemaphoreType.DMA((X_RING,)), pltpu.SemaphoreType.DMA((2,)),
                            pltpu.SemaphoreType.DMA((W_RING,))]),
        out_shape=jax.ShapeDtypeStruct((2, n_slots, c), jnp.uint32),
        compiler_params=_cparams("arbitrary"),
        name="experts",
    )(blk_expert, blk_run, run_expert, blk_start, blk_size, nblk_used, xs, w_g, w_u, w_d)


def _gather(ys, slot_t):
    _, n_slots, c = ys.shape
    k, t = slot_t.shape
    p = 2 * k * t
    win = _tile(p, SC_WINDOW)
    src = jnp.concatenate([slot_t, slot_t + n_slots], axis=0).reshape(1, p)
    mesh = plsc.VectorSubcoreMesh(core_axis_name="core", subcore_axis_name="subcore")

    @pl.kernel(out_type=jax.ShapeDtypeStruct((p, c), ys.dtype), mesh=mesh, scratch_types=[])
    def gather_rows(y_hbm, s_hbm, o_hbm):
        def body(s_vmem, o_vmem):
            pltpu.sync_copy(y_hbm.at[s_vmem.at[0]], o_vmem)

        pltpu.emit_pipeline(
            body, grid=(p // win,),
            in_specs=[pl.BlockSpec((1, win), lambda i: (0, i))],
            out_specs=[pl.BlockSpec((win, c), lambda i: (i, 0))],
            core_axis_name=("core", "subcore"),
            dimension_semantics=(pltpu.PARALLEL,))(s_hbm, o_hbm)

    return gather_rows(ys.reshape(2 * n_slots, c), src).reshape(2, k, t, c)


def _combine_kernel(y_ref, w_ref, x1_ref, h2_ref, mod_ref, wsg_ref, wsu_ref, wsd_ref, o_ref):
    hb = _unpack_halves(h2_ref[0], h2_ref[1])
    g = jnp.dot(hb, wsg_ref[...], preferred_element_type=F32)
    u = jnp.dot(hb, wsu_ref[...], preferred_element_type=F32)
    acc = jnp.dot((_silu(g) * u).astype(BF16), wsd_ref[...], preferred_element_type=F32)
    w = w_ref[...].T
    for kk in range(TOP_K):
        acc = acc + w[:, kk:kk + 1] * _unpack_halves(y_ref[0, kk], y_ref[1, kk]).astype(F32)
    g2 = mod_ref[0][5:6]
    o_ref[...] = x1_ref[...] + g2 * acc


def _combine(y_tok, w_tk, x1, h2p, mod, w_sg, w_su, w_sd, seq):
    t, d = x1.shape
    c = d // 4
    tc = _tile(seq, 256)
    tpb = seq // tc
    full = lambda shp: pl.BlockSpec(shp, lambda i: (0,) * len(shp))
    tok = lambda w: pl.BlockSpec((tc, w), lambda i: (i, 0))
    return pl.pallas_call(
        _combine_kernel,
        grid=(t // tc,),
        in_specs=[pl.BlockSpec((2, TOP_K, tc, c), lambda i: (0, 0, i, 0)),
                  pl.BlockSpec((TOP_K, tc), lambda i: (0, i)), tok(d),
                  pl.BlockSpec((2, tc, c), lambda i: (0, i, 0)),
                  pl.BlockSpec((1, N_ADA, d), lambda i: (i // tpb, 0, 0)),
                  full(w_sg.shape), full(w_su.shape), full(w_sd.shape)],
        out_specs=tok(d),
        out_shape=jax.ShapeDtypeStruct((t, d), F32),
        compiler_params=_cparams("arbitrary"),
        name="combine",
    )(y_tok, w_tk, x1, h2p, mod, w_sg.astype(BF16), w_su.astype(BF16), w_sd.astype(BF16))


def _layer(x, c, w_ada, b_ada, norm1_g, w_in, q_a_norm_g, w_uq, kv_a_norm_g, w_ukv,
           q_norm_g, k_norm_g, w_proj_attn, w_proj_fourier, w_out, norm2_g,
           w_router, router_bias, w_exp_gate, w_exp_up, w_exp_down,
           w_sh_gate, w_sh_up, w_sh_down):
    bsz, seq, d = x.shape
    t = bsz * seq
    e = w_router.shape[1]
    x2 = x.reshape(t, d)

    mod = _ada(c, w_ada, b_ada).reshape(bsz, N_ADA, d)
    q, k, v, zf, sa, sf = _inproj(x2, mod, norm1_g, w_in, q_a_norm_g, w_uq, kv_a_norm_g,
                                  w_ukv, q_norm_g, k_norm_g, bsz, seq)
    per_batch = lambda a: a.reshape(bsz, seq, a.shape[1])
    attn = _attention(per_batch(q), per_batch(k), per_batch(v)).reshape(t, N_HEADS * V_DIM)
    four = _fourier(zf.reshape(bsz, seq, zf.shape[1])).reshape(t, zf.shape[1])
    x1, h2, scores_t = _merge(attn, four, sa, sf, x2, mod, w_proj_attn, w_proj_fourier,
                              w_out, norm2_g, w_router, seq)

    idx_t, w_t, rank_t, cnt = _route(scores_t, router_bias)
    counts = cnt[:, 0].astype(jnp.int32)
    unit, bmax = ROW_UNIT, ROW_UNIT * MAX_UNITS
    n_slots = t * TOP_K + e * unit
    padded = ((counts + unit - 1) // unit) * unit
    p_end = jnp.cumsum(padded)
    p_start = p_end - padded
    eid = jnp.arange(e, dtype=jnp.int32)
    nb = (padded + bmax - 1) // bmax
    b_end = jnp.cumsum(nb)
    b_first = b_end - nb
    nblk_used = b_end[-1].astype(jnp.int32)
    nblk = -(-(t * TOP_K) // bmax) + e
    bid = jnp.minimum(jnp.arange(nblk, dtype=jnp.int32), nblk_used - 1)
    blk_expert = jnp.clip(jnp.sum((b_end[None, :] <= bid[:, None]).astype(jnp.int32), axis=1),
                          0, e - 1)
    onehot = blk_expert[:, None] == eid[None, :]
    look = lambda tab: jnp.sum(jnp.where(onehot, tab[None, :], 0), axis=1)
    piece = bid - look(b_first)
    blk_start = (look(p_start) + piece * bmax).astype(jnp.int32)
    blk_size = jnp.clip(look(padded) - piece * bmax, 0, bmax).astype(jnp.int32)

    slot_t = _slots(idx_t, rank_t, p_start.astype(jnp.int32))
    xs = _dispatch(h2, slot_t, n_slots)
    used = counts > 0
    run = jnp.cumsum(used.astype(jnp.int32)) - 1
    blk_run = look(run).astype(jnp.int32)
    rid = jnp.arange(e + W_RING, dtype=jnp.int32)
    match = jnp.logical_and(used[None, :], run[None, :] == rid[:, None])
    run_expert = jnp.where(jnp.any(match, axis=1),
                           jnp.sum(jnp.where(match, eid[None, :], 0), axis=1), -1).astype(jnp.int32)
    ys = _experts(blk_expert, blk_run, run_expert, blk_start, blk_size, nblk_used.reshape(1), xs,
                  _pack_weights(w_exp_gate), _pack_weights(w_exp_up), _pack_weights(w_exp_down))
    out = _combine(_gather(ys, slot_t), w_t, x1, h2, mod, w_sh_gate, w_sh_up, w_sh_down, seq)
    return out.reshape(bsz, seq, d)


def kernel(x, c, w_ada, b_ada, norm1_g, w_in, q_a_norm_g, w_uq, kv_a_norm_g, w_ukv, q_norm_g,
           k_norm_g, w_proj_attn, w_proj_fourier, w_out, norm2_g, w_router, router_bias,
           w_exp_gate, w_exp_up, w_exp_down, w_sh_gate, w_sh_up, w_sh_down):
    for l in range(w_ada.shape[0]):
        x = _layer(x, c, w_ada[l], b_ada[l], norm1_g[l], w_in[l], q_a_norm_g[l], w_uq[l],
                   kv_a_norm_g[l], w_ukv[l], q_norm_g[l], k_norm_g[l], w_proj_attn[l],
                   w_proj_fourier[l], w_out[l], norm2_g[l], w_router[l], router_bias[l],
                   w_exp_gate[l], w_exp_up[l], w_exp_down[l], w_sh_gate[l], w_sh_up[l],
                   w_sh_down[l])
    return x
```

```python
import functools
import math

import numpy as np
import jax
import jax.numpy as jnp
from jax import lax
from jax.experimental import pallas as pl
from jax.experimental.pallas import tpu as pltpu
from jax.experimental.pallas import tpu_sc as plsc

N_HEADS = 8
QK_NOPE = 64
QK_ROPE = 32
V_DIM = 64
FOURIER_GROUP = 64
TOP_K = 8
ROUTED_SCALE = 2.5
EPS = 1e-6
ROPE_THETA = 10000.0
N_ADA = 6

LANES = 128
ROW_UNIT = 128
MAX_UNITS = 8
X_RING = 4
W_RING = 5
SC_WINDOW = 128
VMEM_LIMIT = 48 * 1024 * 1024

F32 = jnp.float32
BF16 = jnp.bfloat16


def _cparams(*sem):
    return pltpu.CompilerParams(dimension_semantics=sem, vmem_limit_bytes=VMEM_LIMIT)


def _tile(n, pref):
    t = min(n, pref)
    assert n % t == 0, (n, pref)
    return t


def _silu(v):
    return v * jax.nn.sigmoid(v)


def _pack_halves(m):
    d = m.shape[1]
    lo = lax.bitcast_convert_type(m[:, :d // 2].astype(BF16).astype(F32), jnp.uint32)
    hi = lax.bitcast_convert_type(m[:, d // 2:].astype(BF16).astype(F32), jnp.uint32)
    w = (lo >> 16) | (hi & jnp.uint32(0xFFFF0000))
    return w[:, :d // 4], w[:, d // 4:]


def _unpack_halves(w0, w1):
    def lo(w):
        return lax.bitcast_convert_type(w << 16, F32)

    def hi(w):
        return lax.bitcast_convert_type(w & jnp.uint32(0xFFFF0000), F32)
    return jnp.concatenate([lo(w0), lo(w1), hi(w0), hi(w1)], axis=1).astype(BF16)


def _ada_kernel(c_ref, w_ref, b_ref, o_ref):
    a = _silu(c_ref[...])
    o_ref[...] = jnp.dot(a, w_ref[...], preferred_element_type=F32,
                         precision=lax.Precision.HIGHEST) + b_ref[...]


def _ada(c, w_ada, b_ada):
    bsz, d = c.shape
    n = w_ada.shape[1]
    tn = _tile(n, d)
    return pl.pallas_call(
        _ada_kernel,
        grid=(n // tn,),
        in_specs=[pl.BlockSpec((bsz, d), lambda j: (0, 0)),
                  pl.BlockSpec((d, tn), lambda j: (0, j)),
                  pl.BlockSpec((1, tn), lambda j: (0, j))],
        out_specs=pl.BlockSpec((bsz, tn), lambda j: (0, j)),
        out_shape=jax.ShapeDtypeStruct((bsz, n), F32),
        compiler_params=_cparams("arbitrary"),
        name="ada",
    )(c, w_ada, b_ada.reshape(1, n))


def _head_norm_rope(t, trot, a, b):
    ms = jnp.sum(t * t, axis=-1, keepdims=True) * (1.0 / (QK_NOPE + QK_ROPE))
    return (t * a + trot * b) * lax.rsqrt(ms + EPS)


def _inproj_kernel(x_ref, mod_ref, g1_ref, wa_ref, wf_ref, wga_ref, wgf_ref,
                   gq_ref, gkv_ref, wuq_ref, wkv_ref,
                   aq_ref, bq_ref, ak_ref, bk_ref,
                   q_ref, k_ref, v_ref, zf_ref, sa_ref, sf_ref, *, ql, kvl):
    x = x_ref[...]
    mod = mod_ref[0]
    sh1, sc1 = mod[0:1], mod[1:2]
    r = lax.rsqrt(jnp.mean(x * x, axis=-1, keepdims=True) + EPS)
    h = (x * r * g1_ref[...]) * (1.0 + sc1) + sh1
    hb = h.astype(BF16)

    zf_ref[...] = jnp.dot(hb, wf_ref[...], preferred_element_type=F32).astype(BF16)
    sa_ref[...] = jax.nn.sigmoid(
        jnp.dot(hb, wga_ref[...], preferred_element_type=F32)).astype(BF16)
    sf_ref[...] = jax.nn.sigmoid(
        jnp.dot(hb, wgf_ref[...], preferred_element_type=F32)).astype(BF16)

    za = jnp.dot(hb, wa_ref[...], preferred_element_type=F32)
    zq = za[:, :ql]
    cq = zq * lax.rsqrt(jnp.mean(zq * zq, axis=-1, keepdims=True) + EPS) * gq_ref[...]
    qall = jnp.dot(cq.astype(BF16), wuq_ref[...], preferred_element_type=F32)

    zk = za[:, ql:]
    kvn = zk[:, :kvl]
    rk = lax.rsqrt(jnp.mean(kvn * kvn, axis=-1, keepdims=True) + EPS)
    lane = lax.broadcasted_iota(jnp.int32, zk.shape, 1)
    u = zk * jnp.where(lane < kvl, rk, 1.0) * gkv_ref[...]
    kvall = jnp.dot(u.astype(BF16), wkv_ref[...], preferred_element_type=F32)

    aq, bq, ak, bk = aq_ref[...], bq_ref[...], ak_ref[...], bk_ref[...]
    hw = N_HEADS * LANES
    for hd in range(N_HEADS):
        lo, hi = hd * LANES, (hd + 1) * LANES
        q_ref[:, lo:hi] = _head_norm_rope(qall[:, lo:hi], qall[:, hw + lo:hw + hi],
                                       aq, bq).astype(BF16)
        k_ref[:, lo:hi] = _head_norm_rope(kvall[:, lo:hi], kvall[:, hw + lo:hw + hi],
                                       ak, bk).astype(BF16)
    v_ref[...] = kvall[:, 2 * hw:].astype(BF16)


def _rope_tables(seq):
    half = QK_ROPE // 2
    pos = np.arange(seq, dtype=np.float64)
    inv = ROPE_THETA ** (-np.arange(0, QK_ROPE, 2, dtype=np.float64) / QK_ROPE)
    ang = pos[:, None] * inv[None, :]
    c, s = np.cos(ang), np.sin(ang)
    cos = np.ones((seq, LANES)); sin = np.zeros((seq, LANES))
    cos[:, QK_NOPE:QK_NOPE + half] = c
    cos[:, QK_NOPE + half:QK_NOPE + QK_ROPE] = c
    sin[:, QK_NOPE:QK_NOPE + half] = -s
    sin[:, QK_NOPE + half:QK_NOPE + QK_ROPE] = s
    return jnp.asarray(cos, F32), jnp.asarray(sin, F32)


def _partner_columns(w):
    half = QK_ROPE // 2
    lo, mid, hi = QK_NOPE, QK_NOPE + half, QK_NOPE + QK_ROPE
    z = jnp.zeros_like(w)
    return jnp.concatenate([z[..., :lo], w[..., mid:hi], w[..., lo:mid], z[..., hi:]], axis=-1)


def _inproj(x2, mod, norm1_g, w_in, q_a_g, w_uq, kv_a_g, w_ukv, q_g, k_g, bsz, seq):
    t, d = x2.shape
    ql, kvl = q_a_g.shape[0], kv_a_g.shape[0]
    hq = QK_NOPE + QK_ROPE
    fw = w_in.shape[1] - ql - kvl - QK_ROPE - 2 * d
    o1, o2, o3, o4, o5 = ql, ql + kvl, ql + kvl + QK_ROPE, ql + kvl + QK_ROPE + fw, \
        ql + kvl + QK_ROPE + fw + d
    assert ql % LANES == 0 and kvl % LANES == 0

    wa = jnp.concatenate([w_in[:, :o3], jnp.zeros((d, LANES - QK_ROPE), F32)], axis=1).astype(BF16)
    wf = w_in[:, o3:o4].astype(BF16)
    wga = w_in[:, o4:o5].astype(BF16)
    wgf = w_in[:, o5:].astype(BF16)

    wuq = w_uq.reshape(ql, N_HEADS, hq)
    wuq = jnp.pad(wuq, ((0, 0), (0, 0), (0, LANES - hq)))
    wuq = jnp.concatenate([wuq.reshape(ql, N_HEADS * LANES),
                           _partner_columns(wuq).reshape(ql, N_HEADS * LANES)], axis=1).astype(BF16)
    wukv = w_ukv.reshape(kvl, N_HEADS, QK_NOPE + V_DIM)
    wk = jnp.pad(wukv[:, :, :QK_NOPE], ((0, 0), (0, 0), (0, LANES - QK_NOPE)))
    place = jnp.zeros((QK_ROPE, N_HEADS, LANES), F32)
    place = place.at[jnp.arange(QK_ROPE), :, QK_NOPE + jnp.arange(QK_ROPE)].set(1.0)
    wk = jnp.concatenate([wk, place, jnp.zeros((LANES - QK_ROPE, N_HEADS, LANES), F32)], axis=0)
    wv = jnp.concatenate([wukv[:, :, QK_NOPE:], jnp.zeros((LANES, N_HEADS, V_DIM), F32)], axis=0)
    wkv = jnp.concatenate([wk.reshape(kvl + LANES, N_HEADS * LANES),
                           _partner_columns(wk).reshape(kvl + LANES, N_HEADS * LANES),
                           wv.reshape(kvl + LANES, N_HEADS * V_DIM)], axis=1).astype(BF16)

    gkv = jnp.concatenate([kv_a_g, jnp.ones((LANES,), F32)]).reshape(1, kvl + LANES)
    pad = jnp.zeros((LANES - hq,), F32)
    qg = jnp.concatenate([q_g * (hq ** -0.5), pad])
    kg = jnp.concatenate([k_g, pad])
    cos, sin = _rope_tables(seq)
    aq, bq = qg[None, :] * cos, _partner_columns(qg)[None, :] * sin
    ak, bk = kg[None, :] * cos, _partner_columns(kg)[None, :] * sin

    tm = _tile(seq, 512)
    tpb = seq // tm
    full = lambda shp: pl.BlockSpec(shp, lambda i: (0,) * len(shp))
    tok = lambda w: pl.BlockSpec((tm, w), lambda i: (i, 0))
    rope = pl.BlockSpec((tm, LANES), lambda i: (i % tpb, 0))
    return pl.pallas_call(
        functools.partial(_inproj_kernel, ql=ql, kvl=kvl),
        grid=(t // tm,),
        in_specs=[tok(d),
                  pl.BlockSpec((1, N_ADA, d), lambda i: (i // tpb, 0, 0)),
                  full((1, d)), full(wa.shape), full(wf.shape), full(wga.shape), full(wgf.shape),
                  full((1, ql)), full(gkv.shape), full(wuq.shape), full(wkv.shape),
                  rope, rope, rope, rope],
        out_specs=[tok(N_HEADS * LANES), tok(N_HEADS * LANES), tok(N_HEADS * V_DIM), tok(fw), tok(d), tok(d)],
        out_shape=[jax.ShapeDtypeStruct((t, N_HEADS * LANES), BF16),
                   jax.ShapeDtypeStruct((t, N_HEADS * LANES), BF16),
                   jax.ShapeDtypeStruct((t, N_HEADS * V_DIM), BF16),
                   jax.ShapeDtypeStruct((t, fw), BF16),
                   jax.ShapeDtypeStruct((t, d), BF16),
                   jax.ShapeDtypeStruct((t, d), BF16)],
        compiler_params=_cparams("arbitrary"),
        name="inproj",
    )(x2, mod, norm1_g.reshape(1, d), wa, wf, wga, wgf, q_a_g.reshape(1, ql), gkv, wuq, wkv,
      aq, bq, ak, bk)


def _attn_kernel(q_ref, k_ref, v_ref, o_ref):
    for hd in range(N_HEADS):
        s = lax.dot_general(q_ref[0, :, hd * LANES:(hd + 1) * LANES],
                            k_ref[0, :, hd * LANES:(hd + 1) * LANES], (((1,), (1,)), ((), ())),
                            preferred_element_type=F32)
        m = jnp.max(s, axis=-1, keepdims=True)
        p = jnp.exp(s - m)
        l = jnp.sum(p, axis=-1, keepdims=True)
        o = jnp.dot(p.astype(BF16), v_ref[0, :, hd * V_DIM:(hd + 1) * V_DIM],
                    preferred_element_type=F32)
        o_ref[0, :, hd * V_DIM:(hd + 1) * V_DIM] = (o / l).astype(BF16)


def _attention(q, k, v):
    bsz, seq, _ = q.shape
    tq = _tile(seq, 512)
    return pl.pallas_call(
        _attn_kernel,
        grid=(bsz, seq // tq),
        in_specs=[pl.BlockSpec((1, tq, N_HEADS * LANES), lambda b, j: (b, j, 0)),
                  pl.BlockSpec((1, seq, N_HEADS * LANES), lambda b, j: (b, 0, 0)),
                  pl.BlockSpec((1, seq, N_HEADS * V_DIM), lambda b, j: (b, 0, 0))],
        out_specs=pl.BlockSpec((1, tq, N_HEADS * V_DIM), lambda b, j: (b, j, 0)),
        out_shape=jax.ShapeDtypeStruct((bsz, seq, N_HEADS * V_DIM), BF16),
        compiler_params=_cparams("arbitrary", "arbitrary"),
        name="attn",
    )(q, k, v)


def _fourier_kernel(z_ref, wc_ref, ws_ref, tab_ref, o_ref, u_ref, *, seq):
    @pl.when(pl.program_id(1) == 0)
    def _():
        z = z_ref[0]
        u_ref[:seq, :] = jnp.dot(z, wc_ref[...], preferred_element_type=F32).astype(BF16)
        u_ref[seq:, :] = jnp.dot(z, ws_ref[...], preferred_element_type=F32).astype(BF16)

    o_ref[0] = jnp.dot(tab_ref[...], u_ref[...], preferred_element_type=F32).astype(BF16)


def _fourier_tables(seq, fw):
    g = FOURIER_GROUP
    n = np.arange(seq, dtype=np.int64)
    ang = 2.0 * np.pi * ((n[:, None] * n[None, :]) % seq).astype(np.float64) / seq
    tab = np.concatenate([np.cos(ang), -np.sin(ang)], axis=1)
    c = np.arange(g, dtype=np.int64)
    angc = 2.0 * np.pi * ((c[:, None] * c[None, :]) % g).astype(np.float64) / g
    scale = 1.0 / math.sqrt(seq * g)
    eye = np.eye(fw // g)
    wc = np.kron(eye, np.cos(angc) * scale)
    ws = np.kron(eye, np.sin(angc) * scale)
    return (jnp.asarray(tab, F32).astype(BF16), jnp.asarray(wc, F32).astype(BF16),
            jnp.asarray(ws, F32).astype(BF16))


def _fourier(zf):
    bsz, seq, fw = zf.shape
    tab, wc, ws = _fourier_tables(seq, fw)
    tr = _tile(seq, 512)
    return pl.pallas_call(
        functools.partial(_fourier_kernel, seq=seq),
        grid=(bsz, seq // tr),
        in_specs=[pl.BlockSpec((1, seq, fw), lambda b, j: (b, 0, 0)),
                  pl.BlockSpec((fw, fw), lambda b, j: (0, 0)),
                  pl.BlockSpec((fw, fw), lambda b, j: (0, 0)),
                  pl.BlockSpec((tr, 2 * seq), lambda b, j: (j, 0))],
        out_specs=pl.BlockSpec((1, tr, fw), lambda b, j: (b, j, 0)),
        out_shape=jax.ShapeDtypeStruct((bsz, seq, fw), BF16),
        scratch_shapes=[pltpu.VMEM((2 * seq, fw), BF16)],
        compiler_params=_cparams("arbitrary", "arbitrary"),
        name="fourier",
    )(zf, wc, ws, tab)


def _merge_kernel(a_ref, f_ref, sa_ref, sf_ref, x_ref, mod_ref, wpa_ref, wpf_ref, wo_ref,
                  g2_ref, wrh_ref, wrl_ref, x1_ref, h2_ref, sc_ref):
    ya = jnp.dot(a_ref[...], wpa_ref[...], preferred_element_type=F32)
    yf = jnp.dot(f_ref[...], wpf_ref[...], preferred_element_type=F32)
    merged = sa_ref[...].astype(F32) * ya + sf_ref[...].astype(F32) * yf
    mod = mod_ref[0]
    g1, sh2, sc2 = mod[2:3], mod[3:4], mod[4:5]
    x1 = x_ref[...] + g1 * jnp.dot(merged.astype(BF16), wo_ref[...], preferred_element_type=F32)
    x1_ref[...] = x1
    r = lax.rsqrt(jnp.mean(x1 * x1, axis=-1, keepdims=True) + EPS)
    h2 = (x1 * r * g2_ref[...]) * (1.0 + sc2) + sh2
    h2_ref[0], h2_ref[1] = _pack_halves(h2)
    hh = h2.astype(BF16)
    hl = (h2 - hh.astype(F32)).astype(BF16)
    nt = (((1,), (1,)), ((), ()))
    lt = (lax.dot_general(wrh_ref[...], hh, nt, preferred_element_type=F32)
          + lax.dot_general(wrh_ref[...], hl, nt, preferred_element_type=F32)
          + lax.dot_general(wrl_ref[...], hh, nt, preferred_element_type=F32))
    sc_ref[...] = jax.nn.sigmoid(lt)


def _merge(attn, four, sa, sf, x2, mod, w_pa, w_pf, w_out, norm2_g, w_router, seq):
    t, d = x2.shape
    e = w_router.shape[1]
    wrt = w_router.T
    wrh = wrt.astype(BF16)
    wrl = (wrt - wrh.astype(F32)).astype(BF16)
    tm = _tile(seq, 512)
    tpb = seq // tm
    full = lambda shp: pl.BlockSpec(shp, lambda i: (0,) * len(shp))
    tok = lambda w: pl.BlockSpec((tm, w), lambda i: (i, 0))
    return pl.pallas_call(
        _merge_kernel,
        grid=(t // tm,),
        in_specs=[tok(attn.shape[1]), tok(four.shape[1]), tok(d), tok(d), tok(d),
                  pl.BlockSpec((1, N_ADA, d), lambda i: (i // tpb, 0, 0)),
                  full(w_pa.shape), full(w_pf.shape), full(w_out.shape), full((1, d)),
                  full((e, d)), full((e, d))],
        out_specs=[tok(d), pl.BlockSpec((2, tm, d // 4), lambda i: (0, i, 0)),
                   pl.BlockSpec((e, tm), lambda i: (0, i))],
        out_shape=[jax.ShapeDtypeStruct((t, d), F32),
                   jax.ShapeDtypeStruct((2, t, d // 4), jnp.uint32),
                   jax.ShapeDtypeStruct((e, t), F32)],
        compiler_params=_cparams("arbitrary"),
        name="merge",
    )(attn, four, sa, sf, x2, mod, w_pa.astype(BF16), w_pf.astype(BF16), w_out.astype(BF16),
      norm2_g.reshape(1, d), wrh, wrl)


def _route_kernel(s_ref, b_ref, tri_ref, idx_ref, w_ref, rank_ref, cnt_ref, carry_ref):
    @pl.when(pl.program_id(0) == 0)
    def _():
        carry_ref[...] = jnp.zeros_like(carry_ref)

    sc = s_ref[...]
    e, tr = sc.shape
    row = lax.broadcasted_iota(jnp.int32, (e, tr), 0)
    v = sc + b_ref[...]
    sel = jnp.zeros((e, tr), F32)
    idxs, ws = [], []
    for _ in range(TOP_K):
        m = jnp.max(v, axis=0, keepdims=True)
        idx = jnp.min(jnp.where(v == m, row, e), axis=0, keepdims=True)
        oh = row == idx
        ws.append(jnp.sum(jnp.where(oh, sc, 0.0), axis=0, keepdims=True))
        idxs.append(idx)
        v = jnp.where(oh, -jnp.inf, v)
        sel = sel + oh.astype(F32)
    wsum = ws[0]
    for w in ws[1:]:
        wsum = wsum + w
    selb = sel.astype(BF16)
    cum = jnp.dot(selb, tri_ref[...], preferred_element_type=F32) + carry_ref[...]
    for kk in range(TOP_K):
        oh = row == idxs[kk]
        rk = jnp.sum(jnp.where(oh, cum, 0.0), axis=0, keepdims=True)
        idx_ref[kk:kk + 1, :] = idxs[kk]
        rank_ref[kk:kk + 1, :] = rk.astype(jnp.int32)
        w_ref[kk:kk + 1, :] = ws[kk] / wsum * ROUTED_SCALE
    tot = carry_ref[...] + jnp.dot(selb, jnp.ones((tr, tr), BF16), preferred_element_type=F32)
    carry_ref[...] = tot
    cnt_ref[...] = tot


def _route(scores_t, router_bias):
    e, t = scores_t.shape
    tr = _tile(t, 256)
    tri = jnp.asarray(np.triu(np.ones((tr, tr), np.float32), 1), BF16)
    bias = jnp.broadcast_to(router_bias.reshape(e, 1), (e, tr)).astype(F32)
    blk = pl.BlockSpec((TOP_K, tr), lambda i: (0, i))
    return pl.pallas_call(
        _route_kernel,
        grid=(t // tr,),
        in_specs=[pl.BlockSpec((e, tr), lambda i: (0, i)),
                  pl.BlockSpec((e, tr), lambda i: (0, 0)),
                  pl.BlockSpec((tr, tr), lambda i: (0, 0))],
        out_specs=[blk, blk, blk, pl.BlockSpec((e, tr), lambda i: (0, 0))],
        out_shape=[jax.ShapeDtypeStruct((TOP_K, t), jnp.int32),
                   jax.ShapeDtypeStruct((TOP_K, t), F32),
                   jax.ShapeDtypeStruct((TOP_K, t), jnp.int32),
                   jax.ShapeDtypeStruct((e, tr), F32)],
        scratch_shapes=[pltpu.VMEM((e, tr), F32)],
        compiler_params=_cparams("arbitrary"),
        name="route",
    )(scores_t, bias, tri)


def _slots_kernel(idx_ref, rank_ref, ps_ref, slot_ref):
    ps = ps_ref[...]
    row = lax.broadcasted_iota(jnp.int32, ps.shape, 0)
    for kk in range(TOP_K):
        oh = row == idx_ref[kk:kk + 1, :]
        start = jnp.sum(jnp.where(oh, ps, 0), axis=0, keepdims=True)
        slot_ref[kk:kk + 1, :] = start + rank_ref[kk:kk + 1, :]


def _slots(idx_t, rank_t, p_start):
    _, t = idx_t.shape
    e = p_start.shape[0]
    ts = _tile(t, 512)
    ps = jnp.broadcast_to(p_start.reshape(e, 1), (e, ts))
    blk = pl.BlockSpec((TOP_K, ts), lambda i: (0, i))
    return pl.pallas_call(
        _slots_kernel,
        grid=(t // ts,),
        in_specs=[blk, blk, pl.BlockSpec((e, ts), lambda i: (0, 0))],
        out_specs=blk,
        out_shape=jax.ShapeDtypeStruct((TOP_K, t), jnp.int32),
        compiler_params=_cparams("arbitrary"),
        name="slots",
    )(idx_t, rank_t, ps)


def _dispatch(h2p, slot_t, n_slots):
    _, t, c = h2p.shape
    k = slot_t.shape[0]
    win = _tile(2 * t, SC_WINDOW)
    rows = h2p.reshape(2 * t, c)
    dest = jnp.concatenate([slot_t, slot_t + n_slots], axis=1)
    mesh = plsc.VectorSubcoreMesh(core_axis_name="core", subcore_axis_name="subcore")

    @pl.kernel(out_type=jax.ShapeDtypeStruct((2 * n_slots, c), h2p.dtype), mesh=mesh,
               scratch_types=[])
    def scatter_rows(x_hbm, s_hbm, o_hbm):
        def body(x_vmem, s_vmem):
            pltpu.sync_copy(x_vmem, o_hbm.at[s_vmem.at[0]])

        pltpu.emit_pipeline(
            body, grid=(2 * t // win, k),
            in_specs=[pl.BlockSpec((win, c), lambda i, j: (i, 0)),
                      pl.BlockSpec((1, win), lambda i, j: (j, i))],
            out_specs=[], core_axis_name=("core", "subcore"),
            dimension_semantics=(pltpu.PARALLEL, pltpu.ARBITRARY))(x_hbm, s_hbm)

    return scatter_rows(rows, dest).reshape(2, n_slots, c)


def _expert_kernel(be_ref, run_ref, rex_ref, bst_ref, bsz_ref, nu_ref, xs_hbm, wg_hbm, wu_hbm,
                   wd_hbm, ys_hbm, xbuf, ybuf, wgf, wuf, wdf, wgb, wub, wdb, xsem, ysem, wsem,
                   *, n_slots):
    nu = nu_ref[0]
    unit = ROW_UNIT

    def x_stream(blk, slot, start):
        r0 = pl.multiple_of(bst_ref[blk], unit)
        for ch in range(MAX_UNITS):
            @pl.when(ch * unit < bsz_ref[blk])
            def _():
                for h in range(2):
                    cp = pltpu.make_async_copy(xs_hbm.at[h, pl.ds(r0 + ch * unit, unit)],
                                               xbuf.at[slot, h, pl.ds(ch * unit, unit)],
                                               xsem.at[slot])
                    cp.start() if start else cp.wait()

    def y_copies(r0, slot, m):
        return [pltpu.make_async_copy(ybuf.at[slot, h, pl.ds(0, m)],
                                      ys_hbm.at[h, pl.ds(pl.multiple_of(r0, unit), m)],
                                      ysem.at[slot]) for h in range(2)]

    def y_wait(blk, slot):
        for n in range(1, MAX_UNITS + 1):
            @pl.when(bsz_ref[blk] == n * unit)
            def _():
                for cp in y_copies(bst_ref[blk], slot, n * unit):
                    cp.wait()

    def weight_copies(e, which):
        return (pltpu.make_async_copy(wg_hbm.at[e], wgf.at[which], wsem.at[which]),
                pltpu.make_async_copy(wu_hbm.at[e], wuf.at[which], wsem.at[which]),
                pltpu.make_async_copy(wd_hbm.at[e], wdf.at[which], wsem.at[which]))

    for j in range(W_RING - 1):
        @pl.when(rex_ref[j] >= 0)
        def _():
            for cp in weight_copies(rex_ref[j], j):
                cp.start(priority=1)
    for j in range(X_RING - 1):
        @pl.when(j < nu)
        def _():
            x_stream(j, j, True)

    def step(i, carry):
        ahead = i + X_RING - 1

        @pl.when(ahead < nu)
        def _():
            x_stream(ahead, ahead % X_RING, True)

        prev = be_ref[jnp.maximum(i - 1, 0)]

        @pl.when(jnp.logical_or(i == 0, be_ref[i] != prev))
        def _():
            r = run_ref[i]
            par = r % W_RING
            for cp in weight_copies(be_ref[i], par):
                cp.wait()
            wgb[...] = wgf[par].astype(BF16)
            wub[...] = wuf[par].astype(BF16)
            wdb[...] = wdf[par].astype(BF16)
            later = rex_ref[r + W_RING - 1]

            @pl.when(later >= 0)
            def _():
                for cp in weight_copies(later, (r + W_RING - 1) % W_RING):
                    cp.start(priority=1)

        slot = i % X_RING
        x_stream(i, slot, False)
        out = i % 2

        @pl.when(i >= 2)
        def _():
            y_wait(i - 2, out)

        for n in range(1, MAX_UNITS + 1):
            @pl.when(bsz_ref[i] == n * unit)
            def _():
                m = n * unit
                x = _unpack_halves(xbuf[slot, 0, pl.ds(0, m)], xbuf[slot, 1, pl.ds(0, m)])
                g = jnp.dot(x, wgb[...], preferred_element_type=F32)
                u = jnp.dot(x, wub[...], preferred_element_type=F32)
                a = (_silu(g) * u).astype(BF16)
                y0, y1 = _pack_halves(jnp.dot(a, wdb[...], preferred_element_type=F32))
                ybuf[out, 0, pl.ds(0, m)] = y0
                ybuf[out, 1, pl.ds(0, m)] = y1
                for cp in y_copies(bst_ref[i], out, m):
                    cp.start()
        return carry

    lax.fori_loop(0, nu, step, 0)

    @pl.when(nu >= 2)
    def _():
        y_wait(nu - 2, nu % 2)
    y_wait(nu - 1, (nu - 1) % 2)

    ybuf[0, :, pl.ds(0, unit)] = jnp.zeros((2, unit, ybuf.shape[3]), ybuf.dtype)
    used = bst_ref[nu - 1] + bsz_ref[nu - 1]
    spare = (n_slots - used) // unit

    def zero_start(j, carry):
        for cp in y_copies(used + j * unit, 0, unit):
            cp.start()
        return carry

    def zero_wait(j, carry):
        for cp in y_copies(used + j * unit, 0, unit):
            cp.wait()
        return carry
    lax.fori_loop(0, spare, zero_start, 0)
    lax.fori_loop(0, spare, zero_wait, 0)


def _experts(blk_expert, blk_run, run_expert, blk_start, blk_size, nblk_used, xs, w_g, w_u, w_d):
    d, f = w_g.shape[1], w_g.shape[2]
    rows = ROW_UNIT * MAX_UNITS
    n_slots = xs.shape[1]
    c = xs.shape[2]
    hbm = pl.BlockSpec(memory_space=pl.ANY)
    return pl.pallas_call(
        functools.partial(_expert_kernel, n_slots=n_slots),
        grid_spec=pltpu.PrefetchScalarGridSpec(
            num_scalar_prefetch=6,
            grid=(1,),
            in_specs=[hbm, hbm, hbm, hbm],
            out_specs=hbm,
            scratch_shapes=[pltpu.VMEM((X_RING, 2, rows, c), jnp.uint32),
                            pltpu.VMEM((2, 2, rows, c), jnp.uint32),
                            pltpu.VMEM((W_RING, d, f), F32), pltpu.VMEM((W_RING, d, f), F32),
                            pltpu.VMEM((W_RING, f, d), F32),
                            pltpu.VMEM((d, f), BF16), pltpu.VMEM((d, f), BF16),
                            pltpu.VMEM((f, d), BF16),
                            pltpu.SemaphoreType.DMA((X_RING,)), pltpu.SemaphoreType.DMA((2,)),
                            pltpu.SemaphoreType.DMA((W_RING,))]),
        out_shape=jax.ShapeDtypeStruct((2, n_slots, c), jnp.uint32),
        compiler_params=_cparams("arbitrary"),
        name="experts",
    )(blk_expert, blk_run, run_expert, blk_start, blk_size, nblk_used, xs, w_g, w_u, w_d)


def _gather(ys, slot_t):
    _, n_slots, c = ys.shape
    k, t = slot_t.shape
    p = 2 * k * t
    win = _tile(p, SC_WINDOW)
    src = jnp.concatenate([slot_t, slot_t + n_slots], axis=0).reshape(1, p)
    mesh = plsc.VectorSubcoreMesh(core_axis_name="core", subcore_axis_name="subcore")

    @pl.kernel(out_type=jax.ShapeDtypeStruct((p, c), ys.dtype), mesh=mesh, scratch_types=[])
    def gather_rows(y_hbm, s_hbm, o_hbm):
        def body(s_vmem, o_vmem):
            pltpu.sync_copy(y_hbm.at[s_vmem.at[0]], o_vmem)

        pltpu.emit_pipeline(
            body, grid=(p // win,),
            in_specs=[pl.BlockSpec((1, win), lambda i: (0, i))],
            out_specs=[pl.BlockSpec((win, c), lambda i: (i, 0))],
            core_axis_name=("core", "subcore"),
            dimension_semantics=(pltpu.PARALLEL,))(s_hbm, o_hbm)

    return gather_rows(ys.reshape(2 * n_slots, c), src).reshape(2, k, t, c)


def _combine_kernel(y_ref, w_ref, x1_ref, h2_ref, mod_ref, wsg_ref, wsu_ref, wsd_ref, o_ref):
    hb = _unpack_halves(h2_ref[0], h2_ref[1])
    g = jnp.dot(hb, wsg_ref[...], preferred_element_type=F32)
    u = jnp.dot(hb, wsu_ref[...], preferred_element_type=F32)
    acc = jnp.dot((_silu(g) * u).astype(BF16), wsd_ref[...], preferred_element_type=F32)
    w = w_ref[...].T
    for kk in range(TOP_K):
        acc = acc + w[:, kk:kk + 1] * _unpack_halves(y_ref[0, kk], y_ref[1, kk]).astype(F32)
    g2 = mod_ref[0][5:6]
    o_ref[...] = x1_ref[...] + g2 * acc


def _combine(y_tok, w_tk, x1, h2p, mod, w_sg, w_su, w_sd, seq):
    t, d = x1.shape
    c = d // 4
    tc = _tile(seq, 256)
    tpb = seq // tc
    full = lambda shp: pl.BlockSpec(shp, lambda i: (0,) * len(shp))
    tok = lambda w: pl.BlockSpec((tc, w), lambda i: (i, 0))
    return pl.pallas_call(
        _combine_kernel,
        grid=(t // tc,),
        in_specs=[pl.BlockSpec((2, TOP_K, tc, c), lambda i: (0, 0, i, 0)),
                  pl.BlockSpec((TOP_K, tc), lambda i: (0, i)), tok(d),
                  pl.BlockSpec((2, tc, c), lambda i: (0, i, 0)),
                  pl.BlockSpec((1, N_ADA, d), lambda i: (i // tpb, 0, 0)),
                  full(w_sg.shape), full(w_su.shape), full(w_sd.shape)],
        out_specs=tok(d),
        out_shape=jax.ShapeDtypeStruct((t, d), F32),
        compiler_params=_cparams("arbitrary"),
        name="combine",
    )(y_tok, w_tk, x1, h2p, mod, w_sg.astype(BF16), w_su.astype(BF16), w_sd.astype(BF16))


def _layer(x, c, w_ada, b_ada, norm1_g, w_in, q_a_norm_g, w_uq, kv_a_norm_g, w_ukv,
           q_norm_g, k_norm_g, w_proj_attn, w_proj_fourier, w_out, norm2_g,
           w_router, router_bias, w_exp_gate, w_exp_up, w_exp_down,
           w_sh_gate, w_sh_up, w_sh_down):
    bsz, seq, d = x.shape
    t = bsz * seq
    e = w_router.shape[1]
    x2 = x.reshape(t, d)

    mod = _ada(c, w_ada, b_ada).reshape(bsz, N_ADA, d)
    q, k, v, zf, sa, sf = _inproj(x2, mod, norm1_g, w_in, q_a_norm_g, w_uq, kv_a_norm_g,
                                  w_ukv, q_norm_g, k_norm_g, bsz, seq)
    per_batch = lambda a: a.reshape(bsz, seq, a.shape[1])
    attn = _attention(per_batch(q), per_batch(k), per_batch(v)).reshape(t, N_HEADS * V_DIM)
    four = _fourier(zf.reshape(bsz, seq, zf.shape[1])).reshape(t, zf.shape[1])
    x1, h2, scores_t = _merge(attn, four, sa, sf, x2, mod, w_proj_attn, w_proj_fourier,
                              w_out, norm2_g, w_router, seq)

    idx_t, w_t, rank_t, cnt = _route(scores_t, router_bias)
    counts = cnt[:, 0].astype(jnp.int32)
    unit, bmax = ROW_UNIT, ROW_UNIT * MAX_UNITS
    n_slots = t * TOP_K + e * unit
    padded = ((counts + unit - 1) // unit) * unit
    p_end = jnp.cumsum(padded)
    p_start = p_end - padded
    eid = jnp.arange(e, dtype=jnp.int32)
    nb = (padded + bmax - 1) // bmax
    b_end = jnp.cumsum(nb)
    b_first = b_end - nb
    nblk_used = b_end[-1].astype(jnp.int32)
    nblk = -(-(t * TOP_K) // bmax) + e
    bid = jnp.minimum(jnp.arange(nblk, dtype=jnp.int32), nblk_used - 1)
    blk_expert = jnp.clip(jnp.sum((b_end[None, :] <= bid[:, None]).astype(jnp.int32), axis=1),
                          0, e - 1)
    onehot = blk_expert[:, None] == eid[None, :]
    look = lambda tab: jnp.sum(jnp.where(onehot, tab[None, :], 0), axis=1)
    piece = bid - look(b_first)
    blk_start = (look(p_start) + piece * bmax).astype(jnp.int32)
    blk_size = jnp.clip(look(padded) - piece * bmax, 0, bmax).astype(jnp.int32)

    slot_t = _slots(idx_t, rank_t, p_start.astype(jnp.int32))
    xs = _dispatch(h2, slot_t, n_slots)
    used = counts > 0
    run = jnp.cumsum(used.astype(jnp.int32)) - 1
    blk_run = look(run).astype(jnp.int32)
    rid = jnp.arange(e + W_RING, dtype=jnp.int32)
    match = jnp.logical_and(used[None, :], run[None, :] == rid[:, None])
    run_expert = jnp.where(jnp.any(match, axis=1),
                           jnp.sum(jnp.where(match, eid[None, :], 0), axis=1), -1).astype(jnp.int32)
    ys = _experts(blk_expert, blk_run, run_expert, blk_start, blk_size, nblk_used.reshape(1), xs,
                  w_exp_gate, w_exp_up, w_exp_down)
    out = _combine(_gather(ys, slot_t), w_t, x1, h2, mod, w_sh_gate, w_sh_up, w_sh_down, seq)
    return out.reshape(bsz, seq, d)


def kernel(x, c, w_ada, b_ada, norm1_g, w_in, q_a_norm_g, w_uq, kv_a_norm_g, w_ukv, q_norm_g,
           k_norm_g, w_proj_attn, w_proj_fourier, w_out, norm2_g, w_router, router_bias,
           w_exp_gate, w_exp_up, w_exp_down, w_sh_gate, w_sh_up, w_sh_down):
    for l in range(w_ada.shape[0]):
        x = _layer(x, c, w_ada[l], b_ada[l], norm1_g[l], w_in[l], q_a_norm_g[l], w_uq[l],
                   kv_a_norm_g[l], w_ukv[l], q_norm_g[l], k_norm_g[l], w_proj_attn[l],
                   w_proj_fourier[l], w_out[l], norm2_g[l], w_router[l], router_bias[l],
                   w_exp_gate[l], w_exp_up[l], w_exp_down[l], w_sh_gate[l], w_sh_up[l],
                   w_sh_down[l])
    return x
```

```python
import functools
import math

import numpy as np
import jax
import jax.numpy as jnp
from jax import lax
from jax.experimental import pallas as pl
from jax.experimental.pallas import tpu as pltpu
from jax.experimental.pallas import tpu_sc as plsc

N_HEADS = 8
QK_NOPE = 64
QK_ROPE = 32
V_DIM = 64
FOURIER_GROUP = 64
TOP_K = 8
ROUTED_SCALE = 2.5
EPS = 1e-6
ROPE_THETA = 10000.0
N_ADA = 6

LANES = 128
ROW_UNIT = 128
MAX_UNITS = 8
X_RING = 4
W_RING = 5
SC_WINDOW = 128
VMEM_LIMIT = 48 * 1024 * 1024

F32 = jnp.float32
BF16 = jnp.bfloat16


def _cparams(*sem):
    return pltpu.CompilerParams(dimension_semantics=sem, vmem_limit_bytes=VMEM_LIMIT)


def _tile(n, pref):
    t = min(n, pref)
    assert n % t == 0, (n, pref)
    return t


def _silu(v):
    return v * jax.nn.sigmoid(v)


def _pack_halves(m):
    d = m.shape[1]
    lo = lax.bitcast_convert_type(m[:, :d // 2].astype(BF16).astype(F32), jnp.uint32)
    hi = lax.bitcast_convert_type(m[:, d // 2:].astype(BF16).astype(F32), jnp.uint32)
    w = (lo >> 16) | (hi & jnp.uint32(0xFFFF0000))
    return w[:, :d // 4], w[:, d // 4:]


def _unpack_halves(w0, w1):
    def lo(w):
        return lax.bitcast_convert_type(w << 16, F32)

    def hi(w):
        return lax.bitcast_convert_type(w & jnp.uint32(0xFFFF0000), F32)
    return jnp.concatenate([lo(w0), lo(w1), hi(w0), hi(w1)], axis=1).astype(BF16)


def _ada_kernel(c_ref, w_ref, b_ref, o_ref):
    a = _silu(c_ref[...])
    o_ref[...] = jnp.dot(a, w_ref[...], preferred_element_type=F32,
                         precision=lax.Precision.HIGHEST) + b_ref[...]


def _ada(c, w_ada, b_ada):
    bsz, d = c.shape
    n = w_ada.shape[1]
    tn = _tile(n, d)
    return pl.pallas_call(
        _ada_kernel,
        grid=(n // tn,),
        in_specs=[pl.BlockSpec((bsz, d), lambda j: (0, 0)),
                  pl.BlockSpec((d, tn), lambda j: (0, j)),
                  pl.BlockSpec((1, tn), lambda j: (0, j))],
        out_specs=pl.BlockSpec((bsz, tn), lambda j: (0, j)),
        out_shape=jax.ShapeDtypeStruct((bsz, n), F32),
        compiler_params=_cparams("arbitrary"),
        name="ada",
    )(c, w_ada, b_ada.reshape(1, n))


def _head_norm_rope(t, trot, a, b):
    ms = jnp.sum(t * t, axis=-1, keepdims=True) * (1.0 / (QK_NOPE + QK_ROPE))
    return (t * a + trot * b) * lax.rsqrt(ms + EPS)


def _inproj_kernel(x_ref, mod_ref, g1_ref, wa_ref, wf_ref, wga_ref, wgf_ref,
                   gq_ref, gkv_ref, wuq_ref, wkv_ref,
                   aq_ref, bq_ref, ak_ref, bk_ref,
                   q_ref, k_ref, v_ref, zf_ref, sa_ref, sf_ref, *, ql, kvl):
    x = x_ref[...]
    mod = mod_ref[0]
    sh1, sc1 = mod[0:1], mod[1:2]
    r = lax.rsqrt(jnp.mean(x * x, axis=-1, keepdims=True) + EPS)
    h = (x * r * g1_ref[...]) * (1.0 + sc1) + sh1
    hb = h.astype(BF16)

    zf_ref[...] = jnp.dot(hb, wf_ref[...], preferred_element_type=F32).astype(BF16)
    sa_ref[...] = jax.nn.sigmoid(
        jnp.dot(hb, wga_ref[...], preferred_element_type=F32)).astype(BF16)
    sf_ref[...] = jax.nn.sigmoid(
        jnp.dot(hb, wgf_ref[...], preferred_element_type=F32)).astype(BF16)

    za = jnp.dot(hb, wa_ref[...], preferred_element_type=F32)
    zq = za[:, :ql]
    cq = zq * lax.rsqrt(jnp.mean(zq * zq, axis=-1, keepdims=True) + EPS) * gq_ref[...]
    qall = jnp.dot(cq.astype(BF16), wuq_ref[...], preferred_element_type=F32)

    zk = za[:, ql:]
    kvn = zk[:, :kvl]
    rk = lax.rsqrt(jnp.mean(kvn * kvn, axis=-1, keepdims=True) + EPS)
    lane = lax.broadcasted_iota(jnp.int32, zk.shape, 1)
    u = zk * jnp.where(lane < kvl, rk, 1.0) * gkv_ref[...]
    kvall = jnp.dot(u.astype(BF16), wkv_ref[...], preferred_element_type=F32)

    aq, bq, ak, bk = aq_ref[...], bq_ref[...], ak_ref[...], bk_ref[...]
    hw = N_HEADS * LANES
    for hd in range(N_HEADS):
        lo, hi = hd * LANES, (hd + 1) * LANES
        q_ref[:, lo:hi] = _head_norm_rope(qall[:, lo:hi], qall[:, hw + lo:hw + hi],
                                       aq, bq).astype(BF16)
        k_ref[:, lo:hi] = _head_norm_rope(kvall[:, lo:hi], kvall[:, hw + lo:hw + hi],
                                       ak, bk).astype(BF16)
    v_ref[...] = kvall[:, 2 * hw:].astype(BF16)


def _rope_tables(seq):
    half = QK_ROPE // 2
    pos = np.arange(seq, dtype=np.float64)
    inv = ROPE_THETA ** (-np.arange(0, QK_ROPE, 2, dtype=np.float64) / QK_ROPE)
    ang = pos[:, None] * inv[None, :]
    c, s = np.cos(ang), np.sin(ang)
    cos = np.ones((seq, LANES)); sin = np.zeros((seq, LANES))
    cos[:, QK_NOPE:QK_NOPE + half] = c
    cos[:, QK_NOPE + half:QK_NOPE + QK_ROPE] = c
    sin[:, QK_NOPE:QK_NOPE + half] = -s
    sin[:, QK_NOPE + half:QK_NOPE + QK_ROPE] = s
    return jnp.asarray(cos, F32), jnp.asarray(sin, F32)


def _partner_columns(w):
    half = QK_ROPE // 2
    lo, mid, hi = QK_NOPE, QK_NOPE + half, QK_NOPE + QK_ROPE
    z = jnp.zeros_like(w)
    return jnp.concatenate([z[..., :lo], w[..., mid:hi], w[..., lo:mid], z[..., hi:]], axis=-1)


def _inproj(x2, mod, norm1_g, w_in, q_a_g, w_uq, kv_a_g, w_ukv, q_g, k_g, bsz, seq):
    t, d = x2.shape
    ql, kvl = q_a_g.shape[0], kv_a_g.shape[0]
    hq = QK_NOPE + QK_ROPE
    fw = w_in.shape[1] - ql - kvl - QK_ROPE - 2 * d
    o1, o2, o3, o4, o5 = ql, ql + kvl, ql + kvl + QK_ROPE, ql + kvl + QK_ROPE + fw, \
        ql + kvl + QK_ROPE + fw + d
    assert ql % LANES == 0 and kvl % LANES == 0

    wa = jnp.concatenate([w_in[:, :o3], jnp.zeros((d, LANES - QK_ROPE), F32)], axis=1).astype(BF16)
    wf = w_in[:, o3:o4].astype(BF16)
    wga = w_in[:, o4:o5].astype(BF16)
    wgf = w_in[:, o5:].astype(BF16)

    wuq = w_uq.reshape(ql, N_HEADS, hq)
    wuq = jnp.pad(wuq, ((0, 0), (0, 0), (0, LANES - hq)))
    wuq = jnp.concatenate([wuq.reshape(ql, N_HEADS * LANES),
                           _partner_columns(wuq).reshape(ql, N_HEADS * LANES)], axis=1).astype(BF16)
    wukv = w_ukv.reshape(kvl, N_HEADS, QK_NOPE + V_DIM)
    wk = jnp.pad(wukv[:, :, :QK_NOPE], ((0, 0), (0, 0), (0, LANES - QK_NOPE)))
    place = jnp.zeros((QK_ROPE, N_HEADS, LANES), F32)
    place = place.at[jnp.arange(QK_ROPE), :, QK_NOPE + jnp.arange(QK_ROPE)].set(1.0)
    wk = jnp.concatenate([wk, place, jnp.zeros((LANES - QK_ROPE, N_HEADS, LANES), F32)], axis=0)
    wv = jnp.concatenate([wukv[:, :, QK_NOPE:], jnp.zeros((LANES, N_HEADS, V_DIM), F32)], axis=0)
    wkv = jnp.concatenate([wk.reshape(kvl + LANES, N_HEADS * LANES),
                           _partner_columns(wk).reshape(kvl + LANES, N_HEADS * LANES),
                           wv.reshape(kvl + LANES, N_HEADS * V_DIM)], axis=1).astype(BF16)

    gkv = jnp.concatenate([kv_a_g, jnp.ones((LANES,), F32)]).reshape(1, kvl + LANES)
    pad = jnp.zeros((LANES - hq,), F32)
    qg = jnp.concatenate([q_g * (hq ** -0.5), pad])
    kg = jnp.concatenate([k_g, pad])
    cos, sin = _rope_tables(seq)
    aq, bq = qg[None, :] * cos, _partner_columns(qg)[None, :] * sin
    ak, bk = kg[None, :] * cos, _partner_columns(kg)[None, :] * sin

    tm = _tile(seq, 512)
    tpb = seq // tm
    full = lambda shp: pl.BlockSpec(shp, lambda i: (0,) * len(shp))
    tok = lambda w: pl.BlockSpec((tm, w), lambda i: (i, 0))
    rope = pl.BlockSpec((tm, LANES), lambda i: (i % tpb, 0))
    return pl.pallas_call(
        functools.partial(_inproj_kernel, ql=ql, kvl=kvl),
        grid=(t // tm,),
        in_specs=[tok(d),
                  pl.BlockSpec((1, N_ADA, d), lambda i: (i // tpb, 0, 0)),
                  full((1, d)), full(wa.shape), full(wf.shape), full(wga.shape), full(wgf.shape),
                  full((1, ql)), full(gkv.shape), full(wuq.shape), full(wkv.shape),
                  rope, rope, rope, rope],
        out_specs=[tok(N_HEADS * LANES), tok(N_HEADS * LANES), tok(N_HEADS * V_DIM), tok(fw), tok(d), tok(d)],
        out_shape=[jax.ShapeDtypeStruct((t, N_HEADS * LANES), BF16),
                   jax.ShapeDtypeStruct((t, N_HEADS * LANES), BF16),
                   jax.ShapeDtypeStruct((t, N_HEADS * V_DIM), BF16),
                   jax.ShapeDtypeStruct((t, fw), BF16),
                   jax.ShapeDtypeStruct((t, d), BF16),
                   jax.ShapeDtypeStruct((t, d), BF16)],
        compiler_params=_cparams("arbitrary"),
        name="inproj",
    )(x2, mod, norm1_g.reshape(1, d), wa, wf, wga, wgf, q_a_g.reshape(1, ql), gkv, wuq, wkv,
      aq, bq, ak, bk)


def _attn_kernel(q_ref, k_ref, v_ref, o_ref, s_ref):
    def scores(hd):
        s_ref[hd % 2] = lax.dot_general(
            q_ref[0, :, hd * LANES:(hd + 1) * LANES], k_ref[0, :, hd * LANES:(hd + 1) * LANES],
            (((1,), (1,)), ((), ())), preferred_element_type=F32)

    scores(0)
    for hd in range(N_HEADS):
        if hd + 1 < N_HEADS:
            scores(hd + 1)
        s = s_ref[hd % 2]
        m = jnp.max(s, axis=-1, keepdims=True)
        p = jnp.exp(s - m)
        l = jnp.sum(p, axis=-1, keepdims=True)
        o = jnp.dot(p.astype(BF16), v_ref[0, :, hd * V_DIM:(hd + 1) * V_DIM],
                    preferred_element_type=F32)
        o_ref[0, :, hd * V_DIM:(hd + 1) * V_DIM] = (o / l).astype(BF16)


def _attention(q, k, v):
    bsz, seq, _ = q.shape
    tq = _tile(seq, 512)
    return pl.pallas_call(
        _attn_kernel,
        grid=(bsz, seq // tq),
        in_specs=[pl.BlockSpec((1, tq, N_HEADS * LANES), lambda b, j: (b, j, 0)),
                  pl.BlockSpec((1, seq, N_HEADS * LANES), lambda b, j: (b, 0, 0)),
                  pl.BlockSpec((1, seq, N_HEADS * V_DIM), lambda b, j: (b, 0, 0))],
        out_specs=pl.BlockSpec((1, tq, N_HEADS * V_DIM), lambda b, j: (b, j, 0)),
        out_shape=jax.ShapeDtypeStruct((bsz, seq, N_HEADS * V_DIM), BF16),
        scratch_shapes=[pltpu.VMEM((2, tq, seq), F32)],
        compiler_params=_cparams("arbitrary", "arbitrary"),
        name="attn",
    )(q, k, v)


def _fourier_kernel(z_ref, wc_ref, ws_ref, tab_ref, o_ref, u_ref, *, seq):
    @pl.when(pl.program_id(1) == 0)
    def _():
        z = z_ref[0]
        u_ref[:seq, :] = jnp.dot(z, wc_ref[...], preferred_element_type=F32).astype(BF16)
        u_ref[seq:, :] = jnp.dot(z, ws_ref[...], preferred_element_type=F32).astype(BF16)

    o_ref[0] = jnp.dot(tab_ref[...], u_ref[...], preferred_element_type=F32).astype(BF16)


def _fourier_tables(seq, fw):
    g = FOURIER_GROUP
    n = np.arange(seq, dtype=np.int64)
    ang = 2.0 * np.pi * ((n[:, None] * n[None, :]) % seq).astype(np.float64) / seq
    tab = np.concatenate([np.cos(ang), -np.sin(ang)], axis=1)
    c = np.arange(g, dtype=np.int64)
    angc = 2.0 * np.pi * ((c[:, None] * c[None, :]) % g).astype(np.float64) / g
    scale = 1.0 / math.sqrt(seq * g)
    eye = np.eye(fw // g)
    wc = np.kron(eye, np.cos(angc) * scale)
    ws = np.kron(eye, np.sin(angc) * scale)
    return (jnp.asarray(tab, F32).astype(BF16), jnp.asarray(wc, F32).astype(BF16),
            jnp.asarray(ws, F32).astype(BF16))


def _fourier(zf):
    bsz, seq, fw = zf.shape
    tab, wc, ws = _fourier_tables(seq, fw)
    tr = _tile(seq, 512)
    return pl.pallas_call(
        functools.partial(_fourier_kernel, seq=seq),
        grid=(bsz, seq // tr),
        in_specs=[pl.BlockSpec((1, seq, fw), lambda b, j: (b, 0, 0)),
                  pl.BlockSpec((fw, fw), lambda b, j: (0, 0)),
                  pl.BlockSpec((fw, fw), lambda b, j: (0, 0)),
                  pl.BlockSpec((tr, 2 * seq), lambda b, j: (j, 0))],
        out_specs=pl.BlockSpec((1, tr, fw), lambda b, j: (b, j, 0)),
        out_shape=jax.ShapeDtypeStruct((bsz, seq, fw), BF16),
        scratch_shapes=[pltpu.VMEM((2 * seq, fw), BF16)],
        compiler_params=_cparams("arbitrary", "arbitrary"),
        name="fourier",
    )(zf, wc, ws, tab)


def _merge_kernel(a_ref, f_ref, sa_ref, sf_ref, x_ref, mod_ref, wpa_ref, wpf_ref, wo_ref,
                  g2_ref, wrh_ref, wrl_ref, x1_ref, h2_ref, sc_ref):
    ya = jnp.dot(a_ref[...], wpa_ref[...], preferred_element_type=F32)
    yf = jnp.dot(f_ref[...], wpf_ref[...], preferred_element_type=F32)
    merged = sa_ref[...].astype(F32) * ya + sf_ref[...].astype(F32) * yf
    mod = mod_ref[0]
    g1, sh2, sc2 = mod[2:3], mod[3:4], mod[4:5]
    x1 = x_ref[...] + g1 * jnp.dot(merged.astype(BF16), wo_ref[...], preferred_element_type=F32)
    x1_ref[...] = x1
    r = lax.rsqrt(jnp.mean(x1 * x1, axis=-1, keepdims=True) + EPS)
    h2 = (x1 * r * g2_ref[...]) * (1.0 + sc2) + sh2
    h2_ref[0], h2_ref[1] = _pack_halves(h2)
    hh = h2.astype(BF16)
    hl = (h2 - hh.astype(F32)).astype(BF16)
    nt = (((1,), (1,)), ((), ()))
    lt = (lax.dot_general(wrh_ref[...], hh, nt, preferred_element_type=F32)
          + lax.dot_general(wrh_ref[...], hl, nt, preferred_element_type=F32)
          + lax.dot_general(wrl_ref[...], hh, nt, preferred_element_type=F32))
    sc_ref[...] = jax.nn.sigmoid(lt)


def _merge(attn, four, sa, sf, x2, mod, w_pa, w_pf, w_out, norm2_g, w_router, seq):
    t, d = x2.shape
    e = w_router.shape[1]
    wrt = w_router.T
    wrh = wrt.astype(BF16)
    wrl = (wrt - wrh.astype(F32)).astype(BF16)
    tm = _tile(seq, 512)
    tpb = seq // tm
    full = lambda shp: pl.BlockSpec(shp, lambda i: (0,) * len(shp))
    tok = lambda w: pl.BlockSpec((tm, w), lambda i: (i, 0))
    return pl.pallas_call(
        _merge_kernel,
        grid=(t // tm,),
        in_specs=[tok(attn.shape[1]), tok(four.shape[1]), tok(d), tok(d), tok(d),
                  pl.BlockSpec((1, N_ADA, d), lambda i: (i // tpb, 0, 0)),
                  full(w_pa.shape), full(w_pf.shape), full(w_out.shape), full((1, d)),
                  full((e, d)), full((e, d))],
        out_specs=[tok(d), pl.BlockSpec((2, tm, d // 4), lambda i: (0, i, 0)),
                   pl.BlockSpec((e, tm), lambda i: (0, i))],
        out_shape=[jax.ShapeDtypeStruct((t, d), F32),
                   jax.ShapeDtypeStruct((2, t, d // 4), jnp.uint32),
                   jax.ShapeDtypeStruct((e, t), F32)],
        compiler_params=_cparams("arbitrary"),
        name="merge",
    )(attn, four, sa, sf, x2, mod, w_pa.astype(BF16), w_pf.astype(BF16), w_out.astype(BF16),
      norm2_g.reshape(1, d), wrh, wrl)


def _route_kernel(s_ref, b_ref, tri_ref, idx_ref, w_ref, rank_ref, cnt_ref, carry_ref):
    @pl.when(pl.program_id(0) == 0)
    def _():
        carry_ref[...] = jnp.zeros_like(carry_ref)

    sc = s_ref[...]
    e, tr = sc.shape
    row = lax.broadcasted_iota(jnp.int32, (e, tr), 0)
    v = sc + b_ref[...]
    sel = jnp.zeros((e, tr), F32)
    idxs, ws = [], []
    for _ in range(TOP_K):
        m = jnp.max(v, axis=0, keepdims=True)
        idx = jnp.min(jnp.where(v == m, row, e), axis=0, keepdims=True)
        oh = row == idx
        ws.append(jnp.sum(jnp.where(oh, sc, 0.0), axis=0, keepdims=True))
        idxs.append(idx)
        v = jnp.where(oh, -jnp.inf, v)
        sel = sel + oh.astype(F32)
    wsum = ws[0]
    for w in ws[1:]:
        wsum = wsum + w
    selb = sel.astype(BF16)
    cum = jnp.dot(selb, tri_ref[...], preferred_element_type=F32) + carry_ref[...]
    for kk in range(TOP_K):
        oh = row == idxs[kk]
        rk = jnp.sum(jnp.where(oh, cum, 0.0), axis=0, keepdims=True)
        idx_ref[kk:kk + 1, :] = idxs[kk]
        rank_ref[kk:kk + 1, :] = rk.astype(jnp.int32)
        w_ref[kk:kk + 1, :] = ws[kk] / wsum * ROUTED_SCALE
    tot = carry_ref[...] + jnp.dot(selb, jnp.ones((tr, tr), BF16), preferred_element_type=F32)
    carry_ref[...] = tot
    cnt_ref[...] = tot


def _route(scores_t, router_bias):
    e, t = scores_t.shape
    tr = _tile(t, 256)
    tri = jnp.asarray(np.triu(np.ones((tr, tr), np.float32), 1), BF16)
    bias = jnp.broadcast_to(router_bias.reshape(e, 1), (e, tr)).astype(F32)
    blk = pl.BlockSpec((TOP_K, tr), lambda i: (0, i))
    return pl.pallas_call(
        _route_kernel,
        grid=(t // tr,),
        in_specs=[pl.BlockSpec((e, tr), lambda i: (0, i)),
                  pl.BlockSpec((e, tr), lambda i: (0, 0)),
                  pl.BlockSpec((tr, tr), lambda i: (0, 0))],
        out_specs=[blk, blk, blk, pl.BlockSpec((e, tr), lambda i: (0, 0))],
        out_shape=[jax.ShapeDtypeStruct((TOP_K, t), jnp.int32),
                   jax.ShapeDtypeStruct((TOP_K, t), F32),
                   jax.ShapeDtypeStruct((TOP_K, t), jnp.int32),
                   jax.ShapeDtypeStruct((e, tr), F32)],
        scratch_shapes=[pltpu.VMEM((e, tr), F32)],
        compiler_params=_cparams("arbitrary"),
        name="route",
    )(scores_t, bias, tri)


def _slots_kernel(idx_ref, rank_ref, ps_ref, slot_ref):
    ps = ps_ref[...]
    row = lax.broadcasted_iota(jnp.int32, ps.shape, 0)
    for kk in range(TOP_K):
        oh = row == idx_ref[kk:kk + 1, :]
        start = jnp.sum(jnp.where(oh, ps, 0), axis=0, keepdims=True)
        slot_ref[kk:kk + 1, :] = start + rank_ref[kk:kk + 1, :]


def _slots(idx_t, rank_t, p_start):
    _, t = idx_t.shape
    e = p_start.shape[0]
    ts = _tile(t, 512)
    ps = jnp.broadcast_to(p_start.reshape(e, 1), (e, ts))
    blk = pl.BlockSpec((TOP_K, ts), lambda i: (0, i))
    return pl.pallas_call(
        _slots_kernel,
        grid=(t // ts,),
        in_specs=[blk, blk, pl.BlockSpec((e, ts), lambda i: (0, 0))],
        out_specs=blk,
        out_shape=jax.ShapeDtypeStruct((TOP_K, t), jnp.int32),
        compiler_params=_cparams("arbitrary"),
        name="slots",
    )(idx_t, rank_t, ps)


def _dispatch(h2p, slot_t, n_slots):
    _, t, c = h2p.shape
    k = slot_t.shape[0]
    win = _tile(2 * t, SC_WINDOW)
    rows = h2p.reshape(2 * t, c)
    dest = jnp.concatenate([slot_t, slot_t + n_slots], axis=1)
    mesh = plsc.VectorSubcoreMesh(core_axis_name="core", subcore_axis_name="subcore")

    @pl.kernel(out_type=jax.ShapeDtypeStruct((2 * n_slots, c), h2p.dtype), mesh=mesh,
               scratch_types=[])
    def scatter_rows(x_hbm, s_hbm, o_hbm):
        def body(x_vmem, s_vmem):
            pltpu.sync_copy(x_vmem, o_hbm.at[s_vmem.at[0]])

        pltpu.emit_pipeline(
            body, grid=(2 * t // win, k),
            in_specs=[pl.BlockSpec((win, c), lambda i, j: (i, 0)),
                      pl.BlockSpec((1, win), lambda i, j: (j, i))],
            out_specs=[], core_axis_name=("core", "subcore"),
            dimension_semantics=(pltpu.PARALLEL, pltpu.ARBITRARY))(x_hbm, s_hbm)

    return scatter_rows(rows, dest).reshape(2, n_slots, c)


def _expert_kernel(be_ref, run_ref, rex_ref, bst_ref, bsz_ref, nu_ref, xs_hbm, wg_hbm, wu_hbm,
                   wd_hbm, ys_hbm, xbuf, ybuf, wgf, wuf, wdf, wgb, wub, wdb, xsem, ysem, wsem,
                   *, n_slots):
    nu = nu_ref[0]
    unit = ROW_UNIT

    def x_stream(blk, slot, start):
        r0 = pl.multiple_of(bst_ref[blk], unit)
        for ch in range(MAX_UNITS):
            @pl.when(ch * unit < bsz_ref[blk])
            def _():
                for h in range(2):
                    cp = pltpu.make_async_copy(xs_hbm.at[h, pl.ds(r0 + ch * unit, unit)],
                                               xbuf.at[slot, h, pl.ds(ch * unit, unit)],
                                               xsem.at[slot])
                    cp.start() if start else cp.wait()

    def y_copies(r0, slot, m):
        return [pltpu.make_async_copy(ybuf.at[slot, h, pl.ds(0, m)],
                                      ys_hbm.at[h, pl.ds(pl.multiple_of(r0, unit), m)],
                                      ysem.at[slot]) for h in range(2)]

    def y_wait(blk, slot):
        for n in range(1, MAX_UNITS + 1):
            @pl.when(bsz_ref[blk] == n * unit)
            def _():
                for cp in y_copies(bst_ref[blk], slot, n * unit):
                    cp.wait()

    def weight_copies(e, which):
        return (pltpu.make_async_copy(wg_hbm.at[e], wgf.at[which], wsem.at[which]),
                pltpu.make_async_copy(wu_hbm.at[e], wuf.at[which], wsem.at[which]),
                pltpu.make_async_copy(wd_hbm.at[e], wdf.at[which], wsem.at[which]))

    for j in range(W_RING - 1):
        @pl.when(rex_ref[j] >= 0)
        def _():
            for cp in weight_copies(rex_ref[j], j):
                cp.start(priority=1)
    for j in range(X_RING - 1):
        @pl.when(j < nu)
        def _():
            x_stream(j, j, True)

    def step(i, carry):
        ahead = i + X_RING - 1

        @pl.when(ahead < nu)
        def _():
            x_stream(ahead, ahead % X_RING, True)

        prev = be_ref[jnp.maximum(i - 1, 0)]

        @pl.when(jnp.logical_or(i == 0, be_ref[i] != prev))
        def _():
            r = run_ref[i]
            par = r % W_RING
            for cp in weight_copies(be_ref[i], par):
                cp.wait()
            wgb[...] = wgf[par].astype(BF16)
            wub[...] = wuf[par].astype(BF16)
            wdb[...] = wdf[par].astype(BF16)
            later = rex_ref[r + W_RING - 1]

            @pl.when(later >= 0)
            def _():
                for cp in weight_copies(later, (r + W_RING - 1) % W_RING):
                    cp.start(priority=1)

        slot = i % X_RING
        x_stream(i, slot, False)
        out = i % 2

        @pl.when(i >= 2)
        def _():
            y_wait(i - 2, out)

        for n in range(1, MAX_UNITS + 1):
            @pl.when(bsz_ref[i] == n * unit)
            def _():
                m = n * unit
                x = _unpack_halves(xbuf[slot, 0, pl.ds(0, m)], xbuf[slot, 1, pl.ds(0, m)])
                g = jnp.dot(x, wgb[...], preferred_element_type=F32)
                u = jnp.dot(x, wub[...], preferred_element_type=F32)
                a = (_silu(g) * u).astype(BF16)
                y0, y1 = _pack_halves(jnp.dot(a, wdb[...], preferred_element_type=F32))
                ybuf[out, 0, pl.ds(0, m)] = y0
                ybuf[out, 1, pl.ds(0, m)] = y1
                for cp in y_copies(bst_ref[i], out, m):
                    cp.start()
        return carry

    lax.fori_loop(0, nu, step, 0)

    @pl.when(nu >= 2)
    def _():
        y_wait(nu - 2, nu % 2)
    y_wait(nu - 1, (nu - 1) % 2)

    ybuf[0, :, pl.ds(0, unit)] = jnp.zeros((2, unit, ybuf.shape[3]), ybuf.dtype)
    used = bst_ref[nu - 1] + bsz_ref[nu - 1]
    spare = (n_slots - used) // unit

    def zero_start(j, carry):
        for cp in y_copies(used + j * unit, 0, unit):
            cp.start()
        return carry

    def zero_wait(j, carry):
        for cp in y_copies(used + j * unit, 0, unit):
            cp.wait()
        return carry
    lax.fori_loop(0, spare, zero_start, 0)
    lax.fori_loop(0, spare, zero_wait, 0)


def _experts(blk_expert, blk_run, run_expert, blk_start, blk_size, nblk_used, xs, w_g, w_u, w_d):
    d, f = w_g.shape[1], w_g.shape[2]
    rows = ROW_UNIT * MAX_UNITS
    n_slots = xs.shape[1]
    c = xs.shape[2]
    hbm = pl.BlockSpec(memory_space=pl.ANY)
    return pl.pallas_call(
        functools.partial(_expert_kernel, n_slots=n_slots),
        grid_spec=pltpu.PrefetchScalarGridSpec(
            num_scalar_prefetch=6,
            grid=(1,),
            in_specs=[hbm, hbm, hbm, hbm],
            out_specs=hbm,
            scratch_shapes=[pltpu.VMEM((X_RING, 2, rows, c), jnp.uint32),
                            pltpu.VMEM((2, 2, rows, c), jnp.uint32),
                            pltpu.VMEM((W_RING, d, f), F32), pltpu.VMEM((W_RING, d, f), F32),
                            pltpu.VMEM((W_RING, f, d), F32),
                            pltpu.VMEM((d, f), BF16), pltpu.VMEM((d, f), BF16),
                            pltpu.VMEM((f, d), BF16),
                            pltpu.SemaphoreType.DMA((X_RING,)), pltpu.SemaphoreType.DMA((2,)),
                            pltpu.SemaphoreType.DMA((W_RING,))]),
        out_shape=jax.ShapeDtypeStruct((2, n_slots, c), jnp.uint32),
        compiler_params=_cparams("arbitrary"),
        name="experts",
    )(blk_expert, blk_run, run_expert, blk_start, blk_size, nblk_used, xs, w_g, w_u, w_d)


def _gather(ys, slot_t):
    _, n_slots, c = ys.shape
    k, t = slot_t.shape
    p = 2 * k * t
    win = _tile(p, SC_WINDOW)
    src = jnp.concatenate([slot_t, slot_t + n_slots], axis=0).reshape(1, p)
    mesh = plsc.VectorSubcoreMesh(core_axis_name="core", subcore_axis_name="subcore")

    @pl.kernel(out_type=jax.ShapeDtypeStruct((p, c), ys.dtype), mesh=mesh, scratch_types=[])
    def gather_rows(y_hbm, s_hbm, o_hbm):
        def body(s_vmem, o_vmem):
            pltpu.sync_copy(y_hbm.at[s_vmem.at[0]], o_vmem)

        pltpu.emit_pipeline(
            body, grid=(p // win,),
            in_specs=[pl.BlockSpec((1, win), lambda i: (0, i))],
            out_specs=[pl.BlockSpec((win, c), lambda i: (i, 0))],
            core_axis_name=("core", "subcore"),
            dimension_semantics=(pltpu.PARALLEL,))(s_hbm, o_hbm)

    return gather_rows(ys.reshape(2 * n_slots, c), src).reshape(2, k, t, c)


def _combine_kernel(y_ref, w_ref, x1_ref, h2_ref, mod_ref, wsg_ref, wsu_ref, wsd_ref, o_ref):
    hb = _unpack_halves(h2_ref[0], h2_ref[1])
    g = jnp.dot(hb, wsg_ref[...], preferred_element_type=F32)
    u = jnp.dot(hb, wsu_ref[...], preferred_element_type=F32)
    acc = jnp.dot((_silu(g) * u).astype(BF16), wsd_ref[...], preferred_element_type=F32)
    w = w_ref[...].T
    for kk in range(TOP_K):
        acc = acc + w[:, kk:kk + 1] * _unpack_halves(y_ref[0, kk], y_ref[1, kk]).astype(F32)
    g2 = mod_ref[0][5:6]
    o_ref[...] = x1_ref[...] + g2 * acc


def _combine(y_tok, w_tk, x1, h2p, mod, w_sg, w_su, w_sd, seq):
    t, d = x1.shape
    c = d // 4
    tc = _tile(seq, 256)
    tpb = seq // tc
    full = lambda shp: pl.BlockSpec(shp, lambda i: (0,) * len(shp))
    tok = lambda w: pl.BlockSpec((tc, w), lambda i: (i, 0))
    return pl.pallas_call(
        _combine_kernel,
        grid=(t // tc,),
        in_specs=[pl.BlockSpec((2, TOP_K, tc, c), lambda i: (0, 0, i, 0)),
                  pl.BlockSpec((TOP_K, tc), lambda i: (0, i)), tok(d),
                  pl.BlockSpec((2, tc, c), lambda i: (0, i, 0)),
                  pl.BlockSpec((1, N_ADA, d), lambda i: (i // tpb, 0, 0)),
                  full(w_sg.shape), full(w_su.shape), full(w_sd.shape)],
        out_specs=tok(d),
        out_shape=jax.ShapeDtypeStruct((t, d), F32),
        compiler_params=_cparams("arbitrary"),
        name="combine",
    )(y_tok, w_tk, x1, h2p, mod, w_sg.astype(BF16), w_su.astype(BF16), w_sd.astype(BF16))


def _layer(x, c, w_ada, b_ada, norm1_g, w_in, q_a_norm_g, w_uq, kv_a_norm_g, w_ukv,
           q_norm_g, k_norm_g, w_proj_attn, w_proj_fourier, w_out, norm2_g,
           w_router, router_bias, w_exp_gate, w_exp_up, w_exp_down,
           w_sh_gate, w_sh_up, w_sh_down):
    bsz, seq, d = x.shape
    t = bsz * seq
    e = w_router.shape[1]
    x2 = x.reshape(t, d)

    mod = _ada(c, w_ada, b_ada).reshape(bsz, N_ADA, d)
    q, k, v, zf, sa, sf = _inproj(x2, mod, norm1_g, w_in, q_a_norm_g, w_uq, kv_a_norm_g,
                                  w_ukv, q_norm_g, k_norm_g, bsz, seq)
    per_batch = lambda a: a.reshape(bsz, seq, a.shape[1])
    attn = _attention(per_batch(q), per_batch(k), per_batch(v)).reshape(t, N_HEADS * V_DIM)
    four = _fourier(zf.reshape(bsz, seq, zf.shape[1])).reshape(t, zf.shape[1])
    x1, h2, scores_t = _merge(attn, four, sa, sf, x2, mod, w_proj_attn, w_proj_fourier,
                              w_out, norm2_g, w_router, seq)

    idx_t, w_t, rank_t, cnt = _route(scores_t, router_bias)
    counts = cnt[:, 0].astype(jnp.int32)
    unit, bmax = ROW_UNIT, ROW_UNIT * MAX_UNITS
    n_slots = t * TOP_K + e * unit
    padded = ((counts + unit - 1) // unit) * unit
    p_end = jnp.cumsum(padded)
    p_start = p_end - padded
    eid = jnp.arange(e, dtype=jnp.int32)
    nb = (padded + bmax - 1) // bmax
    b_end = jnp.cumsum(nb)
    b_first = b_end - nb
    nblk_used = b_end[-1].astype(jnp.int32)
    nblk = -(-(t * TOP_K) // bmax) + e
    bid = jnp.minimum(jnp.arange(nblk, dtype=jnp.int32), nblk_used - 1)
    blk_expert = jnp.clip(jnp.sum((b_end[None, :] <= bid[:, None]).astype(jnp.int32), axis=1),
                          0, e - 1)
    onehot = blk_expert[:, None] == eid[None, :]
    look = lambda tab: jnp.sum(jnp.where(onehot, tab[None, :], 0), axis=1)
    piece = bid - look(b_first)
    blk_start = (look(p_start) + piece * bmax).astype(jnp.int32)
    blk_size = jnp.clip(look(padded) - piece * bmax, 0, bmax).astype(jnp.int32)

    slot_t = _slots(idx_t, rank_t, p_start.astype(jnp.int32))
    xs = _dispatch(h2, slot_t, n_slots)
    used = counts > 0
    run = jnp.cumsum(used.astype(jnp.int32)) - 1
    blk_run = look(run).astype(jnp.int32)
    rid = jnp.arange(e + W_RING, dtype=jnp.int32)
    match = jnp.logical_and(used[None, :], run[None, :] == rid[:, None])
    run_expert = jnp.where(jnp.any(match, axis=1),
                           jnp.sum(jnp.where(match, eid[None, :], 0), axis=1), -1).astype(jnp.int32)
    ys = _experts(blk_expert, blk_run, run_expert, blk_start, blk_size, nblk_used.reshape(1), xs,
                  w_exp_gate, w_exp_up, w_exp_down)
    out = _combine(_gather(ys, slot_t), w_t, x1, h2, mod, w_sh_gate, w_sh_up, w_sh_down, seq)
    return out.reshape(bsz, seq, d)


def kernel(x, c, w_ada, b_ada, norm1_g, w_in, q_a_norm_g, w_uq, kv_a_norm_g, w_ukv, q_norm_g,
           k_norm_g, w_proj_attn, w_proj_fourier, w_out, norm2_g, w_router, router_bias,
           w_exp_gate, w_exp_up, w_exp_down, w_sh_gate, w_sh_up, w_sh_down):
    for l in range(w_ada.shape[0]):
        x = _layer(x, c, w_ada[l], b_ada[l], norm1_g[l], w_in[l], q_a_norm_g[l], w_uq[l],
                   kv_a_norm_g[l], w_ukv[l], q_norm_g[l], k_norm_g[l], w_proj_attn[l],
                   w_proj_fourier[l], w_out[l], norm2_g[l], w_router[l], router_bias[l],
                   w_exp_gate[l], w_exp_up[l], w_exp_down[l], w_sh_gate[l], w_sh_up[l],
                   w_sh_down[l])
    return x
```

```python
import functools
import math

import numpy as np
import jax
import jax.numpy as jnp
from jax import lax
from jax.experimental import pallas as pl
from jax.experimental.pallas import tpu as pltpu
from jax.experimental.pallas import tpu_sc as plsc

N_HEADS = 8
QK_NOPE = 64
QK_ROPE = 32
V_DIM = 64
FOURIER_GROUP = 64
TOP_K = 8
ROUTED_SCALE = 2.5
EPS = 1e-6
ROPE_THETA = 10000.0
N_ADA = 6

LANES = 128
ROW_UNIT = 128
MAX_UNITS = 8
X_RING = 4
W_RING = 5
SC_WINDOW = 128
VMEM_LIMIT = 48 * 1024 * 1024

F32 = jnp.float32
BF16 = jnp.bfloat16


def _cparams(*sem):
    return pltpu.CompilerParams(dimension_semantics=sem, vmem_limit_bytes=VMEM_LIMIT)


def _tile(n, pref):
    t = min(n, pref)
    assert n % t == 0, (n, pref)
    return t


def _silu(v):
    return v * jax.nn.sigmoid(v)


def _pack_halves(m):
    d = m.shape[1]
    lo = lax.bitcast_convert_type(m[:, :d // 2].astype(BF16).astype(F32), jnp.uint32)
    hi = lax.bitcast_convert_type(m[:, d // 2:].astype(BF16).astype(F32), jnp.uint32)
    w = (lo >> 16) | (hi & jnp.uint32(0xFFFF0000))
    return w[:, :d // 4], w[:, d // 4:]


def _unpack_halves(w0, w1):
    def lo(w):
        return lax.bitcast_convert_type(w << 16, F32)

    def hi(w):
        return lax.bitcast_convert_type(w & jnp.uint32(0xFFFF0000), F32)
    return jnp.concatenate([lo(w0), lo(w1), hi(w0), hi(w1)], axis=1).astype(BF16)


def _ada_kernel(c_ref, w_ref, b_ref, o_ref):
    a = _silu(c_ref[...])
    o_ref[...] = jnp.dot(a, w_ref[...], preferred_element_type=F32,
                         precision=lax.Precision.HIGHEST) + b_ref[...]


def _ada(c, w_ada, b_ada):
    bsz, d = c.shape
    n = w_ada.shape[1]
    tn = _tile(n, d)
    return pl.pallas_call(
        _ada_kernel,
        grid=(n // tn,),
        in_specs=[pl.BlockSpec((bsz, d), lambda j: (0, 0)),
                  pl.BlockSpec((d, tn), lambda j: (0, j)),
                  pl.BlockSpec((1, tn), lambda j: (0, j))],
        out_specs=pl.BlockSpec((bsz, tn), lambda j: (0, j)),
        out_shape=jax.ShapeDtypeStruct((bsz, n), F32),
        compiler_params=_cparams("arbitrary"),
        name="ada",
    )(c, w_ada, b_ada.reshape(1, n))


def _head_norm_rope(t, trot, a, b):
    ms = jnp.sum(t * t, axis=-1, keepdims=True) * (1.0 / (QK_NOPE + QK_ROPE))
    return (t * a + trot * b) * lax.rsqrt(ms + EPS)


def _inproj_kernel(x_ref, mod_ref, g1_ref, wa_ref, wf_ref, wga_ref, wgf_ref,
                   gq_ref, gkv_ref, wuq_ref, wkv_ref,
                   aq_ref, bq_ref, ak_ref, bk_ref,
                   q_ref, k_ref, v_ref, zf_ref, sa_ref, sf_ref, *, ql, kvl):
    x = x_ref[...]
    mod = mod_ref[0]
    sh1, sc1 = mod[0:1], mod[1:2]
    r = lax.rsqrt(jnp.mean(x * x, axis=-1, keepdims=True) + EPS)
    h = (x * r * g1_ref[...]) * (1.0 + sc1) + sh1
    hb = h.astype(BF16)

    zf_ref[...] = jnp.dot(hb, wf_ref[...], preferred_element_type=F32).astype(BF16)
    sa_ref[...] = jax.nn.sigmoid(
        jnp.dot(hb, wga_ref[...], preferred_element_type=F32)).astype(BF16)
    sf_ref[...] = jax.nn.sigmoid(
        jnp.dot(hb, wgf_ref[...], preferred_element_type=F32)).astype(BF16)

    za = jnp.dot(hb, wa_ref[...], preferred_element_type=F32)
    zq = za[:, :ql]
    cq = zq * lax.rsqrt(jnp.mean(zq * zq, axis=-1, keepdims=True) + EPS) * gq_ref[...]
    qall = jnp.dot(cq.astype(BF16), wuq_ref[...], preferred_element_type=F32)

    zk = za[:, ql:]
    kvn = zk[:, :kvl]
    rk = lax.rsqrt(jnp.mean(kvn * kvn, axis=-1, keepdims=True) + EPS)
    lane = lax.broadcasted_iota(jnp.int32, zk.shape, 1)
    u = zk * jnp.where(lane < kvl, rk, 1.0) * gkv_ref[...]
    kvall = jnp.dot(u.astype(BF16), wkv_ref[...], preferred_element_type=F32)

    aq, bq, ak, bk = aq_ref[...], bq_ref[...], ak_ref[...], bk_ref[...]
    hw = N_HEADS * LANES
    for hd in range(N_HEADS):
        lo, hi = hd * LANES, (hd + 1) * LANES
        q_ref[:, lo:hi] = _head_norm_rope(qall[:, lo:hi], qall[:, hw + lo:hw + hi],
                                       aq, bq).astype(BF16)
        k_ref[:, lo:hi] = _head_norm_rope(kvall[:, lo:hi], kvall[:, hw + lo:hw + hi],
                                       ak, bk).astype(BF16)
    v_ref[...] = kvall[:, 2 * hw:].astype(BF16)


def _rope_tables(seq):
    half = QK_ROPE // 2
    pos = np.arange(seq, dtype=np.float64)
    inv = ROPE_THETA ** (-np.arange(0, QK_ROPE, 2, dtype=np.float64) / QK_ROPE)
    ang = pos[:, None] * inv[None, :]
    c, s = np.cos(ang), np.sin(ang)
    cos = np.ones((seq, LANES)); sin = np.zeros((seq, LANES))
    cos[:, QK_NOPE:QK_NOPE + half] = c
    cos[:, QK_NOPE + half:QK_NOPE + QK_ROPE] = c
    sin[:, QK_NOPE:QK_NOPE + half] = -s
    sin[:, QK_NOPE + half:QK_NOPE + QK_ROPE] = s
    return jnp.asarray(cos, F32), jnp.asarray(sin, F32)


def _partner_columns(w):
    half = QK_ROPE // 2
    lo, mid, hi = QK_NOPE, QK_NOPE + half, QK_NOPE + QK_ROPE
    z = jnp.zeros_like(w)
    return jnp.concatenate([z[..., :lo], w[..., mid:hi], w[..., lo:mid], z[..., hi:]], axis=-1)


def _inproj(x2, mod, norm1_g, w_in, q_a_g, w_uq, kv_a_g, w_ukv, q_g, k_g, bsz, seq):
    t, d = x2.shape
    ql, kvl = q_a_g.shape[0], kv_a_g.shape[0]
    hq = QK_NOPE + QK_ROPE
    fw = w_in.shape[1] - ql - kvl - QK_ROPE - 2 * d
    o1, o2, o3, o4, o5 = ql, ql + kvl, ql + kvl + QK_ROPE, ql + kvl + QK_ROPE + fw, \
        ql + kvl + QK_ROPE + fw + d
    assert ql % LANES == 0 and kvl % LANES == 0

    wa = jnp.concatenate([w_in[:, :o3], jnp.zeros((d, LANES - QK_ROPE), F32)], axis=1).astype(BF16)
    wf = w_in[:, o3:o4].astype(BF16)
    wga = w_in[:, o4:o5].astype(BF16)
    wgf = w_in[:, o5:].astype(BF16)

    wuq = w_uq.reshape(ql, N_HEADS, hq)
    wuq = jnp.pad(wuq, ((0, 0), (0, 0), (0, LANES - hq)))
    wuq = jnp.concatenate([wuq.reshape(ql, N_HEADS * LANES),
                           _partner_columns(wuq).reshape(ql, N_HEADS * LANES)], axis=1).astype(BF16)
    wukv = w_ukv.reshape(kvl, N_HEADS, QK_NOPE + V_DIM)
    wk = jnp.pad(wukv[:, :, :QK_NOPE], ((0, 0), (0, 0), (0, LANES - QK_NOPE)))
    place = jnp.zeros((QK_ROPE, N_HEADS, LANES), F32)
    place = place.at[jnp.arange(QK_ROPE), :, QK_NOPE + jnp.arange(QK_ROPE)].set(1.0)
    wk = jnp.concatenate([wk, place, jnp.zeros((LANES - QK_ROPE, N_HEADS, LANES), F32)], axis=0)
    wv = jnp.concatenate([wukv[:, :, QK_NOPE:], jnp.zeros((LANES, N_HEADS, V_DIM), F32)], axis=0)
    wkv = jnp.concatenate([wk.reshape(kvl + LANES, N_HEADS * LANES),
                           _partner_columns(wk).reshape(kvl + LANES, N_HEADS * LANES),
                           wv.reshape(kvl + LANES, N_HEADS * V_DIM)], axis=1).astype(BF16)

    gkv = jnp.concatenate([kv_a_g, jnp.ones((LANES,), F32)]).reshape(1, kvl + LANES)
    pad = jnp.zeros((LANES - hq,), F32)
    qg = jnp.concatenate([q_g * (hq ** -0.5), pad])
    kg = jnp.concatenate([k_g, pad])
    cos, sin = _rope_tables(seq)
    aq, bq = qg[None, :] * cos, _partner_columns(qg)[None, :] * sin
    ak, bk = kg[None, :] * cos, _partner_columns(kg)[None, :] * sin

    tm = _tile(seq, 512)
    tpb = seq // tm
    full = lambda shp: pl.BlockSpec(shp, lambda i: (0,) * len(shp))
    tok = lambda w: pl.BlockSpec((tm, w), lambda i: (i, 0))
    rope = pl.BlockSpec((tm, LANES), lambda i: (i % tpb, 0))
    return pl.pallas_call(
        functools.partial(_inproj_kernel, ql=ql, kvl=kvl),
        grid=(t // tm,),
        in_specs=[tok(d),
                  pl.BlockSpec((1, N_ADA, d), lambda i: (i // tpb, 0, 0)),
                  full((1, d)), full(wa.shape), full(wf.shape), full(wga.shape), full(wgf.shape),
                  full((1, ql)), full(gkv.shape), full(wuq.shape), full(wkv.shape),
                  rope, rope, rope, rope],
        out_specs=[tok(N_HEADS * LANES), tok(N_HEADS * LANES), tok(N_HEADS * V_DIM), tok(fw), tok(d), tok(d)],
        out_shape=[jax.ShapeDtypeStruct((t, N_HEADS * LANES), BF16),
                   jax.ShapeDtypeStruct((t, N_HEADS * LANES), BF16),
                   jax.ShapeDtypeStruct((t, N_HEADS * V_DIM), BF16),
                   jax.ShapeDtypeStruct((t, fw), BF16),
                   jax.ShapeDtypeStruct((t, d), BF16),
                   jax.ShapeDtypeStruct((t, d), BF16)],
        compiler_params=_cparams("arbitrary"),
        name="inproj",
    )(x2, mod, norm1_g.reshape(1, d), wa, wf, wga, wgf, q_a_g.reshape(1, ql), gkv, wuq, wkv,
      aq, bq, ak, bk)


def _attn_kernel(q_ref, k_ref, v_ref, o_ref, s_ref):
    def scores(hd):
        s_ref[hd % 2] = lax.dot_general(
            q_ref[0, :, hd * LANES:(hd + 1) * LANES], k_ref[0, :, hd * LANES:(hd + 1) * LANES],
            (((1,), (1,)), ((), ())), preferred_element_type=F32)

    scores(0)
    for hd in range(N_HEADS):
        if hd + 1 < N_HEADS:
            scores(hd + 1)
        s = s_ref[hd % 2]
        m = jnp.max(s, axis=-1, keepdims=True)
        p = jnp.exp(s - m)
        l = jnp.sum(p, axis=-1, keepdims=True)
        o = jnp.dot(p.astype(BF16), v_ref[0, :, hd * V_DIM:(hd + 1) * V_DIM],
                    preferred_element_type=F32)
        o_ref[0, :, hd * V_DIM:(hd + 1) * V_DIM] = (o / l).astype(BF16)


def _attention(q, k, v):
    bsz, seq, _ = q.shape
    tq = _tile(seq, 512)
    return pl.pallas_call(
        _attn_kernel,
        grid=(bsz, seq // tq),
        in_specs=[pl.BlockSpec((1, tq, N_HEADS * LANES), lambda b, j: (b, j, 0)),
                  pl.BlockSpec((1, seq, N_HEADS * LANES), lambda b, j: (b, 0, 0)),
                  pl.BlockSpec((1, seq, N_HEADS * V_DIM), lambda b, j: (b, 0, 0))],
        out_specs=pl.BlockSpec((1, tq, N_HEADS * V_DIM), lambda b, j: (b, j, 0)),
        out_shape=jax.ShapeDtypeStruct((bsz, seq, N_HEADS * V_DIM), BF16),
        scratch_shapes=[pltpu.VMEM((2, tq, seq), F32)],
        compiler_params=_cparams("arbitrary", "arbitrary"),
        name="attn",
    )(q, k, v)


def _fourier_kernel(z_ref, wc_ref, ws_ref, tab_ref, o_ref, u_ref, *, seq):
    @pl.when(pl.program_id(1) == 0)
    def _():
        z = z_ref[0]
        u_ref[:seq, :] = jnp.dot(z, wc_ref[...], preferred_element_type=F32).astype(BF16)
        u_ref[seq:, :] = jnp.dot(z, ws_ref[...], preferred_element_type=F32).astype(BF16)

    o_ref[0] = jnp.dot(tab_ref[...], u_ref[...], preferred_element_type=F32).astype(BF16)


def _fourier_tables(seq, fw):
    g = FOURIER_GROUP
    n = np.arange(seq, dtype=np.int64)
    ang = 2.0 * np.pi * ((n[:, None] * n[None, :]) % seq).astype(np.float64) / seq
    tab = np.concatenate([np.cos(ang), -np.sin(ang)], axis=1)
    c = np.arange(g, dtype=np.int64)
    angc = 2.0 * np.pi * ((c[:, None] * c[None, :]) % g).astype(np.float64) / g
    scale = 1.0 / math.sqrt(seq * g)
    eye = np.eye(fw // g)
    wc = np.kron(eye, np.cos(angc) * scale)
    ws = np.kron(eye, np.sin(angc) * scale)
    return (jnp.asarray(tab, F32).astype(BF16), jnp.asarray(wc, F32).astype(BF16),
            jnp.asarray(ws, F32).astype(BF16))


def _fourier(zf):
    bsz, seq, fw = zf.shape
    tab, wc, ws = _fourier_tables(seq, fw)
    tr = _tile(seq, 512)
    return pl.pallas_call(
        functools.partial(_fourier_kernel, seq=seq),
        grid=(bsz, seq // tr),
        in_specs=[pl.BlockSpec((1, seq, fw), lambda b, j: (b, 0, 0)),
                  pl.BlockSpec((fw, fw), lambda b, j: (0, 0)),
                  pl.BlockSpec((fw, fw), lambda b, j: (0, 0)),
                  pl.BlockSpec((tr, 2 * seq), lambda b, j: (j, 0))],
        out_specs=pl.BlockSpec((1, tr, fw), lambda b, j: (b, j, 0)),
        out_shape=jax.ShapeDtypeStruct((bsz, seq, fw), BF16),
        scratch_shapes=[pltpu.VMEM((2 * seq, fw), BF16)],
        compiler_params=_cparams("arbitrary", "arbitrary"),
        name="fourier",
    )(zf, wc, ws, tab)


def _merge_kernel(a_ref, f_ref, sa_ref, sf_ref, x_ref, mod_ref, wpa_ref, wpf_ref, wo_ref,
                  g2_ref, wrh_ref, wrl_ref, x1_ref, h2_ref, sc_ref):
    ya = jnp.dot(a_ref[...], wpa_ref[...], preferred_element_type=F32)
    yf = jnp.dot(f_ref[...], wpf_ref[...], preferred_element_type=F32)
    merged = sa_ref[...].astype(F32) * ya + sf_ref[...].astype(F32) * yf
    mod = mod_ref[0]
    g1, sh2, sc2 = mod[2:3], mod[3:4], mod[4:5]
    x1 = x_ref[...] + g1 * jnp.dot(merged.astype(BF16), wo_ref[...], preferred_element_type=F32)
    x1_ref[...] = x1
    r = lax.rsqrt(jnp.mean(x1 * x1, axis=-1, keepdims=True) + EPS)
    h2 = (x1 * r * g2_ref[...]) * (1.0 + sc2) + sh2
    h2_ref[0], h2_ref[1] = _pack_halves(h2)
    hh = h2.astype(BF16)
    hl = (h2 - hh.astype(F32)).astype(BF16)
    nt = (((1,), (1,)), ((), ()))
    lt = (lax.dot_general(wrh_ref[...], hh, nt, preferred_element_type=F32)
          + lax.dot_general(wrh_ref[...], hl, nt, preferred_element_type=F32)
          + lax.dot_general(wrl_ref[...], hh, nt, preferred_element_type=F32))
    sc_ref[...] = jax.nn.sigmoid(lt)


def _merge(attn, four, sa, sf, x2, mod, w_pa, w_pf, w_out, norm2_g, w_router, seq):
    t, d = x2.shape
    e = w_router.shape[1]
    wrt = w_router.T
    wrh = wrt.astype(BF16)
    wrl = (wrt - wrh.astype(F32)).astype(BF16)
    tm = _tile(seq, 512)
    tpb = seq // tm
    full = lambda shp: pl.BlockSpec(shp, lambda i: (0,) * len(shp))
    tok = lambda w: pl.BlockSpec((tm, w), lambda i: (i, 0))
    return pl.pallas_call(
        _merge_kernel,
        grid=(t // tm,),
        in_specs=[tok(attn.shape[1]), tok(four.shape[1]), tok(d), tok(d), tok(d),
                  pl.BlockSpec((1, N_ADA, d), lambda i: (i // tpb, 0, 0)),
                  full(w_pa.shape), full(w_pf.shape), full(w_out.shape), full((1, d)),
                  full((e, d)), full((e, d))],
        out_specs=[tok(d), pl.BlockSpec((2, tm, d // 4), lambda i: (0, i, 0)),
                   pl.BlockSpec((e, tm), lambda i: (0, i))],
        out_shape=[jax.ShapeDtypeStruct((t, d), F32),
                   jax.ShapeDtypeStruct((2, t, d // 4), jnp.uint32),
                   jax.ShapeDtypeStruct((e, t), F32)],
        compiler_params=_cparams("arbitrary"),
        name="merge",
    )(attn, four, sa, sf, x2, mod, w_pa.astype(BF16), w_pf.astype(BF16), w_out.astype(BF16),
      norm2_g.reshape(1, d), wrh, wrl)


def _route_kernel(s_ref, b_ref, tri_ref, idx_ref, w_ref, rank_ref, cnt_ref, carry_ref):
    @pl.when(pl.program_id(0) == 0)
    def _():
        carry_ref[...] = jnp.zeros_like(carry_ref)

    sc = s_ref[...]
    e, tr = sc.shape
    row = lax.broadcasted_iota(jnp.int32, (e, tr), 0)
    v = sc + b_ref[...]
    sel = jnp.zeros((e, tr), F32)
    idxs, ws = [], []
    for _ in range(TOP_K):
        m = jnp.max(v, axis=0, keepdims=True)
        idx = jnp.min(jnp.where(v == m, row, e), axis=0, keepdims=True)
        oh = row == idx
        ws.append(jnp.sum(jnp.where(oh, sc, 0.0), axis=0, keepdims=True))
        idxs.append(idx)
        v = jnp.where(oh, -jnp.inf, v)
        sel = sel + oh.astype(F32)
    wsum = ws[0]
    for w in ws[1:]:
        wsum = wsum + w
    selb = sel.astype(BF16)
    cum = jnp.dot(selb, tri_ref[...], preferred_element_type=F32) + carry_ref[...]
    for kk in range(TOP_K):
        oh = row == idxs[kk]
        rk = jnp.sum(jnp.where(oh, cum, 0.0), axis=0, keepdims=True)
        idx_ref[kk:kk + 1, :] = idxs[kk]
        rank_ref[kk:kk + 1, :] = rk.astype(jnp.int32)
        w_ref[kk:kk + 1, :] = ws[kk] / wsum * ROUTED_SCALE
    tot = carry_ref[...] + jnp.dot(selb, jnp.ones((tr, tr), BF16), preferred_element_type=F32)
    carry_ref[...] = tot
    cnt_ref[...] = tot


def _route(scores_t, router_bias):
    e, t = scores_t.shape
    tr = _tile(t, 256)
    tri = jnp.asarray(np.triu(np.ones((tr, tr), np.float32), 1), BF16)
    bias = jnp.broadcast_to(router_bias.reshape(e, 1), (e, tr)).astype(F32)
    blk = pl.BlockSpec((TOP_K, tr), lambda i: (0, i))
    return pl.pallas_call(
        _route_kernel,
        grid=(t // tr,),
        in_specs=[pl.BlockSpec((e, tr), lambda i: (0, i)),
                  pl.BlockSpec((e, tr), lambda i: (0, 0)),
                  pl.BlockSpec((tr, tr), lambda i: (0, 0))],
        out_specs=[blk, blk, blk, pl.BlockSpec((e, tr), lambda i: (0, 0))],
        out_shape=[jax.ShapeDtypeStruct((TOP_K, t), jnp.int32),
                   jax.ShapeDtypeStruct((TOP_K, t), F32),
                   jax.ShapeDtypeStruct((TOP_K, t), jnp.int32),
                   jax.ShapeDtypeStruct((e, tr), F32)],
        scratch_shapes=[pltpu.VMEM((e, tr), F32)],
        compiler_params=_cparams("arbitrary"),
        name="route",
    )(scores_t, bias, tri)


def _slots_kernel(idx_ref, rank_ref, ps_ref, slot_ref):
    ps = ps_ref[...]
    row = lax.broadcasted_iota(jnp.int32, ps.shape, 0)
    for kk in range(TOP_K):
        oh = row == idx_ref[kk:kk + 1, :]
        start = jnp.sum(jnp.where(oh, ps, 0), axis=0, keepdims=True)
        slot_ref[kk:kk + 1, :] = start + rank_ref[kk:kk + 1, :]


def _slots(idx_t, rank_t, p_start):
    _, t = idx_t.shape
    e = p_start.shape[0]
    ts = _tile(t, 2048)
    ps = jnp.broadcast_to(p_start.reshape(e, 1), (e, ts))
    blk = pl.BlockSpec((TOP_K, ts), lambda i: (0, i))
    return pl.pallas_call(
        _slots_kernel,
        grid=(t // ts,),
        in_specs=[blk, blk, pl.BlockSpec((e, ts), lambda i: (0, 0))],
        out_specs=blk,
        out_shape=jax.ShapeDtypeStruct((TOP_K, t), jnp.int32),
        compiler_params=_cparams("arbitrary"),
        name="slots",
    )(idx_t, rank_t, ps)


def _dispatch(h2p, slot_t, n_slots):
    _, t, c = h2p.shape
    k = slot_t.shape[0]
    win = _tile(2 * t, SC_WINDOW)
    rows = h2p.reshape(2 * t, c)
    dest = jnp.concatenate([slot_t, slot_t + n_slots], axis=1)
    mesh = plsc.VectorSubcoreMesh(core_axis_name="core", subcore_axis_name="subcore")

    @pl.kernel(out_type=jax.ShapeDtypeStruct((2 * n_slots, c), h2p.dtype), mesh=mesh,
               scratch_types=[])
    def scatter_rows(x_hbm, s_hbm, o_hbm):
        def body(x_vmem, s_vmem):
            pltpu.sync_copy(x_vmem, o_hbm.at[s_vmem.at[0]])

        pltpu.emit_pipeline(
            body, grid=(2 * t // win, k),
            in_specs=[pl.BlockSpec((win, c), lambda i, j: (i, 0)),
                      pl.BlockSpec((1, win), lambda i, j: (j, i))],
            out_specs=[], core_axis_name=("core", "subcore"),
            dimension_semantics=(pltpu.PARALLEL, pltpu.ARBITRARY))(x_hbm, s_hbm)

    return scatter_rows(rows, dest).reshape(2, n_slots, c)


def _expert_kernel(be_ref, run_ref, rex_ref, bst_ref, bsz_ref, nu_ref, xs_hbm, wg_hbm, wu_hbm,
                   wd_hbm, ys_hbm, xbuf, ybuf, wgf, wuf, wdf, wgb, wub, wdb, xsem, ysem, wsem,
                   *, n_slots):
    nu = nu_ref[0]
    unit = ROW_UNIT

    def x_stream(blk, slot, start):
        r0 = pl.multiple_of(bst_ref[blk], unit)
        for ch in range(MAX_UNITS):
            @pl.when(ch * unit < bsz_ref[blk])
            def _():
                for h in range(2):
                    cp = pltpu.make_async_copy(xs_hbm.at[h, pl.ds(r0 + ch * unit, unit)],
                                               xbuf.at[slot, h, pl.ds(ch * unit, unit)],
                                               xsem.at[slot])
                    cp.start() if start else cp.wait()

    def y_copies(r0, slot, m):
        return [pltpu.make_async_copy(ybuf.at[slot, h, pl.ds(0, m)],
                                      ys_hbm.at[h, pl.ds(pl.multiple_of(r0, unit), m)],
                                      ysem.at[slot]) for h in range(2)]

    def y_wait(blk, slot):
        for n in range(1, MAX_UNITS + 1):
            @pl.when(bsz_ref[blk] == n * unit)
            def _():
                for cp in y_copies(bst_ref[blk], slot, n * unit):
                    cp.wait()

    def weight_copies(e, which):
        return (pltpu.make_async_copy(wg_hbm.at[e], wgf.at[which], wsem.at[which]),
                pltpu.make_async_copy(wu_hbm.at[e], wuf.at[which], wsem.at[which]),
                pltpu.make_async_copy(wd_hbm.at[e], wdf.at[which], wsem.at[which]))

    for j in range(W_RING - 1):
        @pl.when(rex_ref[j] >= 0)
        def _():
            for cp in weight_copies(rex_ref[j], j):
                cp.start(priority=1)
    for j in range(X_RING - 1):
        @pl.when(j < nu)
        def _():
            x_stream(j, j, True)

    def step(i, carry):
        ahead = i + X_RING - 1

        @pl.when(ahead < nu)
        def _():
            x_stream(ahead, ahead % X_RING, True)

        prev = be_ref[jnp.maximum(i - 1, 0)]

        @pl.when(jnp.logical_or(i == 0, be_ref[i] != prev))
        def _():
            r = run_ref[i]
            par = r % W_RING
            for cp in weight_copies(be_ref[i], par):
                cp.wait()
            wgb[...] = wgf[par].astype(BF16)
            wub[...] = wuf[par].astype(BF16)
            wdb[...] = wdf[par].astype(BF16)
            later = rex_ref[r + W_RING - 1]

            @pl.when(later >= 0)
            def _():
                for cp in weight_copies(later, (r + W_RING - 1) % W_RING):
                    cp.start(priority=1)

        slot = i % X_RING
        x_stream(i, slot, False)
        out = i % 2

        @pl.when(i >= 2)
        def _():
            y_wait(i - 2, out)

        for n in range(1, MAX_UNITS + 1):
            @pl.when(bsz_ref[i] == n * unit)
            def _():
                m = n * unit
                x = _unpack_halves(xbuf[slot, 0, pl.ds(0, m)], xbuf[slot, 1, pl.ds(0, m)])
                g = jnp.dot(x, wgb[...], preferred_element_type=F32)
                u = jnp.dot(x, wub[...], preferred_element_type=F32)
                a = (_silu(g) * u).astype(BF16)
                y0, y1 = _pack_halves(jnp.dot(a, wdb[...], preferred_element_type=F32))
                ybuf[out, 0, pl.ds(0, m)] = y0
                ybuf[out, 1, pl.ds(0, m)] = y1
                for cp in y_copies(bst_ref[i], out, m):
                    cp.start()
        return carry

    lax.fori_loop(0, nu, step, 0)

    @pl.when(nu >= 2)
    def _():
        y_wait(nu - 2, nu % 2)
    y_wait(nu - 1, (nu - 1) % 2)

    ybuf[0, :, pl.ds(0, unit)] = jnp.zeros((2, unit, ybuf.shape[3]), ybuf.dtype)
    used = bst_ref[nu - 1] + bsz_ref[nu - 1]
    spare = (n_slots - used) // unit

    def zero_start(j, carry):
        for cp in y_copies(used + j * unit, 0, unit):
            cp.start()
        return carry

    def zero_wait(j, carry):
        for cp in y_copies(used + j * unit, 0, unit):
            cp.wait()
        return carry
    lax.fori_loop(0, spare, zero_start, 0)
    lax.fori_loop(0, spare, zero_wait, 0)


def _experts(blk_expert, blk_run, run_expert, blk_start, blk_size, nblk_used, xs, w_g, w_u, w_d):
    d, f = w_g.shape[1], w_g.shape[2]
    rows = ROW_UNIT * MAX_UNITS
    n_slots = xs.shape[1]
    c = xs.shape[2]
    hbm = pl.BlockSpec(memory_space=pl.ANY)
    return pl.pallas_call(
        functools.partial(_expert_kernel, n_slots=n_slots),
        grid_spec=pltpu.PrefetchScalarGridSpec(
            num_scalar_prefetch=6,
            grid=(1,),
            in_specs=[hbm, hbm, hbm, hbm],
            out_specs=hbm,
            scratch_shapes=[pltpu.VMEM((X_RING, 2, rows, c), jnp.uint32),
                            pltpu.VMEM((2, 2, rows, c), jnp.uint32),
                            pltpu.VMEM((W_RING, d, f), F32), pltpu.VMEM((W_RING, d, f), F32),
                            pltpu.VMEM((W_RING, f, d), F32),
                            pltpu.VMEM((d, f), BF16), pltpu.VMEM((d, f), BF16),
                            pltpu.VMEM((f, d), BF16),
                            pltpu.SemaphoreType.DMA((X_RING,)), pltpu.SemaphoreType.DMA((2,)),
                            pltpu.SemaphoreType.DMA((W_RING,))]),
        out_shape=jax.ShapeDtypeStruct((2, n_slots, c), jnp.uint32),
        compiler_params=_cparams("arbitrary"),
        name="experts",
    )(blk_expert, blk_run, run_expert, blk_start, blk_size, nblk_used, xs, w_g, w_u, w_d)


def _gather(ys, slot_t):
    _, n_slots, c = ys.shape
    k, t = slot_t.shape
    p = 2 * k * t
    win = _tile(p, SC_WINDOW)
    src = jnp.concatenate([slot_t, slot_t + n_slots], axis=0).reshape(1, p)
    mesh = plsc.VectorSubcoreMesh(core_axis_name="core", subcore_axis_name="subcore")

    @pl.kernel(out_type=jax.ShapeDtypeStruct((p, c), ys.dtype), mesh=mesh, scratch_types=[])
    def gather_rows(y_hbm, s_hbm, o_hbm):
        def body(s_vmem, o_vmem):
            pltpu.sync_copy(y_hbm.at[s_vmem.at[0]], o_vmem)

        pltpu.emit_pipeline(
            body, grid=(p // win,),
            in_specs=[pl.BlockSpec((1, win), lambda i: (0, i))],
            out_specs=[pl.BlockSpec((win, c), lambda i: (i, 0))],
            core_axis_name=("core", "subcore"),
            dimension_semantics=(pltpu.PARALLEL,))(s_hbm, o_hbm)

    return gather_rows(ys.reshape(2 * n_slots, c), src).reshape(2, k, t, c)


def _combine_kernel(y_ref, w_ref, x1_ref, h2_ref, mod_ref, wsg_ref, wsu_ref, wsd_ref, o_ref):
    hb = _unpack_halves(h2_ref[0], h2_ref[1])
    g = jnp.dot(hb, wsg_ref[...], preferred_element_type=F32)
    u = jnp.dot(hb, wsu_ref[...], preferred_element_type=F32)
    acc = jnp.dot((_silu(g) * u).astype(BF16), wsd_ref[...], preferred_element_type=F32)
    w = w_ref[...].T
    for kk in range(TOP_K):
        acc = acc + w[:, kk:kk + 1] * _unpack_halves(y_ref[0, kk], y_ref[1, kk]).astype(F32)
    g2 = mod_ref[0][5:6]
    o_ref[...] = x1_ref[...] + g2 * acc


def _combine(y_tok, w_tk, x1, h2p, mod, w_sg, w_su, w_sd, seq):
    t, d = x1.shape
    c = d // 4
    tc = _tile(seq, 512)
    tpb = seq // tc
    full = lambda shp: pl.BlockSpec(shp, lambda i: (0,) * len(shp))
    tok = lambda w: pl.BlockSpec((tc, w), lambda i: (i, 0))
    return pl.pallas_call(
        _combine_kernel,
        grid=(t // tc,),
        in_specs=[pl.BlockSpec((2, TOP_K, tc, c), lambda i: (0, 0, i, 0)),
                  pl.BlockSpec((TOP_K, tc), lambda i: (0, i)), tok(d),
                  pl.BlockSpec((2, tc, c), lambda i: (0, i, 0)),
                  pl.BlockSpec((1, N_ADA, d), lambda i: (i // tpb, 0, 0)),
                  full(w_sg.shape), full(w_su.shape), full(w_sd.shape)],
        out_specs=tok(d),
        out_shape=jax.ShapeDtypeStruct((t, d), F32),
        compiler_params=_cparams("arbitrary"),
        name="combine",
    )(y_tok, w_tk, x1, h2p, mod, w_sg.astype(BF16), w_su.astype(BF16), w_sd.astype(BF16))


def _layer(x, c, w_ada, b_ada, norm1_g, w_in, q_a_norm_g, w_uq, kv_a_norm_g, w_ukv,
           q_norm_g, k_norm_g, w_proj_attn, w_proj_fourier, w_out, norm2_g,
           w_router, router_bias, w_exp_gate, w_exp_up, w_exp_down,
           w_sh_gate, w_sh_up, w_sh_down):
    bsz, seq, d = x.shape
    t = bsz * seq
    e = w_router.shape[1]
    x2 = x.reshape(t, d)

    mod = _ada(c, w_ada, b_ada).reshape(bsz, N_ADA, d)
    q, k, v, zf, sa, sf = _inproj(x2, mod, norm1_g, w_in, q_a_norm_g, w_uq, kv_a_norm_g,
                                  w_ukv, q_norm_g, k_norm_g, bsz, seq)
    per_batch = lambda a: a.reshape(bsz, seq, a.shape[1])
    attn = _attention(per_batch(q), per_batch(k), per_batch(v)).reshape(t, N_HEADS * V_DIM)
    four = _fourier(zf.reshape(bsz, seq, zf.shape[1])).reshape(t, zf.shape[1])
    x1, h2, scores_t = _merge(attn, four, sa, sf, x2, mod, w_proj_attn, w_proj_fourier,
                              w_out, norm2_g, w_router, seq)

    idx_t, w_t, rank_t, cnt = _route(scores_t, router_bias)
    counts = cnt[:, 0].astype(jnp.int32)
    unit, bmax = ROW_UNIT, ROW_UNIT * MAX_UNITS
    n_slots = t * TOP_K + e * unit
    padded = ((counts + unit - 1) // unit) * unit
    p_end = jnp.cumsum(padded)
    p_start = p_end - padded
    eid = jnp.arange(e, dtype=jnp.int32)
    nb = (padded + bmax - 1) // bmax
    b_end = jnp.cumsum(nb)
    b_first = b_end - nb
    nblk_used = b_end[-1].astype(jnp.int32)
    nblk = -(-(t * TOP_K) // bmax) + e
    bid = jnp.minimum(jnp.arange(nblk, dtype=jnp.int32), nblk_used - 1)
    blk_expert = jnp.clip(jnp.sum((b_end[None, :] <= bid[:, None]).astype(jnp.int32), axis=1),
                          0, e - 1)
    onehot = blk_expert[:, None] == eid[None, :]
    look = lambda tab: jnp.sum(jnp.where(onehot, tab[None, :], 0), axis=1)
    piece = bid - look(b_first)
    blk_start = (look(p_start) + piece * bmax).astype(jnp.int32)
    blk_size = jnp.clip(look(padded) - piece * bmax, 0, bmax).astype(jnp.int32)

    slot_t = _slots(idx_t, rank_t, p_start.astype(jnp.int32))
    xs = _dispatch(h2, slot_t, n_slots)
    used = counts > 0
    run = jnp.cumsum(used.astype(jnp.int32)) - 1
    blk_run = look(run).astype(jnp.int32)
    rid = jnp.arange(e + W_RING, dtype=jnp.int32)
    match = jnp.logical_and(used[None, :], run[None, :] == rid[:, None])
    run_expert = jnp.where(jnp.any(match, axis=1),
                           jnp.sum(jnp.where(match, eid[None, :], 0), axis=1), -1).astype(jnp.int32)
    ys = _experts(blk_expert, blk_run, run_expert, blk_start, blk_size, nblk_used.reshape(1), xs,
                  w_exp_gate, w_exp_up, w_exp_down)
    out = _combine(_gather(ys, slot_t), w_t, x1, h2, mod, w_sh_gate, w_sh_up, w_sh_down, seq)
    return out.reshape(bsz, seq, d)


def kernel(x, c, w_ada, b_ada, norm1_g, w_in, q_a_norm_g, w_uq, kv_a_norm_g, w_ukv, q_norm_g,
           k_norm_g, w_proj_attn, w_proj_fourier, w_out, norm2_g, w_router, router_bias,
           w_exp_gate, w_exp_up, w_exp_down, w_sh_gate, w_sh_up, w_sh_down):
    for l in range(w_ada.shape[0]):
        x = _layer(x, c, w_ada[l], b_ada[l], norm1_g[l], w_in[l], q_a_norm_g[l], w_uq[l],
                   kv_a_norm_g[l], w_ukv[l], q_norm_g[l], k_norm_g[l], w_proj_attn[l],
                   w_proj_fourier[l], w_out[l], norm2_g[l], w_router[l], router_bias[l],
                   w_exp_gate[l], w_exp_up[l], w_exp_down[l], w_sh_gate[l], w_sh_up[l],
                   w_sh_down[l])
    return x
```

```python
import functools
import math

import numpy as np
import jax
import jax.numpy as jnp
from jax import lax
from jax.experimental import pallas as pl
from jax.experimental.pallas import tpu as pltpu
from jax.experimental.pallas import tpu_sc as plsc

N_HEADS = 8
QK_NOPE = 64
QK_ROPE = 32
V_DIM = 64
FOURIER_GROUP = 64
TOP_K = 8
ROUTED_SCALE = 2.5
EPS = 1e-6
ROPE_THETA = 10000.0
N_ADA = 6

LANES = 128
ROW_UNIT = 128
MAX_UNITS = 8
X_RING = 4
W_RING = 5
SC_WINDOW = 128
VMEM_LIMIT = 48 * 1024 * 1024

F32 = jnp.float32
BF16 = jnp.bfloat16


def _cparams(*sem):
    return pltpu.CompilerParams(dimension_semantics=sem, vmem_limit_bytes=VMEM_LIMIT)


def _tile(n, pref):
    t = min(n, pref)
    assert n % t == 0, (n, pref)
    return t


def _silu(v):
    return v * jax.nn.sigmoid(v)


def _pack_halves(m):
    d = m.shape[1]
    lo = lax.bitcast_convert_type(m[:, :d // 2].astype(BF16).astype(F32), jnp.uint32)
    hi = lax.bitcast_convert_type(m[:, d // 2:].astype(BF16).astype(F32), jnp.uint32)
    w = (lo >> 16) | (hi & jnp.uint32(0xFFFF0000))
    return w[:, :d // 4], w[:, d // 4:]


def _unpack_halves(w0, w1):
    def lo(w):
        return lax.bitcast_convert_type(w << 16, F32)

    def hi(w):
        return lax.bitcast_convert_type(w & jnp.uint32(0xFFFF0000), F32)
    return jnp.concatenate([lo(w0), lo(w1), hi(w0), hi(w1)], axis=1).astype(BF16)


def _ada_kernel(c_ref, w_ref, b_ref, o_ref):
    a = _silu(c_ref[...])
    o_ref[...] = jnp.dot(a, w_ref[...], preferred_element_type=F32,
                         precision=lax.Precision.HIGHEST) + b_ref[...]


def _ada(c, w_ada, b_ada):
    bsz, d = c.shape
    n = w_ada.shape[1]
    tn = _tile(n, d)
    return pl.pallas_call(
        _ada_kernel,
        grid=(n // tn,),
        in_specs=[pl.BlockSpec((bsz, d), lambda j: (0, 0)),
                  pl.BlockSpec((d, tn), lambda j: (0, j)),
                  pl.BlockSpec((1, tn), lambda j: (0, j))],
        out_specs=pl.BlockSpec((bsz, tn), lambda j: (0, j)),
        out_shape=jax.ShapeDtypeStruct((bsz, n), F32),
        compiler_params=_cparams("arbitrary"),
        name="ada",
    )(c, w_ada, b_ada.reshape(1, n))


def _head_norm_rope(t, trot, a, b):
    ms = jnp.sum(t * t, axis=-1, keepdims=True) * (1.0 / (QK_NOPE + QK_ROPE))
    return (t * a + trot * b) * lax.rsqrt(ms + EPS)


def _inproj_kernel(x_ref, mod_ref, g1_ref, wa_ref, wf_ref, wga_ref, wgf_ref,
                   gq_ref, gkv_ref, wuq_ref, wkv_ref,
                   aq_ref, bq_ref, ak_ref, bk_ref,
                   q_ref, k_ref, v_ref, zf_ref, sa_ref, sf_ref, *, ql, kvl):
    x = x_ref[...]
    mod = mod_ref[0]
    sh1, sc1 = mod[0:1], mod[1:2]
    r = lax.rsqrt(jnp.mean(x * x, axis=-1, keepdims=True) + EPS)
    h = (x * r * g1_ref[...]) * (1.0 + sc1) + sh1
    hb = h.astype(BF16)

    zf_ref[...] = jnp.dot(hb, wf_ref[...], preferred_element_type=F32).astype(BF16)
    sa_ref[...] = jax.nn.sigmoid(
        jnp.dot(hb, wga_ref[...], preferred_element_type=F32)).astype(BF16)
    sf_ref[...] = jax.nn.sigmoid(
        jnp.dot(hb, wgf_ref[...], preferred_element_type=F32)).astype(BF16)

    za = jnp.dot(hb, wa_ref[...], preferred_element_type=F32)
    zq = za[:, :ql]
    cq = zq * lax.rsqrt(jnp.mean(zq * zq, axis=-1, keepdims=True) + EPS) * gq_ref[...]
    qall = jnp.dot(cq.astype(BF16), wuq_ref[...], preferred_element_type=F32)

    zk = za[:, ql:]
    kvn = zk[:, :kvl]
    rk = lax.rsqrt(jnp.mean(kvn * kvn, axis=-1, keepdims=True) + EPS)
    lane = lax.broadcasted_iota(jnp.int32, zk.shape, 1)
    u = zk * jnp.where(lane < kvl, rk, 1.0) * gkv_ref[...]
    kvall = jnp.dot(u.astype(BF16), wkv_ref[...], preferred_element_type=F32)

    aq, bq, ak, bk = aq_ref[...], bq_ref[...], ak_ref[...], bk_ref[...]
    hw = N_HEADS * LANES
    for hd in range(N_HEADS):
        lo, hi = hd * LANES, (hd + 1) * LANES
        q_ref[:, lo:hi] = _head_norm_rope(qall[:, lo:hi], qall[:, hw + lo:hw + hi],
                                       aq, bq).astype(BF16)
        k_ref[:, lo:hi] = _head_norm_rope(kvall[:, lo:hi], kvall[:, hw + lo:hw + hi],
                                       ak, bk).astype(BF16)
    v_ref[...] = kvall[:, 2 * hw:].astype(BF16)


def _rope_tables(seq):
    half = QK_ROPE // 2
    pos = np.arange(seq, dtype=np.float64)
    inv = ROPE_THETA ** (-np.arange(0, QK_ROPE, 2, dtype=np.float64) / QK_ROPE)
    ang = pos[:, None] * inv[None, :]
    c, s = np.cos(ang), np.sin(ang)
    cos = np.ones((seq, LANES)); sin = np.zeros((seq, LANES))
    cos[:, QK_NOPE:QK_NOPE + half] = c
    cos[:, QK_NOPE + half:QK_NOPE + QK_ROPE] = c
    sin[:, QK_NOPE:QK_NOPE + half] = -s
    sin[:, QK_NOPE + half:QK_NOPE + QK_ROPE] = s
    return jnp.asarray(cos, F32), jnp.asarray(sin, F32)


def _partner_columns(w):
    half = QK_ROPE // 2
    lo, mid, hi = QK_NOPE, QK_NOPE + half, QK_NOPE + QK_ROPE
    z = jnp.zeros_like(w)
    return jnp.concatenate([z[..., :lo], w[..., mid:hi], w[..., lo:mid], z[..., hi:]], axis=-1)


def _inproj(x2, mod, norm1_g, w_in, q_a_g, w_uq, kv_a_g, w_ukv, q_g, k_g, bsz, seq):
    t, d = x2.shape
    ql, kvl = q_a_g.shape[0], kv_a_g.shape[0]
    hq = QK_NOPE + QK_ROPE
    fw = w_in.shape[1] - ql - kvl - QK_ROPE - 2 * d
    o1, o2, o3, o4, o5 = ql, ql + kvl, ql + kvl + QK_ROPE, ql + kvl + QK_ROPE + fw, \
        ql + kvl + QK_ROPE + fw + d
    assert ql % LANES == 0 and kvl % LANES == 0

    wa = jnp.concatenate([w_in[:, :o3], jnp.zeros((d, LANES - QK_ROPE), F32)], axis=1).astype(BF16)
    wf = w_in[:, o3:o4].astype(BF16)
    wga = w_in[:, o4:o5].astype(BF16)
    wgf = w_in[:, o5:].astype(BF16)

    wuq = w_uq.reshape(ql, N_HEADS, hq)
    wuq = jnp.pad(wuq, ((0, 0), (0, 0), (0, LANES - hq)))
    wuq = jnp.concatenate([wuq.reshape(ql, N_HEADS * LANES),
                           _partner_columns(wuq).reshape(ql, N_HEADS * LANES)], axis=1).astype(BF16)
    wukv = w_ukv.reshape(kvl, N_HEADS, QK_NOPE + V_DIM)
    wk = jnp.pad(wukv[:, :, :QK_NOPE], ((0, 0), (0, 0), (0, LANES - QK_NOPE)))
    place = jnp.zeros((QK_ROPE, N_HEADS, LANES), F32)
    place = place.at[jnp.arange(QK_ROPE), :, QK_NOPE + jnp.arange(QK_ROPE)].set(1.0)
    wk = jnp.concatenate([wk, place, jnp.zeros((LANES - QK_ROPE, N_HEADS, LANES), F32)], axis=0)
    wv = jnp.concatenate([wukv[:, :, QK_NOPE:], jnp.zeros((LANES, N_HEADS, V_DIM), F32)], axis=0)
    wkv = jnp.concatenate([wk.reshape(kvl + LANES, N_HEADS * LANES),
                           _partner_columns(wk).reshape(kvl + LANES, N_HEADS * LANES),
                           wv.reshape(kvl + LANES, N_HEADS * V_DIM)], axis=1).astype(BF16)

    gkv = jnp.concatenate([kv_a_g, jnp.ones((LANES,), F32)]).reshape(1, kvl + LANES)
    pad = jnp.zeros((LANES - hq,), F32)
    qg = jnp.concatenate([q_g * (hq ** -0.5), pad])
    kg = jnp.concatenate([k_g, pad])
    cos, sin = _rope_tables(seq)
    aq, bq = qg[None, :] * cos, _partner_columns(qg)[None, :] * sin
    ak, bk = kg[None, :] * cos, _partner_columns(kg)[None, :] * sin

    tm = _tile(seq, 512)
    tpb = seq // tm
    full = lambda shp: pl.BlockSpec(shp, lambda i: (0,) * len(shp))
    tok = lambda w: pl.BlockSpec((tm, w), lambda i: (i, 0))
    rope = pl.BlockSpec((tm, LANES), lambda i: (i % tpb, 0))
    return pl.pallas_call(
        functools.partial(_inproj_kernel, ql=ql, kvl=kvl),
        grid=(t // tm,),
        in_specs=[tok(d),
                  pl.BlockSpec((1, N_ADA, d), lambda i: (i // tpb, 0, 0)),
                  full((1, d)), full(wa.shape), full(wf.shape), full(wga.shape), full(wgf.shape),
                  full((1, ql)), full(gkv.shape), full(wuq.shape), full(wkv.shape),
                  rope, rope, rope, rope],
        out_specs=[tok(N_HEADS * LANES), tok(N_HEADS * LANES), tok(N_HEADS * V_DIM), tok(fw), tok(d), tok(d)],
        out_shape=[jax.ShapeDtypeStruct((t, N_HEADS * LANES), BF16),
                   jax.ShapeDtypeStruct((t, N_HEADS * LANES), BF16),
                   jax.ShapeDtypeStruct((t, N_HEADS * V_DIM), BF16),
                   jax.ShapeDtypeStruct((t, fw), BF16),
                   jax.ShapeDtypeStruct((t, d), BF16),
                   jax.ShapeDtypeStruct((t, d), BF16)],
        compiler_params=_cparams("arbitrary"),
        name="inproj",
    )(x2, mod, norm1_g.reshape(1, d), wa, wf, wga, wgf, q_a_g.reshape(1, ql), gkv, wuq, wkv,
      aq, bq, ak, bk)


def _attn_kernel(q_ref, k_ref, v_ref, o_ref, s_ref):
    def scores(hd):
        s_ref[hd % 2] = lax.dot_general(
            q_ref[0, :, hd * LANES:(hd + 1) * LANES], k_ref[0, :, hd * LANES:(hd + 1) * LANES],
            (((1,), (1,)), ((), ())), preferred_element_type=F32)

    scores(0)
    for hd in range(N_HEADS):
        if hd + 1 < N_HEADS:
            scores(hd + 1)
        s = s_ref[hd % 2]
        m = jnp.max(s, axis=-1, keepdims=True)
        p = jnp.exp(s - m)
        l = jnp.sum(p, axis=-1, keepdims=True)
        o = jnp.dot(p.astype(BF16), v_ref[0, :, hd * V_DIM:(hd + 1) * V_DIM],
                    preferred_element_type=F32)
        o_ref[0, :, hd * V_DIM:(hd + 1) * V_DIM] = (o / l).astype(BF16)


def _attention(q, k, v):
    bsz, seq, _ = q.shape
    tq = _tile(seq, 512)
    return pl.pallas_call(
        _attn_kernel,
        grid=(bsz, seq // tq),
        in_specs=[pl.BlockSpec((1, tq, N_HEADS * LANES), lambda b, j: (b, j, 0)),
                  pl.BlockSpec((1, seq, N_HEADS * LANES), lambda b, j: (b, 0, 0)),
                  pl.BlockSpec((1, seq, N_HEADS * V_DIM), lambda b, j: (b, 0, 0))],
        out_specs=pl.BlockSpec((1, tq, N_HEADS * V_DIM), lambda b, j: (b, j, 0)),
        out_shape=jax.ShapeDtypeStruct((bsz, seq, N_HEADS * V_DIM), BF16),
        scratch_shapes=[pltpu.VMEM((2, tq, seq), F32)],
        compiler_params=_cparams("arbitrary", "arbitrary"),
        name="attn",
    )(q, k, v)


def _fourier_kernel(z_ref, wc_ref, ws_ref, tab_ref, o_ref, u_ref, *, seq):
    @pl.when(pl.program_id(1) == 0)
    def _():
        z = z_ref[0]
        u_ref[:seq, :] = jnp.dot(z, wc_ref[...], preferred_element_type=F32).astype(BF16)
        u_ref[seq:, :] = jnp.dot(z, ws_ref[...], preferred_element_type=F32).astype(BF16)

    o_ref[0] = jnp.dot(tab_ref[...], u_ref[...], preferred_element_type=F32).astype(BF16)


def _fourier_tables(seq, fw):
    g = FOURIER_GROUP
    n = np.arange(seq, dtype=np.int64)
    ang = 2.0 * np.pi * ((n[:, None] * n[None, :]) % seq).astype(np.float64) / seq
    tab = np.concatenate([np.cos(ang), -np.sin(ang)], axis=1)
    c = np.arange(g, dtype=np.int64)
    angc = 2.0 * np.pi * ((c[:, None] * c[None, :]) % g).astype(np.float64) / g
    scale = 1.0 / math.sqrt(seq * g)
    eye = np.eye(fw // g)
    wc = np.kron(eye, np.cos(angc) * scale)
    ws = np.kron(eye, np.sin(angc) * scale)
    return (jnp.asarray(tab, F32).astype(BF16), jnp.asarray(wc, F32).astype(BF16),
            jnp.asarray(ws, F32).astype(BF16))


def _fourier(zf):
    bsz, seq, fw = zf.shape
    tab, wc, ws = _fourier_tables(seq, fw)
    tr = _tile(seq, 1024)
    return pl.pallas_call(
        functools.partial(_fourier_kernel, seq=seq),
        grid=(bsz, seq // tr),
        in_specs=[pl.BlockSpec((1, seq, fw), lambda b, j: (b, 0, 0)),
                  pl.BlockSpec((fw, fw), lambda b, j: (0, 0)),
                  pl.BlockSpec((fw, fw), lambda b, j: (0, 0)),
                  pl.BlockSpec((tr, 2 * seq), lambda b, j: (j, 0))],
        out_specs=pl.BlockSpec((1, tr, fw), lambda b, j: (b, j, 0)),
        out_shape=jax.ShapeDtypeStruct((bsz, seq, fw), BF16),
        scratch_shapes=[pltpu.VMEM((2 * seq, fw), BF16)],
        compiler_params=_cparams("arbitrary", "arbitrary"),
        name="fourier",
    )(zf, wc, ws, tab)


def _merge_kernel(a_ref, f_ref, sa_ref, sf_ref, x_ref, mod_ref, wpa_ref, wpf_ref, wo_ref,
                  g2_ref, wrh_ref, wrl_ref, x1_ref, h2_ref, sc_ref):
    ya = jnp.dot(a_ref[...], wpa_ref[...], preferred_element_type=F32)
    yf = jnp.dot(f_ref[...], wpf_ref[...], preferred_element_type=F32)
    merged = sa_ref[...].astype(F32) * ya + sf_ref[...].astype(F32) * yf
    mod = mod_ref[0]
    g1, sh2, sc2 = mod[2:3], mod[3:4], mod[4:5]
    x1 = x_ref[...] + g1 * jnp.dot(merged.astype(BF16), wo_ref[...], preferred_element_type=F32)
    x1_ref[...] = x1
    r = lax.rsqrt(jnp.mean(x1 * x1, axis=-1, keepdims=True) + EPS)
    h2 = (x1 * r * g2_ref[...]) * (1.0 + sc2) + sh2
    h2_ref[0], h2_ref[1] = _pack_halves(h2)
    hh = h2.astype(BF16)
    hl = (h2 - hh.astype(F32)).astype(BF16)
    nt = (((1,), (1,)), ((), ()))
    lt = (lax.dot_general(wrh_ref[...], hh, nt, preferred_element_type=F32)
          + lax.dot_general(wrh_ref[...], hl, nt, preferred_element_type=F32)
          + lax.dot_general(wrl_ref[...], hh, nt, preferred_element_type=F32))
    sc_ref[...] = jax.nn.sigmoid(lt)


def _merge(attn, four, sa, sf, x2, mod, w_pa, w_pf, w_out, norm2_g, w_router, seq):
    t, d = x2.shape
    e = w_router.shape[1]
    wrt = w_router.T
    wrh = wrt.astype(BF16)
    wrl = (wrt - wrh.astype(F32)).astype(BF16)
    tm = _tile(seq, 512)
    tpb = seq // tm
    full = lambda shp: pl.BlockSpec(shp, lambda i: (0,) * len(shp))
    tok = lambda w: pl.BlockSpec((tm, w), lambda i: (i, 0))
    return pl.pallas_call(
        _merge_kernel,
        grid=(t // tm,),
        in_specs=[tok(attn.shape[1]), tok(four.shape[1]), tok(d), tok(d), tok(d),
                  pl.BlockSpec((1, N_ADA, d), lambda i: (i // tpb, 0, 0)),
                  full(w_pa.shape), full(w_pf.shape), full(w_out.shape), full((1, d)),
                  full((e, d)), full((e, d))],
        out_specs=[tok(d), pl.BlockSpec((2, tm, d // 4), lambda i: (0, i, 0)),
                   pl.BlockSpec((e, tm), lambda i: (0, i))],
        out_shape=[jax.ShapeDtypeStruct((t, d), F32),
                   jax.ShapeDtypeStruct((2, t, d // 4), jnp.uint32),
                   jax.ShapeDtypeStruct((e, t), F32)],
        compiler_params=_cparams("arbitrary"),
        name="merge",
    )(attn, four, sa, sf, x2, mod, w_pa.astype(BF16), w_pf.astype(BF16), w_out.astype(BF16),
      norm2_g.reshape(1, d), wrh, wrl)


def _route_kernel(s_ref, b_ref, tri_ref, idx_ref, w_ref, rank_ref, cnt_ref, carry_ref):
    @pl.when(pl.program_id(0) == 0)
    def _():
        carry_ref[...] = jnp.zeros_like(carry_ref)

    sc = s_ref[...]
    e, tr = sc.shape
    row = lax.broadcasted_iota(jnp.int32, (e, tr), 0)
    v = sc + b_ref[...]
    sel = jnp.zeros((e, tr), F32)
    idxs, ws = [], []
    for _ in range(TOP_K):
        m = jnp.max(v, axis=0, keepdims=True)
        idx = jnp.min(jnp.where(v == m, row, e), axis=0, keepdims=True)
        oh = row == idx
        ws.append(jnp.sum(jnp.where(oh, sc, 0.0), axis=0, keepdims=True))
        idxs.append(idx)
        v = jnp.where(oh, -jnp.inf, v)
        sel = sel + oh.astype(F32)
    wsum = ws[0]
    for w in ws[1:]:
        wsum = wsum + w
    selb = sel.astype(BF16)
    cum = jnp.dot(selb, tri_ref[...], preferred_element_type=F32) + carry_ref[...]
    for kk in range(TOP_K):
        oh = row == idxs[kk]
        rk = jnp.sum(jnp.where(oh, cum, 0.0), axis=0, keepdims=True)
        idx_ref[kk:kk + 1, :] = idxs[kk]
        rank_ref[kk:kk + 1, :] = rk.astype(jnp.int32)
        w_ref[kk:kk + 1, :] = ws[kk] / wsum * ROUTED_SCALE
    tot = carry_ref[...] + jnp.dot(selb, jnp.ones((tr, tr), BF16), preferred_element_type=F32)
    carry_ref[...] = tot
    cnt_ref[...] = tot


def _route(scores_t, router_bias):
    e, t = scores_t.shape
    tr = _tile(t, 256)
    tri = jnp.asarray(np.triu(np.ones((tr, tr), np.float32), 1), BF16)
    bias = jnp.broadcast_to(router_bias.reshape(e, 1), (e, tr)).astype(F32)
    blk = pl.BlockSpec((TOP_K, tr), lambda i: (0, i))
    return pl.pallas_call(
        _route_kernel,
        grid=(t // tr,),
        in_specs=[pl.BlockSpec((e, tr), lambda i: (0, i)),
                  pl.BlockSpec((e, tr), lambda i: (0, 0)),
                  pl.BlockSpec((tr, tr), lambda i: (0, 0))],
        out_specs=[blk, blk, blk, pl.BlockSpec((e, tr), lambda i: (0, 0))],
        out_shape=[jax.ShapeDtypeStruct((TOP_K, t), jnp.int32),
                   jax.ShapeDtypeStruct((TOP_K, t), F32),
                   jax.ShapeDtypeStruct((TOP_K, t), jnp.int32),
                   jax.ShapeDtypeStruct((e, tr), F32)],
        scratch_shapes=[pltpu.VMEM((e, tr), F32)],
        compiler_params=_cparams("arbitrary"),
        name="route",
    )(scores_t, bias, tri)


def _slots_kernel(idx_ref, rank_ref, ps_ref, slot_ref):
    ps = ps_ref[...]
    row = lax.broadcasted_iota(jnp.int32, ps.shape, 0)
    for kk in range(TOP_K):
        oh = row == idx_ref[kk:kk + 1, :]
        start = jnp.sum(jnp.where(oh, ps, 0), axis=0, keepdims=True)
        slot_ref[kk:kk + 1, :] = start + rank_ref[kk:kk + 1, :]


def _slots(idx_t, rank_t, p_start):
    _, t = idx_t.shape
    e = p_start.shape[0]
    ts = _tile(t, 2048)
    ps = jnp.broadcast_to(p_start.reshape(e, 1), (e, ts))
    blk = pl.BlockSpec((TOP_K, ts), lambda i: (0, i))
    return pl.pallas_call(
        _slots_kernel,
        grid=(t // ts,),
        in_specs=[blk, blk, pl.BlockSpec((e, ts), lambda i: (0, 0))],
        out_specs=blk,
        out_shape=jax.ShapeDtypeStruct((TOP_K, t), jnp.int32),
        compiler_params=_cparams("arbitrary"),
        name="slots",
    )(idx_t, rank_t, ps)


def _dispatch(h2p, slot_t, n_slots):
    _, t, c = h2p.shape
    k = slot_t.shape[0]
    win = _tile(2 * t, SC_WINDOW)
    rows = h2p.reshape(2 * t, c)
    dest = jnp.concatenate([slot_t, slot_t + n_slots], axis=1)
    mesh = plsc.VectorSubcoreMesh(core_axis_name="core", subcore_axis_name="subcore")

    @pl.kernel(out_type=jax.ShapeDtypeStruct((2 * n_slots, c), h2p.dtype), mesh=mesh,
               scratch_types=[])
    def scatter_rows(x_hbm, s_hbm, o_hbm):
        def body(x_vmem, s_vmem):
            pltpu.sync_copy(x_vmem, o_hbm.at[s_vmem.at[0]])

        pltpu.emit_pipeline(
            body, grid=(2 * t // win, k),
            in_specs=[pl.BlockSpec((win, c), lambda i, j: (i, 0)),
                      pl.BlockSpec((1, win), lambda i, j: (j, i))],
            out_specs=[], core_axis_name=("core", "subcore"),
            dimension_semantics=(pltpu.PARALLEL, pltpu.ARBITRARY))(x_hbm, s_hbm)

    return scatter_rows(rows, dest).reshape(2, n_slots, c)


def _expert_kernel(be_ref, run_ref, rex_ref, bst_ref, bsz_ref, nu_ref, xs_hbm, wg_hbm, wu_hbm,
                   wd_hbm, ys_hbm, xbuf, ybuf, wgf, wuf, wdf, wgb, wub, wdb, xsem, ysem, wsem,
                   *, n_slots):
    nu = nu_ref[0]
    unit = ROW_UNIT

    def x_stream(blk, slot, start):
        r0 = pl.multiple_of(bst_ref[blk], unit)
        for ch in range(MAX_UNITS):
            @pl.when(ch * unit < bsz_ref[blk])
            def _():
                for h in range(2):
                    cp = pltpu.make_async_copy(xs_hbm.at[h, pl.ds(r0 + ch * unit, unit)],
                                               xbuf.at[slot, h, pl.ds(ch * unit, unit)],
                                               xsem.at[slot])
                    cp.start() if start else cp.wait()

    def y_copies(r0, slot, m):
        return [pltpu.make_async_copy(ybuf.at[slot, h, pl.ds(0, m)],
                                      ys_hbm.at[h, pl.ds(pl.multiple_of(r0, unit), m)],
                                      ysem.at[slot]) for h in range(2)]

    def y_wait(blk, slot):
        for n in range(1, MAX_UNITS + 1):
            @pl.when(bsz_ref[blk] == n * unit)
            def _():
                for cp in y_copies(bst_ref[blk], slot, n * unit):
                    cp.wait()

    def weight_copies(e, which):
        return (pltpu.make_async_copy(wg_hbm.at[e], wgf.at[which], wsem.at[which]),
                pltpu.make_async_copy(wu_hbm.at[e], wuf.at[which], wsem.at[which]),
                pltpu.make_async_copy(wd_hbm.at[e], wdf.at[which], wsem.at[which]))

    for j in range(W_RING - 1):
        @pl.when(rex_ref[j] >= 0)
        def _():
            for cp in weight_copies(rex_ref[j], j):
                cp.start(priority=1)
    for j in range(X_RING - 1):
        @pl.when(j < nu)
        def _():
            x_stream(j, j, True)

    def step(i, carry):
        ahead = i + X_RING - 1

        @pl.when(ahead < nu)
        def _():
            x_stream(ahead, ahead % X_RING, True)

        prev = be_ref[jnp.maximum(i - 1, 0)]

        @pl.when(jnp.logical_or(i == 0, be_ref[i] != prev))
        def _():
            r = run_ref[i]
            par = r % W_RING
            for cp in weight_copies(be_ref[i], par):
                cp.wait()
            wgb[...] = wgf[par].astype(BF16)
            wub[...] = wuf[par].astype(BF16)
            wdb[...] = wdf[par].astype(BF16)
            later = rex_ref[r + W_RING - 1]

            @pl.when(later >= 0)
            def _():
                for cp in weight_copies(later, (r + W_RING - 1) % W_RING):
                    cp.start(priority=1)

        slot = i % X_RING
        x_stream(i, slot, False)
        out = i % 2

        @pl.when(i >= 2)
        def _():
            y_wait(i - 2, out)

        for n in range(1, MAX_UNITS + 1):
            @pl.when(bsz_ref[i] == n * unit)
            def _():
                m = n * unit
                x = _unpack_halves(xbuf[slot, 0, pl.ds(0, m)], xbuf[slot, 1, pl.ds(0, m)])
                g = jnp.dot(x, wgb[...], preferred_element_type=F32)
                u = jnp.dot(x, wub[...], preferred_element_type=F32)
                a = (_silu(g) * u).astype(BF16)
                y0, y1 = _pack_halves(jnp.dot(a, wdb[...], preferred_element_type=F32))
                ybuf[out, 0, pl.ds(0, m)] = y0
                ybuf[out, 1, pl.ds(0, m)] = y1
                for cp in y_copies(bst_ref[i], out, m):
                    cp.start()
        return carry

    lax.fori_loop(0, nu, step, 0)

    @pl.when(nu >= 2)
    def _():
        y_wait(nu - 2, nu % 2)
    y_wait(nu - 1, (nu - 1) % 2)

    ybuf[0, :, pl.ds(0, unit)] = jnp.zeros((2, unit, ybuf.shape[3]), ybuf.dtype)
    used = bst_ref[nu - 1] + bsz_ref[nu - 1]
    spare = (n_slots - used) // unit

    def zero_start(j, carry):
        for cp in y_copies(used + j * unit, 0, unit):
            cp.start()
        return carry

    def zero_wait(j, carry):
        for cp in y_copies(used + j * unit, 0, unit):
            cp.wait()
        return carry
    lax.fori_loop(0, spare, zero_start, 0)
    lax.fori_loop(0, spare, zero_wait, 0)


def _experts(blk_expert, blk_run, run_expert, blk_start, blk_size, nblk_used, xs, w_g, w_u, w_d):
    d, f = w_g.shape[1], w_g.shape[2]
    rows = ROW_UNIT * MAX_UNITS
    n_slots = xs.shape[1]
    c = xs.shape[2]
    hbm = pl.BlockSpec(memory_space=pl.ANY)
    return pl.pallas_call(
        functools.partial(_expert_kernel, n_slots=n_slots),
        grid_spec=pltpu.PrefetchScalarGridSpec(
            num_scalar_prefetch=6,
            grid=(1,),
            in_specs=[hbm, hbm, hbm, hbm],
            out_specs=hbm,
            scratch_shapes=[pltpu.VMEM((X_RING, 2, rows, c), jnp.uint32),
                            pltpu.VMEM((2, 2, rows, c), jnp.uint32),
                            pltpu.VMEM((W_RING, d, f), F32), pltpu.VMEM((W_RING, d, f), F32),
                            pltpu.VMEM((W_RING, f, d), F32),
                            pltpu.VMEM((d, f), BF16), pltpu.VMEM((d, f), BF16),
                            pltpu.VMEM((f, d), BF16),
                            pltpu.SemaphoreType.DMA((X_RING,)), pltpu.SemaphoreType.DMA((2,)),
                            pltpu.SemaphoreType.DMA((W_RING,))]),
        out_shape=jax.ShapeDtypeStruct((2, n_slots, c), jnp.uint32),
        compiler_params=_cparams("arbitrary"),
        name="experts",
    )(blk_expert, blk_run, run_expert, blk_start, blk_size, nblk_used, xs, w_g, w_u, w_d)


def _gather(ys, slot_t):
    _, n_slots, c = ys.shape
    k, t = slot_t.shape
    p = 2 * k * t
    win = _tile(p, SC_WINDOW)
    src = jnp.concatenate([slot_t, slot_t + n_slots], axis=0).reshape(1, p)
    mesh = plsc.VectorSubcoreMesh(core_axis_name="core", subcore_axis_name="subcore")

    @pl.kernel(out_type=jax.ShapeDtypeStruct((p, c), ys.dtype), mesh=mesh, scratch_types=[])
    def gather_rows(y_hbm, s_hbm, o_hbm):
        def body(s_vmem, o_vmem):
            pltpu.sync_copy(y_hbm.at[s_vmem.at[0]], o_vmem)

        pltpu.emit_pipeline(
            body, grid=(p // win,),
            in_specs=[pl.BlockSpec((1, win), lambda i: (0, i))],
            out_specs=[pl.BlockSpec((win, c), lambda i: (i, 0))],
            core_axis_name=("core", "subcore"),
            dimension_semantics=(pltpu.PARALLEL,))(s_hbm, o_hbm)

    return gather_rows(ys.reshape(2 * n_slots, c), src).reshape(2, k, t, c)


def _combine_kernel(y_ref, w_ref, x1_ref, h2_ref, mod_ref, wsg_ref, wsu_ref, wsd_ref, o_ref):
    hb = _unpack_halves(h2_ref[0], h2_ref[1])
    g = jnp.dot(hb, wsg_ref[...], preferred_element_type=F32)
    u = jnp.dot(hb, wsu_ref[...], preferred_element_type=F32)
    acc = jnp.dot((_silu(g) * u).astype(BF16), wsd_ref[...], preferred_element_type=F32)
    w = w_ref[...].T
    for kk in range(TOP_K):
        acc = acc + w[:, kk:kk + 1] * _unpack_halves(y_ref[0, kk], y_ref[1, kk]).astype(F32)
    g2 = mod_ref[0][5:6]
    o_ref[...] = x1_ref[...] + g2 * acc


def _combine(y_tok, w_tk, x1, h2p, mod, w_sg, w_su, w_sd, seq):
    t, d = x1.shape
    c = d // 4
    tc = _tile(seq, 512)
    tpb = seq // tc
    full = lambda shp: pl.BlockSpec(shp, lambda i: (0,) * len(shp))
    tok = lambda w: pl.BlockSpec((tc, w), lambda i: (i, 0))
    return pl.pallas_call(
        _combine_kernel,
        grid=(t // tc,),
        in_specs=[pl.BlockSpec((2, TOP_K, tc, c), lambda i: (0, 0, i, 0)),
                  pl.BlockSpec((TOP_K, tc), lambda i: (0, i)), tok(d),
                  pl.BlockSpec((2, tc, c), lambda i: (0, i, 0)),
                  pl.BlockSpec((1, N_ADA, d), lambda i: (i // tpb, 0, 0)),
                  full(w_sg.shape), full(w_su.shape), full(w_sd.shape)],
        out_specs=tok(d),
        out_shape=jax.ShapeDtypeStruct((t, d), F32),
        compiler_params=_cparams("arbitrary"),
        name="combine",
    )(y_tok, w_tk, x1, h2p, mod, w_sg.astype(BF16), w_su.astype(BF16), w_sd.astype(BF16))


def _layer(x, c, w_ada, b_ada, norm1_g, w_in, q_a_norm_g, w_uq, kv_a_norm_g, w_ukv,
           q_norm_g, k_norm_g, w_proj_attn, w_proj_fourier, w_out, norm2_g,
           w_router, router_bias, w_exp_gate, w_exp_up, w_exp_down,
           w_sh_gate, w_sh_up, w_sh_down):
    bsz, seq, d = x.shape
    t = bsz * seq
    e = w_router.shape[1]
    x2 = x.reshape(t, d)

    mod = _ada(c, w_ada, b_ada).reshape(bsz, N_ADA, d)
    q, k, v, zf, sa, sf = _inproj(x2, mod, norm1_g, w_in, q_a_norm_g, w_uq, kv_a_norm_g,
                                  w_ukv, q_norm_g, k_norm_g, bsz, seq)
    per_batch = lambda a: a.reshape(bsz, seq, a.shape[1])
    attn = _attention(per_batch(q), per_batch(k), per_batch(v)).reshape(t, N_HEADS * V_DIM)
    four = _fourier(zf.reshape(bsz, seq, zf.shape[1])).reshape(t, zf.shape[1])
    x1, h2, scores_t = _merge(attn, four, sa, sf, x2, mod, w_proj_attn, w_proj_fourier,
                              w_out, norm2_g, w_router, seq)

    idx_t, w_t, rank_t, cnt = _route(scores_t, router_bias)
    counts = cnt[:, 0].astype(jnp.int32)
    unit, bmax = ROW_UNIT, ROW_UNIT * MAX_UNITS
    n_slots = t * TOP_K + e * unit
    padded = ((counts + unit - 1) // unit) * unit
    p_end = jnp.cumsum(padded)
    p_start = p_end - padded
    eid = jnp.arange(e, dtype=jnp.int32)
    nb = (padded + bmax - 1) // bmax
    b_end = jnp.cumsum(nb)
    b_first = b_end - nb
    nblk_used = b_end[-1].astype(jnp.int32)
    nblk = -(-(t * TOP_K) // bmax) + e
    bid = jnp.minimum(jnp.arange(nblk, dtype=jnp.int32), nblk_used - 1)
    blk_expert = jnp.clip(jnp.sum((b_end[None, :] <= bid[:, None]).astype(jnp.int32), axis=1),
                          0, e - 1)
    onehot = blk_expert[:, None] == eid[None, :]
    look = lambda tab: jnp.sum(jnp.where(onehot, tab[None, :], 0), axis=1)
    piece = bid - look(b_first)
    blk_start = (look(p_start) + piece * bmax).astype(jnp.int32)
    blk_size = jnp.clip(look(padded) - piece * bmax, 0, bmax).astype(jnp.int32)

    slot_t = _slots(idx_t, rank_t, p_start.astype(jnp.int32))
    xs = _dispatch(h2, slot_t, n_slots)
    used = counts > 0
    run = jnp.cumsum(used.astype(jnp.int32)) - 1
    blk_run = look(run).astype(jnp.int32)
    rid = jnp.arange(e + W_RING, dtype=jnp.int32)
    match = jnp.logical_and(used[None, :], run[None, :] == rid[:, None])
    run_expert = jnp.where(jnp.any(match, axis=1),
                           jnp.sum(jnp.where(match, eid[None, :], 0), axis=1), -1).astype(jnp.int32)
    ys = _experts(blk_expert, blk_run, run_expert, blk_start, blk_size, nblk_used.reshape(1), xs,
                  w_exp_gate, w_exp_up, w_exp_down)
    out = _combine(_gather(ys, slot_t), w_t, x1, h2, mod, w_sh_gate, w_sh_up, w_sh_down, seq)
    return out.reshape(bsz, seq, d)


def kernel(x, c, w_ada, b_ada, norm1_g, w_in, q_a_norm_g, w_uq, kv_a_norm_g, w_ukv, q_norm_g,
           k_norm_g, w_proj_attn, w_proj_fourier, w_out, norm2_g, w_router, router_bias,
           w_exp_gate, w_exp_up, w_exp_down, w_sh_gate, w_sh_up, w_sh_down):
    for l in range(w_ada.shape[0]):
        x = _layer(x, c, w_ada[l], b_ada[l], norm1_g[l], w_in[l], q_a_norm_g[l], w_uq[l],
                   kv_a_norm_g[l], w_ukv[l], q_norm_g[l], k_norm_g[l], w_proj_attn[l],
                   w_proj_fourier[l], w_out[l], norm2_g[l], w_router[l], router_bias[l],
                   w_exp_gate[l], w_exp_up[l], w_exp_down[l], w_sh_gate[l], w_sh_up[l],
                   w_sh_down[l])
    return x
```

```python
import functools
import math

import numpy as np
import jax
import jax.numpy as jnp
from jax import lax
from jax.experimental import pallas as pl
from jax.experimental.pallas import tpu as pltpu
from jax.experimental.pallas import tpu_sc as plsc

N_HEADS = 8
QK_NOPE = 64
QK_ROPE = 32
V_DIM = 64
FOURIER_GROUP = 64
TOP_K = 8
ROUTED_SCALE = 2.5
EPS = 1e-6
ROPE_THETA = 10000.0
N_ADA = 6

LANES = 128
ROW_UNIT = 128
MAX_UNITS = 8
X_RING = 4
W_RING = 5
KEY_SPLIT = 2
SC_WINDOW = 128
VMEM_LIMIT = 48 * 1024 * 1024

F32 = jnp.float32
BF16 = jnp.bfloat16


def _cparams(*sem):
    return pltpu.CompilerParams(dimension_semantics=sem, vmem_limit_bytes=VMEM_LIMIT)


def _tile(n, pref):
    t = min(n, pref)
    assert n % t == 0, (n, pref)
    return t


def _silu(v):
    return v * jax.nn.sigmoid(v)


def _pack_halves(m):
    d = m.shape[1]
    lo = lax.bitcast_convert_type(m[:, :d // 2].astype(BF16).astype(F32), jnp.uint32)
    hi = lax.bitcast_convert_type(m[:, d // 2:].astype(BF16).astype(F32), jnp.uint32)
    w = (lo >> 16) | (hi & jnp.uint32(0xFFFF0000))
    return w[:, :d // 4], w[:, d // 4:]


def _unpack_halves(w0, w1):
    def lo(w):
        return lax.bitcast_convert_type(w << 16, F32)

    def hi(w):
        return lax.bitcast_convert_type(w & jnp.uint32(0xFFFF0000), F32)
    return jnp.concatenate([lo(w0), lo(w1), hi(w0), hi(w1)], axis=1).astype(BF16)


def _ada_kernel(c_ref, w_ref, b_ref, o_ref):
    a = _silu(c_ref[...])
    o_ref[...] = jnp.dot(a, w_ref[...], preferred_element_type=F32,
                         precision=lax.Precision.HIGHEST) + b_ref[...]


def _ada(c, w_ada, b_ada):
    bsz, d = c.shape
    n = w_ada.shape[1]
    tn = _tile(n, d)
    return pl.pallas_call(
        _ada_kernel,
        grid=(n // tn,),
        in_specs=[pl.BlockSpec((bsz, d), lambda j: (0, 0)),
                  pl.BlockSpec((d, tn), lambda j: (0, j)),
                  pl.BlockSpec((1, tn), lambda j: (0, j))],
        out_specs=pl.BlockSpec((bsz, tn), lambda j: (0, j)),
        out_shape=jax.ShapeDtypeStruct((bsz, n), F32),
        compiler_params=_cparams("arbitrary"),
        name="ada",
    )(c, w_ada, b_ada.reshape(1, n))


def _head_norm_rope(t, trot, a, b):
    ms = jnp.sum(t * t, axis=-1, keepdims=True) * (1.0 / (QK_NOPE + QK_ROPE))
    return (t * a + trot * b) * lax.rsqrt(ms + EPS)


def _inproj_kernel(x_ref, mod_ref, g1_ref, wa_ref, wf_ref, wga_ref, wgf_ref,
                   gq_ref, gkv_ref, wuq_ref, wkv_ref,
                   aq_ref, bq_ref, ak_ref, bk_ref,
                   q_ref, k_ref, v_ref, zf_ref, sa_ref, sf_ref, *, ql, kvl):
    x = x_ref[...]
    mod = mod_ref[0]
    sh1, sc1 = mod[0:1], mod[1:2]
    r = lax.rsqrt(jnp.mean(x * x, axis=-1, keepdims=True) + EPS)
    h = (x * r * g1_ref[...]) * (1.0 + sc1) + sh1
    hb = h.astype(BF16)

    zf_ref[...] = jnp.dot(hb, wf_ref[...], preferred_element_type=F32).astype(BF16)
    sa_ref[...] = jax.nn.sigmoid(
        jnp.dot(hb, wga_ref[...], preferred_element_type=F32)).astype(BF16)
    sf_ref[...] = jax.nn.sigmoid(
        jnp.dot(hb, wgf_ref[...], preferred_element_type=F32)).astype(BF16)

    za = jnp.dot(hb, wa_ref[...], preferred_element_type=F32)
    zq = za[:, :ql]
    cq = zq * lax.rsqrt(jnp.mean(zq * zq, axis=-1, keepdims=True) + EPS) * gq_ref[...]
    qall = jnp.dot(cq.astype(BF16), wuq_ref[...], preferred_element_type=F32)

    zk = za[:, ql:]
    kvn = zk[:, :kvl]
    rk = lax.rsqrt(jnp.mean(kvn * kvn, axis=-1, keepdims=True) + EPS)
    lane = lax.broadcasted_iota(jnp.int32, zk.shape, 1)
    u = zk * jnp.where(lane < kvl, rk, 1.0) * gkv_ref[...]
    kvall = jnp.dot(u.astype(BF16), wkv_ref[...], preferred_element_type=F32)

    aq, bq, ak, bk = aq_ref[...], bq_ref[...], ak_ref[...], bk_ref[...]
    hw = N_HEADS * LANES
    for hd in range(N_HEADS):
        lo, hi = hd * LANES, (hd + 1) * LANES
        q_ref[:, lo:hi] = _head_norm_rope(qall[:, lo:hi], qall[:, hw + lo:hw + hi],
                                       aq, bq).astype(BF16)
        k_ref[:, lo:hi] = _head_norm_rope(kvall[:, lo:hi], kvall[:, hw + lo:hw + hi],
                                       ak, bk).astype(BF16)
    v_ref[...] = kvall[:, 2 * hw:].astype(BF16)


def _rope_tables(seq):
    half = QK_ROPE // 2
    pos = np.arange(seq, dtype=np.float64)
    inv = ROPE_THETA ** (-np.arange(0, QK_ROPE, 2, dtype=np.float64) / QK_ROPE)
    ang = pos[:, None] * inv[None, :]
    c, s = np.cos(ang), np.sin(ang)
    cos = np.ones((seq, LANES)); sin = np.zeros((seq, LANES))
    cos[:, QK_NOPE:QK_NOPE + half] = c
    cos[:, QK_NOPE + half:QK_NOPE + QK_ROPE] = c
    sin[:, QK_NOPE:QK_NOPE + half] = -s
    sin[:, QK_NOPE + half:QK_NOPE + QK_ROPE] = s
    return jnp.asarray(cos, F32), jnp.asarray(sin, F32)


def _partner_columns(w):
    half = QK_ROPE // 2
    lo, mid, hi = QK_NOPE, QK_NOPE + half, QK_NOPE + QK_ROPE
    z = jnp.zeros_like(w)
    return jnp.concatenate([z[..., :lo], w[..., mid:hi], w[..., lo:mid], z[..., hi:]], axis=-1)


def _inproj(x2, mod, norm1_g, w_in, q_a_g, w_uq, kv_a_g, w_ukv, q_g, k_g, bsz, seq):
    t, d = x2.shape
    ql, kvl = q_a_g.shape[0], kv_a_g.shape[0]
    hq = QK_NOPE + QK_ROPE
    fw = w_in.shape[1] - ql - kvl - QK_ROPE - 2 * d
    o1, o2, o3, o4, o5 = ql, ql + kvl, ql + kvl + QK_ROPE, ql + kvl + QK_ROPE + fw, \
        ql + kvl + QK_ROPE + fw + d
    assert ql % LANES == 0 and kvl % LANES == 0

    wa = jnp.concatenate([w_in[:, :o3], jnp.zeros((d, LANES - QK_ROPE), F32)], axis=1).astype(BF16)
    wf = w_in[:, o3:o4].astype(BF16)
    wga = w_in[:, o4:o5].astype(BF16)
    wgf = w_in[:, o5:].astype(BF16)

    wuq = w_uq.reshape(ql, N_HEADS, hq)
    wuq = jnp.pad(wuq, ((0, 0), (0, 0), (0, LANES - hq)))
    wuq = jnp.concatenate([wuq.reshape(ql, N_HEADS * LANES),
                           _partner_columns(wuq).reshape(ql, N_HEADS * LANES)], axis=1).astype(BF16)
    wukv = w_ukv.reshape(kvl, N_HEADS, QK_NOPE + V_DIM)
    wk = jnp.pad(wukv[:, :, :QK_NOPE], ((0, 0), (0, 0), (0, LANES - QK_NOPE)))
    place = jnp.zeros((QK_ROPE, N_HEADS, LANES), F32)
    place = place.at[jnp.arange(QK_ROPE), :, QK_NOPE + jnp.arange(QK_ROPE)].set(1.0)
    wk = jnp.concatenate([wk, place, jnp.zeros((LANES - QK_ROPE, N_HEADS, LANES), F32)], axis=0)
    wv = jnp.concatenate([wukv[:, :, QK_NOPE:], jnp.zeros((LANES, N_HEADS, V_DIM), F32)], axis=0)
    wkv = jnp.concatenate([wk.reshape(kvl + LANES, N_HEADS * LANES),
                           _partner_columns(wk).reshape(kvl + LANES, N_HEADS * LANES),
                           wv.reshape(kvl + LANES, N_HEADS * V_DIM)], axis=1).astype(BF16)

    gkv = jnp.concatenate([kv_a_g, jnp.ones((LANES,), F32)]).reshape(1, kvl + LANES)
    pad = jnp.zeros((LANES - hq,), F32)
    qg = jnp.concatenate([q_g * (hq ** -0.5), pad])
    kg = jnp.concatenate([k_g, pad])
    cos, sin = _rope_tables(seq)
    aq, bq = qg[None, :] * cos, _partner_columns(qg)[None, :] * sin
    ak, bk = kg[None, :] * cos, _partner_columns(kg)[None, :] * sin

    tm = _tile(seq, 512)
    tpb = seq // tm
    full = lambda shp: pl.BlockSpec(shp, lambda i: (0,) * len(shp))
    tok = lambda w: pl.BlockSpec((tm, w), lambda i: (i, 0))
    rope = pl.BlockSpec((tm, LANES), lambda i: (i % tpb, 0))
    return pl.pallas_call(
        functools.partial(_inproj_kernel, ql=ql, kvl=kvl),
        grid=(t // tm,),
        in_specs=[tok(d),
                  pl.BlockSpec((1, N_ADA, d), lambda i: (i // tpb, 0, 0)),
                  full((1, d)), full(wa.shape), full(wf.shape), full(wga.shape), full(wgf.shape),
                  full((1, ql)), full(gkv.shape), full(wuq.shape), full(wkv.shape),
                  rope, rope, rope, rope],
        out_specs=[tok(N_HEADS * LANES), tok(N_HEADS * LANES), tok(N_HEADS * V_DIM), tok(fw), tok(d), tok(d)],
        out_shape=[jax.ShapeDtypeStruct((t, N_HEADS * LANES), BF16),
                   jax.ShapeDtypeStruct((t, N_HEADS * LANES), BF16),
                   jax.ShapeDtypeStruct((t, N_HEADS * V_DIM), BF16),
                   jax.ShapeDtypeStruct((t, fw), BF16),
                   jax.ShapeDtypeStruct((t, d), BF16),
                   jax.ShapeDtypeStruct((t, d), BF16)],
        compiler_params=_cparams("arbitrary"),
        name="inproj",
    )(x2, mod, norm1_g.reshape(1, d), wa, wf, wga, wgf, q_a_g.reshape(1, ql), gkv, wuq, wkv,
      aq, bq, ak, bk)


def _attn_kernel(q_ref, k_ref, v_ref, o_ref, s_ref):
    def scores(hd):
        s_ref[hd % 2] = lax.dot_general(
            q_ref[0, :, hd * LANES:(hd + 1) * LANES], k_ref[0, :, hd * LANES:(hd + 1) * LANES],
            (((1,), (1,)), ((), ())), preferred_element_type=F32)

    scores(0)
    for hd in range(N_HEADS):
        if hd + 1 < N_HEADS:
            scores(hd + 1)
        m = jnp.max(s_ref[hd % 2], axis=-1, keepdims=True)
        nk = s_ref.shape[2]
        l, o = 0.0, 0.0
        for k0 in range(0, nk, nk // KEY_SPLIT):
            ks = slice(k0, k0 + nk // KEY_SPLIT)
            p = jnp.exp(s_ref[hd % 2, :, ks] - m)
            l = l + jnp.sum(p, axis=-1, keepdims=True)
            o = o + jnp.dot(p.astype(BF16), v_ref[0, ks, hd * V_DIM:(hd + 1) * V_DIM],
                            preferred_element_type=F32)
        o_ref[0, :, hd * V_DIM:(hd + 1) * V_DIM] = (o / l).astype(BF16)


def _attention(q, k, v):
    bsz, seq, _ = q.shape
    tq = _tile(seq, 512)
    return pl.pallas_call(
        _attn_kernel,
        grid=(bsz, seq // tq),
        in_specs=[pl.BlockSpec((1, tq, N_HEADS * LANES), lambda b, j: (b, j, 0)),
                  pl.BlockSpec((1, seq, N_HEADS * LANES), lambda b, j: (b, 0, 0)),
                  pl.BlockSpec((1, seq, N_HEADS * V_DIM), lambda b, j: (b, 0, 0))],
        out_specs=pl.BlockSpec((1, tq, N_HEADS * V_DIM), lambda b, j: (b, j, 0)),
        out_shape=jax.ShapeDtypeStruct((bsz, seq, N_HEADS * V_DIM), BF16),
        scratch_shapes=[pltpu.VMEM((2, tq, seq), F32)],
        compiler_params=_cparams("arbitrary", "arbitrary"),
        name="attn",
    )(q, k, v)


def _fourier_kernel(z_ref, wc_ref, ws_ref, tab_ref, o_ref, u_ref, *, seq):
    @pl.when(pl.program_id(1) == 0)
    def _():
        z = z_ref[0]
        u_ref[:seq, :] = jnp.dot(z, wc_ref[...], preferred_element_type=F32).astype(BF16)
        u_ref[seq:, :] = jnp.dot(z, ws_ref[...], preferred_element_type=F32).astype(BF16)

    o_ref[0] = jnp.dot(tab_ref[...], u_ref[...], preferred_element_type=F32).astype(BF16)


def _fourier_tables(seq, fw):
    g = FOURIER_GROUP
    n = np.arange(seq, dtype=np.int64)
    ang = 2.0 * np.pi * ((n[:, None] * n[None, :]) % seq).astype(np.float64) / seq
    tab = np.concatenate([np.cos(ang), -np.sin(ang)], axis=1)
    c = np.arange(g, dtype=np.int64)
    angc = 2.0 * np.pi * ((c[:, None] * c[None, :]) % g).astype(np.float64) / g
    scale = 1.0 / math.sqrt(seq * g)
    eye = np.eye(fw // g)
    wc = np.kron(eye, np.cos(angc) * scale)
    ws = np.kron(eye, np.sin(angc) * scale)
    return (jnp.asarray(tab, F32).astype(BF16), jnp.asarray(wc, F32).astype(BF16),
            jnp.asarray(ws, F32).astype(BF16))


def _fourier(zf):
    bsz, seq, fw = zf.shape
    tab, wc, ws = _fourier_tables(seq, fw)
    tr = _tile(seq, 1024)
    return pl.pallas_call(
        functools.partial(_fourier_kernel, seq=seq),
        grid=(bsz, seq // tr),
        in_specs=[pl.BlockSpec((1, seq, fw), lambda b, j: (b, 0, 0)),
                  pl.BlockSpec((fw, fw), lambda b, j: (0, 0)),
                  pl.BlockSpec((fw, fw), lambda b, j: (0, 0)),
                  pl.BlockSpec((tr, 2 * seq), lambda b, j: (j, 0))],
        out_specs=pl.BlockSpec((1, tr, fw), lambda b, j: (b, j, 0)),
        out_shape=jax.ShapeDtypeStruct((bsz, seq, fw), BF16),
        scratch_shapes=[pltpu.VMEM((2 * seq, fw), BF16)],
        compiler_params=_cparams("arbitrary", "arbitrary"),
        name="fourier",
    )(zf, wc, ws, tab)


def _merge_kernel(a_ref, f_ref, sa_ref, sf_ref, x_ref, mod_ref, wpa_ref, wpf_ref, wo_ref,
                  g2_ref, wrh_ref, wrl_ref, x1_ref, h2_ref, sc_ref):
    ya = jnp.dot(a_ref[...], wpa_ref[...], preferred_element_type=F32)
    yf = jnp.dot(f_ref[...], wpf_ref[...], preferred_element_type=F32)
    merged = sa_ref[...].astype(F32) * ya + sf_ref[...].astype(F32) * yf
    mod = mod_ref[0]
    g1, sh2, sc2 = mod[2:3], mod[3:4], mod[4:5]
    x1 = x_ref[...] + g1 * jnp.dot(merged.astype(BF16), wo_ref[...], preferred_element_type=F32)
    x1_ref[...] = x1
    r = lax.rsqrt(jnp.mean(x1 * x1, axis=-1, keepdims=True) + EPS)
    h2 = (x1 * r * g2_ref[...]) * (1.0 + sc2) + sh2
    h2_ref[0], h2_ref[1] = _pack_halves(h2)
    hh = h2.astype(BF16)
    hl = (h2 - hh.astype(F32)).astype(BF16)
    nt = (((1,), (1,)), ((), ()))
    lt = (lax.dot_general(wrh_ref[...], hh, nt, preferred_element_type=F32)
          + lax.dot_general(wrh_ref[...], hl, nt, preferred_element_type=F32)
          + lax.dot_general(wrl_ref[...], hh, nt, preferred_element_type=F32))
    sc_ref[...] = jax.nn.sigmoid(lt)


def _merge(attn, four, sa, sf, x2, mod, w_pa, w_pf, w_out, norm2_g, w_router, seq):
    t, d = x2.shape
    e = w_router.shape[1]
    wrt = w_router.T
    wrh = wrt.astype(BF16)
    wrl = (wrt - wrh.astype(F32)).astype(BF16)
    tm = _tile(seq, 512)
    tpb = seq // tm
    full = lambda shp: pl.BlockSpec(shp, lambda i: (0,) * len(shp))
    tok = lambda w: pl.BlockSpec((tm, w), lambda i: (i, 0))
    return pl.pallas_call(
        _merge_kernel,
        grid=(t // tm,),
        in_specs=[tok(attn.shape[1]), tok(four.shape[1]), tok(d), tok(d), tok(d),
                  pl.BlockSpec((1, N_ADA, d), lambda i: (i // tpb, 0, 0)),
                  full(w_pa.shape), full(w_pf.shape), full(w_out.shape), full((1, d)),
                  full((e, d)), full((e, d))],
        out_specs=[tok(d), pl.BlockSpec((2, tm, d // 4), lambda i: (0, i, 0)),
                   pl.BlockSpec((e, tm), lambda i: (0, i))],
        out_shape=[jax.ShapeDtypeStruct((t, d), F32),
                   jax.ShapeDtypeStruct((2, t, d // 4), jnp.uint32),
                   jax.ShapeDtypeStruct((e, t), F32)],
        compiler_params=_cparams("arbitrary"),
        name="merge",
    )(attn, four, sa, sf, x2, mod, w_pa.astype(BF16), w_pf.astype(BF16), w_out.astype(BF16),
      norm2_g.reshape(1, d), wrh, wrl)


def _route_kernel(s_ref, b_ref, tri_ref, idx_ref, w_ref, rank_ref, cnt_ref, carry_ref):
    @pl.when(pl.program_id(0) == 0)
    def _():
        carry_ref[...] = jnp.zeros_like(carry_ref)

    sc = s_ref[...]
    e, tr = sc.shape
    row = lax.broadcasted_iota(jnp.int32, (e, tr), 0)
    v = sc + b_ref[...]
    sel = jnp.zeros((e, tr), F32)
    idxs, ws = [], []
    for _ in range(TOP_K):
        m = jnp.max(v, axis=0, keepdims=True)
        idx = jnp.min(jnp.where(v == m, row, e), axis=0, keepdims=True)
        oh = row == idx
        ws.append(jnp.sum(jnp.where(oh, sc, 0.0), axis=0, keepdims=True))
        idxs.append(idx)
        v = jnp.where(oh, -jnp.inf, v)
        sel = sel + oh.astype(F32)
    wsum = ws[0]
    for w in ws[1:]:
        wsum = wsum + w
    selb = sel.astype(BF16)
    cum = jnp.dot(selb, tri_ref[...], preferred_element_type=F32) + carry_ref[...]
    for kk in range(TOP_K):
        oh = row == idxs[kk]
        rk = jnp.sum(jnp.where(oh, cum, 0.0), axis=0, keepdims=True)
        idx_ref[kk:kk + 1, :] = idxs[kk]
        rank_ref[kk:kk + 1, :] = rk.astype(jnp.int32)
        w_ref[kk:kk + 1, :] = ws[kk] / wsum * ROUTED_SCALE
    tot = carry_ref[...] + jnp.dot(selb, jnp.ones((tr, tr), BF16), preferred_element_type=F32)
    carry_ref[...] = tot
    cnt_ref[...] = tot


def _route(scores_t, router_bias):
    e, t = scores_t.shape
    tr = _tile(t, 256)
    tri = jnp.asarray(np.triu(np.ones((tr, tr), np.float32), 1), BF16)
    bias = jnp.broadcast_to(router_bias.reshape(e, 1), (e, tr)).astype(F32)
    blk = pl.BlockSpec((TOP_K, tr), lambda i: (0, i))
    return pl.pallas_call(
        _route_kernel,
        grid=(t // tr,),
        in_specs=[pl.BlockSpec((e, tr), lambda i: (0, i)),
                  pl.BlockSpec((e, tr), lambda i: (0, 0)),
                  pl.BlockSpec((tr, tr), lambda i: (0, 0))],
        out_specs=[blk, blk, blk, pl.BlockSpec((e, tr), lambda i: (0, 0))],
        out_shape=[jax.ShapeDtypeStruct((TOP_K, t), jnp.int32),
                   jax.ShapeDtypeStruct((TOP_K, t), F32),
                   jax.ShapeDtypeStruct((TOP_K, t), jnp.int32),
                   jax.ShapeDtypeStruct((e, tr), F32)],
        scratch_shapes=[pltpu.VMEM((e, tr), F32)],
        compiler_params=_cparams("arbitrary"),
        name="route",
    )(scores_t, bias, tri)


def _slots_kernel(idx_ref, rank_ref, ps_ref, slot_ref):
    ps = ps_ref[...]
    row = lax.broadcasted_iota(jnp.int32, ps.shape, 0)
    for kk in range(TOP_K):
        oh = row == idx_ref[kk:kk + 1, :]
        start = jnp.sum(jnp.where(oh, ps, 0), axis=0, keepdims=True)
        slot_ref[kk:kk + 1, :] = start + rank_ref[kk:kk + 1, :]


def _slots(idx_t, rank_t, p_start):
    _, t = idx_t.shape
    e = p_start.shape[0]
    ts = _tile(t, 2048)
    ps = jnp.broadcast_to(p_start.reshape(e, 1), (e, ts))
    blk = pl.BlockSpec((TOP_K, ts), lambda i: (0, i))
    return pl.pallas_call(
        _slots_kernel,
        grid=(t // ts,),
        in_specs=[blk, blk, pl.BlockSpec((e, ts), lambda i: (0, 0))],
        out_specs=blk,
        out_shape=jax.ShapeDtypeStruct((TOP_K, t), jnp.int32),
        compiler_params=_cparams("arbitrary"),
        name="slots",
    )(idx_t, rank_t, ps)


def _dispatch(h2p, slot_t, n_slots):
    _, t, c = h2p.shape
    k = slot_t.shape[0]
    win = _tile(2 * t, SC_WINDOW)
    rows = h2p.reshape(2 * t, c)
    dest = jnp.concatenate([slot_t, slot_t + n_slots], axis=1)
    mesh = plsc.VectorSubcoreMesh(core_axis_name="core", subcore_axis_name="subcore")

    @pl.kernel(out_type=jax.ShapeDtypeStruct((2 * n_slots, c), h2p.dtype), mesh=mesh,
               scratch_types=[])
    def scatter_rows(x_hbm, s_hbm, o_hbm):
        def body(x_vmem, s_vmem):
            pltpu.sync_copy(x_vmem, o_hbm.at[s_vmem.at[0]])

        pltpu.emit_pipeline(
            body, grid=(2 * t // win, k),
            in_specs=[pl.BlockSpec((win, c), lambda i, j: (i, 0)),
                      pl.BlockSpec((1, win), lambda i, j: (j, i))],
            out_specs=[], core_axis_name=("core", "subcore"),
            dimension_semantics=(pltpu.PARALLEL, pltpu.ARBITRARY))(x_hbm, s_hbm)

    return scatter_rows(rows, dest).reshape(2, n_slots, c)


def _expert_kernel(be_ref, run_ref, rex_ref, bst_ref, bsz_ref, nu_ref, xs_hbm, wg_hbm, wu_hbm,
                   wd_hbm, ys_hbm, xbuf, ybuf, wgf, wuf, wdf, wgb, wub, wdb, xsem, ysem, wsem,
                   *, n_slots):
    nu = nu_ref[0]
    unit = ROW_UNIT

    def x_stream(blk, slot, start):
        r0 = pl.multiple_of(bst_ref[blk], unit)
        for ch in range(MAX_UNITS):
            @pl.when(ch * unit < bsz_ref[blk])
            def _():
                for h in range(2):
                    cp = pltpu.make_async_copy(xs_hbm.at[h, pl.ds(r0 + ch * unit, unit)],
                                               xbuf.at[slot, h, pl.ds(ch * unit, unit)],
                                               xsem.at[slot])
                    cp.start() if start else cp.wait()

    def y_copies(r0, slot, m):
        return [pltpu.make_async_copy(ybuf.at[slot, h, pl.ds(0, m)],
                                      ys_hbm.at[h, pl.ds(pl.multiple_of(r0, unit), m)],
                                      ysem.at[slot]) for h in range(2)]

    def y_wait(blk, slot):
        for n in range(1, MAX_UNITS + 1):
            @pl.when(bsz_ref[blk] == n * unit)
            def _():
                for cp in y_copies(bst_ref[blk], slot, n * unit):
                    cp.wait()

    def weight_copies(e, which):
        return (pltpu.make_async_copy(wg_hbm.at[e], wgf.at[which], wsem.at[which]),
                pltpu.make_async_copy(wu_hbm.at[e], wuf.at[which], wsem.at[which]),
                pltpu.make_async_copy(wd_hbm.at[e], wdf.at[which], wsem.at[which]))

    for j in range(W_RING - 1):
        @pl.when(rex_ref[j] >= 0)
        def _():
            for cp in weight_copies(rex_ref[j], j):
                cp.start(priority=1)
    for j in range(X_RING - 1):
        @pl.when(j < nu)
        def _():
            x_stream(j, j, True)

    def step(i, carry):
        ahead = i + X_RING - 1

        @pl.when(ahead < nu)
        def _():
            x_stream(ahead, ahead % X_RING, True)

        prev = be_ref[jnp.maximum(i - 1, 0)]

        @pl.when(jnp.logical_or(i == 0, be_ref[i] != prev))
        def _():
            r = run_ref[i]
            par = r % W_RING
            for cp in weight_copies(be_ref[i], par):
                cp.wait()
            wgb[...] = wgf[par].astype(BF16)
            wub[...] = wuf[par].astype(BF16)
            wdb[...] = wdf[par].astype(BF16)
            later = rex_ref[r + W_RING - 1]

            @pl.when(later >= 0)
            def _():
                for cp in weight_copies(later, (r + W_RING - 1) % W_RING):
                    cp.start(priority=1)

        slot = i % X_RING
        x_stream(i, slot, False)
        out = i % 2

        @pl.when(i >= 2)
        def _():
            y_wait(i - 2, out)

        for n in range(1, MAX_UNITS + 1):
            @pl.when(bsz_ref[i] == n * unit)
            def _():
                m = n * unit
                x = _unpack_halves(xbuf[slot, 0, pl.ds(0, m)], xbuf[slot, 1, pl.ds(0, m)])
                g = jnp.dot(x, wgb[...], preferred_element_type=F32)
                u = jnp.dot(x, wub[...], preferred_element_type=F32)
                a = (_silu(g) * u).astype(BF16)
                y0, y1 = _pack_halves(jnp.dot(a, wdb[...], preferred_element_type=F32))
                ybuf[out, 0, pl.ds(0, m)] = y0
                ybuf[out, 1, pl.ds(0, m)] = y1
                for cp in y_copies(bst_ref[i], out, m):
                    cp.start()
        return carry

    lax.fori_loop(0, nu, step, 0)

    @pl.when(nu >= 2)
    def _():
        y_wait(nu - 2, nu % 2)
    y_wait(nu - 1, (nu - 1) % 2)

    ybuf[0, :, pl.ds(0, unit)] = jnp.zeros((2, unit, ybuf.shape[3]), ybuf.dtype)
    used = bst_ref[nu - 1] + bsz_ref[nu - 1]
    spare = (n_slots - used) // unit

    def zero_start(j, carry):
        for cp in y_copies(used + j * unit, 0, unit):
            cp.start()
        return carry

    def zero_wait(j, carry):
        for cp in y_copies(used + j * unit, 0, unit):
            cp.wait()
        return carry
    lax.fori_loop(0, spare, zero_start, 0)
    lax.fori_loop(0, spare, zero_wait, 0)


def _experts(blk_expert, blk_run, run_expert, blk_start, blk_size, nblk_used, xs, w_g, w_u, w_d):
    d, f = w_g.shape[1], w_g.shape[2]
    rows = ROW_UNIT * MAX_UNITS
    n_slots = xs.shape[1]
    c = xs.shape[2]
    hbm = pl.BlockSpec(memory_space=pl.ANY)
    return pl.pallas_call(
        functools.partial(_expert_kernel, n_slots=n_slots),
        grid_spec=pltpu.PrefetchScalarGridSpec(
            num_scalar_prefetch=6,
            grid=(1,),
            in_specs=[hbm, hbm, hbm, hbm],
            out_specs=hbm,
            scratch_shapes=[pltpu.VMEM((X_RING, 2, rows, c), jnp.uint32),
                            pltpu.VMEM((2, 2, rows, c), jnp.uint32),
                            pltpu.VMEM((W_RING, d, f), F32), pltpu.VMEM((W_RING, d, f), F32),
                            pltpu.VMEM((W_RING, f, d), F32),
                            pltpu.VMEM((d, f), BF16), pltpu.VMEM((d, f), BF16),
                            pltpu.VMEM((f, d), BF16),
                            pltpu.SemaphoreType.DMA((X_RING,)), pltpu.SemaphoreType.DMA((2,)),
                            pltpu.SemaphoreType.DMA((W_RING,))]),
        out_shape=jax.ShapeDtypeStruct((2, n_slots, c), jnp.uint32),
        compiler_params=_cparams("arbitrary"),
        name="experts",
    )(blk_expert, blk_run, run_expert, blk_start, blk_size, nblk_used, xs, w_g, w_u, w_d)


def _gather(ys, slot_t):
    _, n_slots, c = ys.shape
    k, t = slot_t.shape
    p = 2 * k * t
    win = _tile(p, SC_WINDOW)
    src = jnp.concatenate([slot_t, slot_t + n_slots], axis=0).reshape(1, p)
    mesh = plsc.VectorSubcoreMesh(core_axis_name="core", subcore_axis_name="subcore")

    @pl.kernel(out_type=jax.ShapeDtypeStruct((p, c), ys.dtype), mesh=mesh, scratch_types=[])
    def gather_rows(y_hbm, s_hbm, o_hbm):
        def body(s_vmem, o_vmem):
            pltpu.sync_copy(y_hbm.at[s_vmem.at[0]], o_vmem)

        pltpu.emit_pipeline(
            body, grid=(p // win,),
            in_specs=[pl.BlockSpec((1, win), lambda i: (0, i))],
            out_specs=[pl.BlockSpec((win, c), lambda i: (i, 0))],
            core_axis_name=("core", "subcore"),
            dimension_semantics=(pltpu.PARALLEL,))(s_hbm, o_hbm)

    return gather_rows(ys.reshape(2 * n_slots, c), src).reshape(2, k, t, c)


def _combine_kernel(y_ref, w_ref, x1_ref, h2_ref, mod_ref, wsg_ref, wsu_ref, wsd_ref, o_ref):
    hb = _unpack_halves(h2_ref[0], h2_ref[1])
    g = jnp.dot(hb, wsg_ref[...], preferred_element_type=F32)
    u = jnp.dot(hb, wsu_ref[...], preferred_element_type=F32)
    acc = jnp.dot((_silu(g) * u).astype(BF16), wsd_ref[...], preferred_element_type=F32)
    w = w_ref[...].T
    for kk in range(TOP_K):
        acc = acc + w[:, kk:kk + 1] * _unpack_halves(y_ref[0, kk], y_ref[1, kk]).astype(F32)
    g2 = mod_ref[0][5:6]
    o_ref[...] = x1_ref[...] + g2 * acc


def _combine(y_tok, w_tk, x1, h2p, mod, w_sg, w_su, w_sd, seq):
    t, d = x1.shape
    c = d // 4
    tc = _tile(seq, 512)
    tpb = seq // tc
    full = lambda shp: pl.BlockSpec(shp, lambda i: (0,) * len(shp))
    tok = lambda w: pl.BlockSpec((tc, w), lambda i: (i, 0))
    return pl.pallas_call(
        _combine_kernel,
        grid=(t // tc,),
        in_specs=[pl.BlockSpec((2, TOP_K, tc, c), lambda i: (0, 0, i, 0)),
                  pl.BlockSpec((TOP_K, tc), lambda i: (0, i)), tok(d),
                  pl.BlockSpec((2, tc, c), lambda i: (0, i, 0)),
                  pl.BlockSpec((1, N_ADA, d), lambda i: (i // tpb, 0, 0)),
                  full(w_sg.shape), full(w_su.shape), full(w_sd.shape)],
        out_specs=tok(d),
        out_shape=jax.ShapeDtypeStruct((t, d), F32),
        compiler_params=_cparams("arbitrary"),
        name="combine",
    )(y_tok, w_tk, x1, h2p, mod, w_sg.astype(BF16), w_su.astype(BF16), w_sd.astype(BF16))


def _layer(x, c, w_ada, b_ada, norm1_g, w_in, q_a_norm_g, w_uq, kv_a_norm_g, w_ukv,
           q_norm_g, k_norm_g, w_proj_attn, w_proj_fourier, w_out, norm2_g,
           w_router, router_bias, w_exp_gate, w_exp_up, w_exp_down,
           w_sh_gate, w_sh_up, w_sh_down):
    bsz, seq, d = x.shape
    t = bsz * seq
    e = w_router.shape[1]
    x2 = x.reshape(t, d)

    mod = _ada(c, w_ada, b_ada).reshape(bsz, N_ADA, d)
    q, k, v, zf, sa, sf = _inproj(x2, mod, norm1_g, w_in, q_a_norm_g, w_uq, kv_a_norm_g,
                                  w_ukv, q_norm_g, k_norm_g, bsz, seq)
    per_batch = lambda a: a.reshape(bsz, seq, a.shape[1])
    attn = _attention(per_batch(q), per_batch(k), per_batch(v)).reshape(t, N_HEADS * V_DIM)
    four = _fourier(zf.reshape(bsz, seq, zf.shape[1])).reshape(t, zf.shape[1])
    x1, h2, scores_t = _merge(attn, four, sa, sf, x2, mod, w_proj_attn, w_proj_fourier,
                              w_out, norm2_g, w_router, seq)

    idx_t, w_t, rank_t, cnt = _route(scores_t, router_bias)
    counts = cnt[:, 0].astype(jnp.int32)
    unit, bmax = ROW_UNIT, ROW_UNIT * MAX_UNITS
    n_slots = t * TOP_K + e * unit
    padded = ((counts + unit - 1) // unit) * unit
    p_end = jnp.cumsum(padded)
    p_start = p_end - padded
    eid = jnp.arange(e, dtype=jnp.int32)
    nb = (padded + bmax - 1) // bmax
    b_end = jnp.cumsum(nb)
    b_first = b_end - nb
    nblk_used = b_end[-1].astype(jnp.int32)
    nblk = -(-(t * TOP_K) // bmax) + e
    bid = jnp.minimum(jnp.arange(nblk, dtype=jnp.int32), nblk_used - 1)
    blk_expert = jnp.clip(jnp.sum((b_end[None, :] <= bid[:, None]).astype(jnp.int32), axis=1),
                          0, e - 1)
    onehot = blk_expert[:, None] == eid[None, :]
    look = lambda tab: jnp.sum(jnp.where(onehot, tab[None, :], 0), axis=1)
    piece = bid - look(b_first)
    blk_start = (look(p_start) + piece * bmax).astype(jnp.int32)
    blk_size = jnp.clip(look(padded) - piece * bmax, 0, bmax).astype(jnp.int32)

    slot_t = _slots(idx_t, rank_t, p_start.astype(jnp.int32))
    xs = _dispatch(h2, slot_t, n_slots)
    used = counts > 0
    run = jnp.cumsum(used.astype(jnp.int32)) - 1
    blk_run = look(run).astype(jnp.int32)
    rid = jnp.arange(e + W_RING, dtype=jnp.int32)
    match = jnp.logical_and(used[None, :], run[None, :] == rid[:, None])
    run_expert = jnp.where(jnp.any(match, axis=1),
                           jnp.sum(jnp.where(match, eid[None, :], 0), axis=1), -1).astype(jnp.int32)
    ys = _experts(blk_expert, blk_run, run_expert, blk_start, blk_size, nblk_used.reshape(1), xs,
                  w_exp_gate, w_exp_up, w_exp_down)
    out = _combine(_gather(ys, slot_t), w_t, x1, h2, mod, w_sh_gate, w_sh_up, w_sh_down, seq)
    return out.reshape(bsz, seq, d)


def kernel(x, c, w_ada, b_ada, norm1_g, w_in, q_a_norm_g, w_uq, kv_a_norm_g, w_ukv, q_norm_g,
           k_norm_g, w_proj_attn, w_proj_fourier, w_out, norm2_g, w_router, router_bias,
           w_exp_gate, w_exp_up, w_exp_down, w_sh_gate, w_sh_up, w_sh_down):
    for l in range(w_ada.shape[0]):
        x = _layer(x, c, w_ada[l], b_ada[l], norm1_g[l], w_in[l], q_a_norm_g[l], w_uq[l],
                   kv_a_norm_g[l], w_ukv[l], q_norm_g[l], k_norm_g[l], w_proj_attn[l],
                   w_proj_fourier[l], w_out[l], norm2_g[l], w_router[l], router_bias[l],
                   w_exp_gate[l], w_exp_up[l], w_exp_down[l], w_sh_gate[l], w_sh_up[l],
                   w_sh_down[l])
    return x
```
